```python
import math
import jax, jax.numpy as jnp
from jax import lax
import numpy as np

D_MODEL = 1024
BATCH = 16
SEQ = 2048
DEPTH = 2

N_EVEN = (DEPTH + 1) // 2
N_ODD = DEPTH // 2
EPS = 1e-6

S5_WIDTH = D_MODEL // 2
S5_GROUP = 16
S5_GROUPS = S5_WIDTH // S5_GROUP
S5_STATE = 64

HG_WIDTH = D_MODEL - S5_WIDTH
HG_HEAD_DIM = 128
HG_HEADS = HG_WIDTH // HG_HEAD_DIM
HG_CHUNK = 32

EVEN_IN = S5_WIDTH + 4 * HG_WIDTH

HEAD_DIM = 64
N_Q_HEADS = D_MODEL // HEAD_DIM
N_KV_HEADS = max(1, N_Q_HEADS // 8)
GQA_GROUP = N_Q_HEADS // N_KV_HEADS
Q_WIDTH = N_Q_HEADS * HEAD_DIM
KV_WIDTH = N_KV_HEADS * HEAD_DIM
WINDOW = 128
ATT_BLOCK = 128

N_GROUPS = 4
EXPERTS_PER_GROUP = 4
N_EXPERTS = N_GROUPS * EXPERTS_PER_GROUP
TOP_K = 2
D_EXPERT = D_MODEL // 4

kernel_name = "hybrid_s5_hgrn2_swa_hmoe"


def rms_norm(x, gain):
    xf = x.astype(jnp.float32)
    return xf * lax.rsqrt(jnp.mean(xf * xf, axis=-1, keepdims=True) + EPS) * gain.astype(jnp.float32)


def alibi_slopes():
    return jnp.asarray(2.0 ** (-8.0 * np.arange(1, N_Q_HEADS + 1) / N_Q_HEADS), dtype=jnp.float32)


def s5_mixer(u, lam_re, lam_im, log_step, b_re, b_im, c_re, c_im, d_skip, w_glu):
    f32 = jnp.float32
    bsz, seqlen, _ = u.shape
    uf = u.astype(f32).reshape(bsz, seqlen, S5_GROUPS, S5_GROUP)
    lr, li = lam_re.astype(f32), lam_im.astype(f32)
    step = jnp.exp(log_step.astype(f32))[:, None]
    mag = jnp.exp(lr * step)
    ab_re = mag * jnp.cos(li * step)
    ab_im = mag * jnp.sin(li * step)
    den = lr * lr + li * li
    nr, ni = ab_re - 1.0, ab_im
    z_re = (nr * lr + ni * li) / den
    z_im = (ni * lr - nr * li) / den
    br, bi = b_re.astype(f32), b_im.astype(f32)
    bb_re = z_re[..., None] * br - z_im[..., None] * bi
    bb_im = z_re[..., None] * bi + z_im[..., None] * br
    bu_re = jnp.einsum('blgh,gph->blgp', uf, bb_re)
    bu_im = jnp.einsum('blgh,gph->blgp', uf, bb_im)
    a_re = jnp.broadcast_to(ab_re[None, None], (1, seqlen, S5_GROUPS, S5_STATE))
    a_im = jnp.broadcast_to(ab_im[None, None], (1, seqlen, S5_GROUPS, S5_STATE))

    def combine(left, right):
        a1r, a1i, b1r, b1i = left
        a2r, a2i, b2r, b2i = right
        return (a2r * a1r - a2i * a1i,
                a2r * a1i + a2i * a1r,
                a2r * b1r - a2i * b1i + b2r,
                a2r * b1i + a2i * b1r + b2i)

    _, _, xs_re, xs_im = lax.associative_scan(combine, (a_re, a_im, bu_re, bu_im), axis=1)
    y = (jnp.einsum('blgp,ghp->blgh', xs_re, c_re.astype(f32))
         - jnp.einsum('blgp,ghp->blgh', xs_im, c_im.astype(f32)))
    y = y.reshape(bsz, seqlen, S5_WIDTH) + d_skip.astype(f32) * u.astype(f32)
    y = jax.nn.gelu(y)
    return y * jax.nn.sigmoid(jnp.einsum('blc,ce->ble', y, w_glu.astype(f32)))


def hgrn2_mixer(q, f_logit, i_val, g, lower_bound, o_gain):
    f32 = jnp.float32
    bsz, seqlen, _ = q.shape
    nc = seqlen // HG_CHUNK
    qs = jax.nn.silu(q.astype(f32))
    f = lower_bound + (1.0 - lower_bound) * jax.nn.sigmoid(f_logit.astype(f32))
    log_f = jnp.log(f)
    k = 1.0 - f

    def split(t):
        return t.reshape(bsz, nc, HG_CHUNK, HG_HEADS, HG_HEAD_DIM).transpose(0, 3, 1, 2, 4)

    qc, kc, vc, lfc = split(qs), split(k), split(i_val.astype(f32)), split(log_f)
    b = jnp.cumsum(lfc, axis=3)
    b_last = b[:, :, :, -1:, :]
    b_ref = b[:, :, :, HG_CHUNK // 2 - 1:HG_CHUNK // 2, :]
    scores = jnp.einsum('bhntd,bhnsd->bhnts', qc * jnp.exp(b - b_ref), kc * jnp.exp(b_ref - b))
    causal = jnp.tril(jnp.ones((HG_CHUNK, HG_CHUNK), dtype=bool))
    scores = jnp.where(causal, scores, 0.0)
    o_intra = jnp.einsum('bhnts,bhnsv->bhntv', scores, vc)
    u_chunk = jnp.einsum('bhnsd,bhnsv->bhndv', kc * jnp.exp(b_last - b), vc)
    decay = jnp.exp(b_last[:, :, :, 0, :])

    def step(state, inp):
        dec, uc = inp
        return dec[..., None] * state + uc, state

    s0 = jnp.zeros((bsz, HG_HEADS, HG_HEAD_DIM, HG_HEAD_DIM), f32)
    _, s_prev = lax.scan(step, s0, (jnp.moveaxis(decay, 2, 0), jnp.moveaxis(u_chunk, 2, 0)))
    s_prev = jnp.moveaxis(s_prev, 0, 2)
    o_inter = jnp.einsum('bhntd,bhndv->bhntv', qc * jnp.exp(b), s_prev)
    o = (o_intra + o_inter).transpose(0, 2, 3, 1, 4).reshape(bsz, seqlen, HG_HEADS, HG_HEAD_DIM)
    o = rms_norm(o, o_gain).reshape(bsz, seqlen, HG_WIDTH)
    return o * jax.nn.silu(g.astype(f32))


def sliding_window_attention(h, w_qkv, q_gain, k_gain, sinks):
    f32 = jnp.float32
    bsz, seqlen, _ = h.shape
    nb = seqlen // ATT_BLOCK
    qkv = jnp.einsum('bld,de->ble', h, w_qkv)
    q = qkv[..., :Q_WIDTH].reshape(bsz, seqlen, N_KV_HEADS, GQA_GROUP, HEAD_DIM)
    k = qkv[..., Q_WIDTH:Q_WIDTH + KV_WIDTH].reshape(bsz, seqlen, N_KV_HEADS, HEAD_DIM)
    v = qkv[..., Q_WIDTH + KV_WIDTH:].reshape(bsz, seqlen, N_KV_HEADS, HEAD_DIM).astype(f32)
    q = rms_norm(q, q_gain) * (HEAD_DIM ** -0.5)
    k = rms_norm(k, k_gain)
    qb = q.reshape(bsz, nb, ATT_BLOCK, N_KV_HEADS, GQA_GROUP, HEAD_DIM)

    def band(t):
        tb = t.reshape(bsz, nb, ATT_BLOCK, N_KV_HEADS, HEAD_DIM)
        prev = jnp.concatenate([jnp.zeros_like(tb[:, :1]), tb[:, :-1]], axis=1)
        return jnp.concatenate([prev, tb], axis=2)

    kw, vw = band(k), band(v)
    s = jnp.einsum('bnqhgd,bnkhd->bhgnqk', qb, kw)
    qi = jnp.arange(ATT_BLOCK)[:, None]
    ki = jnp.arange(2 * ATT_BLOCK)[None, :]
    dist = qi - ki + ATT_BLOCK
    blk = jnp.arange(nb)[:, None, None]
    valid = (dist >= 0) & (dist < WINDOW) & (blk * ATT_BLOCK + ki - ATT_BLOCK >= 0)
    slopes = alibi_slopes().reshape(N_KV_HEADS, GQA_GROUP)[:, :, None, None, None]
    s = s - slopes * dist.astype(f32)
    s = jnp.where(valid, s, -jnp.inf)
    sink = sinks.astype(f32).reshape(N_KV_HEADS, GQA_GROUP)[:, :, None, None, None]
    m = jnp.maximum(jnp.max(s, axis=-1, keepdims=True), sink)
    p = jnp.exp(s - m)
    p = p / (jnp.sum(p, axis=-1, keepdims=True) + jnp.exp(sink - m))
    o = jnp.einsum('bhgnqk,bnkhd->bnqhgd', p, vw)
    return o.reshape(bsz, seqlen, Q_WIDTH)


def hierarchical_moe(h, w_rg, b_rg, w_re, b_re, w_gate, w_up, w_down):
    f32 = jnp.float32
    bsz, seqlen, dm = h.shape
    t = h.reshape(bsz * seqlen, dm)
    g_prob = jax.nn.softmax(jnp.einsum('nd,dg->ng', t, w_rg).astype(f32) + b_rg.astype(f32), axis=-1)
    g_gate, g_idx = lax.top_k(g_prob, 1)
    e_logits = (jnp.einsum('nd,de->ne', t, w_re).astype(f32) + b_re.astype(f32))
    e_logits = e_logits.reshape(-1, N_GROUPS, EXPERTS_PER_GROUP)
    grp_onehot = jax.nn.one_hot(g_idx[:, 0], N_GROUPS, dtype=f32)
    e_sel = jnp.sum(e_logits * grp_onehot[:, :, None], axis=1)
    e_prob = jax.nn.softmax(e_sel, axis=-1)
    e_w, e_idx = lax.top_k(e_prob, TOP_K)
    e_w = e_w / jnp.sum(e_w, axis=-1, keepdims=True)
    weights = g_gate * e_w
    expert = g_idx * EXPERTS_PER_GROUP + e_idx
    combine = jnp.sum(jax.nn.one_hot(expert, N_EXPERTS, dtype=f32) * weights[..., None], axis=1)
    hid = jax.nn.silu(jnp.einsum('nd,edf->nef', t, w_gate)) * jnp.einsum('nd,edf->nef', t, w_up)
    hid = hid * combine[:, :, None].astype(hid.dtype)
    y = jnp.einsum('nef,efd->nd', hid, w_down)
    return y.reshape(bsz, seqlen, dm)


def setup_inputs(seed: int = 0) -> dict:
    key = jax.random.key(seed)
    ks = iter(jax.random.split(key, 40))

    def nrm(shape, scale):
        return jax.random.normal(next(ks), shape, jnp.float32) * scale

    n_idx = jnp.arange(S5_STATE, dtype=jnp.float32)
    return {
        "x": nrm((BATCH, SEQ, D_MODEL), 1.0),
        "even_mix_norm": 1.0 + nrm((N_EVEN, D_MODEL), 0.02),
        "even_in_proj": nrm((N_EVEN, D_MODEL, EVEN_IN), D_MODEL ** -0.5),
        "s5_lambda_re": -0.5 + nrm((N_EVEN, S5_GROUPS, S5_STATE), 0.01),
        "s5_lambda_im": math.pi * n_idx + nrm((N_EVEN, S5_GROUPS, S5_STATE), 0.01),
        "s5_log_step": jax.random.uniform(next(ks), (N_EVEN, S5_GROUPS), jnp.float32, math.log(1e-3), math.log(1e-1)),
        "s5_b_re": nrm((N_EVEN, S5_GROUPS, S5_STATE, S5_GROUP), (2 * S5_GROUP) ** -0.5),
        "s5_b_im": nrm((N_EVEN, S5_GROUPS, S5_STATE, S5_GROUP), (2 * S5_GROUP) ** -0.5),
        "s5_c_re": nrm((N_EVEN, S5_GROUPS, S5_GROUP, S5_STATE), S5_STATE ** -0.5),
        "s5_c_im": nrm((N_EVEN, S5_GROUPS, S5_GROUP, S5_STATE), S5_STATE ** -0.5),
        "s5_d": nrm((N_EVEN, S5_WIDTH), 0.5),
        "s5_glu_w": nrm((N_EVEN, S5_WIDTH, S5_WIDTH), S5_WIDTH ** -0.5),
        "hgrn_lower_bounds": nrm((DEPTH + 1, HG_WIDTH), 0.1),
        "hgrn_o_norm": 1.0 + nrm((N_EVEN, HG_HEAD_DIM), 0.02),
        "even_out_proj": nrm((N_EVEN, D_MODEL, D_MODEL), D_MODEL ** -0.5),
        "odd_mix_norm": 1.0 + nrm((N_ODD, D_MODEL), 0.02),
        "odd_wqkv": nrm((N_ODD, D_MODEL, Q_WIDTH + 2 * KV_WIDTH), D_MODEL ** -0.5),
        "odd_q_norm": 1.0 + nrm((N_ODD, HEAD_DIM), 0.02),
        "odd_k_norm": 1.0 + nrm((N_ODD, HEAD_DIM), 0.02),
        "odd_sinks": nrm((N_ODD, N_Q_HEADS), 0.5),
        "odd_out_proj": nrm((N_ODD, Q_WIDTH, D_MODEL), Q_WIDTH ** -0.5),
        "moe_norm": 1.0 + nrm((DEPTH, D_MODEL), 0.02),
        "moe_router_group": nrm((DEPTH, D_MODEL, N_GROUPS), D_MODEL ** -0.5),
        "moe_router_group_bias": nrm((DEPTH, N_GROUPS), 0.01),
        "moe_router_expert": nrm((DEPTH, D_MODEL, N_EXPERTS), D_MODEL ** -0.5),
        "moe_router_expert_bias": nrm((DEPTH, N_EXPERTS), 0.01),
        "moe_w_gate": nrm((DEPTH, N_EXPERTS, D_MODEL, D_EXPERT), D_MODEL ** -0.5),
        "moe_w_up": nrm((DEPTH, N_EXPERTS, D_MODEL, D_EXPERT), D_MODEL ** -0.5),
        "moe_w_down": nrm((DEPTH, N_EXPERTS, D_EXPERT, D_MODEL), D_EXPERT ** -0.5),
    }


def reference(x, even_mix_norm, even_in_proj, s5_lambda_re, s5_lambda_im, s5_log_step,
              s5_b_re, s5_b_im, s5_c_re, s5_c_im, s5_d, s5_glu_w, hgrn_lower_bounds,
              hgrn_o_norm, even_out_proj, odd_mix_norm, odd_wqkv, odd_q_norm, odd_k_norm,
              odd_sinks, odd_out_proj, moe_norm, moe_router_group, moe_router_group_bias,
              moe_router_expert, moe_router_expert_bias, moe_w_gate, moe_w_up, moe_w_down):
    lower_bounds = jnp.cumsum(jax.nn.softmax(hgrn_lower_bounds.astype(jnp.float32), axis=0), axis=0)
    for layer in range(DEPTH):
        j = layer // 2
        if layer % 2 == 0:
            h = rms_norm(x, even_mix_norm[j])
            proj = jnp.einsum('bld,de->ble', h, even_in_proj[j])
            u = proj[..., :S5_WIDTH]
            hq, hf, hi, hg = jnp.split(proj[..., S5_WIDTH:], 4, axis=-1)
            a_out = s5_mixer(u, s5_lambda_re[j], s5_lambda_im[j], s5_log_step[j], s5_b_re[j],
                             s5_b_im[j], s5_c_re[j], s5_c_im[j], s5_d[j], s5_glu_w[j])
            b_out = hgrn2_mixer(hq, hf, hi, hg, lower_bounds[layer], hgrn_o_norm[j])
            mix = jnp.einsum('blc,cd->bld', jnp.concatenate([a_out, b_out], axis=-1), even_out_proj[j])
        else:
            h = rms_norm(x, odd_mix_norm[j])
            att = sliding_window_attention(h, odd_wqkv[j], odd_q_norm[j], odd_k_norm[j], odd_sinks[j])
            mix = jnp.einsum('blc,cd->bld', att, odd_out_proj[j])
        x = x + mix.astype(x.dtype)
        h = rms_norm(x, moe_norm[layer])
        ffn = hierarchical_moe(h, moe_router_group[layer], moe_router_group_bias[layer],
                               moe_router_expert[layer], moe_router_expert_bias[layer],
                               moe_w_gate[layer], moe_w_up[layer], moe_w_down[layer])
        x = x + ffn.astype(x.dtype)
    return x
```

```python
import functools
import math

import jax
import jax.numpy as jnp
import numpy as np
from jax import lax
from jax.experimental import pallas as pl
from jax.experimental.pallas import tpu as pltpu

F32 = jnp.float32
BF16 = jnp.bfloat16
EPS = 1e-6

D_MODEL = 1024
S5_WIDTH = 512
S5_GROUP = 16
S5_GROUPS = 32
S5_STATE = 64
S5_CHUNK = 16
HG_WIDTH = 512
HG_HEAD_DIM = 128
HG_HEADS = 4
HG_CHUNK = 32
HEAD_DIM = 64
N_Q_HEADS = 16
N_KV_HEADS = 2
GQA_GROUP = 8
KV_WIDTH = N_KV_HEADS * HEAD_DIM
ATT_BLOCK = 128
N_GROUPS = 4
EXPERTS_PER_GROUP = 4
N_EXPERTS = 16
D_EXPERT = 256
ROUTER_ROWS = 32

VMEM_LIMIT_BYTES = 56 * 1024 * 1024


def _params(*semantics):
    return pltpu.CompilerParams(dimension_semantics=semantics, vmem_limit_bytes=VMEM_LIMIT_BYTES)


def _rms(xf, gain):
    return xf * lax.rsqrt(jnp.mean(xf * xf, axis=-1, keepdims=True) + EPS) * gain


def _sigmoid(x):
    return 1.0 / (1.0 + jnp.exp(-x))


def _silu(x):
    return x * _sigmoid(x)


def _inproj_kernel(x_ref, g_ref, w_ref, u_ref, h4_ref):
    h = _rms(x_ref[...], g_ref[...]).astype(BF16)
    p = jnp.dot(h, w_ref[...], preferred_element_type=F32)
    u_ref[...] = p[:, :S5_WIDTH]
    h4_ref[...] = p[:, S5_WIDTH:]


def _inproj(x2, gain, w_bf16, tm=512):
    n = x2.shape[0]
    e_in = w_bf16.shape[1]
    return pl.pallas_call(
        _inproj_kernel,
        out_shape=(jax.ShapeDtypeStruct((n, S5_WIDTH), F32),
                   jax.ShapeDtypeStruct((n, e_in - S5_WIDTH), F32)),
        grid=(n // tm,),
        in_specs=[pl.BlockSpec((tm, D_MODEL), lambda i: (i, 0)),
                  pl.BlockSpec((1, D_MODEL), lambda i: (0, 0)),
                  pl.BlockSpec((D_MODEL, e_in), lambda i: (0, 0))],
        out_specs=(pl.BlockSpec((tm, S5_WIDTH), lambda i: (i, 0)),
                   pl.BlockSpec((tm, e_in - S5_WIDTH), lambda i: (i, 0))),
        compiler_params=_params("parallel"),
        name="even_inproj",
    )(x2, gain.reshape(1, D_MODEL), w_bf16)


def _s5_lagkernel_kernel(ca_ref, bb_ref, k_ref):
    k_ref[0] = jnp.dot(ca_ref[0], bb_ref[0], preferred_element_type=F32,
                       precision=lax.Precision.HIGHEST)


def _s5_lagkernel(ca, bb):
    g, rows, k = ca.shape
    return pl.pallas_call(
        _s5_lagkernel_kernel,
        out_shape=jax.ShapeDtypeStruct((g, rows, S5_GROUP), F32),
        grid=(g,),
        in_specs=[pl.BlockSpec((1, rows, k), lambda i: (i, 0, 0)),
                  pl.BlockSpec((1, k, S5_GROUP), lambda i: (i, 0, 0))],
        out_specs=pl.BlockSpec((1, rows, S5_GROUP), lambda i: (i, 0, 0)),
        compiler_params=_params("parallel"),
        name="s5_lag_kernel",
    )(ca, bb)


def _s5_operators(lam_re, lam_im, log_step, b_re, b_im, c_re, c_im):
    t = S5_CHUNK
    lr, li = lam_re.astype(F32), lam_im.astype(F32)
    step = jnp.exp(log_step.astype(F32))[:, None]
    mag = jnp.exp(lr * step)
    ab_re = mag * jnp.cos(li * step)
    ab_im = mag * jnp.sin(li * step)
    den = lr * lr + li * li
    nr, ni = ab_re - 1.0, ab_im
    z_re = (nr * lr + ni * li) / den
    z_im = (ni * lr - nr * li) / den
    br, bi = b_re.astype(F32), b_im.astype(F32)
    bb_re = z_re[..., None] * br - z_im[..., None] * bi
    bb_im = z_re[..., None] * bi + z_im[..., None] * br
    kk = jnp.arange(t + 1, dtype=F32)[:, None, None]
    pmag = jnp.exp(kk * (lr * step)[None])
    pw_re = pmag * jnp.cos(kk * (li * step)[None])
    pw_im = pmag * jnp.sin(kk * (li * step)[None])
    cr = jnp.transpose(c_re.astype(F32), (0, 1, 2))
    ci = c_im.astype(F32)
    ca_re = cr[None] * pw_re[:, :, None, :] - ci[None] * pw_im[:, :, None, :]
    ca_im = cr[None] * pw_im[:, :, None, :] + ci[None] * pw_re[:, :, None, :]
    g = lr.shape[0]
    ca_cat = jnp.concatenate([ca_re[:t], -ca_im[:t]], axis=-1)
    ca_cat = jnp.transpose(ca_cat, (1, 0, 2, 3)).reshape(g, t * S5_GROUP, 2 * S5_STATE)
    bb_cat = jnp.concatenate([bb_re, bb_im], axis=1)
    kern = _s5_lagkernel(ca_cat, bb_cat).reshape(g, t, S5_GROUP, S5_GROUP)
    s_idx = jnp.arange(t)[:, None]
    t_idx = jnp.arange(t)[None, :]
    lag = t_idx - s_idx
    kg = kern[:, jnp.clip(lag, 0, t - 1)]
    kg = jnp.where((lag >= 0)[None, :, :, None, None], kg, 0.0)
    mt = jnp.transpose(kg, (0, 1, 4, 2, 3)).reshape(g, t * S5_GROUP, t * S5_GROUP)
    pr = pw_re[:t][::-1]
    pi = pw_im[:t][::-1]
    sb_re = pr[:, :, :, None] * bb_re[None] - pi[:, :, :, None] * bb_im[None]
    sb_im = pr[:, :, :, None] * bb_im[None] + pi[:, :, :, None] * bb_re[None]
    sb_re = jnp.transpose(sb_re, (1, 0, 3, 2)).reshape(g, t * S5_GROUP, S5_STATE)
    sb_im = jnp.transpose(sb_im, (1, 0, 3, 2)).reshape(g, t * S5_GROUP, S5_STATE)
    cp_re = jnp.transpose(ca_re[1:], (1, 3, 0, 2)).reshape(g, S5_STATE, t * S5_GROUP)
    cp_im = jnp.transpose(-ca_im[1:], (1, 3, 0, 2)).reshape(g, S5_STATE, t * S5_GROUP)

    def pair_rows(m):
        m = m.reshape(g // 2, 2, m.shape[1], m.shape[2])
        z = jnp.zeros_like(m[:, 0])
        top = jnp.concatenate([m[:, 0], z], axis=2)
        bot = jnp.concatenate([z, m[:, 1]], axis=2)
        return jnp.concatenate([top, bot], axis=1)

    at_re = pw_re[t].reshape(g // 2, 1, 2 * S5_STATE)
    at_im = pw_im[t].reshape(g // 2, 1, 2 * S5_STATE)
    return (mt.astype(BF16), pair_rows(sb_re).astype(BF16), pair_rows(sb_im).astype(BF16),
            pair_rows(cp_re).astype(BF16), pair_rows(cp_im).astype(BF16), at_re, at_im)


def _s5_kernel(u_ref, mt_ref, wre_ref, wim_ref, cre_ref, cim_ref, atr_ref, ati_ref, y_ref,
               sre_ref, sim_ref, xre_ref, xim_ref, *, n_chunks, bsz):
    u0 = u_ref[0]
    u1 = u_ref[1]
    ucat = jnp.concatenate([u0, u1], axis=1)
    sre_ref[...] = jnp.dot(ucat, wre_ref[0], preferred_element_type=F32)
    sim_ref[...] = jnp.dot(ucat, wim_ref[0], preferred_element_type=F32)
    atr = jnp.broadcast_to(atr_ref[0], (bsz, 2 * S5_STATE))
    ati = jnp.broadcast_to(ati_ref[0], (bsz, 2 * S5_STATE))

    def body(c, carry):
        xr, xi = carry
        rows = pl.ds(pl.multiple_of(c * bsz, bsz), bsz)
        xre_ref[rows, :] = xr
        xim_ref[rows, :] = xi
        nxr = atr * xr - ati * xi + sre_ref[rows, :]
        nxi = atr * xi + ati * xr + sim_ref[rows, :]
        return nxr, nxi

    zero = jnp.zeros((bsz, 2 * S5_STATE), F32)
    lax.fori_loop(0, n_chunks, body, (zero, zero))
    ycar = (jnp.dot(xre_ref[...].astype(BF16), cre_ref[0], preferred_element_type=F32)
            + jnp.dot(xim_ref[...].astype(BF16), cim_ref[0], preferred_element_type=F32))
    w = S5_CHUNK * S5_GROUP
    y_ref[0] = jnp.dot(u0, mt_ref[0], preferred_element_type=F32) + ycar[:, :w]
    y_ref[1] = jnp.dot(u1, mt_ref[1], preferred_element_type=F32) + ycar[:, w:]


def _s5_scan(u_g, ops, bsz):
    mt, wre, wim, cre, cim, atr, ati = ops
    g, r, w = u_g.shape
    n_chunks = r // bsz
    p2 = 2 * S5_STATE
    kern = functools.partial(_s5_kernel, n_chunks=n_chunks, bsz=bsz)
    return pl.pallas_call(
        kern,
        out_shape=jax.ShapeDtypeStruct((g, r, w), F32),
        grid=(g // 2,),
        in_specs=[pl.BlockSpec((2, r, w), lambda i: (i, 0, 0)),
                  pl.BlockSpec((2, w, w), lambda i: (i, 0, 0)),
                  pl.BlockSpec((1, 2 * w, p2), lambda i: (i, 0, 0)),
                  pl.BlockSpec((1, 2 * w, p2), lambda i: (i, 0, 0)),
                  pl.BlockSpec((1, p2, 2 * w), lambda i: (i, 0, 0)),
                  pl.BlockSpec((1, p2, 2 * w), lambda i: (i, 0, 0)),
                  pl.BlockSpec((1, 1, p2), lambda i: (i, 0, 0)),
                  pl.BlockSpec((1, 1, p2), lambda i: (i, 0, 0))],
        out_specs=pl.BlockSpec((2, r, w), lambda i: (i, 0, 0)),
        scratch_shapes=[pltpu.VMEM((r, p2), F32)] * 4,
        compiler_params=_params("parallel"),
        name="s5_scan",
    )(u_g, mt, wre, wim, cre, cim, atr, ati)


def _hgrn_kernel(q_ref, f_ref, i_ref, g_ref, lb_ref, og_ref, o_ref, st_ref, *, seqlen):
    c = HG_CHUNK
    nc = seqlen // c
    d = HG_HEAD_DIM
    lb = lb_ref[...]
    q = q_ref[0]
    qs = _silu(q)
    f = lb + (1.0 - lb) * _sigmoid(f_ref[0])
    lf = jnp.log(f)
    k = 1.0 - f
    v = i_ref[0]
    pos = lax.broadcasted_iota(jnp.int32, (seqlen, d), 0) % c
    b = lf
    sh = 1
    while sh < c:
        b = b + jnp.where(pos >= sh, pltpu.roll(b, sh, axis=0), 0.0)
        sh *= 2
    b3 = b.reshape(nc, c, d)
    b_last = b3[:, c - 1:c, :]
    b_ref = b3[:, c // 2 - 1:c // 2, :]
    qs3 = qs.reshape(nc, c, d)
    k3 = k.reshape(nc, c, d)
    v3 = v.reshape(nc, c, d).astype(BF16)
    qe = (qs3 * jnp.exp(b3 - b_ref)).astype(BF16)
    ke = (k3 * jnp.exp(b_ref - b3)).astype(BF16)
    kd = (k3 * jnp.exp(b_last - b3)).astype(BF16)
    qb = (qs3 * jnp.exp(b3)).astype(BF16)
    scores = jnp.einsum('ctd,csd->cts', qe, ke, preferred_element_type=F32)
    ti = lax.broadcasted_iota(jnp.int32, (c, c), 0)
    si = lax.broadcasted_iota(jnp.int32, (c, c), 1)
    scores = jnp.where((ti >= si)[None], scores, 0.0)
    o_intra = jnp.einsum('cts,csv->ctv', scores.astype(BF16), v3, preferred_element_type=F32)
    ut = jnp.einsum('csv,csd->cvd', v3, kd, preferred_element_type=F32)
    decay = jnp.exp(b_last)
    state = jnp.zeros((d, d), F32)
    for ci in range(nc):
        st_ref[ci] = state.astype(BF16)
        state = decay[ci] * state + ut[ci]
    o_inter = jnp.einsum('ctd,cvd->ctv', qb, st_ref[...], preferred_element_type=F32)
    o = (o_intra + o_inter).reshape(seqlen, d)
    o = _rms(o, og_ref[...])
    o_ref[0] = o * _silu(g_ref[0])


def _hgrn(h4, lower_bound, o_gain, bsz, seqlen):
    d = HG_HEAD_DIM
    kern = functools.partial(_hgrn_kernel, seqlen=seqlen)

    def col(part):
        return pl.BlockSpec((1, seqlen, d), lambda b, h: (b, 0, part * HG_HEADS + h))

    return pl.pallas_call(
        kern,
        out_shape=jax.ShapeDtypeStruct((bsz, seqlen, HG_WIDTH), F32),
        grid=(bsz, HG_HEADS),
        in_specs=[col(0), col(1), col(2), col(3),
                  pl.BlockSpec((1, d), lambda b, h: (0, h)),
                  pl.BlockSpec((1, d), lambda b, h: (0, 0))],
        out_specs=pl.BlockSpec((1, seqlen, d), lambda b, h: (b, 0, h)),
        scratch_shapes=[pltpu.VMEM((seqlen // HG_CHUNK, d, d), BF16)],
        compiler_params=_params("parallel", "parallel"),
        name="hgrn2",
    )(h4, h4, h4, h4, lower_bound.reshape(1, HG_WIDTH), o_gain.reshape(1, d))


def _evenout_kernel(x_ref, ys_ref, u_ref, b_ref, d_ref, wglu_ref, wa_ref, wb_ref, o_ref):
    y = ys_ref[...] + d_ref[...] * u_ref[...]
    y = jax.nn.gelu(y)
    gate = _sigmoid(jnp.dot(y.astype(BF16), wglu_ref[...], preferred_element_type=F32))
    a = (y * gate).astype(BF16)
    mix = (jnp.dot(a, wa_ref[...], preferred_element_type=F32)
           + jnp.dot(b_ref[...].astype(BF16), wb_ref[...], preferred_element_type=F32))
    o_ref[...] = x_ref[...] + mix


def _evenout(x2, ys, u, b_out, d_skip, wglu, wout, tm=512):
    n = x2.shape[0]
    row = lambda w: pl.BlockSpec((tm, w), lambda i: (i, 0))
    full = lambda r, c: pl.BlockSpec((r, c), lambda i: (0, 0))
    return pl.pallas_call(
        _evenout_kernel,
        out_shape=jax.ShapeDtypeStruct((n, D_MODEL), F32),
        grid=(n // tm,),
        in_specs=[row(D_MODEL), row(S5_WIDTH), row(S5_WIDTH), row(HG_WIDTH),
                  full(1, S5_WIDTH), full(S5_WIDTH, S5_WIDTH),
                  full(S5_WIDTH, D_MODEL), full(HG_WIDTH, D_MODEL)],
        out_specs=row(D_MODEL),
        compiler_params=_params("parallel"),
        name="even_out",
    )(x2, ys, u, b_out, d_skip.reshape(1, S5_WIDTH), wglu.astype(BF16),
      wout[:S5_WIDTH].astype(BF16), wout[S5_WIDTH:].astype(BF16))


def _router_kernel(x_ref, g_ref, wr_ref, br_ref, hn_ref, comb_ref):
    h = _rms(x_ref[...], g_ref[...])
    hn_ref[...] = h.astype(BF16)
    lt = lax.dot_general(wr_ref[...], h, (((1,), (1,)), ((), ())),
                         preferred_element_type=F32, precision=lax.Precision.HIGHEST)
    lt = lt + br_ref[...]
    gl = [lt[i:i + 1] for i in range(N_GROUPS)]
    el = [lt[N_GROUPS + i:N_GROUPS + i + 1] for i in range(N_EXPERTS)]
    gmax = jnp.maximum(jnp.maximum(gl[0], gl[1]), jnp.maximum(gl[2], gl[3]))
    gexp = [jnp.exp(v - gmax) for v in gl]
    gsum = gexp[0] + gexp[1] + gexp[2] + gexp[3]
    gprob = [v / gsum for v in gexp]
    g_gate = jnp.maximum(jnp.maximum(gprob[0], gprob[1]), jnp.maximum(gprob[2], gprob[3]))
    g_idx = jnp.where(gprob[0] == g_gate, 0,
                      jnp.where(gprob[1] == g_gate, 1, jnp.where(gprob[2] == g_gate, 2, 3)))
    es = []
    for j in range(EXPERTS_PER_GROUP):
        es.append(jnp.where(g_idx == 0, el[j],
                            jnp.where(g_idx == 1, el[4 + j],
                                      jnp.where(g_idx == 2, el[8 + j], el[12 + j]))))
    emax = jnp.maximum(jnp.maximum(es[0], es[1]), jnp.maximum(es[2], es[3]))
    eexp = [jnp.exp(v - emax) for v in es]
    esum = eexp[0] + eexp[1] + eexp[2] + eexp[3]
    ep = [v / esum for v in eexp]
    p1 = jnp.maximum(jnp.maximum(ep[0], ep[1]), jnp.maximum(ep[2], ep[3]))
    i1 = jnp.where(ep[0] == p1, 0, jnp.where(ep[1] == p1, 1, jnp.where(ep[2] == p1, 2, 3)))
    neg = jnp.float32(-1.0)
    rest = [jnp.where(i1 == j, neg, ep[j]) for j in range(EXPERTS_PER_GROUP)]
    p2 = jnp.maximum(jnp.maximum(rest[0], rest[1]), jnp.maximum(rest[2], rest[3]))
    i2 = jnp.where(rest[0] == p2, 0, jnp.where(rest[1] == p2, 1, jnp.where(rest[2] == p2, 2, 3)))
    wsum = p1 + p2
    w1 = g_gate * (p1 / wsum)
    w2 = g_gate * (p2 / wsum)
    e1 = g_idx * EXPERTS_PER_GROUP + i1
    e2 = g_idx * EXPERTS_PER_GROUP + i2
    rows = [jnp.where(e1 == e, w1, 0.0) + jnp.where(e2 == e, w2, 0.0) for e in range(N_EXPERTS)]
    comb_ref[...] = jnp.concatenate(rows, axis=0)


def _router(x2, gain, w_rg, b_rg, w_re, b_re, tm=512):
    n = x2.shape[0]
    wr = jnp.concatenate([w_rg, w_re], axis=1).astype(F32).T
    wr = jnp.pad(wr, ((0, ROUTER_ROWS - wr.shape[0]), (0, 0)))
    br = jnp.pad(jnp.concatenate([b_rg, b_re]).astype(F32), (0, ROUTER_ROWS - N_GROUPS - N_EXPERTS))
    return pl.pallas_call(
        _router_kernel,
        out_shape=(jax.ShapeDtypeStruct((n, D_MODEL), BF16),
                   jax.ShapeDtypeStruct((N_EXPERTS, n), F32)),
        grid=(n // tm,),
        in_specs=[pl.BlockSpec((tm, D_MODEL), lambda i: (i, 0)),
                  pl.BlockSpec((1, D_MODEL), lambda i: (0, 0)),
                  pl.BlockSpec((ROUTER_ROWS, D_MODEL), lambda i: (0, 0)),
                  pl.BlockSpec((ROUTER_ROWS, 1), lambda i: (0, 0))],
        out_specs=(pl.BlockSpec((tm, D_MODEL), lambda i: (i, 0)),
                   pl.BlockSpec((N_EXPERTS, tm), lambda i: (0, i))),
        compiler_params=_params("parallel"),
        name="moe_router",
    )(x2, gain.reshape(1, D_MODEL), wr, br.reshape(ROUTER_ROWS, 1))


def _experts_kernel(hn_ref, comb_ref, x_ref, wg_ref, wu_ref, wd_ref, o_ref, acc_ref):
    e = pl.program_id(1)

    @pl.when(e == 0)
    def _():
        acc_ref[...] = x_ref[...]

    h = hn_ref[...]
    gate = jnp.dot(h, wg_ref[0], preferred_element_type=F32)
    up = jnp.dot(h, wu_ref[0], preferred_element_type=F32)
    lane = lax.broadcasted_iota(jnp.int32, comb_ref.shape, 1)
    w = jnp.sum(jnp.where(lane == e, comb_ref[...], 0.0), axis=1, keepdims=True)
    hid = (_silu(gate) * up * w).astype(BF16)
    acc_ref[...] += jnp.dot(hid, wd_ref[0], preferred_element_type=F32)

    @pl.when(e == N_EXPERTS - 1)
    def _():
        o_ref[...] = acc_ref[...]


def _experts(hn, comb, x2, wg, wu, wd, tm=1024):
    n = hn.shape[0]
    return pl.pallas_call(
        _experts_kernel,
        out_shape=jax.ShapeDtypeStruct((n, D_MODEL), F32),
        grid=(n // tm, N_EXPERTS),
        in_specs=[pl.BlockSpec((tm, D_MODEL), lambda i, e: (i, 0)),
                  pl.BlockSpec((tm, N_EXPERTS), lambda i, e: (i, 0)),
                  pl.BlockSpec((tm, D_MODEL), lambda i, e: (i, 0)),
                  pl.BlockSpec((1, D_MODEL, D_EXPERT), lambda i, e: (e, 0, 0)),
                  pl.BlockSpec((1, D_MODEL, D_EXPERT), lambda i, e: (e, 0, 0)),
                  pl.BlockSpec((1, D_EXPERT, D_MODEL), lambda i, e: (e, 0, 0))],
        out_specs=pl.BlockSpec((tm, D_MODEL), lambda i, e: (i, 0)),
        scratch_shapes=[pltpu.VMEM((tm, D_MODEL), F32)],
        compiler_params=_params("parallel", "arbitrary"),
        name="moe_experts",
    )(hn, comb, x2, wg, wu, wd)


def _moe(x2, gain, w_rg, b_rg, w_re, b_re, wg, wu, wd):
    hn, comb_t = _router(x2, gain, w_rg, b_rg, w_re, b_re)
    return _experts(hn, comb_t.T, x2, wg.astype(BF16), wu.astype(BF16), wd.astype(BF16))


def _qkv_kernel(x_ref, g_ref, wq_ref, wkt_ref, wv_ref, seg_ref, kg_ref, q_ref, ss_ref, kt_ref, v_ref):
    h = _rms(x_ref[...], g_ref[...]).astype(BF16)
    q = jnp.dot(h, wq_ref[...], preferred_element_type=F32)
    q_ref[...] = q.astype(BF16)
    ss_ref[...] = jnp.dot((q * q).astype(BF16), seg_ref[...], preferred_element_type=F32)
    kt = lax.dot_general(wkt_ref[...], h, (((1,), (1,)), ((), ())), preferred_element_type=F32)
    tm = kt.shape[1]
    k3 = kt.reshape(N_KV_HEADS, HEAD_DIM, tm)
    ms = jnp.mean(k3 * k3, axis=1, keepdims=True)
    kn = k3 * lax.rsqrt(ms + EPS) * kg_ref[...].reshape(N_KV_HEADS, HEAD_DIM, 1)
    kt_ref[...] = kn.reshape(KV_WIDTH, tm).astype(BF16)
    v_ref[...] = jnp.dot(h, wv_ref[...], preferred_element_type=F32).astype(BF16)


def _qkv(x2, gain, wqkv, q_gain, k_gain, tm=512):
    n = x2.shape[0]
    qw = N_Q_HEADS * HEAD_DIM
    wq = wqkv[:, :qw].astype(BF16)
    wkt = wqkv[:, qw:qw + KV_WIDTH].T.astype(BF16)
    wv = wqkv[:, qw + KV_WIDTH:].astype(BF16)
    seg = (np.arange(qw)[:, None] // HEAD_DIM == np.arange(128)[None, :]).astype(np.float32)
    kg = jnp.tile((k_gain.astype(F32) * q_gain.astype(F32)), N_KV_HEADS).reshape(KV_WIDTH, 1)
    full = lambda r, c: pl.BlockSpec((r, c), lambda i: (0, 0))
    return pl.pallas_call(
        _qkv_kernel,
        out_shape=(jax.ShapeDtypeStruct((n, qw), BF16),
                   jax.ShapeDtypeStruct((n, 128), F32),
                   jax.ShapeDtypeStruct((KV_WIDTH, n), BF16),
                   jax.ShapeDtypeStruct((n, KV_WIDTH), BF16)),
        grid=(n // tm,),
        in_specs=[pl.BlockSpec((tm, D_MODEL), lambda i: (i, 0)), full(1, D_MODEL),
                  full(D_MODEL, qw), full(KV_WIDTH, D_MODEL), full(D_MODEL, KV_WIDTH),
                  full(qw, 128), full(KV_WIDTH, 1)],
        out_specs=(pl.BlockSpec((tm, qw), lambda i: (i, 0)),
                   pl.BlockSpec((tm, 128), lambda i: (i, 0)),
                   pl.BlockSpec((KV_WIDTH, tm), lambda i: (0, i)),
                   pl.BlockSpec((tm, KV_WIDTH), lambda i: (i, 0))),
        compiler_params=_params("parallel"),
        name="odd_qkv",
    )(x2, gain.reshape(1, D_MODEL), wq, wkt, wv, jnp.asarray(seg, dtype=BF16), kg)


def _attn_kernel(sink_ref, q_ref, ss_ref, ktp_ref, ktc_ref, vp_ref, vc_ref, x_ref, wo_ref, o_ref, att_ref):
    n = pl.program_id(1)
    blk = ATT_BLOCK
    kw = jnp.concatenate([ktp_ref[...], ktc_ref[...]], axis=1)
    vw = jnp.concatenate([vp_ref[...], vc_ref[...]], axis=0)
    qi = lax.broadcasted_iota(jnp.int32, (blk, 2 * blk), 0)
    ki = lax.broadcasted_iota(jnp.int32, (blk, 2 * blk), 1)
    dist = qi - ki + blk
    valid = (dist >= 0) & (dist < blk) & (n * blk + ki - blk >= 0)
    distf = dist.astype(F32)
    ss = ss_ref[...]
    for h in range(N_Q_HEADS):
        hk = h // GQA_GROUP
        slope = float(2.0 ** (-8.0 * (h + 1) / N_Q_HEADS))
        sink = sink_ref[h]
        rs = lax.rsqrt(ss[:, h:h + 1] * (1.0 / HEAD_DIM) + EPS) * (HEAD_DIM ** -0.5)
        qh = (q_ref[:, h * HEAD_DIM:(h + 1) * HEAD_DIM].astype(F32) * rs).astype(BF16)
        s = jnp.dot(qh, kw[hk * HEAD_DIM:(hk + 1) * HEAD_DIM, :], preferred_element_type=F32)
        s = jnp.where(valid, s - slope * distf, -jnp.inf)
        m = jnp.maximum(jnp.max(s, axis=-1, keepdims=True), sink)
        p = jnp.exp(s - m)
        den = jnp.sum(p, axis=-1, keepdims=True) + jnp.exp(sink - m)
        oh = jnp.dot(p.astype(BF16), vw[:, hk * HEAD_DIM:(hk + 1) * HEAD_DIM], preferred_element_type=F32)
        att_ref[:, h * HEAD_DIM:(h + 1) * HEAD_DIM] = (oh / den).astype(BF16)
    o_ref[...] = x_ref[...] + jnp.dot(att_ref[...], wo_ref[...], preferred_element_type=F32)


def _attn(q, ss, kt, v, x2, sinks, wo, bsz, seqlen):
    blk = ATT_BLOCK
    nb = seqlen // blk
    qw = N_Q_HEADS * HEAD_DIM
    cur = lambda b, n, s: (b * nb + n, 0)
    prev = lambda b, n, s: (b * nb + jnp.maximum(n - 1, 0), 0)
    cur_t = lambda b, n, s: (0, b * nb + n)
    prev_t = lambda b, n, s: (0, b * nb + jnp.maximum(n - 1, 0))
    grid_spec = pltpu.PrefetchScalarGridSpec(
        num_scalar_prefetch=1,
        grid=(bsz, nb),
        in_specs=[pl.BlockSpec((blk, qw), cur),
                  pl.BlockSpec((blk, 128), cur),
                  pl.BlockSpec((KV_WIDTH, blk), prev_t),
                  pl.BlockSpec((KV_WIDTH, blk), cur_t),
                  pl.BlockSpec((blk, KV_WIDTH), prev),
                  pl.BlockSpec((blk, KV_WIDTH), cur),
                  pl.BlockSpec((blk, D_MODEL), cur),
                  pl.BlockSpec((qw, D_MODEL), lambda b, n, s: (0, 0))],
        out_specs=pl.BlockSpec((blk, D_MODEL), cur),
        scratch_shapes=[pltpu.VMEM((blk, qw), BF16)],
    )
    return pl.pallas_call(
        _attn_kernel,
        out_shape=jax.ShapeDtypeStruct((bsz * seqlen, D_MODEL), F32),
        grid_spec=grid_spec,
        compiler_params=_params("parallel", "parallel"),
        name="odd_attn",
    )(sinks.astype(F32), q, ss, kt, kt, v, v, x2, wo.astype(BF16))


def kernel(x, even_mix_norm, even_in_proj, s5_lambda_re, s5_lambda_im, s5_log_step, s5_b_re, s5_b_im,
           s5_c_re, s5_c_im, s5_d, s5_glu_w, hgrn_lower_bounds, hgrn_o_norm, even_out_proj, odd_mix_norm,
           odd_wqkv, odd_q_norm, odd_k_norm, odd_sinks, odd_out_proj, moe_norm, moe_router_group,
           moe_router_group_bias, moe_router_expert, moe_router_expert_bias, moe_w_gate, moe_w_up,
           moe_w_down):
    bsz, seqlen, dm = x.shape
    n = bsz * seqlen
    x2 = x.reshape(n, dm)
    lower_bounds = jnp.cumsum(jax.nn.softmax(hgrn_lower_bounds.astype(F32), axis=0), axis=0)

    def moe(xx, layer):
        return _moe(xx, moe_norm[layer], moe_router_group[layer], moe_router_group_bias[layer],
                    moe_router_expert[layer], moe_router_expert_bias[layer],
                    moe_w_gate[layer], moe_w_up[layer], moe_w_down[layer])

    u, h4 = _inproj(x2, even_mix_norm[0], even_in_proj[0].astype(BF16))
    ops = _s5_operators(s5_lambda_re[0], s5_lambda_im[0], s5_log_step[0], s5_b_re[0], s5_b_im[0],
                        s5_c_re[0], s5_c_im[0])
    nch = seqlen // S5_CHUNK
    u_g = u.astype(BF16).reshape(bsz, nch, S5_CHUNK, S5_GROUPS, S5_GROUP)
    u_g = jnp.transpose(u_g, (3, 1, 0, 2, 4)).reshape(S5_GROUPS, nch * bsz, S5_CHUNK * S5_GROUP)
    y_g = _s5_scan(u_g, ops, bsz)
    ys = jnp.transpose(y_g.reshape(S5_GROUPS, nch, bsz, S5_CHUNK, S5_GROUP), (2, 1, 3, 0, 4))
    ys = ys.reshape(n, S5_WIDTH)
    b_out = _hgrn(h4.reshape(bsz, seqlen, 4 * HG_WIDTH), lower_bounds[0], hgrn_o_norm[0], bsz, seqlen)
    x2 = _evenout(x2, ys, u, b_out.reshape(n, HG_WIDTH), s5_d[0], s5_glu_w[0], even_out_proj[0])
    x2 = moe(x2, 0)

    q, ss, kt, v = _qkv(x2, odd_mix_norm[0], odd_wqkv[0], odd_q_norm[0], odd_k_norm[0])
    x2 = _attn(q, ss, kt, v, x2, odd_sinks[0], odd_out_proj[0], bsz, seqlen)
    x2 = moe(x2, 1)
    return x2.reshape(bsz, seqlen, dm)
```

```python
import functools
import math

import jax
import jax.numpy as jnp
import numpy as np
from jax import lax
from jax.experimental import pallas as pl
from jax.experimental.pallas import tpu as pltpu

F32 = jnp.float32
BF16 = jnp.bfloat16
EPS = 1e-6

D_MODEL = 1024
S5_WIDTH = 512
S5_GROUP = 16
S5_GROUPS = 32
S5_STATE = 64
S5_CHUNK = 16
HG_WIDTH = 512
HG_HEAD_DIM = 128
HG_HEADS = 4
HG_CHUNK = 32
HEAD_DIM = 64
N_Q_HEADS = 16
N_KV_HEADS = 2
GQA_GROUP = 8
KV_WIDTH = N_KV_HEADS * HEAD_DIM
ATT_BLOCK = 128
N_GROUPS = 4
EXPERTS_PER_GROUP = 4
N_EXPERTS = 16
D_EXPERT = 256
ROUTER_ROWS = 32
N_PAIRS = 6
N_BUCKETS = N_GROUPS * N_PAIRS
BUCKET_ROWS = 32
MOE_TILE = 256
XS_EXTRA = 128
XS_WIDTH = D_MODEL + XS_EXTRA

VMEM_LIMIT_BYTES = 56 * 1024 * 1024


def _params(*semantics):
    return pltpu.CompilerParams(dimension_semantics=semantics, vmem_limit_bytes=VMEM_LIMIT_BYTES)


def _rms(xf, gain):
    return xf * lax.rsqrt(jnp.mean(xf * xf, axis=-1, keepdims=True) + EPS) * gain


def _sigmoid(x):
    return 1.0 / (1.0 + jnp.exp(-x))


def _silu(x):
    return x * _sigmoid(x)


def _inproj_kernel(x_ref, g_ref, w_ref, u_ref, h4_ref):
    h = _rms(x_ref[...], g_ref[...]).astype(BF16)
    p = jnp.dot(h, w_ref[...], preferred_element_type=F32)
    u_ref[...] = p[:, :S5_WIDTH]
    h4_ref[...] = p[:, S5_WIDTH:]


def _inproj(x2, gain, w_bf16, tm=512):
    n = x2.shape[0]
    e_in = w_bf16.shape[1]
    return pl.pallas_call(
        _inproj_kernel,
        out_shape=(jax.ShapeDtypeStruct((n, S5_WIDTH), F32),
                   jax.ShapeDtypeStruct((n, e_in - S5_WIDTH), F32)),
        grid=(n // tm,),
        in_specs=[pl.BlockSpec((tm, D_MODEL), lambda i: (i, 0)),
                  pl.BlockSpec((1, D_MODEL), lambda i: (0, 0)),
                  pl.BlockSpec((D_MODEL, e_in), lambda i: (0, 0))],
        out_specs=(pl.BlockSpec((tm, S5_WIDTH), lambda i: (i, 0)),
                   pl.BlockSpec((tm, e_in - S5_WIDTH), lambda i: (i, 0))),
        compiler_params=_params("parallel"),
        name="even_inproj",
    )(x2, gain.reshape(1, D_MODEL), w_bf16)


def _s5_lagkernel_kernel(ca_ref, bb_ref, k_ref):
    k_ref[0] = jnp.dot(ca_ref[0], bb_ref[0], preferred_element_type=F32,
                       precision=lax.Precision.HIGHEST)


def _s5_lagkernel(ca, bb):
    g, rows, k = ca.shape
    return pl.pallas_call(
        _s5_lagkernel_kernel,
        out_shape=jax.ShapeDtypeStruct((g, rows, S5_GROUP), F32),
        grid=(g,),
        in_specs=[pl.BlockSpec((1, rows, k), lambda i: (i, 0, 0)),
                  pl.BlockSpec((1, k, S5_GROUP), lambda i: (i, 0, 0))],
        out_specs=pl.BlockSpec((1, rows, S5_GROUP), lambda i: (i, 0, 0)),
        compiler_params=_params("parallel"),
        name="s5_lag_kernel",
    )(ca, bb)


def _s5_operators(lam_re, lam_im, log_step, b_re, b_im, c_re, c_im):
    t = S5_CHUNK
    lr, li = lam_re.astype(F32), lam_im.astype(F32)
    step = jnp.exp(log_step.astype(F32))[:, None]
    mag = jnp.exp(lr * step)
    ab_re = mag * jnp.cos(li * step)
    ab_im = mag * jnp.sin(li * step)
    den = lr * lr + li * li
    nr, ni = ab_re - 1.0, ab_im
    z_re = (nr * lr + ni * li) / den
    z_im = (ni * lr - nr * li) / den
    br, bi = b_re.astype(F32), b_im.astype(F32)
    bb_re = z_re[..., None] * br - z_im[..., None] * bi
    bb_im = z_re[..., None] * bi + z_im[..., None] * br
    kk = jnp.arange(t + 1, dtype=F32)[:, None, None]
    pmag = jnp.exp(kk * (lr * step)[None])
    pw_re = pmag * jnp.cos(kk * (li * step)[None])
    pw_im = pmag * jnp.sin(kk * (li * step)[None])
    cr = jnp.transpose(c_re.astype(F32), (0, 1, 2))
    ci = c_im.astype(F32)
    ca_re = cr[None] * pw_re[:, :, None, :] - ci[None] * pw_im[:, :, None, :]
    ca_im = cr[None] * pw_im[:, :, None, :] + ci[None] * pw_re[:, :, None, :]
    g = lr.shape[0]
    ca_cat = jnp.concatenate([ca_re[:t], -ca_im[:t]], axis=-1)
    ca_cat = jnp.transpose(ca_cat, (1, 0, 2, 3)).reshape(g, t * S5_GROUP, 2 * S5_STATE)
    bb_cat = jnp.concatenate([bb_re, bb_im], axis=1)
    kern = _s5_lagkernel(ca_cat, bb_cat).reshape(g, t, S5_GROUP, S5_GROUP)
    s_idx = jnp.arange(t)[:, None]
    t_idx = jnp.arange(t)[None, :]
    lag = t_idx - s_idx
    kg = kern[:, jnp.clip(lag, 0, t - 1)]
    kg = jnp.where((lag >= 0)[None, :, :, None, None], kg, 0.0)
    mt = jnp.transpose(kg, (0, 1, 4, 2, 3)).reshape(g, t * S5_GROUP, t * S5_GROUP)
    pr = pw_re[:t][::-1]
    pi = pw_im[:t][::-1]
    sb_re = pr[:, :, :, None] * bb_re[None] - pi[:, :, :, None] * bb_im[None]
    sb_im = pr[:, :, :, None] * bb_im[None] + pi[:, :, :, None] * bb_re[None]
    sb_re = jnp.transpose(sb_re, (1, 0, 3, 2)).reshape(g, t * S5_GROUP, S5_STATE)
    sb_im = jnp.transpose(sb_im, (1, 0, 3, 2)).reshape(g, t * S5_GROUP, S5_STATE)
    cp_re = jnp.transpose(ca_re[1:], (1, 3, 0, 2)).reshape(g, S5_STATE, t * S5_GROUP)
    cp_im = jnp.transpose(-ca_im[1:], (1, 3, 0, 2)).reshape(g, S5_STATE, t * S5_GROUP)

    def pair_rows(m):
        m = m.reshape(g // 2, 2, m.shape[1], m.shape[2])
        z = jnp.zeros_like(m[:, 0])
        top = jnp.concatenate([m[:, 0], z], axis=2)
        bot = jnp.concatenate([z, m[:, 1]], axis=2)
        return jnp.concatenate([top, bot], axis=1)

    at_re = pw_re[t].reshape(g // 2, 1, 2 * S5_STATE)
    at_im = pw_im[t].reshape(g // 2, 1, 2 * S5_STATE)
    return (mt.astype(BF16), pair_rows(sb_re).astype(BF16), pair_rows(sb_im).astype(BF16),
            pair_rows(cp_re).astype(BF16), pair_rows(cp_im).astype(BF16), at_re, at_im)


def _s5_kernel(u_ref, mt_ref, wre_ref, wim_ref, cre_ref, cim_ref, atr_ref, ati_ref, y_ref,
               sre_ref, sim_ref, xre_ref, xim_ref, *, n_chunks, bsz):
    u0 = u_ref[0]
    u1 = u_ref[1]
    ucat = jnp.concatenate([u0, u1], axis=1)
    sre_ref[...] = jnp.dot(ucat, wre_ref[0], preferred_element_type=F32)
    sim_ref[...] = jnp.dot(ucat, wim_ref[0], preferred_element_type=F32)
    atr = jnp.broadcast_to(atr_ref[0], (bsz, 2 * S5_STATE))
    ati = jnp.broadcast_to(ati_ref[0], (bsz, 2 * S5_STATE))

    def body(c, carry):
        xr, xi = carry
        rows = pl.ds(pl.multiple_of(c * bsz, bsz), bsz)
        xre_ref[rows, :] = xr
        xim_ref[rows, :] = xi
        nxr = atr * xr - ati * xi + sre_ref[rows, :]
        nxi = atr * xi + ati * xr + sim_ref[rows, :]
        return nxr, nxi

    zero = jnp.zeros((bsz, 2 * S5_STATE), F32)
    lax.fori_loop(0, n_chunks, body, (zero, zero))
    ycar = (jnp.dot(xre_ref[...].astype(BF16), cre_ref[0], preferred_element_type=F32)
            + jnp.dot(xim_ref[...].astype(BF16), cim_ref[0], preferred_element_type=F32))
    w = S5_CHUNK * S5_GROUP
    y_ref[0] = jnp.dot(u0, mt_ref[0], preferred_element_type=F32) + ycar[:, :w]
    y_ref[1] = jnp.dot(u1, mt_ref[1], preferred_element_type=F32) + ycar[:, w:]


def _s5_scan(u_g, ops, bsz):
    mt, wre, wim, cre, cim, atr, ati = ops
    g, r, w = u_g.shape
    n_chunks = r // bsz
    p2 = 2 * S5_STATE
    kern = functools.partial(_s5_kernel, n_chunks=n_chunks, bsz=bsz)
    return pl.pallas_call(
        kern,
        out_shape=jax.ShapeDtypeStruct((g, r, w), F32),
        grid=(g // 2,),
        in_specs=[pl.BlockSpec((2, r, w), lambda i: (i, 0, 0)),
                  pl.BlockSpec((2, w, w), lambda i: (i, 0, 0)),
                  pl.BlockSpec((1, 2 * w, p2), lambda i: (i, 0, 0)),
                  pl.BlockSpec((1, 2 * w, p2), lambda i: (i, 0, 0)),
                  pl.BlockSpec((1, p2, 2 * w), lambda i: (i, 0, 0)),
                  pl.BlockSpec((1, p2, 2 * w), lambda i: (i, 0, 0)),
                  pl.BlockSpec((1, 1, p2), lambda i: (i, 0, 0)),
                  pl.BlockSpec((1, 1, p2), lambda i: (i, 0, 0))],
        out_specs=pl.BlockSpec((2, r, w), lambda i: (i, 0, 0)),
        scratch_shapes=[pltpu.VMEM((r, p2), F32)] * 4,
        compiler_params=_params("parallel"),
        name="s5_scan",
    )(u_g, mt, wre, wim, cre, cim, atr, ati)


def _hgrn_kernel(q_ref, f_ref, i_ref, g_ref, lb_ref, og_ref, o_ref, st_ref, *, seqlen):
    c = HG_CHUNK
    nc = seqlen // c
    d = HG_HEAD_DIM
    lb = lb_ref[...]
    q = q_ref[0]
    qs = _silu(q)
    f = lb + (1.0 - lb) * _sigmoid(f_ref[0])
    lf = jnp.log(f)
    k = 1.0 - f
    v = i_ref[0]
    pos = lax.broadcasted_iota(jnp.int32, (seqlen, d), 0) % c
    b = lf
    sh = 1
    while sh < c:
        b = b + jnp.where(pos >= sh, pltpu.roll(b, sh, axis=0), 0.0)
        sh *= 2
    b3 = b.reshape(nc, c, d)
    b_last = b3[:, c - 1:c, :]
    b_ref = b3[:, c // 2 - 1:c // 2, :]
    qs3 = qs.reshape(nc, c, d)
    k3 = k.reshape(nc, c, d)
    v3 = v.reshape(nc, c, d).astype(BF16)
    qe = (qs3 * jnp.exp(b3 - b_ref)).astype(BF16)
    ke = (k3 * jnp.exp(b_ref - b3)).astype(BF16)
    kd = (k3 * jnp.exp(b_last - b3)).astype(BF16)
    qb = (qs3 * jnp.exp(b3)).astype(BF16)
    scores = jnp.einsum('ctd,csd->cts', qe, ke, preferred_element_type=F32)
    ti = lax.broadcasted_iota(jnp.int32, (c, c), 0)
    si = lax.broadcasted_iota(jnp.int32, (c, c), 1)
    scores = jnp.where((ti >= si)[None], scores, 0.0)
    o_intra = jnp.einsum('cts,csv->ctv', scores.astype(BF16), v3, preferred_element_type=F32)
    ut = jnp.einsum('csv,csd->cvd', v3, kd, preferred_element_type=F32)
    decay = jnp.exp(b_last)
    state = jnp.zeros((d, d), F32)
    for ci in range(nc):
        st_ref[ci] = state.astype(BF16)
        state = decay[ci] * state + ut[ci]
    o_inter = jnp.einsum('ctd,cvd->ctv', qb, st_ref[...], preferred_element_type=F32)
    o = (o_intra + o_inter).reshape(seqlen, d)
    o = _rms(o, og_ref[...])
    o_ref[0] = o * _silu(g_ref[0])


def _hgrn(h4, lower_bound, o_gain, bsz, seqlen):
    d = HG_HEAD_DIM
    kern = functools.partial(_hgrn_kernel, seqlen=seqlen)

    def col(part):
        return pl.BlockSpec((1, seqlen, d), lambda b, h: (b, 0, part * HG_HEADS + h))

    return pl.pallas_call(
        kern,
        out_shape=jax.ShapeDtypeStruct((bsz, seqlen, HG_WIDTH), F32),
        grid=(bsz, HG_HEADS),
        in_specs=[col(0), col(1), col(2), col(3),
                  pl.BlockSpec((1, d), lambda b, h: (0, h)),
                  pl.BlockSpec((1, d), lambda b, h: (0, 0))],
        out_specs=pl.BlockSpec((1, seqlen, d), lambda b, h: (b, 0, h)),
        scratch_shapes=[pltpu.VMEM((seqlen // HG_CHUNK, d, d), BF16)],
        compiler_params=_params("parallel", "parallel"),
        name="hgrn2",
    )(h4, h4, h4, h4, lower_bound.reshape(1, HG_WIDTH), o_gain.reshape(1, d))


def _evenout_kernel(x_ref, ys_ref, u_ref, b_ref, d_ref, wglu_ref, wa_ref, wb_ref, o_ref):
    y = ys_ref[...] + d_ref[...] * u_ref[...]
    y = jax.nn.gelu(y)
    gate = _sigmoid(jnp.dot(y.astype(BF16), wglu_ref[...], preferred_element_type=F32))
    a = (y * gate).astype(BF16)
    mix = (jnp.dot(a, wa_ref[...], preferred_element_type=F32)
           + jnp.dot(b_ref[...].astype(BF16), wb_ref[...], preferred_element_type=F32))
    o_ref[...] = x_ref[...] + mix


def _evenout(x2, ys, u, b_out, d_skip, wglu, wout, tm=512):
    n = x2.shape[0]
    row = lambda w: pl.BlockSpec((tm, w), lambda i: (i, 0))
    full = lambda r, c: pl.BlockSpec((r, c), lambda i: (0, 0))
    return pl.pallas_call(
        _evenout_kernel,
        out_shape=jax.ShapeDtypeStruct((n, D_MODEL), F32),
        grid=(n // tm,),
        in_specs=[row(D_MODEL), row(S5_WIDTH), row(S5_WIDTH), row(HG_WIDTH),
                  full(1, S5_WIDTH), full(S5_WIDTH, S5_WIDTH),
                  full(S5_WIDTH, D_MODEL), full(HG_WIDTH, D_MODEL)],
        out_specs=row(D_MODEL),
        compiler_params=_params("parallel"),
        name="even_out",
    )(x2, ys, u, b_out, d_skip.reshape(1, S5_WIDTH), wglu.astype(BF16),
      wout[:S5_WIDTH].astype(BF16), wout[S5_WIDTH:].astype(BF16))


def _router_kernel(x_ref, g_ref, wr_ref, br_ref, tri_ref, idx_ref, wts_ref, cnt_ref, run_ref):
    @pl.when(pl.program_id(0) == 0)
    def _():
        run_ref[...] = jnp.zeros_like(run_ref)

    h = _rms(x_ref[...], g_ref[...])
    lt = lax.dot_general(wr_ref[...], h, (((1,), (1,)), ((), ())),
                         preferred_element_type=F32, precision=lax.Precision.HIGHEST)
    lt = lt + br_ref[...]
    gl = [lt[i:i + 1] for i in range(N_GROUPS)]
    el = [lt[N_GROUPS + i:N_GROUPS + i + 1] for i in range(N_EXPERTS)]
    gmax = jnp.maximum(jnp.maximum(gl[0], gl[1]), jnp.maximum(gl[2], gl[3]))
    gexp = [jnp.exp(v - gmax) for v in gl]
    gsum = gexp[0] + gexp[1] + gexp[2] + gexp[3]
    gprob = [v / gsum for v in gexp]
    g_gate = jnp.maximum(jnp.maximum(gprob[0], gprob[1]), jnp.maximum(gprob[2], gprob[3]))
    g_idx = jnp.where(gprob[0] == g_gate, 0,
                      jnp.where(gprob[1] == g_gate, 1, jnp.where(gprob[2] == g_gate, 2, 3)))
    es = []
    for j in range(EXPERTS_PER_GROUP):
        es.append(jnp.where(g_idx == 0, el[j],
                            jnp.where(g_idx == 1, el[4 + j],
                                      jnp.where(g_idx == 2, el[8 + j], el[12 + j]))))
    emax = jnp.maximum(jnp.maximum(es[0], es[1]), jnp.maximum(es[2], es[3]))
    eexp = [jnp.exp(v - emax) for v in es]
    esum = eexp[0] + eexp[1] + eexp[2] + eexp[3]
    ep = [v / esum for v in eexp]
    p1 = jnp.maximum(jnp.maximum(ep[0], ep[1]), jnp.maximum(ep[2], ep[3]))
    i1 = jnp.where(ep[0] == p1, 0, jnp.where(ep[1] == p1, 1, jnp.where(ep[2] == p1, 2, 3)))
    neg = jnp.float32(-1.0)
    rest = [jnp.where(i1 == j, neg, ep[j]) for j in range(EXPERTS_PER_GROUP)]
    p2 = jnp.maximum(jnp.maximum(rest[0], rest[1]), jnp.maximum(rest[2], rest[3]))
    i2 = jnp.where(rest[0] == p2, 0, jnp.where(rest[1] == p2, 1, jnp.where(rest[2] == p2, 2, 3)))
    wsum = p1 + p2
    w1 = g_gate * (p1 / wsum)
    w2 = g_gate * (p2 / wsum)
    first_lo = i1 < i2
    lo = jnp.where(first_lo, i1, i2)
    hi = jnp.where(first_lo, i2, i1)
    w_lo = jnp.where(first_lo, w1, w2)
    w_hi = jnp.where(first_lo, w2, w1)
    pair = jnp.where(lo == 0, 0, jnp.where(lo == 1, 3, 5)) + hi - lo - 1
    bucket = g_idx * N_PAIRS + pair
    tm = bucket.shape[1]
    rowid = lax.broadcasted_iota(jnp.int32, (BUCKET_ROWS, tm), 0)
    onehot = (rowid == bucket).astype(F32)
    prefix = jnp.dot(onehot.astype(BF16), tri_ref[...], preferred_element_type=F32)
    run = run_ref[...]
    rank = jnp.sum(onehot * (prefix + run), axis=0, keepdims=True)
    run = run + jnp.sum(onehot, axis=1, keepdims=True)
    run_ref[...] = run
    cnt_ref[...] = jnp.broadcast_to(run, cnt_ref.shape)
    idx_ref[...] = jnp.concatenate([bucket, rank.astype(jnp.int32), jnp.zeros((6, tm), jnp.int32)], axis=0)
    wts_ref[...] = jnp.concatenate([w_lo, w_hi, jnp.zeros((6, tm), F32)], axis=0)


def _router(x2, gain, w_rg, b_rg, w_re, b_re, tm=512):
    n = x2.shape[0]
    wr = jnp.concatenate([w_rg, w_re], axis=1).astype(F32).T
    wr = jnp.pad(wr, ((0, ROUTER_ROWS - wr.shape[0]), (0, 0)))
    br = jnp.pad(jnp.concatenate([b_rg, b_re]).astype(F32), (0, ROUTER_ROWS - N_GROUPS - N_EXPERTS))
    tri = (np.arange(tm)[:, None] < np.arange(tm)[None, :]).astype(np.float32)
    return pl.pallas_call(
        _router_kernel,
        out_shape=(jax.ShapeDtypeStruct((8, n), jnp.int32),
                   jax.ShapeDtypeStruct((8, n), F32),
                   jax.ShapeDtypeStruct((BUCKET_ROWS, 128), F32)),
        grid=(n // tm,),
        in_specs=[pl.BlockSpec((tm, D_MODEL), lambda i: (i, 0)),
                  pl.BlockSpec((1, D_MODEL), lambda i: (0, 0)),
                  pl.BlockSpec((ROUTER_ROWS, D_MODEL), lambda i: (0, 0)),
                  pl.BlockSpec((ROUTER_ROWS, 1), lambda i: (0, 0)),
                  pl.BlockSpec((tm, tm), lambda i: (0, 0))],
        out_specs=(pl.BlockSpec((8, tm), lambda i: (0, i)),
                   pl.BlockSpec((8, tm), lambda i: (0, i)),
                   pl.BlockSpec((BUCKET_ROWS, 128), lambda i: (0, 0))),
        scratch_shapes=[pltpu.VMEM((BUCKET_ROWS, 1), F32)],
        compiler_params=_params("arbitrary"),
        name="moe_router",
    )(x2, gain.reshape(1, D_MODEL), wr, br.reshape(ROUTER_ROWS, 1), jnp.asarray(tri, dtype=BF16))


def _start_row_copies(idx_ref, n_rows, copy_for_row):
    def body(r, carry):
        copy_for_row(r, idx_ref[0, 0, r]).start()
        return carry

    lax.fori_loop(0, n_rows, body, 0)


def _dispatch_kernel(pos_ref, x_ref, w_ref, xs_in_ref, xs_ref, buf_ref, sem):
    del xs_in_ref
    tm = x_ref.shape[0]
    buf_ref[:, :D_MODEL] = x_ref[...]
    wpad = jnp.concatenate([w_ref[...], jnp.zeros((XS_EXTRA - w_ref.shape[0], tm), F32)], axis=0)
    buf_ref[:, D_MODEL:] = wpad.T
    _start_row_copies(pos_ref, tm, lambda r, p: pltpu.make_async_copy(
        buf_ref.at[pl.ds(r, 1)], xs_ref.at[pl.ds(p, 1)], sem))
    pltpu.make_async_copy(buf_ref, xs_ref.at[pl.ds(0, tm)], sem).wait()


def _dispatch(x2, wts, pos3, n_rows_sorted, tm):
    n = x2.shape[0]
    xs0 = jnp.zeros((n_rows_sorted, XS_WIDTH), F32)
    return pl.pallas_call(
        _dispatch_kernel,
        out_shape=jax.ShapeDtypeStruct((n_rows_sorted, XS_WIDTH), F32),
        grid=(n // tm,),
        in_specs=[pl.BlockSpec((1, 1, tm), lambda i: (i, 0, 0), memory_space=pltpu.SMEM),
                  pl.BlockSpec((tm, D_MODEL), lambda i: (i, 0)),
                  pl.BlockSpec((8, tm), lambda i: (0, i)),
                  pl.BlockSpec(memory_space=pl.ANY)],
        out_specs=pl.BlockSpec(memory_space=pl.ANY),
        scratch_shapes=[pltpu.VMEM((tm, XS_WIDTH), F32), pltpu.SemaphoreType.DMA],
        input_output_aliases={3: 0},
        compiler_params=_params("arbitrary"),
        name="moe_dispatch",
    )(pos3, x2, wts, xs0)


def _experts_kernel(elo_ref, ehi_ref, nvalid_ref, xs_ref, g_ref, wg_lo, wu_lo, wg_hi, wu_hi,
                    wd_lo, wd_hi, o_ref):
    del elo_ref, ehi_ref
    t = pl.program_id(0)

    @pl.when(t < nvalid_ref[0])
    def _():
        xt = xs_ref[:, :D_MODEL]
        h = _rms(xt, g_ref[...]).astype(BF16)
        w_lo = xs_ref[:, D_MODEL:D_MODEL + 1]
        w_hi = xs_ref[:, D_MODEL + 1:D_MODEL + 2]

        def expert(wg, wu, wd, w):
            gate = jnp.dot(h, wg[0], preferred_element_type=F32)
            up = jnp.dot(h, wu[0], preferred_element_type=F32)
            hid = (_silu(gate) * up * w).astype(BF16)
            return jnp.dot(hid, wd[0], preferred_element_type=F32)

        o_ref[...] = xt + expert(wg_lo, wu_lo, wd_lo, w_lo) + expert(wg_hi, wu_hi, wd_hi, w_hi)

    @pl.when(t >= nvalid_ref[0])
    def _():
        o_ref[...] = jnp.zeros_like(o_ref)


def _experts(xs, gain, tables, wg, wu, wd, n_tiles, t):
    elo, ehi, nvalid = tables
    row = lambda i, elo, ehi, nv: (i, 0)
    lo3 = lambda i, elo, ehi, nv: (elo[i], 0, 0)
    hi3 = lambda i, elo, ehi, nv: (ehi[i], 0, 0)
    grid_spec = pltpu.PrefetchScalarGridSpec(
        num_scalar_prefetch=3,
        grid=(n_tiles,),
        in_specs=[pl.BlockSpec((t, XS_WIDTH), row),
                  pl.BlockSpec((1, D_MODEL), lambda i, *_: (0, 0)),
                  pl.BlockSpec((1, D_MODEL, D_EXPERT), lo3),
                  pl.BlockSpec((1, D_MODEL, D_EXPERT), lo3),
                  pl.BlockSpec((1, D_MODEL, D_EXPERT), hi3),
                  pl.BlockSpec((1, D_MODEL, D_EXPERT), hi3),
                  pl.BlockSpec((1, D_EXPERT, D_MODEL), lo3),
                  pl.BlockSpec((1, D_EXPERT, D_MODEL), hi3)],
        out_specs=pl.BlockSpec((t, D_MODEL), row),
    )
    return pl.pallas_call(
        _experts_kernel,
        out_shape=jax.ShapeDtypeStruct((xs.shape[0], D_MODEL), F32),
        grid_spec=grid_spec,
        compiler_params=_params("arbitrary"),
        name="moe_experts",
    )(elo, ehi, nvalid, xs, gain.reshape(1, D_MODEL), wg, wu, wg, wu, wd, wd)


def _combine_kernel(pos_ref, ys_ref, o_ref, sem):
    tm = o_ref.shape[0]
    _start_row_copies(pos_ref, tm, lambda r, p: pltpu.make_async_copy(
        ys_ref.at[pl.ds(p, 1)], o_ref.at[pl.ds(r, 1)], sem))
    pltpu.make_async_copy(ys_ref.at[pl.ds(0, tm)], o_ref, sem).wait()


def _combine(ys, pos3, n, tm):
    return pl.pallas_call(
        _combine_kernel,
        out_shape=jax.ShapeDtypeStruct((n, D_MODEL), F32),
        grid=(n // tm,),
        in_specs=[pl.BlockSpec((1, 1, tm), lambda i: (i, 0, 0), memory_space=pltpu.SMEM),
                  pl.BlockSpec(memory_space=pl.ANY)],
        out_specs=pl.BlockSpec((tm, D_MODEL), lambda i: (i, 0)),
        scratch_shapes=[pltpu.SemaphoreType.DMA],
        compiler_params=_params("arbitrary"),
        name="moe_combine",
    )(pos3, ys)


def _moe_tables(idx, cnt, n_tiles, t):
    bucket, rank = idx[0], idx[1]
    counts = cnt[:N_BUCKETS, 0].astype(jnp.int32)
    tiles_b = (counts + t - 1) // t
    tile_end = jnp.cumsum(tiles_b)
    pos = (tile_end - tiles_b)[bucket] * t + rank
    total = tile_end[-1]
    tt = jnp.arange(n_tiles, dtype=jnp.int32)
    valid = tt < total
    tb = jnp.searchsorted(tile_end, jnp.where(valid, tt, total - 1), side='right').astype(jnp.int32)
    tb = jnp.minimum(tb, N_BUCKETS - 1)
    pair_lo = jnp.asarray([0, 0, 0, 1, 1, 2], jnp.int32)
    pair_hi = jnp.asarray([1, 2, 3, 2, 3, 3], jnp.int32)
    base = (tb // N_PAIRS) * EXPERTS_PER_GROUP
    return pos, (base + pair_lo[tb % N_PAIRS], base + pair_hi[tb % N_PAIRS], total.reshape(1))


def _moe(x2, gain, w_rg, b_rg, w_re, b_re, wg, wu, wd, t=MOE_TILE, tm=1024):
    n = x2.shape[0]
    idx, wts, cnt = _router(x2, gain, w_rg, b_rg, w_re, b_re)
    n_tiles = n // t + N_BUCKETS
    pos, tables = _moe_tables(idx, cnt, n_tiles, t)
    pos3 = pos.reshape(n // tm, 1, tm)
    xs = _dispatch(x2, wts, pos3, n_tiles * t, tm)
    ys = _experts(xs, gain, tables, wg.astype(BF16), wu.astype(BF16), wd.astype(BF16), n_tiles, t)
    return _combine(ys, pos3, n, tm)


def _qkv_kernel(x_ref, g_ref, wq_ref, wkt_ref, wv_ref, seg_ref, kg_ref, q_ref, ss_ref, kt_ref, v_ref):
    h = _rms(x_ref[...], g_ref[...]).astype(BF16)
    q = jnp.dot(h, wq_ref[...], preferred_element_type=F32)
    q_ref[...] = q.astype(BF16)
    ss_ref[...] = jnp.dot((q * q).astype(BF16), seg_ref[...], preferred_element_type=F32)
    kt = lax.dot_general(wkt_ref[...], h, (((1,), (1,)), ((), ())), preferred_element_type=F32)
    tm = kt.shape[1]
    k3 = kt.reshape(N_KV_HEADS, HEAD_DIM, tm)
    ms = jnp.mean(k3 * k3, axis=1, keepdims=True)
    kn = k3 * lax.rsqrt(ms + EPS) * kg_ref[...].reshape(N_KV_HEADS, HEAD_DIM, 1)
    kt_ref[...] = kn.reshape(KV_WIDTH, tm).astype(BF16)
    v_ref[...] = jnp.dot(h, wv_ref[...], preferred_element_type=F32).astype(BF16)


def _qkv(x2, gain, wqkv, q_gain, k_gain, tm=512):
    n = x2.shape[0]
    qw = N_Q_HEADS * HEAD_DIM
    wq = wqkv[:, :qw].astype(BF16)
    wkt = wqkv[:, qw:qw + KV_WIDTH].T.astype(BF16)
    wv = wqkv[:, qw + KV_WIDTH:].astype(BF16)
    seg = (np.arange(qw)[:, None] // HEAD_DIM == np.arange(128)[None, :]).astype(np.float32)
    kg = jnp.tile((k_gain.astype(F32) * q_gain.astype(F32)), N_KV_HEADS).reshape(KV_WIDTH, 1)
    full = lambda r, c: pl.BlockSpec((r, c), lambda i: (0, 0))
    return pl.pallas_call(
        _qkv_kernel,
        out_shape=(jax.ShapeDtypeStruct((n, qw), BF16),
                   jax.ShapeDtypeStruct((n, 128), F32),
                   jax.ShapeDtypeStruct((KV_WIDTH, n), BF16),
                   jax.ShapeDtypeStruct((n, KV_WIDTH), BF16)),
        grid=(n // tm,),
        in_specs=[pl.BlockSpec((tm, D_MODEL), lambda i: (i, 0)), full(1, D_MODEL),
                  full(D_MODEL, qw), full(KV_WIDTH, D_MODEL), full(D_MODEL, KV_WIDTH),
                  full(qw, 128), full(KV_WIDTH, 1)],
        out_specs=(pl.BlockSpec((tm, qw), lambda i: (i, 0)),
                   pl.BlockSpec((tm, 128), lambda i: (i, 0)),
                   pl.BlockSpec((KV_WIDTH, tm), lambda i: (0, i)),
                   pl.BlockSpec((tm, KV_WIDTH), lambda i: (i, 0))),
        compiler_params=_params("parallel"),
        name="odd_qkv",
    )(x2, gain.reshape(1, D_MODEL), wq, wkt, wv, jnp.asarray(seg, dtype=BF16), kg)


def _attn_kernel(sink_ref, q_ref, ss_ref, ktp_ref, ktc_ref, vp_ref, vc_ref, x_ref, wo_ref, o_ref, att_ref):
    n = pl.program_id(1)
    blk = ATT_BLOCK
    kw = jnp.concatenate([ktp_ref[...], ktc_ref[...]], axis=1)
    vw = jnp.concatenate([vp_ref[...], vc_ref[...]], axis=0)
    qi = lax.broadcasted_iota(jnp.int32, (blk, 2 * blk), 0)
    ki = lax.broadcasted_iota(jnp.int32, (blk, 2 * blk), 1)
    dist = qi - ki + blk
    valid = (dist >= 0) & (dist < blk) & (n * blk + ki - blk >= 0)
    distf = dist.astype(F32)
    ss = ss_ref[...]
    for h in range(N_Q_HEADS):
        hk = h // GQA_GROUP
        slope = float(2.0 ** (-8.0 * (h + 1) / N_Q_HEADS))
        sink = sink_ref[h]
        rs = lax.rsqrt(ss[:, h:h + 1] * (1.0 / HEAD_DIM) + EPS) * (HEAD_DIM ** -0.5)
        qh = (q_ref[:, h * HEAD_DIM:(h + 1) * HEAD_DIM].astype(F32) * rs).astype(BF16)
        s = jnp.dot(qh, kw[hk * HEAD_DIM:(hk + 1) * HEAD_DIM, :], preferred_element_type=F32)
        s = jnp.where(valid, s - slope * distf, -jnp.inf)
        m = jnp.maximum(jnp.max(s, axis=-1, keepdims=True), sink)
        p = jnp.exp(s - m)
        den = jnp.sum(p, axis=-1, keepdims=True) + jnp.exp(sink - m)
        oh = jnp.dot(p.astype(BF16), vw[:, hk * HEAD_DIM:(hk + 1) * HEAD_DIM], preferred_element_type=F32)
        att_ref[:, h * HEAD_DIM:(h + 1) * HEAD_DIM] = (oh / den).astype(BF16)
    o_ref[...] = x_ref[...] + jnp.dot(att_ref[...], wo_ref[...], preferred_element_type=F32)


def _attn(q, ss, kt, v, x2, sinks, wo, bsz, seqlen):
    blk = ATT_BLOCK
    nb = seqlen // blk
    qw = N_Q_HEADS * HEAD_DIM
    cur = lambda b, n, s: (b * nb + n, 0)
    prev = lambda b, n, s: (b * nb + jnp.maximum(n - 1, 0), 0)
    cur_t = lambda b, n, s: (0, b * nb + n)
    prev_t = lambda b, n, s: (0, b * nb + jnp.maximum(n - 1, 0))
    grid_spec = pltpu.PrefetchScalarGridSpec(
        num_scalar_prefetch=1,
        grid=(bsz, nb),
        in_specs=[pl.BlockSpec((blk, qw), cur),
                  pl.BlockSpec((blk, 128), cur),
                  pl.BlockSpec((KV_WIDTH, blk), prev_t),
                  pl.BlockSpec((KV_WIDTH, blk), cur_t),
                  pl.BlockSpec((blk, KV_WIDTH), prev),
                  pl.BlockSpec((blk, KV_WIDTH), cur),
                  pl.BlockSpec((blk, D_MODEL), cur),
                  pl.BlockSpec((qw, D_MODEL), lambda b, n, s: (0, 0))],
        out_specs=pl.BlockSpec((blk, D_MODEL), cur),
        scratch_shapes=[pltpu.VMEM((blk, qw), BF16)],
    )
    return pl.pallas_call(
        _attn_kernel,
        out_shape=jax.ShapeDtypeStruct((bsz * seqlen, D_MODEL), F32),
        grid_spec=grid_spec,
        compiler_params=_params("parallel", "parallel"),
        name="odd_attn",
    )(sinks.astype(F32), q, ss, kt, kt, v, v, x2, wo.astype(BF16))


def kernel(x, even_mix_norm, even_in_proj, s5_lambda_re, s5_lambda_im, s5_log_step, s5_b_re, s5_b_im,
           s5_c_re, s5_c_im, s5_d, s5_glu_w, hgrn_lower_bounds, hgrn_o_norm, even_out_proj, odd_mix_norm,
           odd_wqkv, odd_q_norm, odd_k_norm, odd_sinks, odd_out_proj, moe_norm, moe_router_group,
           moe_router_group_bias, moe_router_expert, moe_router_expert_bias, moe_w_gate, moe_w_up,
           moe_w_down):
    bsz, seqlen, dm = x.shape
    n = bsz * seqlen
    x2 = x.reshape(n, dm)
    lower_bounds = jnp.cumsum(jax.nn.softmax(hgrn_lower_bounds.astype(F32), axis=0), axis=0)

    def moe(xx, layer):
        return _moe(xx, moe_norm[layer], moe_router_group[layer], moe_router_group_bias[layer],
                    moe_router_expert[layer], moe_router_expert_bias[layer],
                    moe_w_gate[layer], moe_w_up[layer], moe_w_down[layer])

    u, h4 = _inproj(x2, even_mix_norm[0], even_in_proj[0].astype(BF16))
    ops = _s5_operators(s5_lambda_re[0], s5_lambda_im[0], s5_log_step[0], s5_b_re[0], s5_b_im[0],
                        s5_c_re[0], s5_c_im[0])
    nch = seqlen // S5_CHUNK
    u_g = u.astype(BF16).reshape(bsz, nch, S5_CHUNK, S5_GROUPS, S5_GROUP)
    u_g = jnp.transpose(u_g, (3, 1, 0, 2, 4)).reshape(S5_GROUPS, nch * bsz, S5_CHUNK * S5_GROUP)
    y_g = _s5_scan(u_g, ops, bsz)
    ys = jnp.transpose(y_g.reshape(S5_GROUPS, nch, bsz, S5_CHUNK, S5_GROUP), (2, 1, 3, 0, 4))
    ys = ys.reshape(n, S5_WIDTH)
    b_out = _hgrn(h4.reshape(bsz, seqlen, 4 * HG_WIDTH), lower_bounds[0], hgrn_o_norm[0], bsz, seqlen)
    x2 = _evenout(x2, ys, u, b_out.reshape(n, HG_WIDTH), s5_d[0], s5_glu_w[0], even_out_proj[0])
    x2 = moe(x2, 0)

    q, ss, kt, v = _qkv(x2, odd_mix_norm[0], odd_wqkv[0], odd_q_norm[0], odd_k_norm[0])
    x2 = _attn(q, ss, kt, v, x2, odd_sinks[0], odd_out_proj[0], bsz, seqlen)
    x2 = moe(x2, 1)
    return x2.reshape(bsz, seqlen, dm)
```

```python
import functools
import math

import jax
import jax.numpy as jnp
import numpy as np
from jax import lax
from jax.experimental import pallas as pl
from jax.experimental.pallas import tpu as pltpu

F32 = jnp.float32
BF16 = jnp.bfloat16
EPS = 1e-6

D_MODEL = 1024
S5_WIDTH = 512
S5_GROUP = 16
S5_GROUPS = 32
S5_STATE = 64
S5_CHUNK = 16
HG_WIDTH = 512
HG_HEAD_DIM = 128
HG_HEADS = 4
HG_CHUNK = 32
HEAD_DIM = 64
N_Q_HEADS = 16
N_KV_HEADS = 2
GQA_GROUP = 8
KV_WIDTH = N_KV_HEADS * HEAD_DIM
ATT_BLOCK = 128
N_GROUPS = 4
EXPERTS_PER_GROUP = 4
N_EXPERTS = 16
D_EXPERT = 256
ROUTER_ROWS = 32
N_PAIRS = 6
N_BUCKETS = N_GROUPS * N_PAIRS
BUCKET_ROWS = 32
MOE_TILE = 256
XS_EXTRA = 128
XS_WIDTH = D_MODEL + XS_EXTRA

VMEM_LIMIT_BYTES = 56 * 1024 * 1024


def _params(*semantics):
    return pltpu.CompilerParams(dimension_semantics=semantics, vmem_limit_bytes=VMEM_LIMIT_BYTES)


def _rms(xf, gain):
    return xf * lax.rsqrt(jnp.mean(xf * xf, axis=-1, keepdims=True) + EPS) * gain


def _sigmoid(x):
    return 1.0 / (1.0 + jnp.exp(-x))


def _silu(x):
    return x * _sigmoid(x)


def _inproj_kernel(x_ref, g_ref, w_ref, u_ref, h4_ref):
    h = _rms(x_ref[...], g_ref[...]).astype(BF16)
    p = jnp.dot(h, w_ref[...], preferred_element_type=F32)
    u_ref[...] = p[:, :S5_WIDTH]
    h4_ref[...] = p[:, S5_WIDTH:]


def _inproj(x2, gain, w_bf16, tm=512):
    n = x2.shape[0]
    e_in = w_bf16.shape[1]
    return pl.pallas_call(
        _inproj_kernel,
        out_shape=(jax.ShapeDtypeStruct((n, S5_WIDTH), F32),
                   jax.ShapeDtypeStruct((n, e_in - S5_WIDTH), F32)),
        grid=(n // tm,),
        in_specs=[pl.BlockSpec((tm, D_MODEL), lambda i: (i, 0)),
                  pl.BlockSpec((1, D_MODEL), lambda i: (0, 0)),
                  pl.BlockSpec((D_MODEL, e_in), lambda i: (0, 0))],
        out_specs=(pl.BlockSpec((tm, S5_WIDTH), lambda i: (i, 0)),
                   pl.BlockSpec((tm, e_in - S5_WIDTH), lambda i: (i, 0))),
        compiler_params=_params("parallel"),
        name="even_inproj",
    )(x2, gain.reshape(1, D_MODEL), w_bf16)


def _s5_lagkernel_kernel(ca_ref, bb_ref, k_ref):
    k_ref[0] = jnp.dot(ca_ref[0], bb_ref[0], preferred_element_type=F32,
                       precision=lax.Precision.HIGHEST)


def _s5_lagkernel(ca, bb):
    g, rows, k = ca.shape
    return pl.pallas_call(
        _s5_lagkernel_kernel,
        out_shape=jax.ShapeDtypeStruct((g, rows, S5_GROUP), F32),
        grid=(g,),
        in_specs=[pl.BlockSpec((1, rows, k), lambda i: (i, 0, 0)),
                  pl.BlockSpec((1, k, S5_GROUP), lambda i: (i, 0, 0))],
        out_specs=pl.BlockSpec((1, rows, S5_GROUP), lambda i: (i, 0, 0)),
        compiler_params=_params("parallel"),
        name="s5_lag_kernel",
    )(ca, bb)


def _s5_operators(lam_re, lam_im, log_step, b_re, b_im, c_re, c_im):
    t = S5_CHUNK
    lr, li = lam_re.astype(F32), lam_im.astype(F32)
    step = jnp.exp(log_step.astype(F32))[:, None]
    mag = jnp.exp(lr * step)
    ab_re = mag * jnp.cos(li * step)
    ab_im = mag * jnp.sin(li * step)
    den = lr * lr + li * li
    nr, ni = ab_re - 1.0, ab_im
    z_re = (nr * lr + ni * li) / den
    z_im = (ni * lr - nr * li) / den
    br, bi = b_re.astype(F32), b_im.astype(F32)
    bb_re = z_re[..., None] * br - z_im[..., None] * bi
    bb_im = z_re[..., None] * bi + z_im[..., None] * br
    kk = jnp.arange(t + 1, dtype=F32)[:, None, None]
    pmag = jnp.exp(kk * (lr * step)[None])
    pw_re = pmag * jnp.cos(kk * (li * step)[None])
    pw_im = pmag * jnp.sin(kk * (li * step)[None])
    cr = jnp.transpose(c_re.astype(F32), (0, 1, 2))
    ci = c_im.astype(F32)
    ca_re = cr[None] * pw_re[:, :, None, :] - ci[None] * pw_im[:, :, None, :]
    ca_im = cr[None] * pw_im[:, :, None, :] + ci[None] * pw_re[:, :, None, :]
    g = lr.shape[0]
    ca_cat = jnp.concatenate([ca_re[:t], -ca_im[:t]], axis=-1)
    ca_cat = jnp.transpose(ca_cat, (1, 0, 2, 3)).reshape(g, t * S5_GROUP, 2 * S5_STATE)
    bb_cat = jnp.concatenate([bb_re, bb_im], axis=1)
    kern = _s5_lagkernel(ca_cat, bb_cat).reshape(g, t, S5_GROUP, S5_GROUP)
    s_idx = jnp.arange(t)[:, None]
    t_idx = jnp.arange(t)[None, :]
    lag = t_idx - s_idx
    kg = kern[:, jnp.clip(lag, 0, t - 1)]
    kg = jnp.where((lag >= 0)[None, :, :, None, None], kg, 0.0)
    mt = jnp.transpose(kg, (0, 1, 4, 2, 3)).reshape(g, t * S5_GROUP, t * S5_GROUP)
    pr = pw_re[:t][::-1]
    pi = pw_im[:t][::-1]
    sb_re = pr[:, :, :, None] * bb_re[None] - pi[:, :, :, None] * bb_im[None]
    sb_im = pr[:, :, :, None] * bb_im[None] + pi[:, :, :, None] * bb_re[None]
    sb_re = jnp.transpose(sb_re, (1, 0, 3, 2)).reshape(g, t * S5_GROUP, S5_STATE)
    sb_im = jnp.transpose(sb_im, (1, 0, 3, 2)).reshape(g, t * S5_GROUP, S5_STATE)
    cp_re = jnp.transpose(ca_re[1:], (1, 3, 0, 2)).reshape(g, S5_STATE, t * S5_GROUP)
    cp_im = jnp.transpose(-ca_im[1:], (1, 3, 0, 2)).reshape(g, S5_STATE, t * S5_GROUP)

    def pair_rows(m):
        m = m.reshape(g // 2, 2, m.shape[1], m.shape[2])
        z = jnp.zeros_like(m[:, 0])
        top = jnp.concatenate([m[:, 0], z], axis=2)
        bot = jnp.concatenate([z, m[:, 1]], axis=2)
        return jnp.concatenate([top, bot], axis=1)

    at_re = pw_re[t].reshape(g // 2, 1, 2 * S5_STATE)
    at_im = pw_im[t].reshape(g // 2, 1, 2 * S5_STATE)
    return (mt.astype(BF16), pair_rows(sb_re).astype(BF16), pair_rows(sb_im).astype(BF16),
            pair_rows(cp_re).astype(BF16), pair_rows(cp_im).astype(BF16), at_re, at_im)


PACK_TOKENS = 128


LANES = 128
GROUPS_PER_TILE = LANES // S5_GROUP
TOKENS_PER_TILE = LANES // S5_GROUP
CHUNK_HALVES = S5_CHUNK // TOKENS_PER_TILE
PACK_CHUNKS = PACK_TOKENS // S5_CHUNK


def _s5_pack_kernel(u_ref, o_ref, *, bsz):
    def body(b, carry):
        z = [u_ref[b, pl.ds(t, PACK_CHUNKS, stride=S5_CHUNK), :] for t in range(S5_CHUNK)]
        for g in range(GROUPS_PER_TILE):
            for j in range(CHUNK_HALVES):
                row = jnp.concatenate([zt[:, g * S5_GROUP:(g + 1) * S5_GROUP]
                                       for zt in z[j * TOKENS_PER_TILE:(j + 1) * TOKENS_PER_TILE]], axis=1)
                o_ref[g * CHUNK_HALVES + j, pl.ds(b, PACK_CHUNKS, stride=bsz), :] = row
        return carry

    for b in range(bsz):
        body(b, 0)


def _s5_pack(u3):
    bsz, seqlen, w = u3.shape
    rows = PACK_CHUNKS * bsz
    return pl.pallas_call(
        functools.partial(_s5_pack_kernel, bsz=bsz),
        out_shape=jax.ShapeDtypeStruct((S5_GROUPS * CHUNK_HALVES, seqlen // S5_CHUNK * bsz, LANES), F32),
        grid=(seqlen // PACK_TOKENS, w // LANES),
        in_specs=[pl.BlockSpec((bsz, PACK_TOKENS, LANES), lambda i, k: (0, i, k))],
        out_specs=pl.BlockSpec((GROUPS_PER_TILE * CHUNK_HALVES, rows, LANES), lambda i, k: (k, i, 0)),
        compiler_params=_params("parallel", "parallel"),
        name="s5_pack",
    )(u3)


def _s5_unpack_kernel(y_ref, o_ref, *, bsz):
    def body(b, carry):
        yg = [y_ref[r, pl.ds(b, PACK_CHUNKS, stride=bsz), :] for r in range(GROUPS_PER_TILE * CHUNK_HALVES)]
        for t in range(S5_CHUNK):
            j, tt = divmod(t, TOKENS_PER_TILE)
            row = jnp.concatenate([yg[g * CHUNK_HALVES + j][:, tt * S5_GROUP:(tt + 1) * S5_GROUP]
                                   for g in range(GROUPS_PER_TILE)], axis=1)
            o_ref[b, pl.ds(t, PACK_CHUNKS, stride=S5_CHUNK), :] = row
        return carry

    for b in range(bsz):
        body(b, 0)


def _s5_unpack(y_g, bsz, seqlen):
    rows = PACK_CHUNKS * bsz
    return pl.pallas_call(
        functools.partial(_s5_unpack_kernel, bsz=bsz),
        out_shape=jax.ShapeDtypeStruct((bsz, seqlen, S5_WIDTH), F32),
        grid=(seqlen // PACK_TOKENS, S5_WIDTH // LANES),
        in_specs=[pl.BlockSpec((GROUPS_PER_TILE * CHUNK_HALVES, rows, LANES), lambda i, k: (k, i, 0))],
        out_specs=pl.BlockSpec((bsz, PACK_TOKENS, LANES), lambda i, k: (0, i, k)),
        compiler_params=_params("parallel", "parallel"),
        name="s5_unpack",
    )(y_g)


def _s5_kernel(u_ref, mt_ref, wre_ref, wim_ref, cre_ref, cim_ref, atr_ref, ati_ref, y_ref,
               sre_ref, sim_ref, xre_ref, xim_ref, *, n_chunks, bsz):
    ucat = jnp.concatenate([u_ref[i] for i in range(2 * CHUNK_HALVES)], axis=1).astype(BF16)
    w = S5_CHUNK * S5_GROUP
    u0 = ucat[:, :w]
    u1 = ucat[:, w:]
    sre_ref[...] = jnp.dot(ucat, wre_ref[0], preferred_element_type=F32)
    sim_ref[...] = jnp.dot(ucat, wim_ref[0], preferred_element_type=F32)
    atr = jnp.broadcast_to(atr_ref[0], (bsz, 2 * S5_STATE))
    ati = jnp.broadcast_to(ati_ref[0], (bsz, 2 * S5_STATE))

    def body(c, carry):
        xr, xi = carry
        rows = pl.ds(pl.multiple_of(c * bsz, bsz), bsz)
        xre_ref[rows, :] = xr
        xim_ref[rows, :] = xi
        nxr = atr * xr - ati * xi + sre_ref[rows, :]
        nxi = atr * xi + ati * xr + sim_ref[rows, :]
        return nxr, nxi

    zero = jnp.zeros((bsz, 2 * S5_STATE), F32)
    lax.fori_loop(0, n_chunks, body, (zero, zero))
    ycar = (jnp.dot(xre_ref[...].astype(BF16), cre_ref[0], preferred_element_type=F32)
            + jnp.dot(xim_ref[...].astype(BF16), cim_ref[0], preferred_element_type=F32))
    y0 = jnp.dot(u0, mt_ref[0], preferred_element_type=F32) + ycar[:, :w]
    y1 = jnp.dot(u1, mt_ref[1], preferred_element_type=F32) + ycar[:, w:]
    for i in range(CHUNK_HALVES):
        y_ref[i] = y0[:, i * LANES:(i + 1) * LANES]
        y_ref[CHUNK_HALVES + i] = y1[:, i * LANES:(i + 1) * LANES]


def _s5_scan(u_g, ops, bsz):
    mt, wre, wim, cre, cim, atr, ati = ops
    tiles, r, _ = u_g.shape
    g = tiles // CHUNK_HALVES
    w = S5_CHUNK * S5_GROUP
    n_chunks = r // bsz
    p2 = 2 * S5_STATE
    kern = functools.partial(_s5_kernel, n_chunks=n_chunks, bsz=bsz)
    pair_tiles = pl.BlockSpec((2 * CHUNK_HALVES, r, LANES), lambda i: (i, 0, 0))
    return pl.pallas_call(
        kern,
        out_shape=jax.ShapeDtypeStruct((tiles, r, LANES), F32),
        grid=(g // 2,),
        in_specs=[pair_tiles,
                  pl.BlockSpec((2, w, w), lambda i: (i, 0, 0)),
                  pl.BlockSpec((1, 2 * w, p2), lambda i: (i, 0, 0)),
                  pl.BlockSpec((1, 2 * w, p2), lambda i: (i, 0, 0)),
                  pl.BlockSpec((1, p2, 2 * w), lambda i: (i, 0, 0)),
                  pl.BlockSpec((1, p2, 2 * w), lambda i: (i, 0, 0)),
                  pl.BlockSpec((1, 1, p2), lambda i: (i, 0, 0)),
                  pl.BlockSpec((1, 1, p2), lambda i: (i, 0, 0))],
        out_specs=pair_tiles,
        scratch_shapes=[pltpu.VMEM((r, p2), F32)] * 4,
        compiler_params=_params("parallel"),
        name="s5_scan",
    )(u_g, mt, wre, wim, cre, cim, atr, ati)


def _hgrn_kernel(q_ref, f_ref, i_ref, g_ref, lb_ref, og_ref, o_ref, st_ref, *, seqlen):
    c = HG_CHUNK
    nc = seqlen // c
    d = HG_HEAD_DIM
    lb = lb_ref[...]
    q = q_ref[0]
    qs = _silu(q)
    f = lb + (1.0 - lb) * _sigmoid(f_ref[0])
    lf = jnp.log(f)
    k = 1.0 - f
    v = i_ref[0]
    pos = lax.broadcasted_iota(jnp.int32, (seqlen, d), 0) % c
    b = lf
    sh = 1
    while sh < c:
        b = b + jnp.where(pos >= sh, pltpu.roll(b, sh, axis=0), 0.0)
        sh *= 2
    b3 = b.reshape(nc, c, d)
    b_last = b3[:, c - 1:c, :]
    b_ref = b3[:, c // 2 - 1:c // 2, :]
    qs3 = qs.reshape(nc, c, d)
    k3 = k.reshape(nc, c, d)
    v3 = v.reshape(nc, c, d).astype(BF16)
    qe = (qs3 * jnp.exp(b3 - b_ref)).astype(BF16)
    ke = (k3 * jnp.exp(b_ref - b3)).astype(BF16)
    kd = (k3 * jnp.exp(b_last - b3)).astype(BF16)
    qb = (qs3 * jnp.exp(b3)).astype(BF16)
    scores = jnp.einsum('ctd,csd->cts', qe, ke, preferred_element_type=F32)
    ti = lax.broadcasted_iota(jnp.int32, (c, c), 0)
    si = lax.broadcasted_iota(jnp.int32, (c, c), 1)
    scores = jnp.where((ti >= si)[None], scores, 0.0)
    o_intra = jnp.einsum('cts,csv->ctv', scores.astype(BF16), v3, preferred_element_type=F32)
    ut = jnp.einsum('csv,csd->cvd', v3, kd, preferred_element_type=F32)
    decay = jnp.exp(b_last)
    state = jnp.zeros((d, d), F32)
    for ci in range(nc):
        st_ref[ci] = state.astype(BF16)
        state = decay[ci] * state + ut[ci]
    o_inter = jnp.einsum('ctd,cvd->ctv', qb, st_ref[...], preferred_element_type=F32)
    o = (o_intra + o_inter).reshape(seqlen, d)
    o = _rms(o, og_ref[...])
    o_ref[0] = o * _silu(g_ref[0])


def _hgrn(h4, lower_bound, o_gain, bsz, seqlen):
    d = HG_HEAD_DIM
    kern = functools.partial(_hgrn_kernel, seqlen=seqlen)

    def col(part):
        return pl.BlockSpec((1, seqlen, d), lambda b, h: (b, 0, part * HG_HEADS + h))

    return pl.pallas_call(
        kern,
        out_shape=jax.ShapeDtypeStruct((bsz, seqlen, HG_WIDTH), F32),
        grid=(bsz, HG_HEADS),
        in_specs=[col(0), col(1), col(2), col(3),
                  pl.BlockSpec((1, d), lambda b, h: (0, h)),
                  pl.BlockSpec((1, d), lambda b, h: (0, 0))],
        out_specs=pl.BlockSpec((1, seqlen, d), lambda b, h: (b, 0, h)),
        scratch_shapes=[pltpu.VMEM((seqlen // HG_CHUNK, d, d), BF16)],
        compiler_params=_params("parallel", "parallel"),
        name="hgrn2",
    )(h4, h4, h4, h4, lower_bound.reshape(1, HG_WIDTH), o_gain.reshape(1, d))


def _evenout_kernel(x_ref, ys_ref, u_ref, b_ref, d_ref, wglu_ref, wa_ref, wb_ref, o_ref):
    y = ys_ref[...] + d_ref[...] * u_ref[...]
    y = jax.nn.gelu(y)
    gate = _sigmoid(jnp.dot(y.astype(BF16), wglu_ref[...], preferred_element_type=F32))
    a = (y * gate).astype(BF16)
    mix = (jnp.dot(a, wa_ref[...], preferred_element_type=F32)
           + jnp.dot(b_ref[...].astype(BF16), wb_ref[...], preferred_element_type=F32))
    o_ref[...] = x_ref[...] + mix


def _evenout(x2, ys, u, b_out, d_skip, wglu, wout, tm=512):
    n = x2.shape[0]
    row = lambda w: pl.BlockSpec((tm, w), lambda i: (i, 0))
    full = lambda r, c: pl.BlockSpec((r, c), lambda i: (0, 0))
    return pl.pallas_call(
        _evenout_kernel,
        out_shape=jax.ShapeDtypeStruct((n, D_MODEL), F32),
        grid=(n // tm,),
        in_specs=[row(D_MODEL), row(S5_WIDTH), row(S5_WIDTH), row(HG_WIDTH),
                  full(1, S5_WIDTH), full(S5_WIDTH, S5_WIDTH),
                  full(S5_WIDTH, D_MODEL), full(HG_WIDTH, D_MODEL)],
        out_specs=row(D_MODEL),
        compiler_params=_params("parallel"),
        name="even_out",
    )(x2, ys, u, b_out, d_skip.reshape(1, S5_WIDTH), wglu.astype(BF16),
      wout[:S5_WIDTH].astype(BF16), wout[S5_WIDTH:].astype(BF16))


def _router_kernel(x_ref, g_ref, wr_ref, br_ref, tri_ref, idx_ref, wts_ref, cnt_ref, run_ref):
    @pl.when(pl.program_id(0) == 0)
    def _():
        run_ref[...] = jnp.zeros_like(run_ref)

    h = _rms(x_ref[...], g_ref[...])
    lt = lax.dot_general(wr_ref[...], h, (((1,), (1,)), ((), ())),
                         preferred_element_type=F32, precision=lax.Precision.HIGHEST)
    lt = lt + br_ref[...]
    gl = [lt[i:i + 1] for i in range(N_GROUPS)]
    el = [lt[N_GROUPS + i:N_GROUPS + i + 1] for i in range(N_EXPERTS)]
    gmax = jnp.maximum(jnp.maximum(gl[0], gl[1]), jnp.maximum(gl[2], gl[3]))
    gexp = [jnp.exp(v - gmax) for v in gl]
    gsum = gexp[0] + gexp[1] + gexp[2] + gexp[3]
    gprob = [v / gsum for v in gexp]
    g_gate = jnp.maximum(jnp.maximum(gprob[0], gprob[1]), jnp.maximum(gprob[2], gprob[3]))
    g_idx = jnp.where(gprob[0] == g_gate, 0,
                      jnp.where(gprob[1] == g_gate, 1, jnp.where(gprob[2] == g_gate, 2, 3)))
    es = []
    for j in range(EXPERTS_PER_GROUP):
        es.append(jnp.where(g_idx == 0, el[j],
                            jnp.where(g_idx == 1, el[4 + j],
                                      jnp.where(g_idx == 2, el[8 + j], el[12 + j]))))
    emax = jnp.maximum(jnp.maximum(es[0], es[1]), jnp.maximum(es[2], es[3]))
    eexp = [jnp.exp(v - emax) for v in es]
    esum = eexp[0] + eexp[1] + eexp[2] + eexp[3]
    ep = [v / esum for v in eexp]
    p1 = jnp.maximum(jnp.maximum(ep[0], ep[1]), jnp.maximum(ep[2], ep[3]))
    i1 = jnp.where(ep[0] == p1, 0, jnp.where(ep[1] == p1, 1, jnp.where(ep[2] == p1, 2, 3)))
    neg = jnp.float32(-1.0)
    rest = [jnp.where(i1 == j, neg, ep[j]) for j in range(EXPERTS_PER_GROUP)]
    p2 = jnp.maximum(jnp.maximum(rest[0], rest[1]), jnp.maximum(rest[2], rest[3]))
    i2 = jnp.where(rest[0] == p2, 0, jnp.where(rest[1] == p2, 1, jnp.where(rest[2] == p2, 2, 3)))
    wsum = p1 + p2
    w1 = g_gate * (p1 / wsum)
    w2 = g_gate * (p2 / wsum)
    first_lo = i1 < i2
    lo = jnp.where(first_lo, i1, i2)
    hi = jnp.where(first_lo, i2, i1)
    w_lo = jnp.where(first_lo, w1, w2)
    w_hi = jnp.where(first_lo, w2, w1)
    pair = jnp.where(lo == 0, 0, jnp.where(lo == 1, 3, 5)) + hi - lo - 1
    bucket = g_idx * N_PAIRS + pair
    tm = bucket.shape[1]
    rowid = lax.broadcasted_iota(jnp.int32, (BUCKET_ROWS, tm), 0)
    onehot = (rowid == bucket).astype(F32)
    prefix = jnp.dot(onehot.astype(BF16), tri_ref[...], preferred_element_type=F32)
    run = run_ref[...]
    rank = jnp.sum(onehot * (prefix + run), axis=0, keepdims=True)
    run = run + jnp.sum(onehot, axis=1, keepdims=True)
    run_ref[...] = run
    cnt_ref[...] = jnp.broadcast_to(run, cnt_ref.shape)
    idx_ref[...] = jnp.concatenate([bucket, rank.astype(jnp.int32), jnp.zeros((6, tm), jnp.int32)], axis=0)
    wts_ref[...] = jnp.concatenate([w_lo, w_hi, jnp.zeros((6, tm), F32)], axis=0)


def _router(x2, gain, w_rg, b_rg, w_re, b_re, tm=512):
    n = x2.shape[0]
    wr = jnp.concatenate([w_rg, w_re], axis=1).astype(F32).T
    wr = jnp.pad(wr, ((0, ROUTER_ROWS - wr.shape[0]), (0, 0)))
    br = jnp.pad(jnp.concatenate([b_rg, b_re]).astype(F32), (0, ROUTER_ROWS - N_GROUPS - N_EXPERTS))
    tri = (np.arange(tm)[:, None] < np.arange(tm)[None, :]).astype(np.float32)
    return pl.pallas_call(
        _router_kernel,
        out_shape=(jax.ShapeDtypeStruct((8, n), jnp.int32),
                   jax.ShapeDtypeStruct((8, n), F32),
                   jax.ShapeDtypeStruct((BUCKET_ROWS, 128), F32)),
        grid=(n // tm,),
        in_specs=[pl.BlockSpec((tm, D_MODEL), lambda i: (i, 0)),
                  pl.BlockSpec((1, D_MODEL), lambda i: (0, 0)),
                  pl.BlockSpec((ROUTER_ROWS, D_MODEL), lambda i: (0, 0)),
                  pl.BlockSpec((ROUTER_ROWS, 1), lambda i: (0, 0)),
                  pl.BlockSpec((tm, tm), lambda i: (0, 0))],
        out_specs=(pl.BlockSpec((8, tm), lambda i: (0, i)),
                   pl.BlockSpec((8, tm), lambda i: (0, i)),
                   pl.BlockSpec((BUCKET_ROWS, 128), lambda i: (0, 0))),
        scratch_shapes=[pltpu.VMEM((BUCKET_ROWS, 1), F32)],
        compiler_params=_params("arbitrary"),
        name="moe_router",
    )(x2, gain.reshape(1, D_MODEL), wr, br.reshape(ROUTER_ROWS, 1), jnp.asarray(tri, dtype=BF16))


ROW_COPY_UNROLL = 8


def _start_row_copies(idx_ref, n_rows, copy_for_row):
    def body(g, carry):
        base = pl.multiple_of(g * ROW_COPY_UNROLL, ROW_COPY_UNROLL)
        for j in range(ROW_COPY_UNROLL):
            copy_for_row(base + j, idx_ref[0, 0, base + j]).start()
        return carry

    lax.fori_loop(0, n_rows // ROW_COPY_UNROLL, body, 0)


def _dispatch_kernel(tail_row_ref, tail_on_ref, pos_ref, x_ref, w_ref, xs_ref, buf_ref, sem, *, tile):
    tm = x_ref.shape[0]

    @pl.when(pl.program_id(0) == 0)
    def _():
        buf_ref[:tile, :] = jnp.zeros((tile, XS_WIDTH), F32)

        def zero_copy(k):
            return pltpu.make_async_copy(buf_ref.at[pl.ds(0, tile)],
                                         xs_ref.at[pl.ds(pl.multiple_of(tail_row_ref[k], tile), tile)], sem)

        for k in range(2 * N_BUCKETS):
            pl.when(tail_on_ref[k] > 0)(lambda k=k: zero_copy(k).start())
        for k in range(2 * N_BUCKETS):
            pl.when(tail_on_ref[k] > 0)(lambda k=k: zero_copy(k).wait())

    buf_ref[:, :D_MODEL] = x_ref[...]
    wpad = jnp.concatenate([w_ref[...], jnp.zeros((XS_EXTRA - w_ref.shape[0], tm), F32)], axis=0)
    buf_ref[:, D_MODEL:] = wpad.T
    _start_row_copies(pos_ref, tm, lambda r, p: pltpu.make_async_copy(
        buf_ref.at[pl.ds(r, 1)], xs_ref.at[pl.ds(p, 1)], sem))
    pltpu.make_async_copy(buf_ref, xs_ref.at[pl.ds(0, tm)], sem).wait()


def _dispatch(x2, wts, pos3, tails, n_rows_sorted, tile, tm):
    n = x2.shape[0]
    tail_row, tail_on = tails
    grid_spec = pltpu.PrefetchScalarGridSpec(
        num_scalar_prefetch=2,
        grid=(n // tm,),
        in_specs=[pl.BlockSpec((1, 1, tm), lambda i, *_: (i, 0, 0), memory_space=pltpu.SMEM),
                  pl.BlockSpec((tm, D_MODEL), lambda i, *_: (i, 0)),
                  pl.BlockSpec((8, tm), lambda i, *_: (0, i))],
        out_specs=pl.BlockSpec(memory_space=pl.ANY),
        scratch_shapes=[pltpu.VMEM((tm, XS_WIDTH), F32), pltpu.SemaphoreType.DMA],
    )
    return pl.pallas_call(
        functools.partial(_dispatch_kernel, tile=tile),
        out_shape=jax.ShapeDtypeStruct((n_rows_sorted, XS_WIDTH), F32),
        grid_spec=grid_spec,
        compiler_params=_params("arbitrary"),
        name="moe_dispatch",
    )(tail_row, tail_on, pos3, x2, wts)


def _experts_kernel(elo_ref, ehi_ref, nvalid_ref, xs_ref, g_ref, wg_lo, wu_lo, wg_hi, wu_hi,
                    wd_lo, wd_hi, o_ref):
    del elo_ref, ehi_ref
    t = pl.program_id(0)

    @pl.when(t < nvalid_ref[0])
    def _():
        xt = xs_ref[:, :D_MODEL]
        h = _rms(xt, g_ref[...]).astype(BF16)
        w_lo = xs_ref[:, D_MODEL:D_MODEL + 1]
        w_hi = xs_ref[:, D_MODEL + 1:D_MODEL + 2]

        def expert(wg, wu, wd, w):
            gate = jnp.dot(h, wg[0], preferred_element_type=F32)
            up = jnp.dot(h, wu[0], preferred_element_type=F32)
            hid = (_silu(gate) * up * w).astype(BF16)
            return jnp.dot(hid, wd[0], preferred_element_type=F32)

        o_ref[...] = xt + expert(wg_lo, wu_lo, wd_lo, w_lo) + expert(wg_hi, wu_hi, wd_hi, w_hi)

    @pl.when(t >= nvalid_ref[0])
    def _():
        o_ref[...] = jnp.zeros_like(o_ref)


def _experts(xs, gain, tables, wg, wu, wd, n_tiles, t):
    elo, ehi, nvalid = tables
    row = lambda i, elo, ehi, nv: (i, 0)
    row_in = lambda i, elo, ehi, nv: (jnp.minimum(i, nv[0] - 1), 0)
    lo3 = lambda i, elo, ehi, nv: (elo[i], 0, 0)
    hi3 = lambda i, elo, ehi, nv: (ehi[i], 0, 0)
    grid_spec = pltpu.PrefetchScalarGridSpec(
        num_scalar_prefetch=3,
        grid=(n_tiles,),
        in_specs=[pl.BlockSpec((t, XS_WIDTH), row_in),
                  pl.BlockSpec((1, D_MODEL), lambda i, *_: (0, 0)),
                  pl.BlockSpec((1, D_MODEL, D_EXPERT), lo3),
                  pl.BlockSpec((1, D_MODEL, D_EXPERT), lo3),
                  pl.BlockSpec((1, D_MODEL, D_EXPERT), hi3),
                  pl.BlockSpec((1, D_MODEL, D_EXPERT), hi3),
                  pl.BlockSpec((1, D_EXPERT, D_MODEL), lo3),
                  pl.BlockSpec((1, D_EXPERT, D_MODEL), hi3)],
        out_specs=pl.BlockSpec((t, D_MODEL), row),
    )
    return pl.pallas_call(
        _experts_kernel,
        out_shape=jax.ShapeDtypeStruct((xs.shape[0], D_MODEL), F32),
        grid_spec=grid_spec,
        compiler_params=_params("arbitrary"),
        name="moe_experts",
    )(elo, ehi, nvalid, xs, gain.reshape(1, D_MODEL), wg, wu, wg, wu, wd, wd)


def _combine_kernel(pos_ref, ys_ref, o_ref, sem):
    tm = o_ref.shape[0]
    _start_row_copies(pos_ref, tm, lambda r, p: pltpu.make_async_copy(
        ys_ref.at[pl.ds(p, 1)], o_ref.at[pl.ds(r, 1)], sem))
    pltpu.make_async_copy(ys_ref.at[pl.ds(0, tm)], o_ref, sem).wait()


def _combine(ys, pos3, n, tm):
    return pl.pallas_call(
        _combine_kernel,
        out_shape=jax.ShapeDtypeStruct((n, D_MODEL), F32),
        grid=(n // tm,),
        in_specs=[pl.BlockSpec((1, 1, tm), lambda i: (i, 0, 0), memory_space=pltpu.SMEM),
                  pl.BlockSpec(memory_space=pl.ANY)],
        out_specs=pl.BlockSpec((tm, D_MODEL), lambda i: (i, 0)),
        scratch_shapes=[pltpu.SemaphoreType.DMA],
        compiler_params=_params("arbitrary"),
        name="moe_combine",
    )(pos3, ys)


def _moe_tables(idx, cnt, n_tiles, t):
    bucket, rank = idx[0], idx[1]
    counts = cnt[:N_BUCKETS, 0].astype(jnp.int32)
    tiles_b = (counts + t - 1) // t
    tile_end = jnp.cumsum(tiles_b)
    pos = (tile_end - tiles_b)[bucket] * t + rank
    total = tile_end[-1]
    tt = jnp.arange(n_tiles, dtype=jnp.int32)
    valid = tt < total
    tb = jnp.sum((tile_end[None, :] <= jnp.where(valid, tt, total - 1)[:, None]).astype(jnp.int32), axis=1)
    tb = jnp.minimum(tb, N_BUCKETS - 1)
    pair_lo = jnp.asarray([0, 0, 0, 1, 1, 2], jnp.int32)
    pair_hi = jnp.asarray([1, 2, 3, 2, 3, 3], jnp.int32)
    base = (tb // N_PAIRS) * EXPERTS_PER_GROUP
    idle = total + jnp.arange(N_BUCKETS, dtype=jnp.int32)
    idle_on = idle < n_tiles
    tails = (jnp.concatenate([(tile_end - 1) * t, jnp.where(idle_on, idle, 0) * t]),
             jnp.concatenate([tiles_b > 0, idle_on]).astype(jnp.int32))
    return pos, tails, (base + pair_lo[tb % N_PAIRS], base + pair_hi[tb % N_PAIRS], total.reshape(1))


def _moe(x2, gain, w_rg, b_rg, w_re, b_re, wg, wu, wd, t=MOE_TILE, tm=1024):
    n = x2.shape[0]
    idx, wts, cnt = _router(x2, gain, w_rg, b_rg, w_re, b_re)
    n_tiles = n // t + N_BUCKETS
    pos, tails, tables = _moe_tables(idx, cnt, n_tiles, t)
    pos3 = pos.reshape(n // tm, 1, tm)
    xs = _dispatch(x2, wts, pos3, tails, n_tiles * t, t, tm)
    ys = _experts(xs, gain, tables, wg.astype(BF16), wu.astype(BF16), wd.astype(BF16), n_tiles, t)
    return _combine(ys, pos3, n, tm)


def _qkv_kernel(x_ref, g_ref, wq_ref, wkt_ref, wv_ref, seg_ref, kg_ref, q_ref, ss_ref, kt_ref, v_ref):
    h = _rms(x_ref[...], g_ref[...]).astype(BF16)
    q = jnp.dot(h, wq_ref[...], preferred_element_type=F32)
    q_ref[...] = q.astype(BF16)
    ss_ref[...] = jnp.dot((q * q).astype(BF16), seg_ref[...], preferred_element_type=F32)
    kt = lax.dot_general(wkt_ref[...], h, (((1,), (1,)), ((), ())), preferred_element_type=F32)
    tm = kt.shape[1]
    k3 = kt.reshape(N_KV_HEADS, HEAD_DIM, tm)
    ms = jnp.mean(k3 * k3, axis=1, keepdims=True)
    kn = k3 * lax.rsqrt(ms + EPS) * kg_ref[...].reshape(N_KV_HEADS, HEAD_DIM, 1)
    kt_ref[...] = kn.reshape(KV_WIDTH, tm).astype(BF16)
    v_ref[...] = jnp.dot(h, wv_ref[...], preferred_element_type=F32).astype(BF16)


def _qkv(x2, gain, wqkv, q_gain, k_gain, tm=512):
    n = x2.shape[0]
    qw = N_Q_HEADS * HEAD_DIM
    wq = wqkv[:, :qw].astype(BF16)
    wkt = wqkv[:, qw:qw + KV_WIDTH].T.astype(BF16)
    wv = wqkv[:, qw + KV_WIDTH:].astype(BF16)
    seg = (np.arange(qw)[:, None] // HEAD_DIM == np.arange(128)[None, :]).astype(np.float32)
    kg = jnp.tile((k_gain.astype(F32) * q_gain.astype(F32)), N_KV_HEADS).reshape(KV_WIDTH, 1)
    full = lambda r, c: pl.BlockSpec((r, c), lambda i: (0, 0))
    return pl.pallas_call(
        _qkv_kernel,
        out_shape=(jax.ShapeDtypeStruct((n, qw), BF16),
                   jax.ShapeDtypeStruct((n, 128), F32),
                   jax.ShapeDtypeStruct((KV_WIDTH, n), BF16),
                   jax.ShapeDtypeStruct((n, KV_WIDTH), BF16)),
        grid=(n // tm,),
        in_specs=[pl.BlockSpec((tm, D_MODEL), lambda i: (i, 0)), full(1, D_MODEL),
                  full(D_MODEL, qw), full(KV_WIDTH, D_MODEL), full(D_MODEL, KV_WIDTH),
                  full(qw, 128), full(KV_WIDTH, 1)],
        out_specs=(pl.BlockSpec((tm, qw), lambda i: (i, 0)),
                   pl.BlockSpec((tm, 128), lambda i: (i, 0)),
                   pl.BlockSpec((KV_WIDTH, tm), lambda i: (0, i)),
                   pl.BlockSpec((tm, KV_WIDTH), lambda i: (i, 0))),
        compiler_params=_params("parallel"),
        name="odd_qkv",
    )(x2, gain.reshape(1, D_MODEL), wq, wkt, wv, jnp.asarray(seg, dtype=BF16), kg)


def _attn_kernel(sink_ref, q_ref, ss_ref, ktp_ref, ktc_ref, vp_ref, vc_ref, x_ref, wo_ref, o_ref, att_ref):
    n = pl.program_id(1)
    blk = ATT_BLOCK
    kw = jnp.concatenate([ktp_ref[...], ktc_ref[...]], axis=1)
    vw = jnp.concatenate([vp_ref[...], vc_ref[...]], axis=0)
    qi = lax.broadcasted_iota(jnp.int32, (blk, 2 * blk), 0)
    ki = lax.broadcasted_iota(jnp.int32, (blk, 2 * blk), 1)
    dist = qi - ki + blk
    valid = (dist >= 0) & (dist < blk) & (n * blk + ki - blk >= 0)
    distf = dist.astype(F32)
    ss = ss_ref[...]
    for h in range(N_Q_HEADS):
        hk = h // GQA_GROUP
        slope = float(2.0 ** (-8.0 * (h + 1) / N_Q_HEADS))
        sink = sink_ref[h]
        rs = lax.rsqrt(ss[:, h:h + 1] * (1.0 / HEAD_DIM) + EPS) * (HEAD_DIM ** -0.5)
        qh = (q_ref[:, h * HEAD_DIM:(h + 1) * HEAD_DIM].astype(F32) * rs).astype(BF16)
        s = jnp.dot(qh, kw[hk * HEAD_DIM:(hk + 1) * HEAD_DIM, :], preferred_element_type=F32)
        s = jnp.where(valid, s - slope * distf, -jnp.inf)
        m = jnp.maximum(jnp.max(s, axis=-1, keepdims=True), sink)
        p = jnp.exp(s - m)
        den = jnp.sum(p, axis=-1, keepdims=True) + jnp.exp(sink - m)
        oh = jnp.dot(p.astype(BF16), vw[:, hk * HEAD_DIM:(hk + 1) * HEAD_DIM], preferred_element_type=F32)
        att_ref[:, h * HEAD_DIM:(h + 1) * HEAD_DIM] = (oh / den).astype(BF16)
    o_ref[...] = x_ref[...] + jnp.dot(att_ref[...], wo_ref[...], preferred_element_type=F32)


def _attn(q, ss, kt, v, x2, sinks, wo, bsz, seqlen):
    blk = ATT_BLOCK
    nb = seqlen // blk
    qw = N_Q_HEADS * HEAD_DIM
    cur = lambda b, n, s: (b * nb + n, 0)
    prev = lambda b, n, s: (b * nb + jnp.maximum(n - 1, 0), 0)
    cur_t = lambda b, n, s: (0, b * nb + n)
    prev_t = lambda b, n, s: (0, b * nb + jnp.maximum(n - 1, 0))
    grid_spec = pltpu.PrefetchScalarGridSpec(
        num_scalar_prefetch=1,
        grid=(bsz, nb),
        in_specs=[pl.BlockSpec((blk, qw), cur),
                  pl.BlockSpec((blk, 128), cur),
                  pl.BlockSpec((KV_WIDTH, blk), prev_t),
                  pl.BlockSpec((KV_WIDTH, blk), cur_t),
                  pl.BlockSpec((blk, KV_WIDTH), prev),
                  pl.BlockSpec((blk, KV_WIDTH), cur),
                  pl.BlockSpec((blk, D_MODEL), cur),
                  pl.BlockSpec((qw, D_MODEL), lambda b, n, s: (0, 0))],
        out_specs=pl.BlockSpec((blk, D_MODEL), cur),
        scratch_shapes=[pltpu.VMEM((blk, qw), BF16)],
    )
    return pl.pallas_call(
        _attn_kernel,
        out_shape=jax.ShapeDtypeStruct((bsz * seqlen, D_MODEL), F32),
        grid_spec=grid_spec,
        compiler_params=_params("parallel", "parallel"),
        name="odd_attn",
    )(sinks.astype(F32), q, ss, kt, kt, v, v, x2, wo.astype(BF16))


def kernel(x, even_mix_norm, even_in_proj, s5_lambda_re, s5_lambda_im, s5_log_step, s5_b_re, s5_b_im,
           s5_c_re, s5_c_im, s5_d, s5_glu_w, hgrn_lower_bounds, hgrn_o_norm, even_out_proj, odd_mix_norm,
           odd_wqkv, odd_q_norm, odd_k_norm, odd_sinks, odd_out_proj, moe_norm, moe_router_group,
           moe_router_group_bias, moe_router_expert, moe_router_expert_bias, moe_w_gate, moe_w_up,
           moe_w_down):
    bsz, seqlen, dm = x.shape
    n = bsz * seqlen
    x2 = x.reshape(n, dm)
    lower_bounds = jnp.cumsum(jax.nn.softmax(hgrn_lower_bounds.astype(F32), axis=0), axis=0)

    def moe(xx, layer):
        return _moe(xx, moe_norm[layer], moe_router_group[layer], moe_router_group_bias[layer],
                    moe_router_expert[layer], moe_router_expert_bias[layer],
                    moe_w_gate[layer], moe_w_up[layer], moe_w_down[layer])

    u, h4 = _inproj(x2, even_mix_norm[0], even_in_proj[0].astype(BF16))
    ops = _s5_operators(s5_lambda_re[0], s5_lambda_im[0], s5_log_step[0], s5_b_re[0], s5_b_im[0],
                        s5_c_re[0], s5_c_im[0])
    u_g = _s5_pack(u.reshape(bsz, seqlen, S5_WIDTH))
    y_g = _s5_scan(u_g, ops, bsz)
    ys = _s5_unpack(y_g, bsz, seqlen).reshape(n, S5_WIDTH)
    b_out = _hgrn(h4.reshape(bsz, seqlen, 4 * HG_WIDTH), lower_bounds[0], hgrn_o_norm[0], bsz, seqlen)
    x2 = _evenout(x2, ys, u, b_out.reshape(n, HG_WIDTH), s5_d[0], s5_glu_w[0], even_out_proj[0])
    x2 = moe(x2, 0)

    q, ss, kt, v = _qkv(x2, odd_mix_norm[0], odd_wqkv[0], odd_q_norm[0], odd_k_norm[0])
    x2 = _attn(q, ss, kt, v, x2, odd_sinks[0], odd_out_proj[0], bsz, seqlen)
    x2 = moe(x2, 1)
    return x2.reshape(bsz, seqlen, dm)
```

```python
import functools
import math

import jax
import jax.numpy as jnp
import numpy as np
from jax import lax
from jax.experimental import pallas as pl
from jax.experimental.pallas import tpu as pltpu

F32 = jnp.float32
BF16 = jnp.bfloat16
EPS = 1e-6

D_MODEL = 1024
S5_WIDTH = 512
S5_GROUP = 16
S5_GROUPS = 32
S5_STATE = 64
S5_CHUNK = 16
HG_WIDTH = 512
HG_HEAD_DIM = 128
HG_HEADS = 4
HG_CHUNK = 32
HEAD_DIM = 64
N_Q_HEADS = 16
N_KV_HEADS = 2
GQA_GROUP = 8
KV_WIDTH = N_KV_HEADS * HEAD_DIM
ATT_BLOCK = 128
N_GROUPS = 4
EXPERTS_PER_GROUP = 4
N_EXPERTS = 16
D_EXPERT = 256
ROUTER_ROWS = 32
N_PAIRS = 6
N_BUCKETS = N_GROUPS * N_PAIRS
BUCKET_ROWS = 32
MOE_TILE = 256
XS_EXTRA = 128
XS_WIDTH = D_MODEL + XS_EXTRA

VMEM_LIMIT_BYTES = 56 * 1024 * 1024


def _params(*semantics):
    return pltpu.CompilerParams(dimension_semantics=semantics, vmem_limit_bytes=VMEM_LIMIT_BYTES)


def _rms(xf, gain):
    return xf * lax.rsqrt(jnp.mean(xf * xf, axis=-1, keepdims=True) + EPS) * gain


def _nt_dot(w_t, h):
    return lax.dot_general(w_t, h, (((1,), (1,)), ((), ())), preferred_element_type=F32)


def _sigmoid(x):
    return 0.5 * jnp.tanh(0.5 * x) + 0.5


def _silu(x):
    return x * _sigmoid(x)


def _inproj_kernel(x_ref, g_ref, w_ref, u_ref, h4_ref):
    h = _rms(x_ref[...], g_ref[...]).astype(BF16)
    p = jnp.dot(h, w_ref[...], preferred_element_type=F32)
    u_ref[...] = p[:, :S5_WIDTH]
    h4_ref[...] = p[:, S5_WIDTH:]


def _inproj(x2, gain, w_bf16, tm=512):
    n = x2.shape[0]
    e_in = w_bf16.shape[1]
    return pl.pallas_call(
        _inproj_kernel,
        out_shape=(jax.ShapeDtypeStruct((n, S5_WIDTH), F32),
                   jax.ShapeDtypeStruct((n, e_in - S5_WIDTH), F32)),
        grid=(n // tm,),
        in_specs=[pl.BlockSpec((tm, D_MODEL), lambda i: (i, 0)),
                  pl.BlockSpec((1, D_MODEL), lambda i: (0, 0)),
                  pl.BlockSpec((D_MODEL, e_in), lambda i: (0, 0))],
        out_specs=(pl.BlockSpec((tm, S5_WIDTH), lambda i: (i, 0)),
                   pl.BlockSpec((tm, e_in - S5_WIDTH), lambda i: (i, 0))),
        compiler_params=_params("parallel"),
        name="even_inproj",
    )(x2, gain.reshape(1, D_MODEL), w_bf16)


def _s5_lagkernel_kernel(ca_ref, bb_ref, k_ref):
    k_ref[0] = jnp.dot(ca_ref[0], bb_ref[0], preferred_element_type=F32,
                       precision=lax.Precision.HIGHEST)


def _s5_lagkernel(ca, bb):
    g, rows, k = ca.shape
    return pl.pallas_call(
        _s5_lagkernel_kernel,
        out_shape=jax.ShapeDtypeStruct((g, rows, S5_GROUP), F32),
        grid=(g,),
        in_specs=[pl.BlockSpec((1, rows, k), lambda i: (i, 0, 0)),
                  pl.BlockSpec((1, k, S5_GROUP), lambda i: (i, 0, 0))],
        out_specs=pl.BlockSpec((1, rows, S5_GROUP), lambda i: (i, 0, 0)),
        compiler_params=_params("parallel"),
        name="s5_lag_kernel",
    )(ca, bb)


def _s5_operators(lam_re, lam_im, log_step, b_re, b_im, c_re, c_im):
    t = S5_CHUNK
    lr, li = lam_re.astype(F32), lam_im.astype(F32)
    step = jnp.exp(log_step.astype(F32))[:, None]
    mag = jnp.exp(lr * step)
    ab_re = mag * jnp.cos(li * step)
    ab_im = mag * jnp.sin(li * step)
    den = lr * lr + li * li
    nr, ni = ab_re - 1.0, ab_im
    z_re = (nr * lr + ni * li) / den
    z_im = (ni * lr - nr * li) / den
    br, bi = b_re.astype(F32), b_im.astype(F32)
    bb_re = z_re[..., None] * br - z_im[..., None] * bi
    bb_im = z_re[..., None] * bi + z_im[..., None] * br
    kk = jnp.arange(t + 1, dtype=F32)[:, None, None]
    pmag = jnp.exp(kk * (lr * step)[None])
    pw_re = pmag * jnp.cos(kk * (li * step)[None])
    pw_im = pmag * jnp.sin(kk * (li * step)[None])
    cr = jnp.transpose(c_re.astype(F32), (0, 1, 2))
    ci = c_im.astype(F32)
    ca_re = cr[None] * pw_re[:, :, None, :] - ci[None] * pw_im[:, :, None, :]
    ca_im = cr[None] * pw_im[:, :, None, :] + ci[None] * pw_re[:, :, None, :]
    g = lr.shape[0]
    ca_cat = jnp.concatenate([ca_re[:t], -ca_im[:t]], axis=-1)
    ca_cat = jnp.transpose(ca_cat, (1, 0, 2, 3)).reshape(g, t * S5_GROUP, 2 * S5_STATE)
    bb_cat = jnp.concatenate([bb_re, bb_im], axis=1)
    kern = _s5_lagkernel(ca_cat, bb_cat).reshape(g, t, S5_GROUP, S5_GROUP)
    s_idx = jnp.arange(t)[:, None]
    t_idx = jnp.arange(t)[None, :]
    lag = t_idx - s_idx
    kg = kern[:, jnp.clip(lag, 0, t - 1)]
    kg = jnp.where((lag >= 0)[None, :, :, None, None], kg, 0.0)
    mt = jnp.transpose(kg, (0, 1, 4, 2, 3)).reshape(g, t * S5_GROUP, t * S5_GROUP)
    pr = pw_re[:t][::-1]
    pi = pw_im[:t][::-1]
    sb_re = pr[:, :, :, None] * bb_re[None] - pi[:, :, :, None] * bb_im[None]
    sb_im = pr[:, :, :, None] * bb_im[None] + pi[:, :, :, None] * bb_re[None]
    sb_re = jnp.transpose(sb_re, (1, 0, 3, 2)).reshape(g, t * S5_GROUP, S5_STATE)
    sb_im = jnp.transpose(sb_im, (1, 0, 3, 2)).reshape(g, t * S5_GROUP, S5_STATE)
    cp_re = jnp.transpose(ca_re[1:], (1, 3, 0, 2)).reshape(g, S5_STATE, t * S5_GROUP)
    cp_im = jnp.transpose(-ca_im[1:], (1, 3, 0, 2)).reshape(g, S5_STATE, t * S5_GROUP)

    def pair_rows(m):
        m = m.reshape(g // 2, 2, m.shape[1], m.shape[2])
        z = jnp.zeros_like(m[:, 0])
        top = jnp.concatenate([m[:, 0], z], axis=2)
        bot = jnp.concatenate([z, m[:, 1]], axis=2)
        return jnp.concatenate([top, bot], axis=1)

    at_re = pw_re[t].reshape(g // 2, 1, 2 * S5_STATE)
    at_im = pw_im[t].reshape(g // 2, 1, 2 * S5_STATE)
    return (mt.astype(BF16), pair_rows(sb_re).astype(BF16), pair_rows(sb_im).astype(BF16),
            pair_rows(cp_re).astype(BF16), pair_rows(cp_im).astype(BF16), at_re, at_im)


PACK_TOKENS = 128


LANES = 128
GROUPS_PER_TILE = LANES // S5_GROUP
TOKENS_PER_TILE = LANES // S5_GROUP
CHUNK_HALVES = S5_CHUNK // TOKENS_PER_TILE
PACK_CHUNKS = PACK_TOKENS // S5_CHUNK


def _s5_pack_kernel(u_ref, o_ref, *, bsz):
    def body(b, carry):
        z = [u_ref[b, pl.ds(t, PACK_CHUNKS, stride=S5_CHUNK), :] for t in range(S5_CHUNK)]
        for g in range(GROUPS_PER_TILE):
            for j in range(CHUNK_HALVES):
                row = jnp.concatenate([zt[:, g * S5_GROUP:(g + 1) * S5_GROUP]
                                       for zt in z[j * TOKENS_PER_TILE:(j + 1) * TOKENS_PER_TILE]], axis=1)
                o_ref[g * CHUNK_HALVES + j, pl.ds(b, PACK_CHUNKS, stride=bsz), :] = row
        return carry

    for b in range(bsz):
        body(b, 0)


def _s5_pack(u3):
    bsz, seqlen, w = u3.shape
    rows = PACK_CHUNKS * bsz
    return pl.pallas_call(
        functools.partial(_s5_pack_kernel, bsz=bsz),
        out_shape=jax.ShapeDtypeStruct((S5_GROUPS * CHUNK_HALVES, seqlen // S5_CHUNK * bsz, LANES), F32),
        grid=(seqlen // PACK_TOKENS, w // LANES),
        in_specs=[pl.BlockSpec((bsz, PACK_TOKENS, LANES), lambda i, k: (0, i, k))],
        out_specs=pl.BlockSpec((GROUPS_PER_TILE * CHUNK_HALVES, rows, LANES), lambda i, k: (k, i, 0)),
        compiler_params=_params("parallel", "parallel"),
        name="s5_pack",
    )(u3)


def _s5_unpack_kernel(y_ref, o_ref, *, bsz):
    def body(b, carry):
        yg = [y_ref[r, pl.ds(b, PACK_CHUNKS, stride=bsz), :] for r in range(GROUPS_PER_TILE * CHUNK_HALVES)]
        for t in range(S5_CHUNK):
            j, tt = divmod(t, TOKENS_PER_TILE)
            row = jnp.concatenate([yg[g * CHUNK_HALVES + j][:, tt * S5_GROUP:(tt + 1) * S5_GROUP]
                                   for g in range(GROUPS_PER_TILE)], axis=1)
            o_ref[b, pl.ds(t, PACK_CHUNKS, stride=S5_CHUNK), :] = row
        return carry

    for b in range(bsz):
        body(b, 0)


def _s5_unpack(y_g, bsz, seqlen):
    rows = PACK_CHUNKS * bsz
    return pl.pallas_call(
        functools.partial(_s5_unpack_kernel, bsz=bsz),
        out_shape=jax.ShapeDtypeStruct((bsz, seqlen, S5_WIDTH), F32),
        grid=(seqlen // PACK_TOKENS, S5_WIDTH // LANES),
        in_specs=[pl.BlockSpec((GROUPS_PER_TILE * CHUNK_HALVES, rows, LANES), lambda i, k: (k, i, 0))],
        out_specs=pl.BlockSpec((bsz, PACK_TOKENS, LANES), lambda i, k: (0, i, k)),
        compiler_params=_params("parallel", "parallel"),
        name="s5_unpack",
    )(y_g)


def _s5_kernel(u_ref, mt_ref, wre_ref, wim_ref, cre_ref, cim_ref, atr_ref, ati_ref, y_ref,
               sre_ref, sim_ref, xre_ref, xim_ref, *, n_chunks, bsz):
    ucat = jnp.concatenate([u_ref[i] for i in range(2 * CHUNK_HALVES)], axis=1).astype(BF16)
    w = S5_CHUNK * S5_GROUP
    u0 = ucat[:, :w]
    u1 = ucat[:, w:]
    sre_ref[...] = jnp.dot(ucat, wre_ref[0], preferred_element_type=F32)
    sim_ref[...] = jnp.dot(ucat, wim_ref[0], preferred_element_type=F32)
    atr = jnp.broadcast_to(atr_ref[0], (bsz, 2 * S5_STATE))
    ati = jnp.broadcast_to(ati_ref[0], (bsz, 2 * S5_STATE))

    def body(c, carry):
        xr, xi = carry
        rows = pl.ds(pl.multiple_of(c * bsz, bsz), bsz)
        xre_ref[rows, :] = xr
        xim_ref[rows, :] = xi
        nxr = atr * xr - ati * xi + sre_ref[rows, :]
        nxi = atr * xi + ati * xr + sim_ref[rows, :]
        return nxr, nxi

    zero = jnp.zeros((bsz, 2 * S5_STATE), F32)
    lax.fori_loop(0, n_chunks, body, (zero, zero))
    ycar = (jnp.dot(xre_ref[...].astype(BF16), cre_ref[0], preferred_element_type=F32)
            + jnp.dot(xim_ref[...].astype(BF16), cim_ref[0], preferred_element_type=F32))
    y0 = jnp.dot(u0, mt_ref[0], preferred_element_type=F32) + ycar[:, :w]
    y1 = jnp.dot(u1, mt_ref[1], preferred_element_type=F32) + ycar[:, w:]
    for i in range(CHUNK_HALVES):
        y_ref[i] = y0[:, i * LANES:(i + 1) * LANES]
        y_ref[CHUNK_HALVES + i] = y1[:, i * LANES:(i + 1) * LANES]


def _s5_scan(u_g, ops, bsz):
    mt, wre, wim, cre, cim, atr, ati = ops
    tiles, r, _ = u_g.shape
    g = tiles // CHUNK_HALVES
    w = S5_CHUNK * S5_GROUP
    n_chunks = r // bsz
    p2 = 2 * S5_STATE
    kern = functools.partial(_s5_kernel, n_chunks=n_chunks, bsz=bsz)
    pair_tiles = pl.BlockSpec((2 * CHUNK_HALVES, r, LANES), lambda i: (i, 0, 0))
    return pl.pallas_call(
        kern,
        out_shape=jax.ShapeDtypeStruct((tiles, r, LANES), F32),
        grid=(g // 2,),
        in_specs=[pair_tiles,
                  pl.BlockSpec((2, w, w), lambda i: (i, 0, 0)),
                  pl.BlockSpec((1, 2 * w, p2), lambda i: (i, 0, 0)),
                  pl.BlockSpec((1, 2 * w, p2), lambda i: (i, 0, 0)),
                  pl.BlockSpec((1, p2, 2 * w), lambda i: (i, 0, 0)),
                  pl.BlockSpec((1, p2, 2 * w), lambda i: (i, 0, 0)),
                  pl.BlockSpec((1, 1, p2), lambda i: (i, 0, 0)),
                  pl.BlockSpec((1, 1, p2), lambda i: (i, 0, 0))],
        out_specs=pair_tiles,
        scratch_shapes=[pltpu.VMEM((r, p2), F32)] * 4,
        compiler_params=_params("parallel"),
        name="s5_scan",
    )(u_g, mt, wre, wim, cre, cim, atr, ati)


def _hgrn_kernel(q_ref, f_ref, i_ref, g_ref, lb_ref, og_ref, o_ref, st_ref, *, seqlen):
    c = HG_CHUNK
    nc = seqlen // c
    d = HG_HEAD_DIM
    lb = lb_ref[...]
    q = q_ref[0]
    qs = _silu(q)
    f = lb + (1.0 - lb) * _sigmoid(f_ref[0])
    lf = jnp.log(f)
    k = 1.0 - f
    v = i_ref[0]
    pos = lax.broadcasted_iota(jnp.int32, (seqlen, d), 0) % c
    b = lf
    sh = 1
    while sh < c:
        b = b + jnp.where(pos >= sh, pltpu.roll(b, sh, axis=0), 0.0)
        sh *= 2
    b3 = b.reshape(nc, c, d)
    b_last = b3[:, c - 1:c, :]
    b_ref = b3[:, c // 2 - 1:c // 2, :]
    qs3 = qs.reshape(nc, c, d)
    k3 = k.reshape(nc, c, d)
    v3 = v.reshape(nc, c, d).astype(BF16)
    qe_f = qs3 * jnp.exp(b3 - b_ref)
    ke_f = k3 * jnp.exp(b_ref - b3)
    qe = qe_f.astype(BF16)
    ke = ke_f.astype(BF16)
    kd = (ke_f * jnp.exp(b_last - b_ref)).astype(BF16)
    qb = (qe_f * jnp.exp(b_ref)).astype(BF16)
    scores = jnp.einsum('ctd,csd->cts', qe, ke, preferred_element_type=F32)
    ti = lax.broadcasted_iota(jnp.int32, (c, c), 0)
    si = lax.broadcasted_iota(jnp.int32, (c, c), 1)
    scores = jnp.where((ti >= si)[None], scores, 0.0)
    o_intra = jnp.einsum('cts,csv->ctv', scores.astype(BF16), v3, preferred_element_type=F32)
    ut = jnp.einsum('csv,csd->cvd', v3, kd, preferred_element_type=F32)
    decay = jnp.exp(b_last)
    state = jnp.zeros((d, d), F32)
    for ci in range(nc):
        st_ref[ci] = state.astype(BF16)
        state = decay[ci] * state + ut[ci]
    o_inter = jnp.einsum('ctd,cvd->ctv', qb, st_ref[...], preferred_element_type=F32)
    o = (o_intra + o_inter).reshape(seqlen, d)
    o = _rms(o, og_ref[...])
    o_ref[0] = o * _silu(g_ref[0])


def _hgrn(h4, lower_bound, o_gain, bsz, seqlen):
    d = HG_HEAD_DIM
    kern = functools.partial(_hgrn_kernel, seqlen=seqlen)

    def col(part):
        return pl.BlockSpec((1, seqlen, d), lambda b, h: (b, 0, part * HG_HEADS + h))

    return pl.pallas_call(
        kern,
        out_shape=jax.ShapeDtypeStruct((bsz, seqlen, HG_WIDTH), F32),
        grid=(bsz, HG_HEADS),
        in_specs=[col(0), col(1), col(2), col(3),
                  pl.BlockSpec((1, d), lambda b, h: (0, h)),
                  pl.BlockSpec((1, d), lambda b, h: (0, 0))],
        out_specs=pl.BlockSpec((1, seqlen, d), lambda b, h: (b, 0, h)),
        scratch_shapes=[pltpu.VMEM((seqlen // HG_CHUNK, d, d), BF16)],
        compiler_params=_params("parallel", "parallel"),
        name="hgrn2",
    )(h4, h4, h4, h4, lower_bound.reshape(1, HG_WIDTH), o_gain.reshape(1, d))


def _evenout_kernel(x_ref, ys_ref, u_ref, b_ref, d_ref, wglu_ref, wa_ref, wb_ref, o_ref):
    y = ys_ref[...] + d_ref[...] * u_ref[...]
    y = jax.nn.gelu(y)
    gate = _sigmoid(jnp.dot(y.astype(BF16), wglu_ref[...], preferred_element_type=F32))
    a = (y * gate).astype(BF16)
    mix = (jnp.dot(a, wa_ref[...], preferred_element_type=F32)
           + jnp.dot(b_ref[...].astype(BF16), wb_ref[...], preferred_element_type=F32))
    o_ref[...] = x_ref[...] + mix


def _evenout(x2, ys, u, b_out, d_skip, wglu, wout, tm=512):
    n = x2.shape[0]
    row = lambda w: pl.BlockSpec((tm, w), lambda i: (i, 0))
    full = lambda r, c: pl.BlockSpec((r, c), lambda i: (0, 0))
    return pl.pallas_call(
        _evenout_kernel,
        out_shape=jax.ShapeDtypeStruct((n, D_MODEL), F32),
        grid=(n // tm,),
        in_specs=[row(D_MODEL), row(S5_WIDTH), row(S5_WIDTH), row(HG_WIDTH),
                  full(1, S5_WIDTH), full(S5_WIDTH, S5_WIDTH),
                  full(S5_WIDTH, D_MODEL), full(HG_WIDTH, D_MODEL)],
        out_specs=row(D_MODEL),
        compiler_params=_params("parallel"),
        name="even_out",
    )(x2, ys, u, b_out, d_skip.reshape(1, S5_WIDTH), wglu.astype(BF16),
      wout[:S5_WIDTH].astype(BF16), wout[S5_WIDTH:].astype(BF16))


def _router_kernel(x_ref, g_ref, wr_ref, br_ref, tri_ref, idx_ref, wts_ref, cnt_ref, run_ref):
    @pl.when(pl.program_id(0) == 0)
    def _():
        run_ref[...] = jnp.zeros_like(run_ref)

    h = _rms(x_ref[...], g_ref[...])
    h_hi = h.astype(BF16)
    h_lo = (h - h_hi.astype(F32)).astype(BF16)
    both = _nt_dot(wr_ref[...], h_hi)
    lt = (both[:ROUTER_ROWS] + both[ROUTER_ROWS:] + _nt_dot(wr_ref[:ROUTER_ROWS, :], h_lo)
          + br_ref[...])
    gl = [lt[i:i + 1] for i in range(N_GROUPS)]
    el = [lt[N_GROUPS + i:N_GROUPS + i + 1] for i in range(N_EXPERTS)]
    gmax = jnp.maximum(jnp.maximum(gl[0], gl[1]), jnp.maximum(gl[2], gl[3]))
    gexp = [jnp.exp(v - gmax) for v in gl]
    gsum = gexp[0] + gexp[1] + gexp[2] + gexp[3]
    gprob = [v / gsum for v in gexp]
    g_gate = jnp.maximum(jnp.maximum(gprob[0], gprob[1]), jnp.maximum(gprob[2], gprob[3]))
    g_idx = jnp.where(gprob[0] == g_gate, 0,
                      jnp.where(gprob[1] == g_gate, 1, jnp.where(gprob[2] == g_gate, 2, 3)))
    es = []
    for j in range(EXPERTS_PER_GROUP):
        es.append(jnp.where(g_idx == 0, el[j],
                            jnp.where(g_idx == 1, el[4 + j],
                                      jnp.where(g_idx == 2, el[8 + j], el[12 + j]))))
    emax = jnp.maximum(jnp.maximum(es[0], es[1]), jnp.maximum(es[2], es[3]))
    eexp = [jnp.exp(v - emax) for v in es]
    esum = eexp[0] + eexp[1] + eexp[2] + eexp[3]
    ep = [v / esum for v in eexp]
    p1 = jnp.maximum(jnp.maximum(ep[0], ep[1]), jnp.maximum(ep[2], ep[3]))
    i1 = jnp.where(ep[0] == p1, 0, jnp.where(ep[1] == p1, 1, jnp.where(ep[2] == p1, 2, 3)))
    neg = jnp.float32(-1.0)
    rest = [jnp.where(i1 == j, neg, ep[j]) for j in range(EXPERTS_PER_GROUP)]
    p2 = jnp.maximum(jnp.maximum(rest[0], rest[1]), jnp.maximum(rest[2], rest[3]))
    i2 = jnp.where(rest[0] == p2, 0, jnp.where(rest[1] == p2, 1, jnp.where(rest[2] == p2, 2, 3)))
    wsum = p1 + p2
    w1 = g_gate * (p1 / wsum)
    w2 = g_gate * (p2 / wsum)
    first_lo = i1 < i2
    lo = jnp.where(first_lo, i1, i2)
    hi = jnp.where(first_lo, i2, i1)
    w_lo = jnp.where(first_lo, w1, w2)
    w_hi = jnp.where(first_lo, w2, w1)
    pair = jnp.where(lo == 0, 0, jnp.where(lo == 1, 3, 5)) + hi - lo - 1
    bucket = g_idx * N_PAIRS + pair
    tm = bucket.shape[1]
    rowid = lax.broadcasted_iota(jnp.int32, (BUCKET_ROWS, tm), 0)
    onehot = (rowid == bucket).astype(F32)
    prefix = jnp.dot(onehot.astype(BF16), tri_ref[...], preferred_element_type=F32)
    run = run_ref[...]
    rank = jnp.sum(onehot * (prefix + run), axis=0, keepdims=True)
    run = run + jnp.sum(onehot, axis=1, keepdims=True)
    run_ref[...] = run
    cnt_ref[...] = jnp.broadcast_to(run, cnt_ref.shape)
    idx_ref[...] = jnp.concatenate([bucket, rank.astype(jnp.int32), jnp.zeros((6, tm), jnp.int32)], axis=0)
    wts_ref[...] = jnp.concatenate([w_lo, w_hi, jnp.zeros((6, tm), F32)], axis=0)


def _router(x2, gain, w_rg, b_rg, w_re, b_re, tm=512):
    n = x2.shape[0]
    wr = jnp.concatenate([w_rg, w_re], axis=1).astype(F32).T
    wr = jnp.pad(wr, ((0, ROUTER_ROWS - wr.shape[0]), (0, 0)))
    wr_hi = wr.astype(BF16)
    wr = jnp.concatenate([wr_hi, (wr - wr_hi.astype(F32)).astype(BF16)], axis=0)
    br = jnp.pad(jnp.concatenate([b_rg, b_re]).astype(F32), (0, ROUTER_ROWS - N_GROUPS - N_EXPERTS))
    tri = (np.arange(tm)[:, None] < np.arange(tm)[None, :]).astype(np.float32)
    return pl.pallas_call(
        _router_kernel,
        out_shape=(jax.ShapeDtypeStruct((8, n), jnp.int32),
                   jax.ShapeDtypeStruct((8, n), F32),
                   jax.ShapeDtypeStruct((BUCKET_ROWS, 128), F32)),
        grid=(n // tm,),
        in_specs=[pl.BlockSpec((tm, D_MODEL), lambda i: (i, 0)),
                  pl.BlockSpec((1, D_MODEL), lambda i: (0, 0)),
                  pl.BlockSpec((2 * ROUTER_ROWS, D_MODEL), lambda i: (0, 0)),
                  pl.BlockSpec((ROUTER_ROWS, 1), lambda i: (0, 0)),
                  pl.BlockSpec((tm, tm), lambda i: (0, 0))],
        out_specs=(pl.BlockSpec((8, tm), lambda i: (0, i)),
                   pl.BlockSpec((8, tm), lambda i: (0, i)),
                   pl.BlockSpec((BUCKET_ROWS, 128), lambda i: (0, 0))),
        scratch_shapes=[pltpu.VMEM((BUCKET_ROWS, 1), F32)],
        compiler_params=_params("arbitrary"),
        name="moe_router",
    )(x2, gain.reshape(1, D_MODEL), wr, br.reshape(ROUTER_ROWS, 1), jnp.asarray(tri, dtype=BF16))


ROW_COPY_UNROLL = 8


def _start_row_copies(idx_ref, n_rows, copy_for_row):
    def body(g, carry):
        base = pl.multiple_of(g * ROW_COPY_UNROLL, ROW_COPY_UNROLL)
        for j in range(ROW_COPY_UNROLL):
            copy_for_row(base + j, idx_ref[0, 0, base + j]).start()
        return carry

    lax.fori_loop(0, n_rows // ROW_COPY_UNROLL, body, 0)


def _dispatch_kernel(tail_row_ref, tail_on_ref, pos_ref, x_ref, w_ref, xs_ref, buf_ref, sem, *, tile):
    tm = x_ref.shape[0]

    @pl.when(pl.program_id(0) == 0)
    def _():
        buf_ref[:tile, :] = jnp.zeros((tile, XS_WIDTH), F32)

        def zero_copy(k):
            return pltpu.make_async_copy(buf_ref.at[pl.ds(0, tile)],
                                         xs_ref.at[pl.ds(pl.multiple_of(tail_row_ref[k], tile), tile)], sem)

        for k in range(2 * N_BUCKETS):
            pl.when(tail_on_ref[k] > 0)(lambda k=k: zero_copy(k).start())
        for k in range(2 * N_BUCKETS):
            pl.when(tail_on_ref[k] > 0)(lambda k=k: zero_copy(k).wait())

    buf_ref[:, :D_MODEL] = x_ref[...]
    wpad = jnp.concatenate([w_ref[...], jnp.zeros((XS_EXTRA - w_ref.shape[0], tm), F32)], axis=0)
    buf_ref[:, D_MODEL:] = wpad.T
    _start_row_copies(pos_ref, tm, lambda r, p: pltpu.make_async_copy(
        buf_ref.at[pl.ds(r, 1)], xs_ref.at[pl.ds(p, 1)], sem))
    pltpu.make_async_copy(buf_ref, xs_ref.at[pl.ds(0, tm)], sem).wait()


def _dispatch(x2, wts, pos3, tails, n_rows_sorted, tile, tm):
    n = x2.shape[0]
    tail_row, tail_on = tails
    grid_spec = pltpu.PrefetchScalarGridSpec(
        num_scalar_prefetch=2,
        grid=(n // tm,),
        in_specs=[pl.BlockSpec((1, 1, tm), lambda i, *_: (i, 0, 0), memory_space=pltpu.SMEM),
                  pl.BlockSpec((tm, D_MODEL), lambda i, *_: (i, 0)),
                  pl.BlockSpec((8, tm), lambda i, *_: (0, i))],
        out_specs=pl.BlockSpec(memory_space=pl.ANY),
        scratch_shapes=[pltpu.VMEM((tm, XS_WIDTH), F32), pltpu.SemaphoreType.DMA],
    )
    return pl.pallas_call(
        functools.partial(_dispatch_kernel, tile=tile),
        out_shape=jax.ShapeDtypeStruct((n_rows_sorted, XS_WIDTH), F32),
        grid_spec=grid_spec,
        compiler_params=_params("arbitrary"),
        name="moe_dispatch",
    )(tail_row, tail_on, pos3, x2, wts)


def _experts_kernel(elo_ref, ehi_ref, nvalid_ref, xs_ref, g_ref, wg_lo, wu_lo, wg_hi, wu_hi,
                    wd_lo, wd_hi, o_ref):
    del elo_ref, ehi_ref
    t = pl.program_id(0)

    @pl.when(t < nvalid_ref[0])
    def _():
        xt = xs_ref[:, :D_MODEL]
        h = _rms(xt, g_ref[...]).astype(BF16)
        w_lo = xs_ref[:, D_MODEL:D_MODEL + 1]
        w_hi = xs_ref[:, D_MODEL + 1:D_MODEL + 2]

        def expert(wg, wu, wd, w):
            gate = jnp.dot(h, wg[0], preferred_element_type=F32)
            up = jnp.dot(h, wu[0], preferred_element_type=F32)
            hid = (_silu(gate) * up * w).astype(BF16)
            return jnp.dot(hid, wd[0], preferred_element_type=F32)

        o_ref[...] = xt + expert(wg_lo, wu_lo, wd_lo, w_lo) + expert(wg_hi, wu_hi, wd_hi, w_hi)

    @pl.when(t >= nvalid_ref[0])
    def _():
        o_ref[...] = jnp.zeros_like(o_ref)


def _experts(xs, gain, tables, wg, wu, wd, n_tiles, t):
    elo, ehi, nvalid = tables
    row = lambda i, elo, ehi, nv: (i, 0)
    row_in = lambda i, elo, ehi, nv: (jnp.minimum(i, nv[0] - 1), 0)
    lo3 = lambda i, elo, ehi, nv: (elo[i], 0, 0)
    hi3 = lambda i, elo, ehi, nv: (ehi[i], 0, 0)
    grid_spec = pltpu.PrefetchScalarGridSpec(
        num_scalar_prefetch=3,
        grid=(n_tiles,),
        in_specs=[pl.BlockSpec((t, XS_WIDTH), row_in),
                  pl.BlockSpec((1, D_MODEL), lambda i, *_: (0, 0)),
                  pl.BlockSpec((1, D_MODEL, D_EXPERT), lo3),
                  pl.BlockSpec((1, D_MODEL, D_EXPERT), lo3),
                  pl.BlockSpec((1, D_MODEL, D_EXPERT), hi3),
                  pl.BlockSpec((1, D_MODEL, D_EXPERT), hi3),
                  pl.BlockSpec((1, D_EXPERT, D_MODEL), lo3),
                  pl.BlockSpec((1, D_EXPERT, D_MODEL), hi3)],
        out_specs=pl.BlockSpec((t, D_MODEL), row),
    )
    return pl.pallas_call(
        _experts_kernel,
        out_shape=jax.ShapeDtypeStruct((xs.shape[0], D_MODEL), F32),
        grid_spec=grid_spec,
        compiler_params=_params("arbitrary"),
        name="moe_experts",
    )(elo, ehi, nvalid, xs, gain.reshape(1, D_MODEL), wg, wu, wg, wu, wd, wd)


def _combine_kernel(pos_ref, ys_ref, o_ref, sem):
    tm = o_ref.shape[0]
    _start_row_copies(pos_ref, tm, lambda r, p: pltpu.make_async_copy(
        ys_ref.at[pl.ds(p, 1)], o_ref.at[pl.ds(r, 1)], sem))
    pltpu.make_async_copy(ys_ref.at[pl.ds(0, tm)], o_ref, sem).wait()


def _combine(ys, pos3, n, tm):
    return pl.pallas_call(
        _combine_kernel,
        out_shape=jax.ShapeDtypeStruct((n, D_MODEL), F32),
        grid=(n // tm,),
        in_specs=[pl.BlockSpec((1, 1, tm), lambda i: (i, 0, 0), memory_space=pltpu.SMEM),
                  pl.BlockSpec(memory_space=pl.ANY)],
        out_specs=pl.BlockSpec((tm, D_MODEL), lambda i: (i, 0)),
        scratch_shapes=[pltpu.SemaphoreType.DMA],
        compiler_params=_params("arbitrary"),
        name="moe_combine",
    )(pos3, ys)


def _moe_tables(idx, cnt, n_tiles, t):
    bucket, rank = idx[0], idx[1]
    counts = cnt[:N_BUCKETS, 0].astype(jnp.int32)
    tiles_b = (counts + t - 1) // t
    tile_end = jnp.cumsum(tiles_b)
    pos = (tile_end - tiles_b)[bucket] * t + rank
    total = tile_end[-1]
    tt = jnp.arange(n_tiles, dtype=jnp.int32)
    valid = tt < total
    tb = jnp.sum((tile_end[None, :] <= jnp.where(valid, tt, total - 1)[:, None]).astype(jnp.int32), axis=1)
    tb = jnp.minimum(tb, N_BUCKETS - 1)
    pair_lo = jnp.asarray([0, 0, 0, 1, 1, 2], jnp.int32)
    pair_hi = jnp.asarray([1, 2, 3, 2, 3, 3], jnp.int32)
    base = (tb // N_PAIRS) * EXPERTS_PER_GROUP
    idle = total + jnp.arange(N_BUCKETS, dtype=jnp.int32)
    idle_on = idle < n_tiles
    tails = (jnp.concatenate([(tile_end - 1) * t, jnp.where(idle_on, idle, 0) * t]),
             jnp.concatenate([tiles_b > 0, idle_on]).astype(jnp.int32))
    return pos, tails, (base + pair_lo[tb % N_PAIRS], base + pair_hi[tb % N_PAIRS], total.reshape(1))


def _moe(x2, gain, w_rg, b_rg, w_re, b_re, wg, wu, wd, t=MOE_TILE, tm=1024):
    n = x2.shape[0]
    idx, wts, cnt = _router(x2, gain, w_rg, b_rg, w_re, b_re)
    n_tiles = n // t + N_BUCKETS
    pos, tails, tables = _moe_tables(idx, cnt, n_tiles, t)
    pos3 = pos.reshape(n // tm, 1, tm)
    xs = _dispatch(x2, wts, pos3, tails, n_tiles * t, t, tm)
    ys = _experts(xs, gain, tables, wg.astype(BF16), wu.astype(BF16), wd.astype(BF16), n_tiles, t)
    return _combine(ys, pos3, n, tm)


LOG2E = math.log2(math.e)
V_EXT = 2 * HEAD_DIM


def _head_norm_t(y_t, n_heads, scale):
    tm = y_t.shape[1]
    y3 = y_t.reshape(n_heads, HEAD_DIM, tm)
    ms = jnp.mean(y3 * y3, axis=1, keepdims=True)
    return y3 * (lax.rsqrt(ms + EPS) * scale)


def _qkv_kernel(x_ref, g_ref, wqt_ref, wkt_ref, wvt_ref, vone_ref, kg_ref, qt_ref, k_ref, vt_ref):
    h = _rms(x_ref[...], g_ref[...]).astype(BF16)
    tm = h.shape[0]
    qn = _head_norm_t(_nt_dot(wqt_ref[...], h), N_Q_HEADS, HEAD_DIM ** -0.5 * LOG2E)
    qt_ref[...] = qn.reshape(N_Q_HEADS * HEAD_DIM, tm).astype(BF16)
    kn = _head_norm_t(_nt_dot(wkt_ref[...], h), N_KV_HEADS, kg_ref[...].reshape(N_KV_HEADS, HEAD_DIM, 1))
    for hk in range(N_KV_HEADS):
        k_ref[hk] = kn[hk].T.astype(BF16)
    vt_ref[...] = (_nt_dot(wvt_ref[...], h) + vone_ref[...]).astype(BF16)


def _qkv(x2, gain, wqkv, q_gain, k_gain, tm=512):
    n = x2.shape[0]
    qw = N_Q_HEADS * HEAD_DIM
    wqt = wqkv[:, :qw].T.astype(BF16)
    wkt = wqkv[:, qw:qw + KV_WIDTH].T.astype(BF16)
    wvt = wqkv[:, qw + KV_WIDTH:].T.astype(BF16).reshape(N_KV_HEADS, HEAD_DIM, D_MODEL)
    wvt = jnp.pad(wvt, ((0, 0), (0, V_EXT - HEAD_DIM), (0, 0))).reshape(N_KV_HEADS * V_EXT, D_MODEL)
    vone = np.zeros((N_KV_HEADS * V_EXT, 1), np.float32)
    vone[HEAD_DIM::V_EXT, 0] = 1.0
    kg = jnp.tile((k_gain.astype(F32) * q_gain.astype(F32)), N_KV_HEADS).reshape(KV_WIDTH, 1)
    full = lambda r, c: pl.BlockSpec((r, c), lambda i: (0, 0))
    return pl.pallas_call(
        _qkv_kernel,
        out_shape=(jax.ShapeDtypeStruct((qw, n), BF16),
                   jax.ShapeDtypeStruct((N_KV_HEADS, n, HEAD_DIM), BF16),
                   jax.ShapeDtypeStruct((N_KV_HEADS * V_EXT, n), BF16)),
        grid=(n // tm,),
        in_specs=[pl.BlockSpec((tm, D_MODEL), lambda i: (i, 0)), full(1, D_MODEL),
                  full(qw, D_MODEL), full(KV_WIDTH, D_MODEL), full(N_KV_HEADS * V_EXT, D_MODEL),
                  full(N_KV_HEADS * V_EXT, 1), full(KV_WIDTH, 1)],
        out_specs=(pl.BlockSpec((qw, tm), lambda i: (0, i)),
                   pl.BlockSpec((N_KV_HEADS, tm, HEAD_DIM), lambda i: (0, i, 0)),
                   pl.BlockSpec((N_KV_HEADS * V_EXT, tm), lambda i: (0, i))),
        compiler_params=_params("parallel"),
        name="odd_qkv",
    )(x2, gain.reshape(1, D_MODEL), wqt, wkt, wvt, jnp.asarray(vone), kg)


def _attn_bias():
    blk = ATT_BLOCK
    qi = np.arange(blk)[None, :]
    ki = np.arange(2 * blk)[:, None]
    dist = qi - ki + blk
    band = (dist >= 0) & (dist < blk)
    slopes = 2.0 ** (-8.0 * np.arange(1, N_Q_HEADS + 1) / N_Q_HEADS)
    pen = -slopes[:, None, None] * dist[None].astype(np.float64) * LOG2E
    inner = np.where(band[None], pen, -np.inf)
    first = np.where((band & (ki >= blk))[None], pen, -np.inf)
    tab = np.stack([inner, first]).astype(np.float32)
    tab = tab.reshape(2, N_KV_HEADS, GQA_GROUP, 2 * blk, blk).transpose(0, 1, 3, 2, 4)
    return tab.reshape(2, N_KV_HEADS, 2 * blk, GQA_GROUP * blk)


def _attn_kernel(qt_ref, kp_ref, kc_ref, vtp_ref, vtc_ref, bias_ref, sink_ref, x_ref, wo_ref, o_ref):
    first = (pl.program_id(1) == 0).astype(jnp.int32)
    vt = jnp.concatenate([vtp_ref[...], vtc_ref[...]], axis=1)
    att_t = []
    for hk in range(N_KV_HEADS):
        keys = jnp.concatenate([kp_ref[hk], kc_ref[hk]], axis=0)
        q_t = jnp.concatenate([qt_ref[(hk * GQA_GROUP + g) * HEAD_DIM:(hk * GQA_GROUP + g + 1) * HEAD_DIM, :]
                               for g in range(GQA_GROUP)], axis=1)
        s = jnp.dot(keys, q_t, preferred_element_type=F32) + bias_ref[first, hk]
        sink = sink_ref[hk]
        m = jnp.maximum(jnp.max(s, axis=0, keepdims=True), sink)
        p = jnp.exp2(s - m).astype(BF16)
        pv = jnp.dot(vt[hk * V_EXT:(hk + 1) * V_EXT, :], p, preferred_element_type=F32)
        den = pv[HEAD_DIM:HEAD_DIM + 1, :] + jnp.exp2(sink - m)
        o_t = (pv[:HEAD_DIM, :] * (1.0 / den)).astype(BF16)
        att_t += [o_t[:, g * ATT_BLOCK:(g + 1) * ATT_BLOCK] for g in range(GQA_GROUP)]
    att_t = jnp.concatenate(att_t, axis=0)
    mix = lax.dot_general(att_t, wo_ref[...], (((0,), (0,)), ((), ())), preferred_element_type=F32)
    o_ref[...] = x_ref[...] + mix


def _attn(qt, k, vt, x2, sinks, wo, bsz, seqlen):
    blk = ATT_BLOCK
    nb = seqlen // blk
    qw = N_Q_HEADS * HEAD_DIM
    cols = GQA_GROUP * blk
    cur = lambda b, n: (b * nb + n, 0)
    cur_t = lambda b, n: (0, b * nb + n)
    prev_t = lambda b, n: (0, b * nb + jnp.maximum(n - 1, 0))
    sink_row = jnp.repeat(sinks.astype(F32) * LOG2E, blk).reshape(N_KV_HEADS, 1, cols)
    return pl.pallas_call(
        _attn_kernel,
        out_shape=jax.ShapeDtypeStruct((bsz * seqlen, D_MODEL), F32),
        grid=(bsz, nb),
        in_specs=[pl.BlockSpec((qw, blk), cur_t),
                  pl.BlockSpec((N_KV_HEADS, blk, HEAD_DIM), lambda b, n: (0, b * nb + jnp.maximum(n - 1, 0), 0)),
                  pl.BlockSpec((N_KV_HEADS, blk, HEAD_DIM), lambda b, n: (0, b * nb + n, 0)),
                  pl.BlockSpec((N_KV_HEADS * V_EXT, blk), prev_t),
                  pl.BlockSpec((N_KV_HEADS * V_EXT, blk), cur_t),
                  pl.BlockSpec((2, N_KV_HEADS, 2 * blk, cols), lambda b, n: (0, 0, 0, 0)),
                  pl.BlockSpec((N_KV_HEADS, 1, cols), lambda b, n: (0, 0, 0)),
                  pl.BlockSpec((blk, D_MODEL), cur),
                  pl.BlockSpec((qw, D_MODEL), lambda b, n: (0, 0))],
        out_specs=pl.BlockSpec((blk, D_MODEL), cur),
        compiler_params=_params("parallel", "parallel"),
        name="odd_attn",
    )(qt, k, k, vt, vt, jnp.asarray(_attn_bias()), sink_row, x2, wo.astype(BF16))


def kernel(x, even_mix_norm, even_in_proj, s5_lambda_re, s5_lambda_im, s5_log_step, s5_b_re, s5_b_im,
           s5_c_re, s5_c_im, s5_d, s5_glu_w, hgrn_lower_bounds, hgrn_o_norm, even_out_proj, odd_mix_norm,
           odd_wqkv, odd_q_norm, odd_k_norm, odd_sinks, odd_out_proj, moe_norm, moe_router_group,
           moe_router_group_bias, moe_router_expert, moe_router_expert_bias, moe_w_gate, moe_w_up,
           moe_w_down):
    bsz, seqlen, dm = x.shape
    n = bsz * seqlen
    x2 = x.reshape(n, dm)
    lower_bounds = jnp.cumsum(jax.nn.softmax(hgrn_lower_bounds.astype(F32), axis=0), axis=0)

    def moe(xx, layer):
        return _moe(xx, moe_norm[layer], moe_router_group[layer], moe_router_group_bias[layer],
                    moe_router_expert[layer], moe_router_expert_bias[layer],
                    moe_w_gate[layer], moe_w_up[layer], moe_w_down[layer])

    u, h4 = _inproj(x2, even_mix_norm[0], even_in_proj[0].astype(BF16))
    ops = _s5_operators(s5_lambda_re[0], s5_lambda_im[0], s5_log_step[0], s5_b_re[0], s5_b_im[0],
                        s5_c_re[0], s5_c_im[0])
    u_g = _s5_pack(u.reshape(bsz, seqlen, S5_WIDTH))
    y_g = _s5_scan(u_g, ops, bsz)
    ys = _s5_unpack(y_g, bsz, seqlen).reshape(n, S5_WIDTH)
    b_out = _hgrn(h4.reshape(bsz, seqlen, 4 * HG_WIDTH), lower_bounds[0], hgrn_o_norm[0], bsz, seqlen)
    x2 = _evenout(x2, ys, u, b_out.reshape(n, HG_WIDTH), s5_d[0], s5_glu_w[0], even_out_proj[0])
    x2 = moe(x2, 0)

    q, kt, v = _qkv(x2, odd_mix_norm[0], odd_wqkv[0], odd_q_norm[0], odd_k_norm[0])
    x2 = _attn(q, kt, v, x2, odd_sinks[0], odd_out_proj[0], bsz, seqlen)
    x2 = moe(x2, 1)
    return x2.reshape(bsz, seqlen, dm)
```

```python
import functools
import math

import jax
import jax.numpy as jnp
import numpy as np
from jax import lax
from jax.experimental import pallas as pl
from jax.experimental.pallas import tpu as pltpu

F32 = jnp.float32
BF16 = jnp.bfloat16
EPS = 1e-6

D_MODEL = 1024
S5_WIDTH = 512
S5_GROUP = 16
S5_GROUPS = 32
S5_STATE = 64
S5_CHUNK = 16
HG_WIDTH = 512
HG_HEAD_DIM = 128
HG_HEADS = 4
HG_CHUNK = 32
HEAD_DIM = 64
N_Q_HEADS = 16
N_KV_HEADS = 2
GQA_GROUP = 8
KV_WIDTH = N_KV_HEADS * HEAD_DIM
ATT_BLOCK = 128
N_GROUPS = 4
EXPERTS_PER_GROUP = 4
N_EXPERTS = 16
D_EXPERT = 256
ROUTER_ROWS = 32
N_PAIRS = 6
N_BUCKETS = N_GROUPS * N_PAIRS
BUCKET_ROWS = 32
MOE_TILE = 256
ROW_TILE = (8, 128)

VMEM_LIMIT_BYTES = 56 * 1024 * 1024


def _params(*semantics):
    return pltpu.CompilerParams(dimension_semantics=semantics, vmem_limit_bytes=VMEM_LIMIT_BYTES)


def _rms(xf, gain):
    return xf * lax.rsqrt(jnp.mean(xf * xf, axis=-1, keepdims=True) + EPS) * gain


def _nt_dot(w_t, h):
    return lax.dot_general(w_t, h, (((1,), (1,)), ((), ())), preferred_element_type=F32)


def _sigmoid(x):
    return 0.5 * jnp.tanh(0.5 * x) + 0.5


def _silu(x):
    return x * _sigmoid(x)


def _inproj_kernel(x_ref, g_ref, w_ref, u_ref, h4_ref):
    h = _rms(x_ref[...], g_ref[...]).astype(BF16)
    p = jnp.dot(h, w_ref[...], preferred_element_type=F32)
    u_ref[...] = p[:, :S5_WIDTH]
    h4_ref[...] = p[:, S5_WIDTH:]


def _inproj(x2, gain, w_bf16, tm=512):
    n = x2.shape[0]
    e_in = w_bf16.shape[1]
    return pl.pallas_call(
        _inproj_kernel,
        out_shape=(jax.ShapeDtypeStruct((n, S5_WIDTH), F32),
                   jax.ShapeDtypeStruct((n, e_in - S5_WIDTH), F32)),
        grid=(n // tm,),
        in_specs=[pl.BlockSpec((tm, D_MODEL), lambda i: (i, 0)),
                  pl.BlockSpec((1, D_MODEL), lambda i: (0, 0)),
                  pl.BlockSpec((D_MODEL, e_in), lambda i: (0, 0))],
        out_specs=(pl.BlockSpec((tm, S5_WIDTH), lambda i: (i, 0)),
                   pl.BlockSpec((tm, e_in - S5_WIDTH), lambda i: (i, 0))),
        compiler_params=_params("parallel"),
        name="even_inproj",
    )(x2, gain.reshape(1, D_MODEL), w_bf16)


def _s5_lagkernel_kernel(ca_ref, bb_ref, k_ref):
    k_ref[0] = jnp.dot(ca_ref[0], bb_ref[0], preferred_element_type=F32,
                       precision=lax.Precision.HIGHEST)


def _s5_lagkernel(ca, bb):
    g, rows, k = ca.shape
    return pl.pallas_call(
        _s5_lagkernel_kernel,
        out_shape=jax.ShapeDtypeStruct((g, rows, S5_GROUP), F32),
        grid=(g,),
        in_specs=[pl.BlockSpec((1, rows, k), lambda i: (i, 0, 0)),
                  pl.BlockSpec((1, k, S5_GROUP), lambda i: (i, 0, 0))],
        out_specs=pl.BlockSpec((1, rows, S5_GROUP), lambda i: (i, 0, 0)),
        compiler_params=_params("parallel"),
        name="s5_lag_kernel",
    )(ca, bb)


def _s5_operators(lam_re, lam_im, log_step, b_re, b_im, c_re, c_im):
    t = S5_CHUNK
    lr, li = lam_re.astype(F32), lam_im.astype(F32)
    step = jnp.exp(log_step.astype(F32))[:, None]
    mag = jnp.exp(lr * step)
    ab_re = mag * jnp.cos(li * step)
    ab_im = mag * jnp.sin(li * step)
    den = lr * lr + li * li
    nr, ni = ab_re - 1.0, ab_im
    z_re = (nr * lr + ni * li) / den
    z_im = (ni * lr - nr * li) / den
    br, bi = b_re.astype(F32), b_im.astype(F32)
    bb_re = z_re[..., None] * br - z_im[..., None] * bi
    bb_im = z_re[..., None] * bi + z_im[..., None] * br
    kk = jnp.arange(t + 1, dtype=F32)[:, None, None]
    pmag = jnp.exp(kk * (lr * step)[None])
    pw_re = pmag * jnp.cos(kk * (li * step)[None])
    pw_im = pmag * jnp.sin(kk * (li * step)[None])
    cr = jnp.transpose(c_re.astype(F32), (0, 1, 2))
    ci = c_im.astype(F32)
    ca_re = cr[None] * pw_re[:, :, None, :] - ci[None] * pw_im[:, :, None, :]
    ca_im = cr[None] * pw_im[:, :, None, :] + ci[None] * pw_re[:, :, None, :]
    g = lr.shape[0]
    ca_cat = jnp.concatenate([ca_re[:t], -ca_im[:t]], axis=-1)
    ca_cat = jnp.transpose(ca_cat, (1, 0, 2, 3)).reshape(g, t * S5_GROUP, 2 * S5_STATE)
    bb_cat = jnp.concatenate([bb_re, bb_im], axis=1)
    kern = _s5_lagkernel(ca_cat, bb_cat).reshape(g, t, S5_GROUP, S5_GROUP)
    s_idx = jnp.arange(t)[:, None]
    t_idx = jnp.arange(t)[None, :]
    lag = t_idx - s_idx
    kg = kern[:, jnp.clip(lag, 0, t - 1)]
    kg = jnp.where((lag >= 0)[None, :, :, None, None], kg, 0.0)
    mt = jnp.transpose(kg, (0, 1, 4, 2, 3)).reshape(g, t * S5_GROUP, t * S5_GROUP)
    pr = pw_re[:t][::-1]
    pi = pw_im[:t][::-1]
    sb_re = pr[:, :, :, None] * bb_re[None] - pi[:, :, :, None] * bb_im[None]
    sb_im = pr[:, :, :, None] * bb_im[None] + pi[:, :, :, None] * bb_re[None]
    sb_re = jnp.transpose(sb_re, (1, 0, 3, 2)).reshape(g, t * S5_GROUP, S5_STATE)
    sb_im = jnp.transpose(sb_im, (1, 0, 3, 2)).reshape(g, t * S5_GROUP, S5_STATE)
    cp_re = jnp.transpose(ca_re[1:], (1, 3, 0, 2)).reshape(g, S5_STATE, t * S5_GROUP)
    cp_im = jnp.transpose(-ca_im[1:], (1, 3, 0, 2)).reshape(g, S5_STATE, t * S5_GROUP)

    def pair_rows(m):
        m = m.reshape(g // 2, 2, m.shape[1], m.shape[2])
        z = jnp.zeros_like(m[:, 0])
        top = jnp.concatenate([m[:, 0], z], axis=2)
        bot = jnp.concatenate([z, m[:, 1]], axis=2)
        return jnp.concatenate([top, bot], axis=1)

    at_re = pw_re[t].reshape(g // 2, 1, 2 * S5_STATE)
    at_im = pw_im[t].reshape(g // 2, 1, 2 * S5_STATE)
    return (mt.astype(BF16), pair_rows(sb_re).astype(BF16), pair_rows(sb_im).astype(BF16),
            pair_rows(cp_re).astype(BF16), pair_rows(cp_im).astype(BF16), at_re, at_im)


PACK_TOKENS = 128


LANES = 128
GROUPS_PER_TILE = LANES // S5_GROUP
TOKENS_PER_TILE = LANES // S5_GROUP
CHUNK_HALVES = S5_CHUNK // TOKENS_PER_TILE
PACK_CHUNKS = PACK_TOKENS // S5_CHUNK


def _s5_pack_kernel(u_ref, o_ref, *, bsz):
    def body(b, carry):
        z = [u_ref[b, pl.ds(t, PACK_CHUNKS, stride=S5_CHUNK), :] for t in range(S5_CHUNK)]
        for g in range(GROUPS_PER_TILE):
            for j in range(CHUNK_HALVES):
                row = jnp.concatenate([zt[:, g * S5_GROUP:(g + 1) * S5_GROUP]
                                       for zt in z[j * TOKENS_PER_TILE:(j + 1) * TOKENS_PER_TILE]], axis=1)
                o_ref[g * CHUNK_HALVES + j, pl.ds(b, PACK_CHUNKS, stride=bsz), :] = row
        return carry

    for b in range(bsz):
        body(b, 0)


def _s5_pack(u3):
    bsz, seqlen, w = u3.shape
    rows = PACK_CHUNKS * bsz
    return pl.pallas_call(
        functools.partial(_s5_pack_kernel, bsz=bsz),
        out_shape=jax.ShapeDtypeStruct((S5_GROUPS * CHUNK_HALVES, seqlen // S5_CHUNK * bsz, LANES), F32),
        grid=(seqlen // PACK_TOKENS, w // LANES),
        in_specs=[pl.BlockSpec((bsz, PACK_TOKENS, LANES), lambda i, k: (0, i, k))],
        out_specs=pl.BlockSpec((GROUPS_PER_TILE * CHUNK_HALVES, rows, LANES), lambda i, k: (k, i, 0)),
        compiler_params=_params("parallel", "parallel"),
        name="s5_pack",
    )(u3)


def _s5_unpack_kernel(y_ref, o_ref, *, bsz):
    def body(b, carry):
        yg = [y_ref[r, pl.ds(b, PACK_CHUNKS, stride=bsz), :] for r in range(GROUPS_PER_TILE * CHUNK_HALVES)]
        for t in range(S5_CHUNK):
            j, tt = divmod(t, TOKENS_PER_TILE)
            row = jnp.concatenate([yg[g * CHUNK_HALVES + j][:, tt * S5_GROUP:(tt + 1) * S5_GROUP]
                                   for g in range(GROUPS_PER_TILE)], axis=1)
            o_ref[b, pl.ds(t, PACK_CHUNKS, stride=S5_CHUNK), :] = row
        return carry

    for b in range(bsz):
        body(b, 0)


def _s5_unpack(y_g, bsz, seqlen):
    rows = PACK_CHUNKS * bsz
    return pl.pallas_call(
        functools.partial(_s5_unpack_kernel, bsz=bsz),
        out_shape=jax.ShapeDtypeStruct((bsz, seqlen, S5_WIDTH), F32),
        grid=(seqlen // PACK_TOKENS, S5_WIDTH // LANES),
        in_specs=[pl.BlockSpec((GROUPS_PER_TILE * CHUNK_HALVES, rows, LANES), lambda i, k: (k, i, 0))],
        out_specs=pl.BlockSpec((bsz, PACK_TOKENS, LANES), lambda i, k: (0, i, k)),
        compiler_params=_params("parallel", "parallel"),
        name="s5_unpack",
    )(y_g)


def _s5_kernel(u_ref, mt_ref, wre_ref, wim_ref, cre_ref, cim_ref, atr_ref, ati_ref, y_ref,
               sre_ref, sim_ref, xre_ref, xim_ref, *, n_chunks, bsz):
    ucat = jnp.concatenate([u_ref[i] for i in range(2 * CHUNK_HALVES)], axis=1).astype(BF16)
    w = S5_CHUNK * S5_GROUP
    u0 = ucat[:, :w]
    u1 = ucat[:, w:]
    sre_ref[...] = jnp.dot(ucat, wre_ref[0], preferred_element_type=F32)
    sim_ref[...] = jnp.dot(ucat, wim_ref[0], preferred_element_type=F32)
    atr = jnp.broadcast_to(atr_ref[0], (bsz, 2 * S5_STATE))
    ati = jnp.broadcast_to(ati_ref[0], (bsz, 2 * S5_STATE))

    def body(c, carry):
        xr, xi = carry
        rows = pl.ds(pl.multiple_of(c * bsz, bsz), bsz)
        xre_ref[rows, :] = xr
        xim_ref[rows, :] = xi
        nxr = atr * xr - ati * xi + sre_ref[rows, :]
        nxi = atr * xi + ati * xr + sim_ref[rows, :]
        return nxr, nxi

    zero = jnp.zeros((bsz, 2 * S5_STATE), F32)
    lax.fori_loop(0, n_chunks, body, (zero, zero))
    ycar = (jnp.dot(xre_ref[...].astype(BF16), cre_ref[0], preferred_element_type=F32)
            + jnp.dot(xim_ref[...].astype(BF16), cim_ref[0], preferred_element_type=F32))
    y0 = jnp.dot(u0, mt_ref[0], preferred_element_type=F32) + ycar[:, :w]
    y1 = jnp.dot(u1, mt_ref[1], preferred_element_type=F32) + ycar[:, w:]
    for i in range(CHUNK_HALVES):
        y_ref[i] = y0[:, i * LANES:(i + 1) * LANES]
        y_ref[CHUNK_HALVES + i] = y1[:, i * LANES:(i + 1) * LANES]


def _s5_scan(u_g, ops, bsz):
    mt, wre, wim, cre, cim, atr, ati = ops
    tiles, r, _ = u_g.shape
    g = tiles // CHUNK_HALVES
    w = S5_CHUNK * S5_GROUP
    n_chunks = r // bsz
    p2 = 2 * S5_STATE
    kern = functools.partial(_s5_kernel, n_chunks=n_chunks, bsz=bsz)
    pair_tiles = pl.BlockSpec((2 * CHUNK_HALVES, r, LANES), lambda i: (i, 0, 0))
    return pl.pallas_call(
        kern,
        out_shape=jax.ShapeDtypeStruct((tiles, r, LANES), F32),
        grid=(g // 2,),
        in_specs=[pair_tiles,
                  pl.BlockSpec((2, w, w), lambda i: (i, 0, 0)),
                  pl.BlockSpec((1, 2 * w, p2), lambda i: (i, 0, 0)),
                  pl.BlockSpec((1, 2 * w, p2), lambda i: (i, 0, 0)),
                  pl.BlockSpec((1, p2, 2 * w), lambda i: (i, 0, 0)),
                  pl.BlockSpec((1, p2, 2 * w), lambda i: (i, 0, 0)),
                  pl.BlockSpec((1, 1, p2), lambda i: (i, 0, 0)),
                  pl.BlockSpec((1, 1, p2), lambda i: (i, 0, 0))],
        out_specs=pair_tiles,
        scratch_shapes=[pltpu.VMEM((r, p2), F32)] * 4,
        compiler_params=_params("parallel"),
        name="s5_scan",
    )(u_g, mt, wre, wim, cre, cim, atr, ati)


def _hgrn_kernel(q_ref, f_ref, i_ref, g_ref, lb_ref, og_ref, o_ref, st_ref, *, seqlen):
    c = HG_CHUNK
    nc = seqlen // c
    d = HG_HEAD_DIM
    lb = lb_ref[...]
    q = q_ref[0]
    qs = _silu(q)
    f = lb + (1.0 - lb) * _sigmoid(f_ref[0])
    lf = jnp.log(f)
    k = 1.0 - f
    v = i_ref[0]
    pos = lax.broadcasted_iota(jnp.int32, (seqlen, d), 0) % c
    b = lf
    sh = 1
    while sh < c:
        b = b + jnp.where(pos >= sh, pltpu.roll(b, sh, axis=0), 0.0)
        sh *= 2
    b3 = b.reshape(nc, c, d)
    b_last = b3[:, c - 1:c, :]
    b_ref = b3[:, c // 2 - 1:c // 2, :]
    qs3 = qs.reshape(nc, c, d)
    k3 = k.reshape(nc, c, d)
    v3 = v.reshape(nc, c, d).astype(BF16)
    qe_f = qs3 * jnp.exp(b3 - b_ref)
    ke_f = k3 * jnp.exp(b_ref - b3)
    qe = qe_f.astype(BF16)
    ke = ke_f.astype(BF16)
    kd = (ke_f * jnp.exp(b_last - b_ref)).astype(BF16)
    qb = (qe_f * jnp.exp(b_ref)).astype(BF16)
    scores = jnp.einsum('ctd,csd->cts', qe, ke, preferred_element_type=F32)
    ti = lax.broadcasted_iota(jnp.int32, (c, c), 0)
    si = lax.broadcasted_iota(jnp.int32, (c, c), 1)
    scores = jnp.where((ti >= si)[None], scores, 0.0)
    o_intra = jnp.einsum('cts,csv->ctv', scores.astype(BF16), v3, preferred_element_type=F32)
    ut = jnp.einsum('csv,csd->cvd', v3, kd, preferred_element_type=F32)
    decay = jnp.exp(b_last)
    state = jnp.zeros((d, d), F32)
    for ci in range(nc):
        st_ref[ci] = state.astype(BF16)
        state = decay[ci] * state + ut[ci]
    o_inter = jnp.einsum('ctd,cvd->ctv', qb, st_ref[...], preferred_element_type=F32)
    o = (o_intra + o_inter).reshape(seqlen, d)
    o = _rms(o, og_ref[...])
    o_ref[0] = o * _silu(g_ref[0])


def _hgrn(h4, lower_bound, o_gain, bsz, seqlen):
    d = HG_HEAD_DIM
    kern = functools.partial(_hgrn_kernel, seqlen=seqlen)

    def col(part):
        return pl.BlockSpec((1, seqlen, d), lambda b, h: (b, 0, part * HG_HEADS + h))

    return pl.pallas_call(
        kern,
        out_shape=jax.ShapeDtypeStruct((bsz, seqlen, HG_WIDTH), F32),
        grid=(bsz, HG_HEADS),
        in_specs=[col(0), col(1), col(2), col(3),
                  pl.BlockSpec((1, d), lambda b, h: (0, h)),
                  pl.BlockSpec((1, d), lambda b, h: (0, 0))],
        out_specs=pl.BlockSpec((1, seqlen, d), lambda b, h: (b, 0, h)),
        scratch_shapes=[pltpu.VMEM((seqlen // HG_CHUNK, d, d), BF16)],
        compiler_params=_params("parallel", "parallel"),
        name="hgrn2",
    )(h4, h4, h4, h4, lower_bound.reshape(1, HG_WIDTH), o_gain.reshape(1, d))


def _evenout_kernel(x_ref, ys_ref, u_ref, b_ref, d_ref, wglu_ref, wa_ref, wb_ref, o_ref):
    y = ys_ref[...] + d_ref[...] * u_ref[...]
    y = jax.nn.gelu(y)
    gate = _sigmoid(jnp.dot(y.astype(BF16), wglu_ref[...], preferred_element_type=F32))
    a = (y * gate).astype(BF16)
    mix = (jnp.dot(a, wa_ref[...], preferred_element_type=F32)
           + jnp.dot(b_ref[...].astype(BF16), wb_ref[...], preferred_element_type=F32))
    o_ref[...] = x_ref[...] + mix


def _evenout(x2, ys, u, b_out, d_skip, wglu, wout, tm=512):
    n = x2.shape[0]
    row = lambda w: pl.BlockSpec((tm, w), lambda i: (i, 0))
    full = lambda r, c: pl.BlockSpec((r, c), lambda i: (0, 0))
    return pl.pallas_call(
        _evenout_kernel,
        out_shape=jax.ShapeDtypeStruct((n, D_MODEL), F32),
        grid=(n // tm,),
        in_specs=[row(D_MODEL), row(S5_WIDTH), row(S5_WIDTH), row(HG_WIDTH),
                  full(1, S5_WIDTH), full(S5_WIDTH, S5_WIDTH),
                  full(S5_WIDTH, D_MODEL), full(HG_WIDTH, D_MODEL)],
        out_specs=row(D_MODEL),
        compiler_params=_params("parallel"),
        name="even_out",
    )(x2, ys, u, b_out, d_skip.reshape(1, S5_WIDTH), wglu.astype(BF16),
      wout[:S5_WIDTH].astype(BF16), wout[S5_WIDTH:].astype(BF16))


def _router_kernel(x_ref, g_ref, wr_ref, br_ref, tri_ref, idx_ref, cnt_ref, run_ref):
    @pl.when(pl.program_id(0) == 0)
    def _():
        run_ref[...] = jnp.zeros_like(run_ref)

    h = _rms(x_ref[...], g_ref[...])
    h_hi = h.astype(BF16)
    h_lo = (h - h_hi.astype(F32)).astype(BF16)
    both = _nt_dot(wr_ref[...], h_hi)
    lt = (both[:ROUTER_ROWS] + both[ROUTER_ROWS:] + _nt_dot(wr_ref[:ROUTER_ROWS, :], h_lo)
          + br_ref[...])
    gl = [lt[i:i + 1] for i in range(N_GROUPS)]
    el = [lt[N_GROUPS + i:N_GROUPS + i + 1] for i in range(N_EXPERTS)]
    gmax = jnp.maximum(jnp.maximum(gl[0], gl[1]), jnp.maximum(gl[2], gl[3]))
    gexp = [jnp.exp(v - gmax) for v in gl]
    gsum = gexp[0] + gexp[1] + gexp[2] + gexp[3]
    gprob = [v / gsum for v in gexp]
    g_top = jnp.maximum(jnp.maximum(gprob[0], gprob[1]), jnp.maximum(gprob[2], gprob[3]))
    g_idx = jnp.where(gprob[0] == g_top, 0,
                      jnp.where(gprob[1] == g_top, 1, jnp.where(gprob[2] == g_top, 2, 3)))
    es = []
    for j in range(EXPERTS_PER_GROUP):
        es.append(jnp.where(g_idx == 0, el[j],
                            jnp.where(g_idx == 1, el[4 + j],
                                      jnp.where(g_idx == 2, el[8 + j], el[12 + j]))))
    emax = jnp.maximum(jnp.maximum(es[0], es[1]), jnp.maximum(es[2], es[3]))
    eexp = [jnp.exp(v - emax) for v in es]
    esum = eexp[0] + eexp[1] + eexp[2] + eexp[3]
    ep = [v / esum for v in eexp]
    p1 = jnp.maximum(jnp.maximum(ep[0], ep[1]), jnp.maximum(ep[2], ep[3]))
    i1 = jnp.where(ep[0] == p1, 0, jnp.where(ep[1] == p1, 1, jnp.where(ep[2] == p1, 2, 3)))
    neg = jnp.float32(-1.0)
    rest = [jnp.where(i1 == j, neg, ep[j]) for j in range(EXPERTS_PER_GROUP)]
    p2 = jnp.maximum(jnp.maximum(rest[0], rest[1]), jnp.maximum(rest[2], rest[3]))
    i2 = jnp.where(rest[0] == p2, 0, jnp.where(rest[1] == p2, 1, jnp.where(rest[2] == p2, 2, 3)))
    lo = jnp.minimum(i1, i2)
    hi = jnp.maximum(i1, i2)
    pair = jnp.where(lo == 0, 0, jnp.where(lo == 1, 3, 5)) + hi - lo - 1
    bucket = g_idx * N_PAIRS + pair
    tm = bucket.shape[1]
    rowid = lax.broadcasted_iota(jnp.int32, (BUCKET_ROWS, tm), 0)
    onehot = (rowid == bucket).astype(F32)
    prefix = jnp.dot(onehot.astype(BF16), tri_ref[...], preferred_element_type=F32)
    run = run_ref[...]
    rank = jnp.sum(onehot * (prefix + run), axis=0, keepdims=True)
    run = run + jnp.sum(onehot, axis=1, keepdims=True)
    run_ref[...] = run
    cnt_ref[...] = jnp.broadcast_to(run, cnt_ref.shape)
    idx_ref[...] = jnp.concatenate([bucket, rank.astype(jnp.int32), jnp.zeros((6, tm), jnp.int32)], axis=0)


def _router(x2, gain, w_rg, b_rg, w_re, b_re, tm=512):
    n = x2.shape[0]
    wr = jnp.concatenate([w_rg, w_re], axis=1).astype(F32).T
    wr = jnp.pad(wr, ((0, ROUTER_ROWS - wr.shape[0]), (0, 0)))
    wr_hi = wr.astype(BF16)
    wr = jnp.concatenate([wr_hi, (wr - wr_hi.astype(F32)).astype(BF16)], axis=0)
    br = jnp.pad(jnp.concatenate([b_rg, b_re]).astype(F32), (0, ROUTER_ROWS - N_GROUPS - N_EXPERTS))
    tri = (np.arange(tm)[:, None] < np.arange(tm)[None, :]).astype(np.float32)
    return pl.pallas_call(
        _router_kernel,
        out_shape=(jax.ShapeDtypeStruct((8, n), jnp.int32),
                   jax.ShapeDtypeStruct((BUCKET_ROWS, 128), F32)),
        grid=(n // tm,),
        in_specs=[pl.BlockSpec((tm, D_MODEL), lambda i: (i, 0)),
                  pl.BlockSpec((1, D_MODEL), lambda i: (0, 0)),
                  pl.BlockSpec((2 * ROUTER_ROWS, D_MODEL), lambda i: (0, 0)),
                  pl.BlockSpec((ROUTER_ROWS, 1), lambda i: (0, 0)),
                  pl.BlockSpec((tm, tm), lambda i: (0, 0))],
        out_specs=(pl.BlockSpec((8, tm), lambda i: (0, i)),
                   pl.BlockSpec((BUCKET_ROWS, 128), lambda i: (0, 0))),
        scratch_shapes=[pltpu.VMEM((BUCKET_ROWS, 1), F32)],
        compiler_params=_params("arbitrary"),
        name="moe_router",
    )(x2, gain.reshape(1, D_MODEL), wr, br.reshape(ROUTER_ROWS, 1), jnp.asarray(tri, dtype=BF16))


ROW_COPY_UNROLL = 8


def _start_row_copies(idx_ref, n_rows, copy_for_row):
    def body(g, carry):
        base = pl.multiple_of(g * ROW_COPY_UNROLL, ROW_COPY_UNROLL)
        for j in range(ROW_COPY_UNROLL):
            copy_for_row(base + j, idx_ref[0, 0, base + j]).start()
        return carry

    lax.fori_loop(0, n_rows // ROW_COPY_UNROLL, body, 0)


def _rows_to_tiles(x):
    rows = x.shape[0]
    return x.reshape(rows * ROW_TILE[0], ROW_TILE[1]).reshape(rows, *ROW_TILE)


def _tiles_to_rows(x3):
    rows = x3.shape[0]
    return x3.reshape(rows * ROW_TILE[0], ROW_TILE[1]).reshape(rows, D_MODEL)


def _dispatch_kernel(tail_row_ref, tail_on_ref, pos_ref, x_ref, xs_ref, buf_ref, sem, *, tile):
    tm = x_ref.shape[0]

    @pl.when(pl.program_id(0) == 0)
    def _():
        buf_ref[:tile] = jnp.zeros((tile, *ROW_TILE), F32)

        def zero_copy(k):
            return pltpu.make_async_copy(buf_ref.at[pl.ds(0, tile)],
                                         xs_ref.at[pl.ds(pl.multiple_of(tail_row_ref[k], tile), tile)], sem)

        for k in range(2 * N_BUCKETS):
            pl.when(tail_on_ref[k] > 0)(lambda k=k: zero_copy(k).start())
        for k in range(2 * N_BUCKETS):
            pl.when(tail_on_ref[k] > 0)(lambda k=k: zero_copy(k).wait())

    buf_ref[...] = _rows_to_tiles(x_ref[...])
    _start_row_copies(pos_ref, tm, lambda r, p: pltpu.make_async_copy(buf_ref.at[r], xs_ref.at[p], sem))
    pltpu.make_async_copy(buf_ref, xs_ref.at[pl.ds(0, tm)], sem).wait()


def _dispatch(x2, pos3, tails, n_rows_sorted, tile, tm):
    n = x2.shape[0]
    tail_row, tail_on = tails
    grid_spec = pltpu.PrefetchScalarGridSpec(
        num_scalar_prefetch=2,
        grid=(n // tm,),
        in_specs=[pl.BlockSpec((1, 1, tm), lambda i, *_: (i, 0, 0), memory_space=pltpu.SMEM),
                  pl.BlockSpec((tm, D_MODEL), lambda i, *_: (i, 0))],
        out_specs=pl.BlockSpec(memory_space=pl.ANY),
        scratch_shapes=[pltpu.VMEM((tm, *ROW_TILE), F32), pltpu.SemaphoreType.DMA],
    )
    return pl.pallas_call(
        functools.partial(_dispatch_kernel, tile=tile),
        out_shape=jax.ShapeDtypeStruct((n_rows_sorted, *ROW_TILE), F32),
        grid_spec=grid_spec,
        compiler_params=_params("arbitrary"),
        name="moe_dispatch",
    )(tail_row, tail_on, pos3, x2)


def _experts_kernel(elo_ref, ehi_ref, nvalid_ref, xs_ref, g_ref, wsel_ref, bsel_ref, wg_lo, wu_lo, wg_hi, wu_hi,
                    wd_lo, wd_hi, o_ref, lt_ref):
    t = pl.program_id(0)

    @pl.when(t < nvalid_ref[0])
    def _():
        xt = _tiles_to_rows(xs_ref[...])
        h = _rms(xt, g_ref[...]).astype(BF16)
        lt_ref[...] = _nt_dot(wsel_ref[0], h) + bsel_ref[0]
        grp = elo_ref[t] // EXPERTS_PER_GROUP
        lo = elo_ref[t] % EXPERTS_PER_GROUP
        hi = ehi_ref[t] % EXPERTS_PER_GROUP
        gl = lt_ref[0:N_GROUPS, :]
        gexp = jnp.exp(gl - jnp.max(gl, axis=0, keepdims=True))
        g_gate = jnp.exp(lt_ref[pl.ds(grp, 1), :] - jnp.max(gl, axis=0, keepdims=True)) / jnp.sum(
            gexp, axis=0, keepdims=True)
        emax = jnp.max(lt_ref[N_GROUPS:N_GROUPS + EXPERTS_PER_GROUP, :], axis=0, keepdims=True)
        p_lo = jnp.exp(lt_ref[pl.ds(N_GROUPS + lo, 1), :] - emax)
        p_hi = jnp.exp(lt_ref[pl.ds(N_GROUPS + hi, 1), :] - emax)
        w_rows = jnp.concatenate([g_gate * (p_lo / (p_lo + p_hi)), g_gate * (p_hi / (p_lo + p_hi)),
                                  jnp.zeros((LANES - 2, p_lo.shape[1]), F32)], axis=0)
        w_cols = w_rows.T

        def expert(wg, wu, wd):
            gate = jnp.dot(h, wg[0], preferred_element_type=F32)
            up = jnp.dot(h, wu[0], preferred_element_type=F32)
            return jnp.dot((_silu(gate) * up).astype(BF16), wd[0], preferred_element_type=F32)

        out = (xt + w_cols[:, 0:1] * expert(wg_lo, wu_lo, wd_lo)
               + w_cols[:, 1:2] * expert(wg_hi, wu_hi, wd_hi))
        o_ref[...] = _rows_to_tiles(out)

    @pl.when(t >= nvalid_ref[0])
    def _():
        o_ref[...] = jnp.zeros_like(o_ref)


def _experts(xs, gain, tables, wsel, bsel, wg, wu, wd, n_tiles, t):
    elo, ehi, nvalid = tables
    row = lambda i, elo, ehi, nv: (i, 0, 0)
    row_in = lambda i, elo, ehi, nv: (jnp.minimum(i, nv[0] - 1), 0, 0)
    grp3 = lambda i, elo, ehi, nv: (elo[i] // EXPERTS_PER_GROUP, 0, 0)
    lo3 = lambda i, elo, ehi, nv: (elo[i], 0, 0)
    hi3 = lambda i, elo, ehi, nv: (ehi[i], 0, 0)
    grid_spec = pltpu.PrefetchScalarGridSpec(
        num_scalar_prefetch=3,
        grid=(n_tiles,),
        in_specs=[pl.BlockSpec((t, *ROW_TILE), row_in),
                  pl.BlockSpec((1, D_MODEL), lambda i, *_: (0, 0)),
                  pl.BlockSpec((1, LANES, D_MODEL), grp3),
                  pl.BlockSpec((1, LANES, 1), grp3),
                  pl.BlockSpec((1, D_MODEL, D_EXPERT), lo3),
                  pl.BlockSpec((1, D_MODEL, D_EXPERT), lo3),
                  pl.BlockSpec((1, D_MODEL, D_EXPERT), hi3),
                  pl.BlockSpec((1, D_MODEL, D_EXPERT), hi3),
                  pl.BlockSpec((1, D_EXPERT, D_MODEL), lo3),
                  pl.BlockSpec((1, D_EXPERT, D_MODEL), hi3)],
        out_specs=pl.BlockSpec((t, *ROW_TILE), row),
        scratch_shapes=[pltpu.VMEM((LANES, t), F32)],
    )
    return pl.pallas_call(
        _experts_kernel,
        out_shape=jax.ShapeDtypeStruct(xs.shape, F32),
        grid_spec=grid_spec,
        compiler_params=_params("arbitrary"),
        name="moe_experts",
    )(elo, ehi, nvalid, xs, gain.reshape(1, D_MODEL), wsel, bsel, wg, wu, wg, wu, wd, wd)


def _router_by_group(w_rg, b_rg, w_re, b_re):
    wg_rows = jnp.broadcast_to(w_rg.T[None], (N_GROUPS, N_GROUPS, D_MODEL))
    we_rows = w_re.T.reshape(N_GROUPS, EXPERTS_PER_GROUP, D_MODEL)
    wsel = jnp.concatenate([wg_rows, we_rows], axis=1)
    wsel = jnp.pad(wsel, ((0, 0), (0, LANES - wsel.shape[1]), (0, 0))).astype(BF16)
    bsel = jnp.concatenate([jnp.broadcast_to(b_rg[None], (N_GROUPS, N_GROUPS)),
                            b_re.reshape(N_GROUPS, EXPERTS_PER_GROUP)], axis=1).astype(F32)
    bsel = jnp.pad(bsel, ((0, 0), (0, LANES - bsel.shape[1])))[..., None]
    return wsel, bsel


def _combine_kernel(pos_ref, ys_ref, o_ref, buf_ref, sem):
    tm = o_ref.shape[0]
    _start_row_copies(pos_ref, tm, lambda r, p: pltpu.make_async_copy(ys_ref.at[p], buf_ref.at[r], sem))
    pltpu.make_async_copy(ys_ref.at[pl.ds(0, tm)], buf_ref, sem).wait()
    o_ref[...] = _tiles_to_rows(buf_ref[...])


def _combine(ys, pos3, n, tm):
    return pl.pallas_call(
        _combine_kernel,
        out_shape=jax.ShapeDtypeStruct((n, D_MODEL), F32),
        grid=(n // tm,),
        in_specs=[pl.BlockSpec((1, 1, tm), lambda i: (i, 0, 0), memory_space=pltpu.SMEM),
                  pl.BlockSpec(memory_space=pl.ANY)],
        out_specs=pl.BlockSpec((tm, D_MODEL), lambda i: (i, 0)),
        scratch_shapes=[pltpu.VMEM((tm, *ROW_TILE), F32), pltpu.SemaphoreType.DMA],
        compiler_params=_params("arbitrary"),
        name="moe_combine",
    )(pos3, ys)


def _moe_tables(idx, cnt, n_tiles, t):
    bucket, rank = idx[0], idx[1]
    counts = cnt[:N_BUCKETS, 0].astype(jnp.int32)
    tiles_b = (counts + t - 1) // t
    tile_end = jnp.cumsum(tiles_b)
    pos = (tile_end - tiles_b)[bucket] * t + rank
    total = tile_end[-1]
    tt = jnp.arange(n_tiles, dtype=jnp.int32)
    valid = tt < total
    tb = jnp.sum((tile_end[None, :] <= jnp.where(valid, tt, total - 1)[:, None]).astype(jnp.int32), axis=1)
    tb = jnp.minimum(tb, N_BUCKETS - 1)
    pair_lo = jnp.asarray([0, 0, 0, 1, 1, 2], jnp.int32)
    pair_hi = jnp.asarray([1, 2, 3, 2, 3, 3], jnp.int32)
    base = (tb // N_PAIRS) * EXPERTS_PER_GROUP
    idle = total + jnp.arange(N_BUCKETS, dtype=jnp.int32)
    idle_on = idle < n_tiles
    tails = (jnp.concatenate([(tile_end - 1) * t, jnp.where(idle_on, idle, 0) * t]),
             jnp.concatenate([tiles_b > 0, idle_on]).astype(jnp.int32))
    return pos, tails, (base + pair_lo[tb % N_PAIRS], base + pair_hi[tb % N_PAIRS], total.reshape(1))


def _moe(x2, gain, w_rg, b_rg, w_re, b_re, wg, wu, wd, t=MOE_TILE, tm=1024):
    n = x2.shape[0]
    idx, cnt = _router(x2, gain, w_rg, b_rg, w_re, b_re)
    n_tiles = n // t + N_BUCKETS
    pos, tails, tables = _moe_tables(idx, cnt, n_tiles, t)
    pos3 = pos.reshape(n // tm, 1, tm)
    xs = _dispatch(x2, pos3, tails, n_tiles * t, t, tm)
    wsel, bsel = _router_by_group(w_rg, b_rg, w_re, b_re)
    ys = _experts(xs, gain, tables, wsel, bsel, wg.astype(BF16), wu.astype(BF16), wd.astype(BF16), n_tiles, t)
    return _combine(ys, pos3, n, tm)


LOG2E = math.log2(math.e)
V_EXT = 2 * HEAD_DIM


def _head_norm_t(y_t, n_heads, scale):
    tm = y_t.shape[1]
    y3 = y_t.reshape(n_heads, HEAD_DIM, tm)
    ms = jnp.mean(y3 * y3, axis=1, keepdims=True)
    return y3 * (lax.rsqrt(ms + EPS) * scale)


def _qkv_kernel(x_ref, g_ref, wqt_ref, wkt_ref, wvt_ref, vone_ref, kg_ref, qt_ref, k_ref, vt_ref):
    h = _rms(x_ref[...], g_ref[...]).astype(BF16)
    tm = h.shape[0]
    qn = _head_norm_t(_nt_dot(wqt_ref[...], h), N_Q_HEADS, HEAD_DIM ** -0.5 * LOG2E)
    qt_ref[...] = qn.reshape(N_Q_HEADS * HEAD_DIM, tm).astype(BF16)
    kn = _head_norm_t(_nt_dot(wkt_ref[...], h), N_KV_HEADS, kg_ref[...].reshape(N_KV_HEADS, HEAD_DIM, 1))
    for hk in range(N_KV_HEADS):
        k_ref[hk] = kn[hk].T.astype(BF16)
    vt_ref[...] = (_nt_dot(wvt_ref[...], h) + vone_ref[...]).astype(BF16)


def _qkv(x2, gain, wqkv, q_gain, k_gain, tm=512):
    n = x2.shape[0]
    qw = N_Q_HEADS * HEAD_DIM
    wqt = wqkv[:, :qw].T.astype(BF16)
    wkt = wqkv[:, qw:qw + KV_WIDTH].T.astype(BF16)
    wvt = wqkv[:, qw + KV_WIDTH:].T.astype(BF16).reshape(N_KV_HEADS, HEAD_DIM, D_MODEL)
    wvt = jnp.pad(wvt, ((0, 0), (0, V_EXT - HEAD_DIM), (0, 0))).reshape(N_KV_HEADS * V_EXT, D_MODEL)
    vone = np.zeros((N_KV_HEADS * V_EXT, 1), np.float32)
    vone[HEAD_DIM::V_EXT, 0] = 1.0
    kg = jnp.tile((k_gain.astype(F32) * q_gain.astype(F32)), N_KV_HEADS).reshape(KV_WIDTH, 1)
    full = lambda r, c: pl.BlockSpec((r, c), lambda i: (0, 0))
    return pl.pallas_call(
        _qkv_kernel,
        out_shape=(jax.ShapeDtypeStruct((qw, n), BF16),
                   jax.ShapeDtypeStruct((N_KV_HEADS, n, HEAD_DIM), BF16),
                   jax.ShapeDtypeStruct((N_KV_HEADS * V_EXT, n), BF16)),
        grid=(n // tm,),
        in_specs=[pl.BlockSpec((tm, D_MODEL), lambda i: (i, 0)), full(1, D_MODEL),
                  full(qw, D_MODEL), full(KV_WIDTH, D_MODEL), full(N_KV_HEADS * V_EXT, D_MODEL),
                  full(N_KV_HEADS * V_EXT, 1), full(KV_WIDTH, 1)],
        out_specs=(pl.BlockSpec((qw, tm), lambda i: (0, i)),
                   pl.BlockSpec((N_KV_HEADS, tm, HEAD_DIM), lambda i: (0, i, 0)),
                   pl.BlockSpec((N_KV_HEADS * V_EXT, tm), lambda i: (0, i))),
        compiler_params=_params("parallel"),
        name="odd_qkv",
    )(x2, gain.reshape(1, D_MODEL), wqt, wkt, wvt, jnp.asarray(vone), kg)


def _attn_bias():
    blk = ATT_BLOCK
    qi = np.arange(blk)[None, :]
    ki = np.arange(2 * blk)[:, None]
    dist = qi - ki + blk
    band = (dist >= 0) & (dist < blk)
    slopes = 2.0 ** (-8.0 * np.arange(1, N_Q_HEADS + 1) / N_Q_HEADS)
    pen = -slopes[:, None, None] * dist[None].astype(np.float64) * LOG2E
    inner = np.where(band[None], pen, -np.inf)
    first = np.where((band & (ki >= blk))[None], pen, -np.inf)
    tab = np.stack([inner, first]).astype(np.float32)
    tab = tab.reshape(2, N_KV_HEADS, GQA_GROUP, 2 * blk, blk).transpose(0, 1, 3, 2, 4)
    return tab.reshape(2, N_KV_HEADS, 2 * blk, GQA_GROUP * blk)


def _attn_kernel(qt_ref, kp_ref, kc_ref, vtp_ref, vtc_ref, bias_ref, sink_ref, x_ref, wo_ref, o_ref):
    first = (pl.program_id(1) == 0).astype(jnp.int32)
    vt = jnp.concatenate([vtp_ref[...], vtc_ref[...]], axis=1)
    att_t = []
    for hk in range(N_KV_HEADS):
        keys = jnp.concatenate([kp_ref[hk], kc_ref[hk]], axis=0)
        q_t = jnp.concatenate([qt_ref[(hk * GQA_GROUP + g) * HEAD_DIM:(hk * GQA_GROUP + g + 1) * HEAD_DIM, :]
                               for g in range(GQA_GROUP)], axis=1)
        s = jnp.dot(keys, q_t, preferred_element_type=F32) + bias_ref[first, hk]
        sink = sink_ref[hk]
        m = jnp.maximum(jnp.max(s, axis=0, keepdims=True), sink)
        p = jnp.exp2(s - m).astype(BF16)
        pv = jnp.dot(vt[hk * V_EXT:(hk + 1) * V_EXT, :], p, preferred_element_type=F32)
        den = pv[HEAD_DIM:HEAD_DIM + 1, :] + jnp.exp2(sink - m)
        o_t = (pv[:HEAD_DIM, :] * (1.0 / den)).astype(BF16)
        att_t += [o_t[:, g * ATT_BLOCK:(g + 1) * ATT_BLOCK] for g in range(GQA_GROUP)]
    att_t = jnp.concatenate(att_t, axis=0)
    mix = lax.dot_general(att_t, wo_ref[...], (((0,), (0,)), ((), ())), preferred_element_type=F32)
    o_ref[...] = x_ref[...] + mix


def _attn(qt, k, vt, x2, sinks, wo, bsz, seqlen):
    blk = ATT_BLOCK
    nb = seqlen // blk
    qw = N_Q_HEADS * HEAD_DIM
    cols = GQA_GROUP * blk
    cur = lambda b, n: (b * nb + n, 0)
    cur_t = lambda b, n: (0, b * nb + n)
    prev_t = lambda b, n: (0, b * nb + jnp.maximum(n - 1, 0))
    sink_row = jnp.repeat(sinks.astype(F32) * LOG2E, blk).reshape(N_KV_HEADS, 1, cols)
    return pl.pallas_call(
        _attn_kernel,
        out_shape=jax.ShapeDtypeStruct((bsz * seqlen, D_MODEL), F32),
        grid=(bsz, nb),
        in_specs=[pl.BlockSpec((qw, blk), cur_t),
                  pl.BlockSpec((N_KV_HEADS, blk, HEAD_DIM), lambda b, n: (0, b * nb + jnp.maximum(n - 1, 0), 0)),
                  pl.BlockSpec((N_KV_HEADS, blk, HEAD_DIM), lambda b, n: (0, b * nb + n, 0)),
                  pl.BlockSpec((N_KV_HEADS * V_EXT, blk), prev_t),
                  pl.BlockSpec((N_KV_HEADS * V_EXT, blk), cur_t),
                  pl.BlockSpec((2, N_KV_HEADS, 2 * blk, cols), lambda b, n: (0, 0, 0, 0)),
                  pl.BlockSpec((N_KV_HEADS, 1, cols), lambda b, n: (0, 0, 0)),
                  pl.BlockSpec((blk, D_MODEL), cur),
                  pl.BlockSpec((qw, D_MODEL), lambda b, n: (0, 0))],
        out_specs=pl.BlockSpec((blk, D_MODEL), cur),
        compiler_params=_params("parallel", "parallel"),
        name="odd_attn",
    )(qt, k, k, vt, vt, jnp.asarray(_attn_bias()), sink_row, x2, wo.astype(BF16))


def kernel(x, even_mix_norm, even_in_proj, s5_lambda_re, s5_lambda_im, s5_log_step, s5_b_re, s5_b_im,
           s5_c_re, s5_c_im, s5_d, s5_glu_w, hgrn_lower_bounds, hgrn_o_norm, even_out_proj, odd_mix_norm,
           odd_wqkv, odd_q_norm, odd_k_norm, odd_sinks, odd_out_proj, moe_norm, moe_router_group,
           moe_router_group_bias, moe_router_expert, moe_router_expert_bias, moe_w_gate, moe_w_up,
           moe_w_down):
    bsz, seqlen, dm = x.shape
    n = bsz * seqlen
    x2 = x.reshape(n, dm)
    lower_bounds = jnp.cumsum(jax.nn.softmax(hgrn_lower_bounds.astype(F32), axis=0), axis=0)

    def moe(xx, layer):
        return _moe(xx, moe_norm[layer], moe_router_group[layer], moe_router_group_bias[layer],
                    moe_router_expert[layer], moe_router_expert_bias[layer],
                    moe_w_gate[layer], moe_w_up[layer], moe_w_down[layer])

    u, h4 = _inproj(x2, even_mix_norm[0], even_in_proj[0].astype(BF16))
    ops = _s5_operators(s5_lambda_re[0], s5_lambda_im[0], s5_log_step[0], s5_b_re[0], s5_b_im[0],
                        s5_c_re[0], s5_c_im[0])
    u_g = _s5_pack(u.reshape(bsz, seqlen, S5_WIDTH))
    y_g = _s5_scan(u_g, ops, bsz)
    ys = _s5_unpack(y_g, bsz, seqlen).reshape(n, S5_WIDTH)
    b_out = _hgrn(h4.reshape(bsz, seqlen, 4 * HG_WIDTH), lower_bounds[0], hgrn_o_norm[0], bsz, seqlen)
    x2 = _evenout(x2, ys, u, b_out.reshape(n, HG_WIDTH), s5_d[0], s5_glu_w[0], even_out_proj[0])
    x2 = moe(x2, 0)

    q, kt, v = _qkv(x2, odd_mix_norm[0], odd_wqkv[0], odd_q_norm[0], odd_k_norm[0])
    x2 = _attn(q, kt, v, x2, odd_sinks[0], odd_out_proj[0], bsz, seqlen)
    x2 = moe(x2, 1)
    return x2.reshape(bsz, seqlen, dm)
```

```python
import functools
import math

import jax
import jax.numpy as jnp
import numpy as np
from jax import lax
from jax.experimental import pallas as pl
from jax.experimental.pallas import tpu as pltpu

F32 = jnp.float32
BF16 = jnp.bfloat16
EPS = 1e-6

D_MODEL = 1024
S5_WIDTH = 512
S5_GROUP = 16
S5_GROUPS = 32
S5_STATE = 64
S5_CHUNK = 16
HG_WIDTH = 512
HG_HEAD_DIM = 128
HG_HEADS = 4
HG_CHUNK = 32
HEAD_DIM = 64
N_Q_HEADS = 16
N_KV_HEADS = 2
GQA_GROUP = 8
KV_WIDTH = N_KV_HEADS * HEAD_DIM
ATT_BLOCK = 128
N_GROUPS = 4
EXPERTS_PER_GROUP = 4
N_EXPERTS = 16
D_EXPERT = 256
ROUTER_ROWS = 32
N_PAIRS = 6
N_BUCKETS = N_GROUPS * N_PAIRS
BUCKET_ROWS = 32
MOE_TILE = 256
ROW_TILE = (8, 128)
ROW_TILE_SHIFT = ROW_TILE[0].bit_length() - 1
X_TILES = D_MODEL // ROW_TILE[1]
XS_TILES = X_TILES + 1

VMEM_LIMIT_BYTES = 56 * 1024 * 1024


def _params(*semantics):
    return pltpu.CompilerParams(dimension_semantics=semantics, vmem_limit_bytes=VMEM_LIMIT_BYTES)


def _rms(xf, gain):
    return xf * lax.rsqrt(jnp.mean(xf * xf, axis=-1, keepdims=True) + EPS) * gain


def _nt_dot(w_t, h):
    return lax.dot_general(w_t, h, (((1,), (1,)), ((), ())), preferred_element_type=F32)


def _sigmoid(x):
    return 0.5 * jnp.tanh(0.5 * x) + 0.5


def _silu(x):
    return x * _sigmoid(x)


def _inproj_kernel(x_ref, g_ref, w_ref, u_ref, h4_ref):
    h = _rms(x_ref[...], g_ref[...]).astype(BF16)
    p = jnp.dot(h, w_ref[...], preferred_element_type=F32)
    u_ref[...] = p[:, :S5_WIDTH]
    h4_ref[...] = p[:, S5_WIDTH:]


def _inproj(x2, gain, w_bf16, tm=512):
    n = x2.shape[0]
    e_in = w_bf16.shape[1]
    return pl.pallas_call(
        _inproj_kernel,
        out_shape=(jax.ShapeDtypeStruct((n, S5_WIDTH), F32),
                   jax.ShapeDtypeStruct((n, e_in - S5_WIDTH), F32)),
        grid=(n // tm,),
        in_specs=[pl.BlockSpec((tm, D_MODEL), lambda i: (i, 0)),
                  pl.BlockSpec((1, D_MODEL), lambda i: (0, 0)),
                  pl.BlockSpec((D_MODEL, e_in), lambda i: (0, 0))],
        out_specs=(pl.BlockSpec((tm, S5_WIDTH), lambda i: (i, 0)),
                   pl.BlockSpec((tm, e_in - S5_WIDTH), lambda i: (i, 0))),
        compiler_params=_params("parallel"),
        name="even_inproj",
    )(x2, gain.reshape(1, D_MODEL), w_bf16)


def _s5_lagkernel_kernel(ca_ref, bb_ref, k_ref):
    k_ref[0] = jnp.dot(ca_ref[0], bb_ref[0], preferred_element_type=F32,
                       precision=lax.Precision.HIGHEST)


def _s5_lagkernel(ca, bb):
    g, rows, k = ca.shape
    return pl.pallas_call(
        _s5_lagkernel_kernel,
        out_shape=jax.ShapeDtypeStruct((g, rows, S5_GROUP), F32),
        grid=(g,),
        in_specs=[pl.BlockSpec((1, rows, k), lambda i: (i, 0, 0)),
                  pl.BlockSpec((1, k, S5_GROUP), lambda i: (i, 0, 0))],
        out_specs=pl.BlockSpec((1, rows, S5_GROUP), lambda i: (i, 0, 0)),
        compiler_params=_params("parallel"),
        name="s5_lag_kernel",
    )(ca, bb)


def _s5_operators(lam_re, lam_im, log_step, b_re, b_im, c_re, c_im):
    t = S5_CHUNK
    lr, li = lam_re.astype(F32), lam_im.astype(F32)
    step = jnp.exp(log_step.astype(F32))[:, None]
    mag = jnp.exp(lr * step)
    ab_re = mag * jnp.cos(li * step)
    ab_im = mag * jnp.sin(li * step)
    den = lr * lr + li * li
    nr, ni = ab_re - 1.0, ab_im
    z_re = (nr * lr + ni * li) / den
    z_im = (ni * lr - nr * li) / den
    br, bi = b_re.astype(F32), b_im.astype(F32)
    bb_re = z_re[..., None] * br - z_im[..., None] * bi
    bb_im = z_re[..., None] * bi + z_im[..., None] * br
    kk = jnp.arange(t + 1, dtype=F32)[:, None, None]
    pmag = jnp.exp(kk * (lr * step)[None])
    pw_re = pmag * jnp.cos(kk * (li * step)[None])
    pw_im = pmag * jnp.sin(kk * (li * step)[None])
    cr = jnp.transpose(c_re.astype(F32), (0, 1, 2))
    ci = c_im.astype(F32)
    ca_re = cr[None] * pw_re[:, :, None, :] - ci[None] * pw_im[:, :, None, :]
    ca_im = cr[None] * pw_im[:, :, None, :] + ci[None] * pw_re[:, :, None, :]
    g = lr.shape[0]
    ca_cat = jnp.concatenate([ca_re[:t], -ca_im[:t]], axis=-1)
    ca_cat = jnp.transpose(ca_cat, (1, 0, 2, 3)).reshape(g, t * S5_GROUP, 2 * S5_STATE)
    bb_cat = jnp.concatenate([bb_re, bb_im], axis=1)
    kern = _s5_lagkernel(ca_cat, bb_cat).reshape(g, t, S5_GROUP, S5_GROUP)
    s_idx = jnp.arange(t)[:, None]
    t_idx = jnp.arange(t)[None, :]
    lag = t_idx - s_idx
    kg = kern[:, jnp.clip(lag, 0, t - 1)]
    kg = jnp.where((lag >= 0)[None, :, :, None, None], kg, 0.0)
    mt = jnp.transpose(kg, (0, 1, 4, 2, 3)).reshape(g, t * S5_GROUP, t * S5_GROUP)
    pr = pw_re[:t][::-1]
    pi = pw_im[:t][::-1]
    sb_re = pr[:, :, :, None] * bb_re[None] - pi[:, :, :, None] * bb_im[None]
    sb_im = pr[:, :, :, None] * bb_im[None] + pi[:, :, :, None] * bb_re[None]
    sb_re = jnp.transpose(sb_re, (1, 0, 3, 2)).reshape(g, t * S5_GROUP, S5_STATE)
    sb_im = jnp.transpose(sb_im, (1, 0, 3, 2)).reshape(g, t * S5_GROUP, S5_STATE)
    cp_re = jnp.transpose(ca_re[1:], (1, 3, 0, 2)).reshape(g, S5_STATE, t * S5_GROUP)
    cp_im = jnp.transpose(-ca_im[1:], (1, 3, 0, 2)).reshape(g, S5_STATE, t * S5_GROUP)

    def pair_rows(m):
        m = m.reshape(g // 2, 2, m.shape[1], m.shape[2])
        z = jnp.zeros_like(m[:, 0])
        top = jnp.concatenate([m[:, 0], z], axis=2)
        bot = jnp.concatenate([z, m[:, 1]], axis=2)
        return jnp.concatenate([top, bot], axis=1)

    at_re = pw_re[t].reshape(g // 2, 1, 2 * S5_STATE)
    at_im = pw_im[t].reshape(g // 2, 1, 2 * S5_STATE)
    return (mt.astype(BF16), pair_rows(sb_re).astype(BF16), pair_rows(sb_im).astype(BF16),
            pair_rows(cp_re).astype(BF16), pair_rows(cp_im).astype(BF16), at_re, at_im)


PACK_TOKENS = 128


LANES = 128
GROUPS_PER_TILE = LANES // S5_GROUP
TOKENS_PER_TILE = LANES // S5_GROUP
CHUNK_HALVES = S5_CHUNK // TOKENS_PER_TILE
PACK_CHUNKS = PACK_TOKENS // S5_CHUNK


def _s5_pack_kernel(u_ref, o_ref, *, bsz):
    def body(b, carry):
        z = [u_ref[b, pl.ds(t, PACK_CHUNKS, stride=S5_CHUNK), :] for t in range(S5_CHUNK)]
        for g in range(GROUPS_PER_TILE):
            for j in range(CHUNK_HALVES):
                row = jnp.concatenate([zt[:, g * S5_GROUP:(g + 1) * S5_GROUP]
                                       for zt in z[j * TOKENS_PER_TILE:(j + 1) * TOKENS_PER_TILE]], axis=1)
                o_ref[g * CHUNK_HALVES + j, pl.ds(b, PACK_CHUNKS, stride=bsz), :] = row
        return carry

    for b in range(bsz):
        body(b, 0)


def _s5_pack(u3):
    bsz, seqlen, w = u3.shape
    rows = PACK_CHUNKS * bsz
    return pl.pallas_call(
        functools.partial(_s5_pack_kernel, bsz=bsz),
        out_shape=jax.ShapeDtypeStruct((S5_GROUPS * CHUNK_HALVES, seqlen // S5_CHUNK * bsz, LANES), F32),
        grid=(seqlen // PACK_TOKENS, w // LANES),
        in_specs=[pl.BlockSpec((bsz, PACK_TOKENS, LANES), lambda i, k: (0, i, k))],
        out_specs=pl.BlockSpec((GROUPS_PER_TILE * CHUNK_HALVES, rows, LANES), lambda i, k: (k, i, 0)),
        compiler_params=_params("parallel", "parallel"),
        name="s5_pack",
    )(u3)


def _s5_unpack_kernel(y_ref, o_ref, *, bsz):
    def body(b, carry):
        yg = [y_ref[r, pl.ds(b, PACK_CHUNKS, stride=bsz), :] for r in range(GROUPS_PER_TILE * CHUNK_HALVES)]
        for t in range(S5_CHUNK):
            j, tt = divmod(t, TOKENS_PER_TILE)
            row = jnp.concatenate([yg[g * CHUNK_HALVES + j][:, tt * S5_GROUP:(tt + 1) * S5_GROUP]
                                   for g in range(GROUPS_PER_TILE)], axis=1)
            o_ref[b, pl.ds(t, PACK_CHUNKS, stride=S5_CHUNK), :] = row
        return carry

    for b in range(bsz):
        body(b, 0)


def _s5_unpack(y_g, bsz, seqlen):
    rows = PACK_CHUNKS * bsz
    return pl.pallas_call(
        functools.partial(_s5_unpack_kernel, bsz=bsz),
        out_shape=jax.ShapeDtypeStruct((bsz, seqlen, S5_WIDTH), F32),
        grid=(seqlen // PACK_TOKENS, S5_WIDTH // LANES),
        in_specs=[pl.BlockSpec((GROUPS_PER_TILE * CHUNK_HALVES, rows, LANES), lambda i, k: (k, i, 0))],
        out_specs=pl.BlockSpec((bsz, PACK_TOKENS, LANES), lambda i, k: (0, i, k)),
        compiler_params=_params("parallel", "parallel"),
        name="s5_unpack",
    )(y_g)


def _s5_kernel(u_ref, mt_ref, wre_ref, wim_ref, cre_ref, cim_ref, atr_ref, ati_ref, y_ref,
               sre_ref, sim_ref, xre_ref, xim_ref, *, n_chunks, bsz):
    ucat = jnp.concatenate([u_ref[i] for i in range(2 * CHUNK_HALVES)], axis=1).astype(BF16)
    w = S5_CHUNK * S5_GROUP
    u0 = ucat[:, :w]
    u1 = ucat[:, w:]
    sre_ref[...] = jnp.dot(ucat, wre_ref[0], preferred_element_type=F32)
    sim_ref[...] = jnp.dot(ucat, wim_ref[0], preferred_element_type=F32)
    atr = jnp.broadcast_to(atr_ref[0], (bsz, 2 * S5_STATE))
    ati = jnp.broadcast_to(ati_ref[0], (bsz, 2 * S5_STATE))

    def body(c, carry):
        xr, xi = carry
        rows = pl.ds(pl.multiple_of(c * bsz, bsz), bsz)
        xre_ref[rows, :] = xr
        xim_ref[rows, :] = xi
        nxr = atr * xr - ati * xi + sre_ref[rows, :]
        nxi = atr * xi + ati * xr + sim_ref[rows, :]
        return nxr, nxi

    zero = jnp.zeros((bsz, 2 * S5_STATE), F32)
    lax.fori_loop(0, n_chunks, body, (zero, zero))
    ycar = (jnp.dot(xre_ref[...].astype(BF16), cre_ref[0], preferred_element_type=F32)
            + jnp.dot(xim_ref[...].astype(BF16), cim_ref[0], preferred_element_type=F32))
    y0 = jnp.dot(u0, mt_ref[0], preferred_element_type=F32) + ycar[:, :w]
    y1 = jnp.dot(u1, mt_ref[1], preferred_element_type=F32) + ycar[:, w:]
    for i in range(CHUNK_HALVES):
        y_ref[i] = y0[:, i * LANES:(i + 1) * LANES]
        y_ref[CHUNK_HALVES + i] = y1[:, i * LANES:(i + 1) * LANES]


def _s5_scan(u_g, ops, bsz):
    mt, wre, wim, cre, cim, atr, ati = ops
    tiles, r, _ = u_g.shape
    g = tiles // CHUNK_HALVES
    w = S5_CHUNK * S5_GROUP
    n_chunks = r // bsz
    p2 = 2 * S5_STATE
    kern = functools.partial(_s5_kernel, n_chunks=n_chunks, bsz=bsz)
    pair_tiles = pl.BlockSpec((2 * CHUNK_HALVES, r, LANES), lambda i: (i, 0, 0))
    return pl.pallas_call(
        kern,
        out_shape=jax.ShapeDtypeStruct((tiles, r, LANES), F32),
        grid=(g // 2,),
        in_specs=[pair_tiles,
                  pl.BlockSpec((2, w, w), lambda i: (i, 0, 0)),
                  pl.BlockSpec((1, 2 * w, p2), lambda i: (i, 0, 0)),
                  pl.BlockSpec((1, 2 * w, p2), lambda i: (i, 0, 0)),
                  pl.BlockSpec((1, p2, 2 * w), lambda i: (i, 0, 0)),
                  pl.BlockSpec((1, p2, 2 * w), lambda i: (i, 0, 0)),
                  pl.BlockSpec((1, 1, p2), lambda i: (i, 0, 0)),
                  pl.BlockSpec((1, 1, p2), lambda i: (i, 0, 0))],
        out_specs=pair_tiles,
        scratch_shapes=[pltpu.VMEM((r, p2), F32)] * 4,
        compiler_params=_params("parallel"),
        name="s5_scan",
    )(u_g, mt, wre, wim, cre, cim, atr, ati)


def _hgrn_kernel(q_ref, f_ref, i_ref, g_ref, lb_ref, og_ref, o_ref, st_ref, *, seqlen):
    c = HG_CHUNK
    nc = seqlen // c
    d = HG_HEAD_DIM
    lb = lb_ref[...]
    q = q_ref[0]
    qs = _silu(q)
    f = lb + (1.0 - lb) * _sigmoid(f_ref[0])
    lf = jnp.log(f)
    k = 1.0 - f
    v = i_ref[0]
    pos = lax.broadcasted_iota(jnp.int32, (seqlen, d), 0) % c
    b = lf
    sh = 1
    while sh < c:
        b = b + jnp.where(pos >= sh, pltpu.roll(b, sh, axis=0), 0.0)
        sh *= 2
    b3 = b.reshape(nc, c, d)
    b_last = b3[:, c - 1:c, :]
    b_ref = b3[:, c // 2 - 1:c // 2, :]
    qs3 = qs.reshape(nc, c, d)
    k3 = k.reshape(nc, c, d)
    v3 = v.reshape(nc, c, d).astype(BF16)
    qe_f = qs3 * jnp.exp(b3 - b_ref)
    ke_f = k3 * jnp.exp(b_ref - b3)
    qe = qe_f.astype(BF16)
    ke = ke_f.astype(BF16)
    kd = (ke_f * jnp.exp(b_last - b_ref)).astype(BF16)
    qb = (qe_f * jnp.exp(b_ref)).astype(BF16)
    scores = jnp.einsum('ctd,csd->cts', qe, ke, preferred_element_type=F32)
    ti = lax.broadcasted_iota(jnp.int32, (c, c), 0)
    si = lax.broadcasted_iota(jnp.int32, (c, c), 1)
    scores = jnp.where((ti >= si)[None], scores, 0.0)
    o_intra = jnp.einsum('cts,csv->ctv', scores.astype(BF16), v3, preferred_element_type=F32)
    ut = jnp.einsum('csv,csd->cvd', v3, kd, preferred_element_type=F32)
    decay = jnp.exp(b_last)
    state = jnp.zeros((d, d), F32)
    for ci in range(nc):
        st_ref[ci] = state.astype(BF16)
        state = decay[ci] * state + ut[ci]
    o_inter = jnp.einsum('ctd,cvd->ctv', qb, st_ref[...], preferred_element_type=F32)
    o = (o_intra + o_inter).reshape(seqlen, d)
    o = _rms(o, og_ref[...])
    o_ref[0] = o * _silu(g_ref[0])


def _hgrn(h4, lower_bound, o_gain, bsz, seqlen):
    d = HG_HEAD_DIM
    kern = functools.partial(_hgrn_kernel, seqlen=seqlen)

    def col(part):
        return pl.BlockSpec((1, seqlen, d), lambda b, h: (b, 0, part * HG_HEADS + h))

    return pl.pallas_call(
        kern,
        out_shape=jax.ShapeDtypeStruct((bsz, seqlen, HG_WIDTH), F32),
        grid=(bsz, HG_HEADS),
        in_specs=[col(0), col(1), col(2), col(3),
                  pl.BlockSpec((1, d), lambda b, h: (0, h)),
                  pl.BlockSpec((1, d), lambda b, h: (0, 0))],
        out_specs=pl.BlockSpec((1, seqlen, d), lambda b, h: (b, 0, h)),
        scratch_shapes=[pltpu.VMEM((seqlen // HG_CHUNK, d, d), BF16)],
        compiler_params=_params("parallel", "parallel"),
        name="hgrn2",
    )(h4, h4, h4, h4, lower_bound.reshape(1, HG_WIDTH), o_gain.reshape(1, d))


def _evenout_kernel(x_ref, ys_ref, u_ref, b_ref, d_ref, wglu_ref, wa_ref, wb_ref, o_ref):
    y = ys_ref[...] + d_ref[...] * u_ref[...]
    y = jax.nn.gelu(y)
    gate = _sigmoid(jnp.dot(y.astype(BF16), wglu_ref[...], preferred_element_type=F32))
    a = (y * gate).astype(BF16)
    mix = (jnp.dot(a, wa_ref[...], preferred_element_type=F32)
           + jnp.dot(b_ref[...].astype(BF16), wb_ref[...], preferred_element_type=F32))
    o_ref[...] = x_ref[...] + mix


def _evenout(x2, ys, u, b_out, d_skip, wglu, wout, tm=512):
    n = x2.shape[0]
    row = lambda w: pl.BlockSpec((tm, w), lambda i: (i, 0))
    full = lambda r, c: pl.BlockSpec((r, c), lambda i: (0, 0))
    return pl.pallas_call(
        _evenout_kernel,
        out_shape=jax.ShapeDtypeStruct((n, D_MODEL), F32),
        grid=(n // tm,),
        in_specs=[row(D_MODEL), row(S5_WIDTH), row(S5_WIDTH), row(HG_WIDTH),
                  full(1, S5_WIDTH), full(S5_WIDTH, S5_WIDTH),
                  full(S5_WIDTH, D_MODEL), full(HG_WIDTH, D_MODEL)],
        out_specs=row(D_MODEL),
        compiler_params=_params("parallel"),
        name="even_out",
    )(x2, ys, u, b_out, d_skip.reshape(1, S5_WIDTH), wglu.astype(BF16),
      wout[:S5_WIDTH].astype(BF16), wout[S5_WIDTH:].astype(BF16))


def _router_kernel(x_ref, g_ref, wr_ref, br_ref, tri_ref, idx_ref, wts_ref, cnt_ref, run_ref):
    @pl.when(pl.program_id(0) == 0)
    def _():
        run_ref[...] = jnp.zeros_like(run_ref)

    h = _rms(x_ref[...], g_ref[...])
    h_hi = h.astype(BF16)
    h_lo = (h - h_hi.astype(F32)).astype(BF16)
    both = _nt_dot(wr_ref[...], h_hi)
    lt = (both[:ROUTER_ROWS] + both[ROUTER_ROWS:] + _nt_dot(wr_ref[:ROUTER_ROWS, :], h_lo)
          + br_ref[...])
    gl = [lt[i:i + 1] for i in range(N_GROUPS)]
    el = [lt[N_GROUPS + i:N_GROUPS + i + 1] for i in range(N_EXPERTS)]
    gmax = jnp.maximum(jnp.maximum(gl[0], gl[1]), jnp.maximum(gl[2], gl[3]))
    gexp = [jnp.exp(v - gmax) for v in gl]
    gsum = gexp[0] + gexp[1] + gexp[2] + gexp[3]
    gprob = [v / gsum for v in gexp]
    g_gate = jnp.maximum(jnp.maximum(gprob[0], gprob[1]), jnp.maximum(gprob[2], gprob[3]))
    g_idx = jnp.where(gprob[0] == g_gate, 0,
                      jnp.where(gprob[1] == g_gate, 1, jnp.where(gprob[2] == g_gate, 2, 3)))
    es = []
    for j in range(EXPERTS_PER_GROUP):
        es.append(jnp.where(g_idx == 0, el[j],
                            jnp.where(g_idx == 1, el[4 + j],
                                      jnp.where(g_idx == 2, el[8 + j], el[12 + j]))))
    emax = jnp.maximum(jnp.maximum(es[0], es[1]), jnp.maximum(es[2], es[3]))
    eexp = [jnp.exp(v - emax) for v in es]
    esum = eexp[0] + eexp[1] + eexp[2] + eexp[3]
    ep = [v / esum for v in eexp]
    p1 = jnp.maximum(jnp.maximum(ep[0], ep[1]), jnp.maximum(ep[2], ep[3]))
    i1 = jnp.where(ep[0] == p1, 0, jnp.where(ep[1] == p1, 1, jnp.where(ep[2] == p1, 2, 3)))
    neg = jnp.float32(-1.0)
    rest = [jnp.where(i1 == j, neg, ep[j]) for j in range(EXPERTS_PER_GROUP)]
    p2 = jnp.maximum(jnp.maximum(rest[0], rest[1]), jnp.maximum(rest[2], rest[3]))
    i2 = jnp.where(rest[0] == p2, 0, jnp.where(rest[1] == p2, 1, jnp.where(rest[2] == p2, 2, 3)))
    wsum = p1 + p2
    w1 = g_gate * (p1 / wsum)
    w2 = g_gate * (p2 / wsum)
    first_lo = i1 < i2
    lo = jnp.where(first_lo, i1, i2)
    hi = jnp.where(first_lo, i2, i1)
    w_lo = jnp.where(first_lo, w1, w2)
    w_hi = jnp.where(first_lo, w2, w1)
    pair = jnp.where(lo == 0, 0, jnp.where(lo == 1, 3, 5)) + hi - lo - 1
    bucket = g_idx * N_PAIRS + pair
    tm = bucket.shape[1]
    rowid = lax.broadcasted_iota(jnp.int32, (BUCKET_ROWS, tm), 0)
    onehot = (rowid == bucket).astype(F32)
    prefix = jnp.dot(onehot.astype(BF16), tri_ref[...], preferred_element_type=F32)
    run = run_ref[...]
    rank = jnp.sum(onehot * (prefix + run), axis=0, keepdims=True)
    run = run + jnp.sum(onehot, axis=1, keepdims=True)
    run_ref[...] = run
    cnt_ref[...] = jnp.broadcast_to(run, cnt_ref.shape)
    idx_ref[...] = jnp.concatenate([bucket, rank.astype(jnp.int32), jnp.zeros((6, tm), jnp.int32)], axis=0)
    wts_ref[...] = jnp.concatenate([w_lo, w_hi, jnp.zeros((6, tm), F32)], axis=0)


def _router(x2, gain, w_rg, b_rg, w_re, b_re, tm=512):
    n = x2.shape[0]
    wr = jnp.concatenate([w_rg, w_re], axis=1).astype(F32).T
    wr = jnp.pad(wr, ((0, ROUTER_ROWS - wr.shape[0]), (0, 0)))
    wr_hi = wr.astype(BF16)
    wr = jnp.concatenate([wr_hi, (wr - wr_hi.astype(F32)).astype(BF16)], axis=0)
    br = jnp.pad(jnp.concatenate([b_rg, b_re]).astype(F32), (0, ROUTER_ROWS - N_GROUPS - N_EXPERTS))
    tri = (np.arange(tm)[:, None] < np.arange(tm)[None, :]).astype(np.float32)
    return pl.pallas_call(
        _router_kernel,
        out_shape=(jax.ShapeDtypeStruct((8, n), jnp.int32),
                   jax.ShapeDtypeStruct((8, n), F32),
                   jax.ShapeDtypeStruct((BUCKET_ROWS, 128), F32)),
        grid=(n // tm,),
        in_specs=[pl.BlockSpec((tm, D_MODEL), lambda i: (i, 0)),
                  pl.BlockSpec((1, D_MODEL), lambda i: (0, 0)),
                  pl.BlockSpec((2 * ROUTER_ROWS, D_MODEL), lambda i: (0, 0)),
                  pl.BlockSpec((ROUTER_ROWS, 1), lambda i: (0, 0)),
                  pl.BlockSpec((tm, tm), lambda i: (0, 0))],
        out_specs=(pl.BlockSpec((8, tm), lambda i: (0, i)),
                   pl.BlockSpec((8, tm), lambda i: (0, i)),
                   pl.BlockSpec((BUCKET_ROWS, 128), lambda i: (0, 0))),
        scratch_shapes=[pltpu.VMEM((BUCKET_ROWS, 1), F32)],
        compiler_params=_params("arbitrary"),
        name="moe_router",
    )(x2, gain.reshape(1, D_MODEL), wr, br.reshape(ROUTER_ROWS, 1), jnp.asarray(tri, dtype=BF16))


ROW_COPY_UNROLL = 8


def _start_row_copies(idx_ref, n_rows, copy_for_row):
    def body(g, carry):
        base = pl.multiple_of(g * ROW_COPY_UNROLL, ROW_COPY_UNROLL)
        for j in range(ROW_COPY_UNROLL):
            copy_for_row(base + j, idx_ref[0, 0, base + j]).start(priority=j % 2)
        return carry

    lax.fori_loop(0, n_rows // ROW_COPY_UNROLL, body, 0)


def _row_slab(view_ref, p):
    return view_ref.at[p >> ROW_TILE_SHIFT, :, p & (ROW_TILE[0] - 1)]


def _view_columns(view_ref, n_cols):
    rows = view_ref.shape[0] * ROW_TILE[0]
    return jnp.concatenate([view_ref[:, c].reshape(rows, LANES) for c in range(n_cols)], axis=1)


def _rows_to_tiles(x):
    rows = x.shape[0]
    return x.reshape(rows * ROW_TILE[0], ROW_TILE[1]).reshape(rows, *ROW_TILE)


def _tiles_to_rows(x3):
    rows = x3.shape[0]
    return x3.reshape(rows * ROW_TILE[0], ROW_TILE[1]).reshape(rows, D_MODEL)


def _dispatch_kernel(tail_blk_ref, tail_on_ref, pos_ref, x_ref, w_ref, xs_ref, buf_ref, zero_ref, sem, *, tile):
    tm = x_ref.shape[0]
    tile_blks = tile // ROW_TILE[0]

    @pl.when(pl.program_id(0) == 0)
    def _():
        zero_ref[...] = jnp.zeros_like(zero_ref)

        def zero_copy(k):
            blk = pl.multiple_of(tail_blk_ref[k], tile_blks)
            return pltpu.make_async_copy(zero_ref, xs_ref.at[pl.ds(blk, tile_blks)], sem)

        for k in range(2 * N_BUCKETS):
            pl.when(tail_on_ref[k] > 0)(lambda k=k: zero_copy(k).start())
        for k in range(2 * N_BUCKETS):
            pl.when(tail_on_ref[k] > 0)(lambda k=k: zero_copy(k).wait())

    buf_ref[:, :X_TILES, :] = _rows_to_tiles(x_ref[...])
    wpad = jnp.concatenate([w_ref[...], jnp.zeros((LANES - w_ref.shape[0], tm), F32)], axis=0)
    buf_ref[:, X_TILES, :] = wpad.T
    _start_row_copies(pos_ref, tm, lambda r, p: pltpu.make_async_copy(buf_ref.at[r], _row_slab(xs_ref, p), sem))
    done = xs_ref.at[pl.ds(0, tm // ROW_TILE[0])]
    pltpu.make_async_copy(done, done, sem).wait()


def _dispatch(x2, wts, pos3, tails, n_rows_sorted, tile, tm):
    n = x2.shape[0]
    tail_blk, tail_on = tails
    grid_spec = pltpu.PrefetchScalarGridSpec(
        num_scalar_prefetch=2,
        grid=(n // tm,),
        in_specs=[pl.BlockSpec((1, 1, tm), lambda i, *_: (i, 0, 0), memory_space=pltpu.SMEM),
                  pl.BlockSpec((tm, D_MODEL), lambda i, *_: (i, 0)),
                  pl.BlockSpec((8, tm), lambda i, *_: (0, i))],
        out_specs=pl.BlockSpec(memory_space=pl.ANY),
        scratch_shapes=[pltpu.VMEM((tm, XS_TILES, LANES), F32),
                        pltpu.VMEM((tile // ROW_TILE[0], XS_TILES, *ROW_TILE), F32),
                        pltpu.SemaphoreType.DMA],
    )
    return pl.pallas_call(
        functools.partial(_dispatch_kernel, tile=tile),
        out_shape=jax.ShapeDtypeStruct((n_rows_sorted // ROW_TILE[0], XS_TILES, *ROW_TILE), F32),
        grid_spec=grid_spec,
        compiler_params=_params("arbitrary"),
        name="moe_dispatch",
    )(tail_blk, tail_on, pos3, x2, wts)


def _experts_kernel(elo_ref, ehi_ref, nvalid_ref, xs_ref, g_ref, wg_lo, wu_lo, wg_hi, wu_hi,
                    wd_lo, wd_hi, o_ref):
    del elo_ref, ehi_ref
    t = pl.program_id(0)

    @pl.when(t < nvalid_ref[0])
    def _():
        rows = xs_ref.shape[0] * ROW_TILE[0]
        xt = _view_columns(xs_ref, X_TILES)
        h = _rms(xt, g_ref[...]).astype(BF16)
        extra = xs_ref[:, X_TILES].reshape(rows, LANES)
        w_lo = extra[:, 0:1]
        w_hi = extra[:, 1:2]

        def expert(wg, wu, wd, w):
            gate = jnp.dot(h, wg[0], preferred_element_type=F32)
            up = jnp.dot(h, wu[0], preferred_element_type=F32)
            hid = (_silu(gate) * up * w).astype(BF16)
            return jnp.dot(hid, wd[0], preferred_element_type=F32)

        out = xt + expert(wg_lo, wu_lo, wd_lo, w_lo) + expert(wg_hi, wu_hi, wd_hi, w_hi)
        for c in range(X_TILES):
            o_ref[:, c] = out[:, c * LANES:(c + 1) * LANES].reshape(o_ref.shape[0], *ROW_TILE)

    @pl.when(t >= nvalid_ref[0])
    def _():
        o_ref[...] = jnp.zeros_like(o_ref)


def _experts(xs, gain, tables, wg, wu, wd, n_tiles, t):
    elo, ehi, nvalid = tables
    blks = t // ROW_TILE[0]
    row = lambda i, elo, ehi, nv: (i, 0, 0, 0)
    row_in = lambda i, elo, ehi, nv: (jnp.minimum(i, nv[0] - 1), 0, 0, 0)
    lo3 = lambda i, elo, ehi, nv: (elo[i], 0, 0)
    hi3 = lambda i, elo, ehi, nv: (ehi[i], 0, 0)
    grid_spec = pltpu.PrefetchScalarGridSpec(
        num_scalar_prefetch=3,
        grid=(n_tiles,),
        in_specs=[pl.BlockSpec((blks, XS_TILES, *ROW_TILE), row_in),
                  pl.BlockSpec((1, D_MODEL), lambda i, *_: (0, 0)),
                  pl.BlockSpec((1, D_MODEL, D_EXPERT), lo3),
                  pl.BlockSpec((1, D_MODEL, D_EXPERT), lo3),
                  pl.BlockSpec((1, D_MODEL, D_EXPERT), hi3),
                  pl.BlockSpec((1, D_MODEL, D_EXPERT), hi3),
                  pl.BlockSpec((1, D_EXPERT, D_MODEL), lo3),
                  pl.BlockSpec((1, D_EXPERT, D_MODEL), hi3)],
        out_specs=pl.BlockSpec((blks, X_TILES, *ROW_TILE), row),
    )
    return pl.pallas_call(
        _experts_kernel,
        out_shape=jax.ShapeDtypeStruct((xs.shape[0], X_TILES, *ROW_TILE), F32),
        grid_spec=grid_spec,
        compiler_params=_params("arbitrary"),
        name="moe_experts",
    )(elo, ehi, nvalid, xs, gain.reshape(1, D_MODEL), wg, wu, wg, wu, wd, wd)


def _combine_kernel(pos_ref, ys_ref, o_ref, buf_ref, sem):
    tm = o_ref.shape[0]
    _start_row_copies(pos_ref, tm, lambda r, p: pltpu.make_async_copy(_row_slab(ys_ref, p), buf_ref.at[r], sem))
    done = ys_ref.at[pl.ds(0, tm // ROW_TILE[0])]
    pltpu.make_async_copy(done, done, sem).wait()
    o_ref[...] = _tiles_to_rows(buf_ref[...])


def _combine(ys, pos3, n, tm):
    return pl.pallas_call(
        _combine_kernel,
        out_shape=jax.ShapeDtypeStruct((n, D_MODEL), F32),
        grid=(n // tm,),
        in_specs=[pl.BlockSpec((1, 1, tm), lambda i: (i, 0, 0), memory_space=pltpu.SMEM),
                  pl.BlockSpec(memory_space=pl.ANY)],
        out_specs=pl.BlockSpec((tm, D_MODEL), lambda i: (i, 0)),
        scratch_shapes=[pltpu.VMEM((tm, *ROW_TILE), F32), pltpu.SemaphoreType.DMA],
        compiler_params=_params("arbitrary"),
        name="moe_combine",
    )(pos3, ys)


def _moe_tables(idx, cnt, n_tiles, t):
    bucket, rank = idx[0], idx[1]
    counts = cnt[:N_BUCKETS, 0].astype(jnp.int32)
    tiles_b = (counts + t - 1) // t
    tile_end = jnp.cumsum(tiles_b)
    pos = (tile_end - tiles_b)[bucket] * t + rank
    total = tile_end[-1]
    tt = jnp.arange(n_tiles, dtype=jnp.int32)
    valid = tt < total
    tb = jnp.sum((tile_end[None, :] <= jnp.where(valid, tt, total - 1)[:, None]).astype(jnp.int32), axis=1)
    tb = jnp.minimum(tb, N_BUCKETS - 1)
    pair_lo = jnp.asarray([0, 0, 0, 1, 1, 2], jnp.int32)
    pair_hi = jnp.asarray([1, 2, 3, 2, 3, 3], jnp.int32)
    base = (tb // N_PAIRS) * EXPERTS_PER_GROUP
    idle = total + jnp.arange(N_BUCKETS, dtype=jnp.int32)
    idle_on = idle < n_tiles
    blks = t // ROW_TILE[0]
    tails = (jnp.concatenate([(tile_end - 1) * blks, jnp.where(idle_on, idle, 0) * blks]),
             jnp.concatenate([tiles_b > 0, idle_on]).astype(jnp.int32))
    return pos, tails, (base + pair_lo[tb % N_PAIRS], base + pair_hi[tb % N_PAIRS], total.reshape(1))


def _moe(x2, gain, w_rg, b_rg, w_re, b_re, wg, wu, wd, t=MOE_TILE, tm=1024):
    n = x2.shape[0]
    idx, wts, cnt = _router(x2, gain, w_rg, b_rg, w_re, b_re)
    n_tiles = n // t + N_BUCKETS
    pos, tails, tables = _moe_tables(idx, cnt, n_tiles, t)
    pos3 = pos.reshape(n // tm, 1, tm)
    xs = _dispatch(x2, wts, pos3, tails, n_tiles * t, t, tm)
    ys = _experts(xs, gain, tables, wg.astype(BF16), wu.astype(BF16), wd.astype(BF16), n_tiles, t)
    return _combine(ys, pos3, n, tm)


LOG2E = math.log2(math.e)
V_EXT = 2 * HEAD_DIM


def _head_norm_t(y_t, n_heads, scale):
    tm = y_t.shape[1]
    y3 = y_t.reshape(n_heads, HEAD_DIM, tm)
    ms = jnp.mean(y3 * y3, axis=1, keepdims=True)
    return y3 * (lax.rsqrt(ms + EPS) * scale)


def _qkv_kernel(x_ref, g_ref, wqt_ref, wkt_ref, wvt_ref, vone_ref, kg_ref, qt_ref, k_ref, vt_ref):
    h = _rms(x_ref[...], g_ref[...]).astype(BF16)
    tm = h.shape[0]
    qn = _head_norm_t(_nt_dot(wqt_ref[...], h), N_Q_HEADS, HEAD_DIM ** -0.5 * LOG2E)
    qt_ref[...] = qn.reshape(N_Q_HEADS * HEAD_DIM, tm).astype(BF16)
    kn = _head_norm_t(_nt_dot(wkt_ref[...], h), N_KV_HEADS, kg_ref[...].reshape(N_KV_HEADS, HEAD_DIM, 1))
    for hk in range(N_KV_HEADS):
        k_ref[hk] = kn[hk].T.astype(BF16)
    vt_ref[...] = (_nt_dot(wvt_ref[...], h) + vone_ref[...]).astype(BF16)


def _qkv(x2, gain, wqkv, q_gain, k_gain, tm=512):
    n = x2.shape[0]
    qw = N_Q_HEADS * HEAD_DIM
    wqt = wqkv[:, :qw].T.astype(BF16)
    wkt = wqkv[:, qw:qw + KV_WIDTH].T.astype(BF16)
    wvt = wqkv[:, qw + KV_WIDTH:].T.astype(BF16).reshape(N_KV_HEADS, HEAD_DIM, D_MODEL)
    wvt = jnp.pad(wvt, ((0, 0), (0, V_EXT - HEAD_DIM), (0, 0))).reshape(N_KV_HEADS * V_EXT, D_MODEL)
    vone = np.zeros((N_KV_HEADS * V_EXT, 1), np.float32)
    vone[HEAD_DIM::V_EXT, 0] = 1.0
    kg = jnp.tile((k_gain.astype(F32) * q_gain.astype(F32)), N_KV_HEADS).reshape(KV_WIDTH, 1)
    full = lambda r, c: pl.BlockSpec((r, c), lambda i: (0, 0))
    return pl.pallas_call(
        _qkv_kernel,
        out_shape=(jax.ShapeDtypeStruct((qw, n), BF16),
                   jax.ShapeDtypeStruct((N_KV_HEADS, n, HEAD_DIM), BF16),
                   jax.ShapeDtypeStruct((N_KV_HEADS * V_EXT, n), BF16)),
        grid=(n // tm,),
        in_specs=[pl.BlockSpec((tm, D_MODEL), lambda i: (i, 0)), full(1, D_MODEL),
                  full(qw, D_MODEL), full(KV_WIDTH, D_MODEL), full(N_KV_HEADS * V_EXT, D_MODEL),
                  full(N_KV_HEADS * V_EXT, 1), full(KV_WIDTH, 1)],
        out_specs=(pl.BlockSpec((qw, tm), lambda i: (0, i)),
                   pl.BlockSpec((N_KV_HEADS, tm, HEAD_DIM), lambda i: (0, i, 0)),
                   pl.BlockSpec((N_KV_HEADS * V_EXT, tm), lambda i: (0, i))),
        compiler_params=_params("parallel"),
        name="odd_qkv",
    )(x2, gain.reshape(1, D_MODEL), wqt, wkt, wvt, jnp.asarray(vone), kg)


def _attn_bias():
    blk = ATT_BLOCK
    qi = np.arange(blk)[None, :]
    ki = np.arange(2 * blk)[:, None]
    dist = qi - ki + blk
    band = (dist >= 0) & (dist < blk)
    slopes = 2.0 ** (-8.0 * np.arange(1, N_Q_HEADS + 1) / N_Q_HEADS)
    pen = -slopes[:, None, None] * dist[None].astype(np.float64) * LOG2E
    inner = np.where(band[None], pen, -np.inf)
    first = np.where((band & (ki >= blk))[None], pen, -np.inf)
    tab = np.stack([inner, first]).astype(np.float32)
    tab = tab.reshape(2, N_KV_HEADS, GQA_GROUP, 2 * blk, blk).transpose(0, 1, 3, 2, 4)
    return tab.reshape(2, N_KV_HEADS, 2 * blk, GQA_GROUP * blk)


def _attn_kernel(qt_ref, kp_ref, kc_ref, vtp_ref, vtc_ref, bias_ref, sink_ref, x_ref, wo_ref, o_ref):
    first = (pl.program_id(1) == 0).astype(jnp.int32)
    vt = jnp.concatenate([vtp_ref[...], vtc_ref[...]], axis=1)
    att_t = []
    for hk in range(N_KV_HEADS):
        keys = jnp.concatenate([kp_ref[hk], kc_ref[hk]], axis=0)
        q_t = jnp.concatenate([qt_ref[(hk * GQA_GROUP + g) * HEAD_DIM:(hk * GQA_GROUP + g + 1) * HEAD_DIM, :]
                               for g in range(GQA_GROUP)], axis=1)
        s = jnp.dot(keys, q_t, preferred_element_type=F32) + bias_ref[first, hk]
        sink = sink_ref[hk]
        m = jnp.maximum(jnp.max(s, axis=0, keepdims=True), sink)
        p = jnp.exp2(s - m).astype(BF16)
        pv = jnp.dot(vt[hk * V_EXT:(hk + 1) * V_EXT, :], p, preferred_element_type=F32)
        den = pv[HEAD_DIM:HEAD_DIM + 1, :] + jnp.exp2(sink - m)
        o_t = (pv[:HEAD_DIM, :] * (1.0 / den)).astype(BF16)
        att_t += [o_t[:, g * ATT_BLOCK:(g + 1) * ATT_BLOCK] for g in range(GQA_GROUP)]
    att_t = jnp.concatenate(att_t, axis=0)
    mix = lax.dot_general(att_t, wo_ref[...], (((0,), (0,)), ((), ())), preferred_element_type=F32)
    o_ref[...] = x_ref[...] + mix


def _attn(qt, k, vt, x2, sinks, wo, bsz, seqlen):
    blk = ATT_BLOCK
    nb = seqlen // blk
    qw = N_Q_HEADS * HEAD_DIM
    cols = GQA_GROUP * blk
    cur = lambda b, n: (b * nb + n, 0)
    cur_t = lambda b, n: (0, b * nb + n)
    prev_t = lambda b, n: (0, b * nb + jnp.maximum(n - 1, 0))
    sink_row = jnp.repeat(sinks.astype(F32) * LOG2E, blk).reshape(N_KV_HEADS, 1, cols)
    return pl.pallas_call(
        _attn_kernel,
        out_shape=jax.ShapeDtypeStruct((bsz * seqlen, D_MODEL), F32),
        grid=(bsz, nb),
        in_specs=[pl.BlockSpec((qw, blk), cur_t),
                  pl.BlockSpec((N_KV_HEADS, blk, HEAD_DIM), lambda b, n: (0, b * nb + jnp.maximum(n - 1, 0), 0)),
                  pl.BlockSpec((N_KV_HEADS, blk, HEAD_DIM), lambda b, n: (0, b * nb + n, 0)),
                  pl.BlockSpec((N_KV_HEADS * V_EXT, blk), prev_t),
                  pl.BlockSpec((N_KV_HEADS * V_EXT, blk), cur_t),
                  pl.BlockSpec((2, N_KV_HEADS, 2 * blk, cols), lambda b, n: (0, 0, 0, 0)),
                  pl.BlockSpec((N_KV_HEADS, 1, cols), lambda b, n: (0, 0, 0)),
                  pl.BlockSpec((blk, D_MODEL), cur),
                  pl.BlockSpec((qw, D_MODEL), lambda b, n: (0, 0))],
        out_specs=pl.BlockSpec((blk, D_MODEL), cur),
        compiler_params=_params("parallel", "parallel"),
        name="odd_attn",
    )(qt, k, k, vt, vt, jnp.asarray(_attn_bias()), sink_row, x2, wo.astype(BF16))


def kernel(x, even_mix_norm, even_in_proj, s5_lambda_re, s5_lambda_im, s5_log_step, s5_b_re, s5_b_im,
           s5_c_re, s5_c_im, s5_d, s5_glu_w, hgrn_lower_bounds, hgrn_o_norm, even_out_proj, odd_mix_norm,
           odd_wqkv, odd_q_norm, odd_k_norm, odd_sinks, odd_out_proj, moe_norm, moe_router_group,
           moe_router_group_bias, moe_router_expert, moe_router_expert_bias, moe_w_gate, moe_w_up,
           moe_w_down):
    bsz, seqlen, dm = x.shape
    n = bsz * seqlen
    x2 = x.reshape(n, dm)
    lower_bounds = jnp.cumsum(jax.nn.softmax(hgrn_lower_bounds.astype(F32), axis=0), axis=0)

    def moe(xx, layer):
        return _moe(xx, moe_norm[layer], moe_router_group[layer], moe_router_group_bias[layer],
                    moe_router_expert[layer], moe_router_expert_bias[layer],
                    moe_w_gate[layer], moe_w_up[layer], moe_w_down[layer])

    u, h4 = _inproj(x2, even_mix_norm[0], even_in_proj[0].astype(BF16))
    ops = _s5_operators(s5_lambda_re[0], s5_lambda_im[0], s5_log_step[0], s5_b_re[0], s5_b_im[0],
                        s5_c_re[0], s5_c_im[0])
    u_g = _s5_pack(u.reshape(bsz, seqlen, S5_WIDTH))
    y_g = _s5_scan(u_g, ops, bsz)
    ys = _s5_unpack(y_g, bsz, seqlen).reshape(n, S5_WIDTH)
    b_out = _hgrn(h4.reshape(bsz, seqlen, 4 * HG_WIDTH), lower_bounds[0], hgrn_o_norm[0], bsz, seqlen)
    x2 = _evenout(x2, ys, u, b_out.reshape(n, HG_WIDTH), s5_d[0], s5_glu_w[0], even_out_proj[0])
    x2 = moe(x2, 0)

    q, kt, v = _qkv(x2, odd_mix_norm[0], odd_wqkv[0], odd_q_norm[0], odd_k_norm[0])
    x2 = _attn(q, kt, v, x2, odd_sinks[0], odd_out_proj[0], bsz, seqlen)
    x2 = moe(x2, 1)
    return x2.reshape(bsz, seqlen, dm)
```

```python
import functools
import math

import jax
import jax.numpy as jnp
import numpy as np
from jax import lax
from jax.experimental import pallas as pl
from jax.experimental.pallas import tpu as pltpu

F32 = jnp.float32
BF16 = jnp.bfloat16
EPS = 1e-6

D_MODEL = 1024
S5_WIDTH = 512
S5_GROUP = 16
S5_GROUPS = 32
S5_STATE = 64
S5_CHUNK = 16
HG_WIDTH = 512
HG_HEAD_DIM = 128
HG_HEADS = 4
HG_CHUNK = 32
HEAD_DIM = 64
N_Q_HEADS = 16
N_KV_HEADS = 2
GQA_GROUP = 8
KV_WIDTH = N_KV_HEADS * HEAD_DIM
ATT_BLOCK = 128
N_GROUPS = 4
EXPERTS_PER_GROUP = 4
N_EXPERTS = 16
D_EXPERT = 256
ROUTER_ROWS = 32
N_PAIRS = 6
N_BUCKETS = N_GROUPS * N_PAIRS
BUCKET_ROWS = 32
MOE_TILE = 256
ROW_TILE = (8, 128)
ROW_TILE_SHIFT = ROW_TILE[0].bit_length() - 1
X_TILES = D_MODEL // ROW_TILE[1]
XS_TILES = X_TILES + 1

VMEM_LIMIT_BYTES = 56 * 1024 * 1024


def _params(*semantics):
    return pltpu.CompilerParams(dimension_semantics=semantics, vmem_limit_bytes=VMEM_LIMIT_BYTES)


def _rms(xf, gain):
    return xf * lax.rsqrt(jnp.mean(xf * xf, axis=-1, keepdims=True) + EPS) * gain


def _nt_dot(w_t, h):
    return lax.dot_general(w_t, h, (((1,), (1,)), ((), ())), preferred_element_type=F32)


def _sigmoid(x):
    return 0.5 * jnp.tanh(0.5 * x) + 0.5


def _silu(x):
    return x * _sigmoid(x)


def _inproj_kernel(x_ref, g_ref, w_ref, u_ref, h4_ref):
    h = _rms(x_ref[...], g_ref[...]).astype(BF16)
    p = jnp.dot(h, w_ref[...], preferred_element_type=F32)
    u_ref[...] = p[:, :S5_WIDTH]
    h4_ref[...] = p[:, S5_WIDTH:]


def _inproj(x2, gain, w_bf16, tm=512):
    n = x2.shape[0]
    e_in = w_bf16.shape[1]
    return pl.pallas_call(
        _inproj_kernel,
        out_shape=(jax.ShapeDtypeStruct((n, S5_WIDTH), F32),
                   jax.ShapeDtypeStruct((n, e_in - S5_WIDTH), F32)),
        grid=(n // tm,),
        in_specs=[pl.BlockSpec((tm, D_MODEL), lambda i: (i, 0)),
                  pl.BlockSpec((1, D_MODEL), lambda i: (0, 0)),
                  pl.BlockSpec((D_MODEL, e_in), lambda i: (0, 0))],
        out_specs=(pl.BlockSpec((tm, S5_WIDTH), lambda i: (i, 0)),
                   pl.BlockSpec((tm, e_in - S5_WIDTH), lambda i: (i, 0))),
        compiler_params=_params("parallel"),
        name="even_inproj",
    )(x2, gain.reshape(1, D_MODEL), w_bf16)


def _s5_lagkernel_kernel(ca_ref, bb_ref, k_ref):
    k_ref[0] = jnp.dot(ca_ref[0], bb_ref[0], preferred_element_type=F32,
                       precision=lax.Precision.HIGHEST)


def _s5_lagkernel(ca, bb):
    g, rows, k = ca.shape
    return pl.pallas_call(
        _s5_lagkernel_kernel,
        out_shape=jax.ShapeDtypeStruct((g, rows, S5_GROUP), F32),
        grid=(g,),
        in_specs=[pl.BlockSpec((1, rows, k), lambda i: (i, 0, 0)),
                  pl.BlockSpec((1, k, S5_GROUP), lambda i: (i, 0, 0))],
        out_specs=pl.BlockSpec((1, rows, S5_GROUP), lambda i: (i, 0, 0)),
        compiler_params=_params("parallel"),
        name="s5_lag_kernel",
    )(ca, bb)


def _s5_operators(lam_re, lam_im, log_step, b_re, b_im, c_re, c_im):
    t = S5_CHUNK
    lr, li = lam_re.astype(F32), lam_im.astype(F32)
    step = jnp.exp(log_step.astype(F32))[:, None]
    mag = jnp.exp(lr * step)
    ab_re = mag * jnp.cos(li * step)
    ab_im = mag * jnp.sin(li * step)
    den = lr * lr + li * li
    nr, ni = ab_re - 1.0, ab_im
    z_re = (nr * lr + ni * li) / den
    z_im = (ni * lr - nr * li) / den
    br, bi = b_re.astype(F32), b_im.astype(F32)
    bb_re = z_re[..., None] * br - z_im[..., None] * bi
    bb_im = z_re[..., None] * bi + z_im[..., None] * br
    kk = jnp.arange(t + 1, dtype=F32)[:, None, None]
    pmag = jnp.exp(kk * (lr * step)[None])
    pw_re = pmag * jnp.cos(kk * (li * step)[None])
    pw_im = pmag * jnp.sin(kk * (li * step)[None])
    cr = jnp.transpose(c_re.astype(F32), (0, 1, 2))
    ci = c_im.astype(F32)
    ca_re = cr[None] * pw_re[:, :, None, :] - ci[None] * pw_im[:, :, None, :]
    ca_im = cr[None] * pw_im[:, :, None, :] + ci[None] * pw_re[:, :, None, :]
    g = lr.shape[0]
    ca_cat = jnp.concatenate([ca_re[:t], -ca_im[:t]], axis=-1)
    ca_cat = jnp.transpose(ca_cat, (1, 0, 2, 3)).reshape(g, t * S5_GROUP, 2 * S5_STATE)
    bb_cat = jnp.concatenate([bb_re, bb_im], axis=1)
    kern = _s5_lagkernel(ca_cat, bb_cat).reshape(g, t, S5_GROUP, S5_GROUP)
    s_idx = jnp.arange(t)[:, None]
    t_idx = jnp.arange(t)[None, :]
    lag = t_idx - s_idx
    kg = kern[:, jnp.clip(lag, 0, t - 1)]
    kg = jnp.where((lag >= 0)[None, :, :, None, None], kg, 0.0)
    mt = jnp.transpose(kg, (0, 1, 4, 2, 3)).reshape(g, t * S5_GROUP, t * S5_GROUP)
    pr = pw_re[:t][::-1]
    pi = pw_im[:t][::-1]
    sb_re = pr[:, :, :, None] * bb_re[None] - pi[:, :, :, None] * bb_im[None]
    sb_im = pr[:, :, :, None] * bb_im[None] + pi[:, :, :, None] * bb_re[None]
    sb_re = jnp.transpose(sb_re, (1, 0, 3, 2)).reshape(g, t * S5_GROUP, S5_STATE)
    sb_im = jnp.transpose(sb_im, (1, 0, 3, 2)).reshape(g, t * S5_GROUP, S5_STATE)
    cp_re = jnp.transpose(ca_re[1:], (1, 3, 0, 2)).reshape(g, S5_STATE, t * S5_GROUP)
    cp_im = jnp.transpose(-ca_im[1:], (1, 3, 0, 2)).reshape(g, S5_STATE, t * S5_GROUP)

    def pair_rows(m):
        m = m.reshape(g // 2, 2, m.shape[1], m.shape[2])
        z = jnp.zeros_like(m[:, 0])
        top = jnp.concatenate([m[:, 0], z], axis=2)
        bot = jnp.concatenate([z, m[:, 1]], axis=2)
        return jnp.concatenate([top, bot], axis=1)

    at_re = pw_re[t].reshape(g // 2, 1, 2 * S5_STATE)
    at_im = pw_im[t].reshape(g // 2, 1, 2 * S5_STATE)
    return (mt.astype(BF16), pair_rows(sb_re).astype(BF16), pair_rows(sb_im).astype(BF16),
            pair_rows(cp_re).astype(BF16), pair_rows(cp_im).astype(BF16), at_re, at_im)


PACK_TOKENS = 512


LANES = 128
GROUPS_PER_TILE = LANES // S5_GROUP
TOKENS_PER_TILE = LANES // S5_GROUP
CHUNK_HALVES = S5_CHUNK // TOKENS_PER_TILE
PACK_CHUNKS = PACK_TOKENS // S5_CHUNK


def _block_swap_matrix():
    a, b, h = np.meshgrid(np.arange(TOKENS_PER_TILE), np.arange(GROUPS_PER_TILE), np.arange(S5_GROUP),
                          indexing="ij")
    src = (a * GROUPS_PER_TILE + b) * S5_GROUP + h
    dst = (b * TOKENS_PER_TILE + a) * S5_GROUP + h
    m = np.zeros((src.size, src.size), np.float32)
    m[src.ravel(), dst.ravel()] = 1.0
    return jnp.asarray(m, dtype=BF16)


def _s5_pack_kernel(u_ref, swap_ref, o_ref, *, bsz):
    for j in range(CHUNK_HALVES):
        rows = [jnp.concatenate([u_ref[b, pl.ds(j * TOKENS_PER_TILE + tt, PACK_CHUNKS, stride=S5_CHUNK), :]
                                 for tt in range(TOKENS_PER_TILE)], axis=1) for b in range(bsz)]
        lhs = jnp.concatenate(rows, axis=0).astype(BF16)
        out = jnp.dot(lhs, swap_ref[...], preferred_element_type=F32)
        for g in range(GROUPS_PER_TILE):
            for b in range(bsz):
                o_ref[g * CHUNK_HALVES + j, pl.ds(b, PACK_CHUNKS, stride=bsz), :] = (
                    out[b * PACK_CHUNKS:(b + 1) * PACK_CHUNKS, g * LANES:(g + 1) * LANES])


def _s5_pack(u3):
    bsz, seqlen, w = u3.shape
    rows = PACK_CHUNKS * bsz
    swap = _block_swap_matrix()
    return pl.pallas_call(
        functools.partial(_s5_pack_kernel, bsz=bsz),
        out_shape=jax.ShapeDtypeStruct((S5_GROUPS * CHUNK_HALVES, seqlen // S5_CHUNK * bsz, LANES), F32),
        grid=(seqlen // PACK_TOKENS, w // LANES),
        in_specs=[pl.BlockSpec((bsz, PACK_TOKENS, LANES), lambda i, k: (0, i, k)),
                  pl.BlockSpec(swap.shape, lambda i, k: (0, 0))],
        out_specs=pl.BlockSpec((GROUPS_PER_TILE * CHUNK_HALVES, rows, LANES), lambda i, k: (k, i, 0)),
        compiler_params=_params("parallel", "parallel"),
        name="s5_pack",
    )(u3, swap)


def _s5_unpack_kernel(y_ref, swap_ref, o_ref, *, bsz):
    for j in range(CHUNK_HALVES):
        rows = [jnp.concatenate([y_ref[g * CHUNK_HALVES + j, pl.ds(b, PACK_CHUNKS, stride=bsz), :]
                                 for g in range(GROUPS_PER_TILE)], axis=1) for b in range(bsz)]
        lhs = jnp.concatenate(rows, axis=0).astype(BF16)
        out = jnp.dot(lhs, swap_ref[...], preferred_element_type=F32)
        for tt in range(TOKENS_PER_TILE):
            for b in range(bsz):
                o_ref[b, pl.ds(j * TOKENS_PER_TILE + tt, PACK_CHUNKS, stride=S5_CHUNK), :] = (
                    out[b * PACK_CHUNKS:(b + 1) * PACK_CHUNKS, tt * LANES:(tt + 1) * LANES])


def _s5_unpack(y_g, bsz, seqlen):
    rows = PACK_CHUNKS * bsz
    swap = _block_swap_matrix()
    return pl.pallas_call(
        functools.partial(_s5_unpack_kernel, bsz=bsz),
        out_shape=jax.ShapeDtypeStruct((bsz, seqlen, S5_WIDTH), F32),
        grid=(seqlen // PACK_TOKENS, S5_WIDTH // LANES),
        in_specs=[pl.BlockSpec((GROUPS_PER_TILE * CHUNK_HALVES, rows, LANES), lambda i, k: (k, i, 0)),
                  pl.BlockSpec(swap.shape, lambda i, k: (0, 0))],
        out_specs=pl.BlockSpec((bsz, PACK_TOKENS, LANES), lambda i, k: (0, i, k)),
        compiler_params=_params("parallel", "parallel"),
        name="s5_unpack",
    )(y_g, swap)


def _s5_kernel(u_ref, mt_ref, wre_ref, wim_ref, cre_ref, cim_ref, atr_ref, ati_ref, y_ref,
               sre_ref, sim_ref, xre_ref, xim_ref, *, n_chunks, bsz):
    ucat = jnp.concatenate([u_ref[i] for i in range(2 * CHUNK_HALVES)], axis=1).astype(BF16)
    w = S5_CHUNK * S5_GROUP
    u0 = ucat[:, :w]
    u1 = ucat[:, w:]
    sre_ref[...] = jnp.dot(ucat, wre_ref[0], preferred_element_type=F32)
    sim_ref[...] = jnp.dot(ucat, wim_ref[0], preferred_element_type=F32)
    atr = jnp.broadcast_to(atr_ref[0], (bsz, 2 * S5_STATE))
    ati = jnp.broadcast_to(ati_ref[0], (bsz, 2 * S5_STATE))

    def body(c, carry):
        xr, xi = carry
        rows = pl.ds(pl.multiple_of(c * bsz, bsz), bsz)
        xre_ref[rows, :] = xr
        xim_ref[rows, :] = xi
        nxr = atr * xr - ati * xi + sre_ref[rows, :]
        nxi = atr * xi + ati * xr + sim_ref[rows, :]
        return nxr, nxi

    zero = jnp.zeros((bsz, 2 * S5_STATE), F32)
    lax.fori_loop(0, n_chunks, body, (zero, zero))
    ycar = (jnp.dot(xre_ref[...].astype(BF16), cre_ref[0], preferred_element_type=F32)
            + jnp.dot(xim_ref[...].astype(BF16), cim_ref[0], preferred_element_type=F32))
    y0 = jnp.dot(u0, mt_ref[0], preferred_element_type=F32) + ycar[:, :w]
    y1 = jnp.dot(u1, mt_ref[1], preferred_element_type=F32) + ycar[:, w:]
    for i in range(CHUNK_HALVES):
        y_ref[i] = y0[:, i * LANES:(i + 1) * LANES]
        y_ref[CHUNK_HALVES + i] = y1[:, i * LANES:(i + 1) * LANES]


def _s5_scan(u_g, ops, bsz):
    mt, wre, wim, cre, cim, atr, ati = ops
    tiles, r, _ = u_g.shape
    g = tiles // CHUNK_HALVES
    w = S5_CHUNK * S5_GROUP
    n_chunks = r // bsz
    p2 = 2 * S5_STATE
    kern = functools.partial(_s5_kernel, n_chunks=n_chunks, bsz=bsz)
    pair_tiles = pl.BlockSpec((2 * CHUNK_HALVES, r, LANES), lambda i: (i, 0, 0))
    return pl.pallas_call(
        kern,
        out_shape=jax.ShapeDtypeStruct((tiles, r, LANES), F32),
        grid=(g // 2,),
        in_specs=[pair_tiles,
                  pl.BlockSpec((2, w, w), lambda i: (i, 0, 0)),
                  pl.BlockSpec((1, 2 * w, p2), lambda i: (i, 0, 0)),
                  pl.BlockSpec((1, 2 * w, p2), lambda i: (i, 0, 0)),
                  pl.BlockSpec((1, p2, 2 * w), lambda i: (i, 0, 0)),
                  pl.BlockSpec((1, p2, 2 * w), lambda i: (i, 0, 0)),
                  pl.BlockSpec((1, 1, p2), lambda i: (i, 0, 0)),
                  pl.BlockSpec((1, 1, p2), lambda i: (i, 0, 0))],
        out_specs=pair_tiles,
        scratch_shapes=[pltpu.VMEM((r, p2), F32)] * 4,
        compiler_params=_params("parallel"),
        name="s5_scan",
    )(u_g, mt, wre, wim, cre, cim, atr, ati)


def _hgrn_kernel(q_ref, f_ref, i_ref, g_ref, lb_ref, og_ref, o_ref, st_ref, *, seqlen):
    c = HG_CHUNK
    nc = seqlen // c
    d = HG_HEAD_DIM
    lb = lb_ref[...]
    q = q_ref[0]
    qs = _silu(q)
    f = lb + (1.0 - lb) * _sigmoid(f_ref[0])
    lf = jnp.log(f)
    k = 1.0 - f
    v = i_ref[0]
    pos = lax.broadcasted_iota(jnp.int32, (seqlen, d), 0) % c
    b = lf
    sh = 1
    while sh < c:
        b = b + jnp.where(pos >= sh, pltpu.roll(b, sh, axis=0), 0.0)
        sh *= 2
    b3 = b.reshape(nc, c, d)
    b_last = b3[:, c - 1:c, :]
    b_ref = b3[:, c // 2 - 1:c // 2, :]
    qs3 = qs.reshape(nc, c, d)
    k3 = k.reshape(nc, c, d)
    v3 = v.reshape(nc, c, d).astype(BF16)
    qe_f = qs3 * jnp.exp(b3 - b_ref)
    ke_f = k3 * jnp.exp(b_ref - b3)
    qe = qe_f.astype(BF16)
    ke = ke_f.astype(BF16)
    kd = (ke_f * jnp.exp(b_last - b_ref)).astype(BF16)
    qb = (qe_f * jnp.exp(b_ref)).astype(BF16)
    scores = jnp.einsum('ctd,csd->cts', qe, ke, preferred_element_type=F32)
    ti = lax.broadcasted_iota(jnp.int32, (c, c), 0)
    si = lax.broadcasted_iota(jnp.int32, (c, c), 1)
    scores = jnp.where((ti >= si)[None], scores, 0.0)
    o_intra = jnp.einsum('cts,csv->ctv', scores.astype(BF16), v3, preferred_element_type=F32)
    ut = jnp.einsum('csv,csd->cvd', v3, kd, preferred_element_type=F32)
    decay = jnp.exp(b_last)
    state = jnp.zeros((d, d), F32)
    for ci in range(nc):
        st_ref[ci] = state.astype(BF16)
        state = decay[ci] * state + ut[ci]
    o_inter = jnp.einsum('ctd,cvd->ctv', qb, st_ref[...], preferred_element_type=F32)
    o = (o_intra + o_inter).reshape(seqlen, d)
    o = _rms(o, og_ref[...])
    o_ref[0] = (o * _silu(g_ref[0])).astype(BF16)


def _hgrn(h4, lower_bound, o_gain, bsz, seqlen):
    d = HG_HEAD_DIM
    kern = functools.partial(_hgrn_kernel, seqlen=seqlen)

    def col(part):
        return pl.BlockSpec((1, seqlen, d), lambda b, h: (b, 0, part * HG_HEADS + h))

    return pl.pallas_call(
        kern,
        out_shape=jax.ShapeDtypeStruct((bsz, seqlen, HG_WIDTH), BF16),
        grid=(bsz, HG_HEADS),
        in_specs=[col(0), col(1), col(2), col(3),
                  pl.BlockSpec((1, d), lambda b, h: (0, h)),
                  pl.BlockSpec((1, d), lambda b, h: (0, 0))],
        out_specs=pl.BlockSpec((1, seqlen, d), lambda b, h: (b, 0, h)),
        scratch_shapes=[pltpu.VMEM((seqlen // HG_CHUNK, d, d), BF16)],
        compiler_params=_params("parallel", "parallel"),
        name="hgrn2",
    )(h4, h4, h4, h4, lower_bound.reshape(1, HG_WIDTH), o_gain.reshape(1, d))


def _evenout_kernel(x_ref, ys_ref, u_ref, b_ref, d_ref, wglu_ref, wa_ref, wb_ref, o_ref):
    y = ys_ref[...] + d_ref[...] * u_ref[...]
    y = jax.nn.gelu(y)
    gate = _sigmoid(jnp.dot(y.astype(BF16), wglu_ref[...], preferred_element_type=F32))
    a = (y * gate).astype(BF16)
    mix = (jnp.dot(a, wa_ref[...], preferred_element_type=F32)
           + jnp.dot(b_ref[...], wb_ref[...], preferred_element_type=F32))
    o_ref[...] = x_ref[...] + mix


def _evenout(x2, ys, u, b_out, d_skip, wglu, wout, tm=512):
    n = x2.shape[0]
    row = lambda w: pl.BlockSpec((tm, w), lambda i: (i, 0))
    full = lambda r, c: pl.BlockSpec((r, c), lambda i: (0, 0))
    return pl.pallas_call(
        _evenout_kernel,
        out_shape=jax.ShapeDtypeStruct((n, D_MODEL), F32),
        grid=(n // tm,),
        in_specs=[row(D_MODEL), row(S5_WIDTH), row(S5_WIDTH), row(HG_WIDTH),
                  full(1, S5_WIDTH), full(S5_WIDTH, S5_WIDTH),
                  full(S5_WIDTH, D_MODEL), full(HG_WIDTH, D_MODEL)],
        out_specs=row(D_MODEL),
        compiler_params=_params("parallel"),
        name="even_out",
    )(x2, ys, u, b_out, d_skip.reshape(1, S5_WIDTH), wglu.astype(BF16),
      wout[:S5_WIDTH].astype(BF16), wout[S5_WIDTH:].astype(BF16))


def _router_kernel(x_ref, g_ref, wr_ref, br_ref, tri_ref, idx_ref, wts_ref, cnt_ref, run_ref):
    @pl.when(pl.program_id(0) == 0)
    def _():
        run_ref[...] = jnp.zeros_like(run_ref)

    h = _rms(x_ref[...], g_ref[...])
    h_hi = h.astype(BF16)
    h_lo = (h - h_hi.astype(F32)).astype(BF16)
    both = _nt_dot(wr_ref[...], h_hi)
    lt = (both[:ROUTER_ROWS] + both[ROUTER_ROWS:] + _nt_dot(wr_ref[:ROUTER_ROWS, :], h_lo)
          + br_ref[...])
    gl = [lt[i:i + 1] for i in range(N_GROUPS)]
    el = [lt[N_GROUPS + i:N_GROUPS + i + 1] for i in range(N_EXPERTS)]
    gmax = jnp.maximum(jnp.maximum(gl[0], gl[1]), jnp.maximum(gl[2], gl[3]))
    gexp = [jnp.exp(v - gmax) for v in gl]
    gsum = gexp[0] + gexp[1] + gexp[2] + gexp[3]
    gprob = [v / gsum for v in gexp]
    g_gate = jnp.maximum(jnp.maximum(gprob[0], gprob[1]), jnp.maximum(gprob[2], gprob[3]))
    g_idx = jnp.where(gprob[0] == g_gate, 0,
                      jnp.where(gprob[1] == g_gate, 1, jnp.where(gprob[2] == g_gate, 2, 3)))
    es = []
    for j in range(EXPERTS_PER_GROUP):
        es.append(jnp.where(g_idx == 0, el[j],
                            jnp.where(g_idx == 1, el[4 + j],
                                      jnp.where(g_idx == 2, el[8 + j], el[12 + j]))))
    emax = jnp.maximum(jnp.maximum(es[0], es[1]), jnp.maximum(es[2], es[3]))
    eexp = [jnp.exp(v - emax) for v in es]
    esum = eexp[0] + eexp[1] + eexp[2] + eexp[3]
    ep = [v / esum for v in eexp]
    p1 = jnp.maximum(jnp.maximum(ep[0], ep[1]), jnp.maximum(ep[2], ep[3]))
    i1 = jnp.where(ep[0] == p1, 0, jnp.where(ep[1] == p1, 1, jnp.where(ep[2] == p1, 2, 3)))
    neg = jnp.float32(-1.0)
    rest = [jnp.where(i1 == j, neg, ep[j]) for j in range(EXPERTS_PER_GROUP)]
    p2 = jnp.maximum(jnp.maximum(rest[0], rest[1]), jnp.maximum(rest[2], rest[3]))
    i2 = jnp.where(rest[0] == p2, 0, jnp.where(rest[1] == p2, 1, jnp.where(rest[2] == p2, 2, 3)))
    wsum = p1 + p2
    w1 = g_gate * (p1 / wsum)
    w2 = g_gate * (p2 / wsum)
    first_lo = i1 < i2
    lo = jnp.where(first_lo, i1, i2)
    hi = jnp.where(first_lo, i2, i1)
    w_lo = jnp.where(first_lo, w1, w2)
    w_hi = jnp.where(first_lo, w2, w1)
    pair = jnp.where(lo == 0, 0, jnp.where(lo == 1, 3, 5)) + hi - lo - 1
    bucket = g_idx * N_PAIRS + pair
    tm = bucket.shape[1]
    rowid = lax.broadcasted_iota(jnp.int32, (BUCKET_ROWS, tm), 0)
    onehot = (rowid == bucket).astype(F32)
    prefix = jnp.dot(onehot.astype(BF16), tri_ref[...], preferred_element_type=F32)
    run = run_ref[...]
    rank = jnp.sum(onehot * (prefix + run), axis=0, keepdims=True)
    run = run + jnp.sum(onehot, axis=1, keepdims=True)
    run_ref[...] = run
    cnt_ref[...] = jnp.broadcast_to(run, cnt_ref.shape)
    idx_ref[...] = jnp.concatenate([bucket, rank.astype(jnp.int32), jnp.zeros((6, tm), jnp.int32)], axis=0)
    wts_ref[...] = jnp.concatenate([w_lo, w_hi, jnp.zeros((6, tm), F32)], axis=0)


def _router(x2, gain, w_rg, b_rg, w_re, b_re, tm=512):
    n = x2.shape[0]
    wr = jnp.concatenate([w_rg, w_re], axis=1).astype(F32).T
    wr = jnp.pad(wr, ((0, ROUTER_ROWS - wr.shape[0]), (0, 0)))
    wr_hi = wr.astype(BF16)
    wr = jnp.concatenate([wr_hi, (wr - wr_hi.astype(F32)).astype(BF16)], axis=0)
    br = jnp.pad(jnp.concatenate([b_rg, b_re]).astype(F32), (0, ROUTER_ROWS - N_GROUPS - N_EXPERTS))
    tri = (np.arange(tm)[:, None] < np.arange(tm)[None, :]).astype(np.float32)
    return pl.pallas_call(
        _router_kernel,
        out_shape=(jax.ShapeDtypeStruct((8, n), jnp.int32),
                   jax.ShapeDtypeStruct((8, n), F32),
                   jax.ShapeDtypeStruct((BUCKET_ROWS, 128), F32)),
        grid=(n // tm,),
        in_specs=[pl.BlockSpec((tm, D_MODEL), lambda i: (i, 0)),
                  pl.BlockSpec((1, D_MODEL), lambda i: (0, 0)),
                  pl.BlockSpec((2 * ROUTER_ROWS, D_MODEL), lambda i: (0, 0)),
                  pl.BlockSpec((ROUTER_ROWS, 1), lambda i: (0, 0)),
                  pl.BlockSpec((tm, tm), lambda i: (0, 0))],
        out_specs=(pl.BlockSpec((8, tm), lambda i: (0, i)),
                   pl.BlockSpec((8, tm), lambda i: (0, i)),
                   pl.BlockSpec((BUCKET_ROWS, 128), lambda i: (0, 0))),
        scratch_shapes=[pltpu.VMEM((BUCKET_ROWS, 1), F32)],
        compiler_params=_params("arbitrary"),
        name="moe_router",
    )(x2, gain.reshape(1, D_MODEL), wr, br.reshape(ROUTER_ROWS, 1), jnp.asarray(tri, dtype=BF16))


ROW_COPY_UNROLL = 8


def _start_row_copies(idx_ref, n_rows, copy_for_row):
    def body(g, carry):
        base = pl.multiple_of(g * ROW_COPY_UNROLL, ROW_COPY_UNROLL)
        for j in range(ROW_COPY_UNROLL):
            copy_for_row(base + j, idx_ref[0, 0, base + j]).start(priority=j % 2)
        return carry

    lax.fori_loop(0, n_rows // ROW_COPY_UNROLL, body, 0)


def _row_slab(view_ref, p):
    return view_ref.at[p >> ROW_TILE_SHIFT, :, p & (ROW_TILE[0] - 1)]


def _view_columns(view_ref, n_cols):
    rows = view_ref.shape[0] * ROW_TILE[0]
    return jnp.concatenate([view_ref[:, c].reshape(rows, LANES) for c in range(n_cols)], axis=1)


def _rows_to_tiles(x):
    rows = x.shape[0]
    return x.reshape(rows * ROW_TILE[0], ROW_TILE[1]).reshape(rows, *ROW_TILE)


def _tiles_to_rows(x3):
    rows = x3.shape[0]
    return x3.reshape(rows * ROW_TILE[0], ROW_TILE[1]).reshape(rows, D_MODEL)


def _dispatch_kernel(tail_blk_ref, tail_on_ref, pos_ref, x_ref, w_ref, xs_ref, buf_ref, zero_ref, sem, *, tile):
    tm = x_ref.shape[0]
    tile_blks = tile // ROW_TILE[0]

    @pl.when(pl.program_id(0) == 0)
    def _():
        zero_ref[...] = jnp.zeros_like(zero_ref)

        def zero_copy(k):
            blk = pl.multiple_of(tail_blk_ref[k], tile_blks)
            return pltpu.make_async_copy(zero_ref, xs_ref.at[pl.ds(blk, tile_blks)], sem)

        for k in range(2 * N_BUCKETS):
            pl.when(tail_on_ref[k] > 0)(lambda k=k: zero_copy(k).start())
        for k in range(2 * N_BUCKETS):
            pl.when(tail_on_ref[k] > 0)(lambda k=k: zero_copy(k).wait())

    buf_ref[:, :X_TILES, :] = _rows_to_tiles(x_ref[...])
    wpad = jnp.concatenate([w_ref[...], jnp.zeros((LANES - w_ref.shape[0], tm), F32)], axis=0)
    buf_ref[:, X_TILES, :] = wpad.T
    _start_row_copies(pos_ref, tm, lambda r, p: pltpu.make_async_copy(buf_ref.at[r], _row_slab(xs_ref, p), sem))
    done = xs_ref.at[pl.ds(0, tm // ROW_TILE[0])]
    pltpu.make_async_copy(done, done, sem).wait()


def _dispatch(x2, wts, pos3, tails, n_rows_sorted, tile, tm):
    n = x2.shape[0]
    tail_blk, tail_on = tails
    grid_spec = pltpu.PrefetchScalarGridSpec(
        num_scalar_prefetch=2,
        grid=(n // tm,),
        in_specs=[pl.BlockSpec((1, 1, tm), lambda i, *_: (i, 0, 0), memory_space=pltpu.SMEM),
                  pl.BlockSpec((tm, D_MODEL), lambda i, *_: (i, 0)),
                  pl.BlockSpec((8, tm), lambda i, *_: (0, i))],
        out_specs=pl.BlockSpec(memory_space=pl.ANY),
        scratch_shapes=[pltpu.VMEM((tm, XS_TILES, LANES), F32),
                        pltpu.VMEM((tile // ROW_TILE[0], XS_TILES, *ROW_TILE), F32),
                        pltpu.SemaphoreType.DMA],
    )
    return pl.pallas_call(
        functools.partial(_dispatch_kernel, tile=tile),
        out_shape=jax.ShapeDtypeStruct((n_rows_sorted // ROW_TILE[0], XS_TILES, *ROW_TILE), F32),
        grid_spec=grid_spec,
        compiler_params=_params("arbitrary"),
        name="moe_dispatch",
    )(tail_blk, tail_on, pos3, x2, wts)


def _experts_kernel(elo_ref, ehi_ref, nvalid_ref, xs_ref, g_ref, wg_lo, wu_lo, wg_hi, wu_hi,
                    wd_lo, wd_hi, o_ref):
    del elo_ref, ehi_ref
    t = pl.program_id(0)

    @pl.when(t < nvalid_ref[0])
    def _():
        rows = xs_ref.shape[0] * ROW_TILE[0]
        xt = _view_columns(xs_ref, X_TILES)
        h = _rms(xt, g_ref[...]).astype(BF16)
        extra = xs_ref[:, X_TILES].reshape(rows, LANES)
        w_lo = extra[:, 0:1]
        w_hi = extra[:, 1:2]

        def expert(wg, wu, wd, w):
            gate = jnp.dot(h, wg[0], preferred_element_type=F32)
            up = jnp.dot(h, wu[0], preferred_element_type=F32)
            hid = (_silu(gate) * up * w).astype(BF16)
            return jnp.dot(hid, wd[0], preferred_element_type=F32)

        out = xt + expert(wg_lo, wu_lo, wd_lo, w_lo) + expert(wg_hi, wu_hi, wd_hi, w_hi)
        for c in range(X_TILES):
            o_ref[:, c] = out[:, c * LANES:(c + 1) * LANES].reshape(o_ref.shape[0], *ROW_TILE)

    @pl.when(t >= nvalid_ref[0])
    def _():
        o_ref[...] = jnp.zeros_like(o_ref)


def _experts(xs, gain, tables, wg, wu, wd, n_tiles, t):
    elo, ehi, nvalid = tables
    blks = t // ROW_TILE[0]
    row = lambda i, elo, ehi, nv: (i, 0, 0, 0)
    row_in = lambda i, elo, ehi, nv: (jnp.minimum(i, nv[0] - 1), 0, 0, 0)
    lo3 = lambda i, elo, ehi, nv: (elo[i], 0, 0)
    hi3 = lambda i, elo, ehi, nv: (ehi[i], 0, 0)
    grid_spec = pltpu.PrefetchScalarGridSpec(
        num_scalar_prefetch=3,
        grid=(n_tiles,),
        in_specs=[pl.BlockSpec((blks, XS_TILES, *ROW_TILE), row_in),
                  pl.BlockSpec((1, D_MODEL), lambda i, *_: (0, 0)),
                  pl.BlockSpec((1, D_MODEL, D_EXPERT), lo3),
                  pl.BlockSpec((1, D_MODEL, D_EXPERT), lo3),
                  pl.BlockSpec((1, D_MODEL, D_EXPERT), hi3),
                  pl.BlockSpec((1, D_MODEL, D_EXPERT), hi3),
                  pl.BlockSpec((1, D_EXPERT, D_MODEL), lo3),
                  pl.BlockSpec((1, D_EXPERT, D_MODEL), hi3)],
        out_specs=pl.BlockSpec((blks, X_TILES, *ROW_TILE), row),
    )
    return pl.pallas_call(
        _experts_kernel,
        out_shape=jax.ShapeDtypeStruct((xs.shape[0], X_TILES, *ROW_TILE), F32),
        grid_spec=grid_spec,
        compiler_params=_params("arbitrary"),
        name="moe_experts",
    )(elo, ehi, nvalid, xs, gain.reshape(1, D_MODEL), wg, wu, wg, wu, wd, wd)


def _combine_kernel(pos_ref, ys_ref, o_ref, buf_ref, sem):
    tm = o_ref.shape[0]
    _start_row_copies(pos_ref, tm, lambda r, p: pltpu.make_async_copy(_row_slab(ys_ref, p), buf_ref.at[r], sem))
    done = ys_ref.at[pl.ds(0, tm // ROW_TILE[0])]
    pltpu.make_async_copy(done, done, sem).wait()
    o_ref[...] = _tiles_to_rows(buf_ref[...])


def _combine(ys, pos3, n, tm):
    return pl.pallas_call(
        _combine_kernel,
        out_shape=jax.ShapeDtypeStruct((n, D_MODEL), F32),
        grid=(n // tm,),
        in_specs=[pl.BlockSpec((1, 1, tm), lambda i: (i, 0, 0), memory_space=pltpu.SMEM),
                  pl.BlockSpec(memory_space=pl.ANY)],
        out_specs=pl.BlockSpec((tm, D_MODEL), lambda i: (i, 0)),
        scratch_shapes=[pltpu.VMEM((tm, *ROW_TILE), F32), pltpu.SemaphoreType.DMA],
        compiler_params=_params("arbitrary"),
        name="moe_combine",
    )(pos3, ys)


def _moe_tables(idx, cnt, n_tiles, t):
    bucket, rank = idx[0], idx[1]
    counts = cnt[:N_BUCKETS, 0].astype(jnp.int32)
    tiles_b = (counts + t - 1) // t
    tile_end = jnp.cumsum(tiles_b)
    pos = (tile_end - tiles_b)[bucket] * t + rank
    total = tile_end[-1]
    tt = jnp.arange(n_tiles, dtype=jnp.int32)
    valid = tt < total
    tb = jnp.sum((tile_end[None, :] <= jnp.where(valid, tt, total - 1)[:, None]).astype(jnp.int32), axis=1)
    tb = jnp.minimum(tb, N_BUCKETS - 1)
    pair_lo = jnp.asarray([0, 0, 0, 1, 1, 2], jnp.int32)
    pair_hi = jnp.asarray([1, 2, 3, 2, 3, 3], jnp.int32)
    base = (tb // N_PAIRS) * EXPERTS_PER_GROUP
    idle = total + jnp.arange(N_BUCKETS, dtype=jnp.int32)
    idle_on = idle < n_tiles
    blks = t // ROW_TILE[0]
    tails = (jnp.concatenate([(tile_end - 1) * blks, jnp.where(idle_on, idle, 0) * blks]),
             jnp.concatenate([tiles_b > 0, idle_on]).astype(jnp.int32))
    return pos, tails, (base + pair_lo[tb % N_PAIRS], base + pair_hi[tb % N_PAIRS], total.reshape(1))


def _moe(x2, gain, w_rg, b_rg, w_re, b_re, wg, wu, wd, t=MOE_TILE, tm=1024):
    n = x2.shape[0]
    idx, wts, cnt = _router(x2, gain, w_rg, b_rg, w_re, b_re)
    n_tiles = n // t + N_BUCKETS
    pos, tails, tables = _moe_tables(idx, cnt, n_tiles, t)
    pos3 = pos.reshape(n // tm, 1, tm)
    xs = _dispatch(x2, wts, pos3, tails, n_tiles * t, t, tm)
    ys = _experts(xs, gain, tables, wg.astype(BF16), wu.astype(BF16), wd.astype(BF16), n_tiles, t)
    return _combine(ys, pos3, n, tm)


LOG2E = math.log2(math.e)
V_EXT = 2 * HEAD_DIM


def _head_norm_t(y_t, n_heads, scale):
    tm = y_t.shape[1]
    y3 = y_t.reshape(n_heads, HEAD_DIM, tm)
    ms = jnp.mean(y3 * y3, axis=1, keepdims=True)
    return y3 * (lax.rsqrt(ms + EPS) * scale)


def _qkv_kernel(x_ref, g_ref, wqt_ref, wkt_ref, wvt_ref, vone_ref, kg_ref, qt_ref, k_ref, vt_ref):
    h = _rms(x_ref[...], g_ref[...]).astype(BF16)
    tm = h.shape[0]
    qn = _head_norm_t(_nt_dot(wqt_ref[...], h), N_Q_HEADS, HEAD_DIM ** -0.5 * LOG2E)
    qt_ref[...] = qn.reshape(N_Q_HEADS * HEAD_DIM, tm).astype(BF16)
    kn = _head_norm_t(_nt_dot(wkt_ref[...], h), N_KV_HEADS, kg_ref[...].reshape(N_KV_HEADS, HEAD_DIM, 1))
    for hk in range(N_KV_HEADS):
        k_ref[hk] = kn[hk].T.astype(BF16)
    vt_ref[...] = (_nt_dot(wvt_ref[...], h) + vone_ref[...]).astype(BF16)


def _qkv(x2, gain, wqkv, q_gain, k_gain, tm=512):
    n = x2.shape[0]
    qw = N_Q_HEADS * HEAD_DIM
    wqt = wqkv[:, :qw].T.astype(BF16)
    wkt = wqkv[:, qw:qw + KV_WIDTH].T.astype(BF16)
    wvt = wqkv[:, qw + KV_WIDTH:].T.astype(BF16).reshape(N_KV_HEADS, HEAD_DIM, D_MODEL)
    wvt = jnp.pad(wvt, ((0, 0), (0, V_EXT - HEAD_DIM), (0, 0))).reshape(N_KV_HEADS * V_EXT, D_MODEL)
    vone = np.zeros((N_KV_HEADS * V_EXT, 1), np.float32)
    vone[HEAD_DIM::V_EXT, 0] = 1.0
    kg = jnp.tile((k_gain.astype(F32) * q_gain.astype(F32)), N_KV_HEADS).reshape(KV_WIDTH, 1)
    full = lambda r, c: pl.BlockSpec((r, c), lambda i: (0, 0))
    return pl.pallas_call(
        _qkv_kernel,
        out_shape=(jax.ShapeDtypeStruct((qw, n), BF16),
                   jax.ShapeDtypeStruct((N_KV_HEADS, n, HEAD_DIM), BF16),
                   jax.ShapeDtypeStruct((N_KV_HEADS * V_EXT, n), BF16)),
        grid=(n // tm,),
        in_specs=[pl.BlockSpec((tm, D_MODEL), lambda i: (i, 0)), full(1, D_MODEL),
                  full(qw, D_MODEL), full(KV_WIDTH, D_MODEL), full(N_KV_HEADS * V_EXT, D_MODEL),
                  full(N_KV_HEADS * V_EXT, 1), full(KV_WIDTH, 1)],
        out_specs=(pl.BlockSpec((qw, tm), lambda i: (0, i)),
                   pl.BlockSpec((N_KV_HEADS, tm, HEAD_DIM), lambda i: (0, i, 0)),
                   pl.BlockSpec((N_KV_HEADS * V_EXT, tm), lambda i: (0, i))),
        compiler_params=_params("parallel"),
        name="odd_qkv",
    )(x2, gain.reshape(1, D_MODEL), wqt, wkt, wvt, jnp.asarray(vone), kg)


def _attn_bias():
    blk = ATT_BLOCK
    qi = np.arange(blk)[None, :]
    ki = np.arange(2 * blk)[:, None]
    dist = qi - ki + blk
    band = (dist >= 0) & (dist < blk)
    slopes = 2.0 ** (-8.0 * np.arange(1, N_Q_HEADS + 1) / N_Q_HEADS)
    pen = -slopes[:, None, None] * dist[None].astype(np.float64) * LOG2E
    inner = np.where(band[None], pen, -np.inf)
    first = np.where((band & (ki >= blk))[None], pen, -np.inf)
    tab = np.stack([inner, first]).astype(np.float32)
    tab = tab.reshape(2, N_KV_HEADS, GQA_GROUP, 2 * blk, blk).transpose(0, 1, 3, 2, 4)
    return tab.reshape(2, N_KV_HEADS, 2 * blk, GQA_GROUP * blk)


def _attn_kernel(qt_ref, kp_ref, kc_ref, vtp_ref, vtc_ref, bias_ref, sink_ref, x_ref, wo_ref, o_ref):
    first = (pl.program_id(1) == 0).astype(jnp.int32)
    vt = jnp.concatenate([vtp_ref[...], vtc_ref[...]], axis=1)
    att_t = []
    for hk in range(N_KV_HEADS):
        keys = jnp.concatenate([kp_ref[hk], kc_ref[hk]], axis=0)
        q_t = jnp.concatenate([qt_ref[(hk * GQA_GROUP + g) * HEAD_DIM:(hk * GQA_GROUP + g + 1) * HEAD_DIM, :]
                               for g in range(GQA_GROUP)], axis=1)
        s = jnp.dot(keys, q_t, preferred_element_type=F32) + bias_ref[first, hk]
        sink = sink_ref[hk]
        m = jnp.maximum(jnp.max(s, axis=0, keepdims=True), sink)
        p = jnp.exp2(s - m).astype(BF16)
        pv = jnp.dot(vt[hk * V_EXT:(hk + 1) * V_EXT, :], p, preferred_element_type=F32)
        den = pv[HEAD_DIM:HEAD_DIM + 1, :] + jnp.exp2(sink - m)
        o_t = (pv[:HEAD_DIM, :] * (1.0 / den)).astype(BF16)
        att_t += [o_t[:, g * ATT_BLOCK:(g + 1) * ATT_BLOCK] for g in range(GQA_GROUP)]
    att_t = jnp.concatenate(att_t, axis=0)
    mix = lax.dot_general(att_t, wo_ref[...], (((0,), (0,)), ((), ())), preferred_element_type=F32)
    o_ref[...] = x_ref[...] + mix


def _attn(qt, k, vt, x2, sinks, wo, bsz, seqlen):
    blk = ATT_BLOCK
    nb = seqlen // blk
    qw = N_Q_HEADS * HEAD_DIM
    cols = GQA_GROUP * blk
    cur = lambda b, n: (b * nb + n, 0)
    cur_t = lambda b, n: (0, b * nb + n)
    prev_t = lambda b, n: (0, b * nb + jnp.maximum(n - 1, 0))
    sink_row = jnp.repeat(sinks.astype(F32) * LOG2E, blk).reshape(N_KV_HEADS, 1, cols)
    return pl.pallas_call(
        _attn_kernel,
        out_shape=jax.ShapeDtypeStruct((bsz * seqlen, D_MODEL), F32),
        grid=(bsz, nb),
        in_specs=[pl.BlockSpec((qw, blk), cur_t),
                  pl.BlockSpec((N_KV_HEADS, blk, HEAD_DIM), lambda b, n: (0, b * nb + jnp.maximum(n - 1, 0), 0)),
                  pl.BlockSpec((N_KV_HEADS, blk, HEAD_DIM), lambda b, n: (0, b * nb + n, 0)),
                  pl.BlockSpec((N_KV_HEADS * V_EXT, blk), prev_t),
                  pl.BlockSpec((N_KV_HEADS * V_EXT, blk), cur_t),
                  pl.BlockSpec((2, N_KV_HEADS, 2 * blk, cols), lambda b, n: (0, 0, 0, 0)),
                  pl.BlockSpec((N_KV_HEADS, 1, cols), lambda b, n: (0, 0, 0)),
                  pl.BlockSpec((blk, D_MODEL), cur),
                  pl.BlockSpec((qw, D_MODEL), lambda b, n: (0, 0))],
        out_specs=pl.BlockSpec((blk, D_MODEL), cur),
        compiler_params=_params("parallel", "parallel"),
        name="odd_attn",
    )(qt, k, k, vt, vt, jnp.asarray(_attn_bias()), sink_row, x2, wo.astype(BF16))


def kernel(x, even_mix_norm, even_in_proj, s5_lambda_re, s5_lambda_im, s5_log_step, s5_b_re, s5_b_im,
           s5_c_re, s5_c_im, s5_d, s5_glu_w, hgrn_lower_bounds, hgrn_o_norm, even_out_proj, odd_mix_norm,
           odd_wqkv, odd_q_norm, odd_k_norm, odd_sinks, odd_out_proj, moe_norm, moe_router_group,
           moe_router_group_bias, moe_router_expert, moe_router_expert_bias, moe_w_gate, moe_w_up,
           moe_w_down):
    bsz, seqlen, dm = x.shape
    n = bsz * seqlen
    x2 = x.reshape(n, dm)
    lower_bounds = jnp.cumsum(jax.nn.softmax(hgrn_lower_bounds.astype(F32), axis=0), axis=0)

    def moe(xx, layer):
        return _moe(xx, moe_norm[layer], moe_router_group[layer], moe_router_group_bias[layer],
                    moe_router_expert[layer], moe_router_expert_bias[layer],
                    moe_w_gate[layer], moe_w_up[layer], moe_w_down[layer])

    u, h4 = _inproj(x2, even_mix_norm[0], even_in_proj[0].astype(BF16))
    ops = _s5_operators(s5_lambda_re[0], s5_lambda_im[0], s5_log_step[0], s5_b_re[0], s5_b_im[0],
                        s5_c_re[0], s5_c_im[0])
    u_g = _s5_pack(u.reshape(bsz, seqlen, S5_WIDTH))
    y_g = _s5_scan(u_g, ops, bsz)
    ys = _s5_unpack(y_g, bsz, seqlen).reshape(n, S5_WIDTH)
    b_out = _hgrn(h4.reshape(bsz, seqlen, 4 * HG_WIDTH), lower_bounds[0], hgrn_o_norm[0], bsz, seqlen)
    x2 = _evenout(x2, ys, u, b_out.reshape(n, HG_WIDTH), s5_d[0], s5_glu_w[0], even_out_proj[0])
    x2 = moe(x2, 0)

    q, kt, v = _qkv(x2, odd_mix_norm[0], odd_wqkv[0], odd_q_norm[0], odd_k_norm[0])
    x2 = _attn(q, kt, v, x2, odd_sinks[0], odd_out_proj[0], bsz, seqlen)
    x2 = moe(x2, 1)
    return x2.reshape(bsz, seqlen, dm)
```

```python
import functools
import math

import jax
import jax.numpy as jnp
import numpy as np
from jax import lax
from jax.experimental import pallas as pl
from jax.experimental.pallas import tpu as pltpu

F32 = jnp.float32
BF16 = jnp.bfloat16
EPS = 1e-6

D_MODEL = 1024
S5_WIDTH = 512
S5_GROUP = 16
S5_GROUPS = 32
S5_STATE = 64
S5_CHUNK = 16
HG_WIDTH = 512
HG_HEAD_DIM = 128
HG_HEADS = 4
HG_CHUNK = 32
HEAD_DIM = 64
N_Q_HEADS = 16
N_KV_HEADS = 2
GQA_GROUP = 8
KV_WIDTH = N_KV_HEADS * HEAD_DIM
ATT_BLOCK = 128
N_GROUPS = 4
EXPERTS_PER_GROUP = 4
N_EXPERTS = 16
D_EXPERT = 256
ROUTER_ROWS = 32
N_PAIRS = 6
N_BUCKETS = N_GROUPS * N_PAIRS
BUCKET_ROWS = 32
MOE_TILE = 512
ROW_TILE = (8, 128)
ROW_TILE_SHIFT = ROW_TILE[0].bit_length() - 1
X_TILES = D_MODEL // ROW_TILE[1]
XS_TILES = X_TILES + 1

VMEM_LIMIT_BYTES = 56 * 1024 * 1024


def _params(*semantics):
    return pltpu.CompilerParams(dimension_semantics=semantics, vmem_limit_bytes=VMEM_LIMIT_BYTES)


def _rms(xf, gain):
    return xf * lax.rsqrt(jnp.mean(xf * xf, axis=-1, keepdims=True) + EPS) * gain


def _nt_dot(w_t, h):
    return lax.dot_general(w_t, h, (((1,), (1,)), ((), ())), preferred_element_type=F32)


def _sigmoid(x):
    return 0.5 * jnp.tanh(0.5 * x) + 0.5


def _silu(x):
    return x * _sigmoid(x)


def _inproj_kernel(x_ref, g_ref, w_ref, u_ref, h4_ref):
    h = _rms(x_ref[...], g_ref[...]).astype(BF16)
    p = jnp.dot(h, w_ref[...], preferred_element_type=F32)
    u_ref[...] = p[:, :S5_WIDTH]
    h4_ref[...] = p[:, S5_WIDTH:]


def _inproj(x2, gain, w_bf16, tm=512):
    n = x2.shape[0]
    e_in = w_bf16.shape[1]
    return pl.pallas_call(
        _inproj_kernel,
        out_shape=(jax.ShapeDtypeStruct((n, S5_WIDTH), F32),
                   jax.ShapeDtypeStruct((n, e_in - S5_WIDTH), F32)),
        grid=(n // tm,),
        in_specs=[pl.BlockSpec((tm, D_MODEL), lambda i: (i, 0)),
                  pl.BlockSpec((1, D_MODEL), lambda i: (0, 0)),
                  pl.BlockSpec((D_MODEL, e_in), lambda i: (0, 0))],
        out_specs=(pl.BlockSpec((tm, S5_WIDTH), lambda i: (i, 0)),
                   pl.BlockSpec((tm, e_in - S5_WIDTH), lambda i: (i, 0))),
        compiler_params=_params("parallel"),
        name="even_inproj",
    )(x2, gain.reshape(1, D_MODEL), w_bf16)


def _s5_lagkernel_kernel(ca_ref, bb_ref, k_ref):
    k_ref[0] = jnp.dot(ca_ref[0], bb_ref[0], preferred_element_type=F32,
                       precision=lax.Precision.HIGHEST)


def _s5_lagkernel(ca, bb):
    g, rows, k = ca.shape
    return pl.pallas_call(
        _s5_lagkernel_kernel,
        out_shape=jax.ShapeDtypeStruct((g, rows, S5_GROUP), F32),
        grid=(g,),
        in_specs=[pl.BlockSpec((1, rows, k), lambda i: (i, 0, 0)),
                  pl.BlockSpec((1, k, S5_GROUP), lambda i: (i, 0, 0))],
        out_specs=pl.BlockSpec((1, rows, S5_GROUP), lambda i: (i, 0, 0)),
        compiler_params=_params("parallel"),
        name="s5_lag_kernel",
    )(ca, bb)


def _s5_operators(lam_re, lam_im, log_step, b_re, b_im, c_re, c_im):
    t = S5_CHUNK
    lr, li = lam_re.astype(F32), lam_im.astype(F32)
    step = jnp.exp(log_step.astype(F32))[:, None]
    mag = jnp.exp(lr * step)
    ab_re = mag * jnp.cos(li * step)
    ab_im = mag * jnp.sin(li * step)
    den = lr * lr + li * li
    nr, ni = ab_re - 1.0, ab_im
    z_re = (nr * lr + ni * li) / den
    z_im = (ni * lr - nr * li) / den
    br, bi = b_re.astype(F32), b_im.astype(F32)
    bb_re = z_re[..., None] * br - z_im[..., None] * bi
    bb_im = z_re[..., None] * bi + z_im[..., None] * br
    kk = jnp.arange(t + 1, dtype=F32)[:, None, None]
    pmag = jnp.exp(kk * (lr * step)[None])
    pw_re = pmag * jnp.cos(kk * (li * step)[None])
    pw_im = pmag * jnp.sin(kk * (li * step)[None])
    cr = jnp.transpose(c_re.astype(F32), (0, 1, 2))
    ci = c_im.astype(F32)
    ca_re = cr[None] * pw_re[:, :, None, :] - ci[None] * pw_im[:, :, None, :]
    ca_im = cr[None] * pw_im[:, :, None, :] + ci[None] * pw_re[:, :, None, :]
    g = lr.shape[0]
    ca_cat = jnp.concatenate([ca_re[:t], -ca_im[:t]], axis=-1)
    ca_cat = jnp.transpose(ca_cat, (1, 0, 2, 3)).reshape(g, t * S5_GROUP, 2 * S5_STATE)
    bb_cat = jnp.concatenate([bb_re, bb_im], axis=1)
    kern = _s5_lagkernel(ca_cat, bb_cat).reshape(g, t, S5_GROUP, S5_GROUP)
    s_idx = jnp.arange(t)[:, None]
    t_idx = jnp.arange(t)[None, :]
    lag = t_idx - s_idx
    kg = kern[:, jnp.clip(lag, 0, t - 1)]
    kg = jnp.where((lag >= 0)[None, :, :, None, None], kg, 0.0)
    mt = jnp.transpose(kg, (0, 1, 4, 2, 3)).reshape(g, t * S5_GROUP, t * S5_GROUP)
    pr = pw_re[:t][::-1]
    pi = pw_im[:t][::-1]
    sb_re = pr[:, :, :, None] * bb_re[None] - pi[:, :, :, None] * bb_im[None]
    sb_im = pr[:, :, :, None] * bb_im[None] + pi[:, :, :, None] * bb_re[None]
    sb_re = jnp.transpose(sb_re, (1, 0, 3, 2)).reshape(g, t * S5_GROUP, S5_STATE)
    sb_im = jnp.transpose(sb_im, (1, 0, 3, 2)).reshape(g, t * S5_GROUP, S5_STATE)
    cp_re = jnp.transpose(ca_re[1:], (1, 3, 0, 2)).reshape(g, S5_STATE, t * S5_GROUP)
    cp_im = jnp.transpose(-ca_im[1:], (1, 3, 0, 2)).reshape(g, S5_STATE, t * S5_GROUP)

    def pair_rows(m):
        m = m.reshape(g // 2, 2, m.shape[1], m.shape[2])
        z = jnp.zeros_like(m[:, 0])
        top = jnp.concatenate([m[:, 0], z], axis=2)
        bot = jnp.concatenate([z, m[:, 1]], axis=2)
        return jnp.concatenate([top, bot], axis=1)

    at_re = pw_re[t].reshape(g // 2, 1, 2 * S5_STATE)
    at_im = pw_im[t].reshape(g // 2, 1, 2 * S5_STATE)
    return (mt.astype(BF16), pair_rows(sb_re).astype(BF16), pair_rows(sb_im).astype(BF16),
            pair_rows(cp_re).astype(BF16), pair_rows(cp_im).astype(BF16), at_re, at_im)


PACK_TOKENS = 512


LANES = 128
GROUPS_PER_TILE = LANES // S5_GROUP
TOKENS_PER_TILE = LANES // S5_GROUP
CHUNK_HALVES = S5_CHUNK // TOKENS_PER_TILE
PACK_CHUNKS = PACK_TOKENS // S5_CHUNK


def _block_swap_matrix():
    a, b, h = np.meshgrid(np.arange(TOKENS_PER_TILE), np.arange(GROUPS_PER_TILE), np.arange(S5_GROUP),
                          indexing="ij")
    src = (a * GROUPS_PER_TILE + b) * S5_GROUP + h
    dst = (b * TOKENS_PER_TILE + a) * S5_GROUP + h
    m = np.zeros((src.size, src.size), np.float32)
    m[src.ravel(), dst.ravel()] = 1.0
    return jnp.asarray(m, dtype=BF16)


def _s5_pack_kernel(u_ref, swap_ref, o_ref, *, bsz):
    for j in range(CHUNK_HALVES):
        rows = [jnp.concatenate([u_ref[b, pl.ds(j * TOKENS_PER_TILE + tt, PACK_CHUNKS, stride=S5_CHUNK), :]
                                 for tt in range(TOKENS_PER_TILE)], axis=1) for b in range(bsz)]
        lhs = jnp.concatenate(rows, axis=0).astype(BF16)
        out = jnp.dot(lhs, swap_ref[...], preferred_element_type=F32)
        for g in range(GROUPS_PER_TILE):
            for b in range(bsz):
                o_ref[g * CHUNK_HALVES + j, pl.ds(b, PACK_CHUNKS, stride=bsz), :] = (
                    out[b * PACK_CHUNKS:(b + 1) * PACK_CHUNKS, g * LANES:(g + 1) * LANES])


def _s5_pack(u3):
    bsz, seqlen, w = u3.shape
    rows = PACK_CHUNKS * bsz
    swap = _block_swap_matrix()
    return pl.pallas_call(
        functools.partial(_s5_pack_kernel, bsz=bsz),
        out_shape=jax.ShapeDtypeStruct((S5_GROUPS * CHUNK_HALVES, seqlen // S5_CHUNK * bsz, LANES), F32),
        grid=(seqlen // PACK_TOKENS, w // LANES),
        in_specs=[pl.BlockSpec((bsz, PACK_TOKENS, LANES), lambda i, k: (0, i, k)),
                  pl.BlockSpec(swap.shape, lambda i, k: (0, 0))],
        out_specs=pl.BlockSpec((GROUPS_PER_TILE * CHUNK_HALVES, rows, LANES), lambda i, k: (k, i, 0)),
        compiler_params=_params("parallel", "parallel"),
        name="s5_pack",
    )(u3, swap)


def _s5_unpack_kernel(y_ref, swap_ref, o_ref, *, bsz):
    for j in range(CHUNK_HALVES):
        rows = [jnp.concatenate([y_ref[g * CHUNK_HALVES + j, pl.ds(b, PACK_CHUNKS, stride=bsz), :]
                                 for g in range(GROUPS_PER_TILE)], axis=1) for b in range(bsz)]
        lhs = jnp.concatenate(rows, axis=0).astype(BF16)
        out = jnp.dot(lhs, swap_ref[...], preferred_element_type=F32)
        for tt in range(TOKENS_PER_TILE):
            for b in range(bsz):
                o_ref[b, pl.ds(j * TOKENS_PER_TILE + tt, PACK_CHUNKS, stride=S5_CHUNK), :] = (
                    out[b * PACK_CHUNKS:(b + 1) * PACK_CHUNKS, tt * LANES:(tt + 1) * LANES])


def _s5_unpack(y_g, bsz, seqlen):
    rows = PACK_CHUNKS * bsz
    swap = _block_swap_matrix()
    return pl.pallas_call(
        functools.partial(_s5_unpack_kernel, bsz=bsz),
        out_shape=jax.ShapeDtypeStruct((bsz, seqlen, S5_WIDTH), F32),
        grid=(seqlen // PACK_TOKENS, S5_WIDTH // LANES),
        in_specs=[pl.BlockSpec((GROUPS_PER_TILE * CHUNK_HALVES, rows, LANES), lambda i, k: (k, i, 0)),
                  pl.BlockSpec(swap.shape, lambda i, k: (0, 0))],
        out_specs=pl.BlockSpec((bsz, PACK_TOKENS, LANES), lambda i, k: (0, i, k)),
        compiler_params=_params("parallel", "parallel"),
        name="s5_unpack",
    )(y_g, swap)


def _s5_kernel(u_ref, mt_ref, wre_ref, wim_ref, cre_ref, cim_ref, atr_ref, ati_ref, y_ref,
               sre_ref, sim_ref, xre_ref, xim_ref, *, n_chunks, bsz):
    ucat = jnp.concatenate([u_ref[i] for i in range(2 * CHUNK_HALVES)], axis=1).astype(BF16)
    w = S5_CHUNK * S5_GROUP
    u0 = ucat[:, :w]
    u1 = ucat[:, w:]
    sre_ref[...] = jnp.dot(ucat, wre_ref[0], preferred_element_type=F32)
    sim_ref[...] = jnp.dot(ucat, wim_ref[0], preferred_element_type=F32)
    atr = jnp.broadcast_to(atr_ref[0], (bsz, 2 * S5_STATE))
    ati = jnp.broadcast_to(ati_ref[0], (bsz, 2 * S5_STATE))

    def body(c, carry):
        xr, xi = carry
        rows = pl.ds(pl.multiple_of(c * bsz, bsz), bsz)
        xre_ref[rows, :] = xr
        xim_ref[rows, :] = xi
        nxr = atr * xr - ati * xi + sre_ref[rows, :]
        nxi = atr * xi + ati * xr + sim_ref[rows, :]
        return nxr, nxi

    zero = jnp.zeros((bsz, 2 * S5_STATE), F32)
    lax.fori_loop(0, n_chunks, body, (zero, zero))
    ycar = (jnp.dot(xre_ref[...].astype(BF16), cre_ref[0], preferred_element_type=F32)
            + jnp.dot(xim_ref[...].astype(BF16), cim_ref[0], preferred_element_type=F32))
    y0 = jnp.dot(u0, mt_ref[0], preferred_element_type=F32) + ycar[:, :w]
    y1 = jnp.dot(u1, mt_ref[1], preferred_element_type=F32) + ycar[:, w:]
    for i in range(CHUNK_HALVES):
        y_ref[i] = y0[:, i * LANES:(i + 1) * LANES]
        y_ref[CHUNK_HALVES + i] = y1[:, i * LANES:(i + 1) * LANES]


def _s5_scan(u_g, ops, bsz):
    mt, wre, wim, cre, cim, atr, ati = ops
    tiles, r, _ = u_g.shape
    g = tiles // CHUNK_HALVES
    w = S5_CHUNK * S5_GROUP
    n_chunks = r // bsz
    p2 = 2 * S5_STATE
    kern = functools.partial(_s5_kernel, n_chunks=n_chunks, bsz=bsz)
    pair_tiles = pl.BlockSpec((2 * CHUNK_HALVES, r, LANES), lambda i: (i, 0, 0))
    return pl.pallas_call(
        kern,
        out_shape=jax.ShapeDtypeStruct((tiles, r, LANES), F32),
        grid=(g // 2,),
        in_specs=[pair_tiles,
                  pl.BlockSpec((2, w, w), lambda i: (i, 0, 0)),
                  pl.BlockSpec((1, 2 * w, p2), lambda i: (i, 0, 0)),
                  pl.BlockSpec((1, 2 * w, p2), lambda i: (i, 0, 0)),
                  pl.BlockSpec((1, p2, 2 * w), lambda i: (i, 0, 0)),
                  pl.BlockSpec((1, p2, 2 * w), lambda i: (i, 0, 0)),
                  pl.BlockSpec((1, 1, p2), lambda i: (i, 0, 0)),
                  pl.BlockSpec((1, 1, p2), lambda i: (i, 0, 0))],
        out_specs=pair_tiles,
        scratch_shapes=[pltpu.VMEM((r, p2), F32)] * 4,
        compiler_params=_params("parallel"),
        name="s5_scan",
    )(u_g, mt, wre, wim, cre, cim, atr, ati)


def _hgrn_kernel(q_ref, f_ref, i_ref, g_ref, lb_ref, og_ref, o_ref, st_ref, *, seqlen):
    c = HG_CHUNK
    nc = seqlen // c
    d = HG_HEAD_DIM
    lb = lb_ref[...]
    q = q_ref[0]
    qs = _silu(q)
    f = lb + (1.0 - lb) * _sigmoid(f_ref[0])
    lf = jnp.log(f)
    k = 1.0 - f
    v = i_ref[0]
    pos = lax.broadcasted_iota(jnp.int32, (seqlen, d), 0) % c
    b = lf
    sh = 1
    while sh < c:
        b = b + jnp.where(pos >= sh, pltpu.roll(b, sh, axis=0), 0.0)
        sh *= 2
    b3 = b.reshape(nc, c, d)
    b_last = b3[:, c - 1:c, :]
    b_ref = b3[:, c // 2 - 1:c // 2, :]
    qs3 = qs.reshape(nc, c, d)
    k3 = k.reshape(nc, c, d)
    v3 = v.reshape(nc, c, d).astype(BF16)
    qe_f = qs3 * jnp.exp(b3 - b_ref)
    ke_f = k3 * jnp.exp(b_ref - b3)
    qe = qe_f.astype(BF16)
    ke = ke_f.astype(BF16)
    kd = (ke_f * jnp.exp(b_last - b_ref)).astype(BF16)
    qb = (qe_f * jnp.exp(b_ref)).astype(BF16)
    scores = jnp.einsum('ctd,csd->cts', qe, ke, preferred_element_type=F32)
    ti = lax.broadcasted_iota(jnp.int32, (c, c), 0)
    si = lax.broadcasted_iota(jnp.int32, (c, c), 1)
    scores = jnp.where((ti >= si)[None], scores, 0.0)
    o_intra = jnp.einsum('cts,csv->ctv', scores.astype(BF16), v3, preferred_element_type=F32)
    ut = jnp.einsum('csv,csd->cvd', v3, kd, preferred_element_type=F32)
    decay = jnp.exp(b_last)
    state = jnp.zeros((d, d), F32)
    for ci in range(nc):
        st_ref[ci] = state.astype(BF16)
        state = decay[ci] * state + ut[ci]
    o_inter = jnp.einsum('ctd,cvd->ctv', qb, st_ref[...], preferred_element_type=F32)
    o = (o_intra + o_inter).reshape(seqlen, d)
    o = _rms(o, og_ref[...])
    o_ref[0] = (o * _silu(g_ref[0])).astype(BF16)


def _hgrn(h4, lower_bound, o_gain, bsz, seqlen):
    d = HG_HEAD_DIM
    kern = functools.partial(_hgrn_kernel, seqlen=seqlen)

    def col(part):
        return pl.BlockSpec((1, seqlen, d), lambda b, h: (b, 0, part * HG_HEADS + h))

    return pl.pallas_call(
        kern,
        out_shape=jax.ShapeDtypeStruct((bsz, seqlen, HG_WIDTH), BF16),
        grid=(bsz, HG_HEADS),
        in_specs=[col(0), col(1), col(2), col(3),
                  pl.BlockSpec((1, d), lambda b, h: (0, h)),
                  pl.BlockSpec((1, d), lambda b, h: (0, 0))],
        out_specs=pl.BlockSpec((1, seqlen, d), lambda b, h: (b, 0, h)),
        scratch_shapes=[pltpu.VMEM((seqlen // HG_CHUNK, d, d), BF16)],
        compiler_params=_params("parallel", "parallel"),
        name="hgrn2",
    )(h4, h4, h4, h4, lower_bound.reshape(1, HG_WIDTH), o_gain.reshape(1, d))


def _evenout_kernel(x_ref, ys_ref, u_ref, b_ref, d_ref, wglu_ref, wa_ref, wb_ref, o_ref):
    y = ys_ref[...] + d_ref[...] * u_ref[...]
    y = jax.nn.gelu(y)
    gate = _sigmoid(jnp.dot(y.astype(BF16), wglu_ref[...], preferred_element_type=F32))
    a = (y * gate).astype(BF16)
    mix = (jnp.dot(a, wa_ref[...], preferred_element_type=F32)
           + jnp.dot(b_ref[...], wb_ref[...], preferred_element_type=F32))
    o_ref[...] = x_ref[...] + mix


def _evenout(x2, ys, u, b_out, d_skip, wglu, wout, tm=512):
    n = x2.shape[0]
    row = lambda w: pl.BlockSpec((tm, w), lambda i: (i, 0))
    full = lambda r, c: pl.BlockSpec((r, c), lambda i: (0, 0))
    return pl.pallas_call(
        _evenout_kernel,
        out_shape=jax.ShapeDtypeStruct((n, D_MODEL), F32),
        grid=(n // tm,),
        in_specs=[row(D_MODEL), row(S5_WIDTH), row(S5_WIDTH), row(HG_WIDTH),
                  full(1, S5_WIDTH), full(S5_WIDTH, S5_WIDTH),
                  full(S5_WIDTH, D_MODEL), full(HG_WIDTH, D_MODEL)],
        out_specs=row(D_MODEL),
        compiler_params=_params("parallel"),
        name="even_out",
    )(x2, ys, u, b_out, d_skip.reshape(1, S5_WIDTH), wglu.astype(BF16),
      wout[:S5_WIDTH].astype(BF16), wout[S5_WIDTH:].astype(BF16))


def _router_kernel(x_ref, g_ref, wr_ref, br_ref, tri_ref, idx_ref, wts_ref, cnt_ref, run_ref):
    @pl.when(pl.program_id(0) == 0)
    def _():
        run_ref[...] = jnp.zeros_like(run_ref)

    h = _rms(x_ref[...], g_ref[...])
    h_hi = h.astype(BF16)
    h_lo = (h - h_hi.astype(F32)).astype(BF16)
    both = _nt_dot(wr_ref[...], h_hi)
    lt = (both[:ROUTER_ROWS] + both[ROUTER_ROWS:] + _nt_dot(wr_ref[:ROUTER_ROWS, :], h_lo)
          + br_ref[...])
    gl = [lt[i:i + 1] for i in range(N_GROUPS)]
    el = [lt[N_GROUPS + i:N_GROUPS + i + 1] for i in range(N_EXPERTS)]
    gmax = jnp.maximum(jnp.maximum(gl[0], gl[1]), jnp.maximum(gl[2], gl[3]))
    gexp = [jnp.exp(v - gmax) for v in gl]
    gsum = gexp[0] + gexp[1] + gexp[2] + gexp[3]
    gprob = [v / gsum for v in gexp]
    g_gate = jnp.maximum(jnp.maximum(gprob[0], gprob[1]), jnp.maximum(gprob[2], gprob[3]))
    g_idx = jnp.where(gprob[0] == g_gate, 0,
                      jnp.where(gprob[1] == g_gate, 1, jnp.where(gprob[2] == g_gate, 2, 3)))
    es = []
    for j in range(EXPERTS_PER_GROUP):
        es.append(jnp.where(g_idx == 0, el[j],
                            jnp.where(g_idx == 1, el[4 + j],
                                      jnp.where(g_idx == 2, el[8 + j], el[12 + j]))))
    emax = jnp.maximum(jnp.maximum(es[0], es[1]), jnp.maximum(es[2], es[3]))
    eexp = [jnp.exp(v - emax) for v in es]
    esum = eexp[0] + eexp[1] + eexp[2] + eexp[3]
    ep = [v / esum for v in eexp]
    p1 = jnp.maximum(jnp.maximum(ep[0], ep[1]), jnp.maximum(ep[2], ep[3]))
    i1 = jnp.where(ep[0] == p1, 0, jnp.where(ep[1] == p1, 1, jnp.where(ep[2] == p1, 2, 3)))
    neg = jnp.float32(-1.0)
    rest = [jnp.where(i1 == j, neg, ep[j]) for j in range(EXPERTS_PER_GROUP)]
    p2 = jnp.maximum(jnp.maximum(rest[0], rest[1]), jnp.maximum(rest[2], rest[3]))
    i2 = jnp.where(rest[0] == p2, 0, jnp.where(rest[1] == p2, 1, jnp.where(rest[2] == p2, 2, 3)))
    wsum = p1 + p2
    w1 = g_gate * (p1 / wsum)
    w2 = g_gate * (p2 / wsum)
    first_lo = i1 < i2
    lo = jnp.where(first_lo, i1, i2)
    hi = jnp.where(first_lo, i2, i1)
    w_lo = jnp.where(first_lo, w1, w2)
    w_hi = jnp.where(first_lo, w2, w1)
    pair = jnp.where(lo == 0, 0, jnp.where(lo == 1, 3, 5)) + hi - lo - 1
    bucket = g_idx * N_PAIRS + pair
    tm = bucket.shape[1]
    rowid = lax.broadcasted_iota(jnp.int32, (BUCKET_ROWS, tm), 0)
    onehot = (rowid == bucket).astype(F32)
    prefix = jnp.dot(onehot.astype(BF16), tri_ref[...], preferred_element_type=F32)
    run = run_ref[...]
    rank = jnp.sum(onehot * (prefix + run), axis=0, keepdims=True)
    run = run + jnp.sum(onehot, axis=1, keepdims=True)
    run_ref[...] = run
    cnt_ref[...] = jnp.broadcast_to(run, cnt_ref.shape)
    idx_ref[...] = jnp.concatenate([bucket, rank.astype(jnp.int32), jnp.zeros((6, tm), jnp.int32)], axis=0)
    wts_ref[...] = jnp.concatenate([w_lo, w_hi, jnp.zeros((6, tm), F32)], axis=0)


def _router(x2, gain, w_rg, b_rg, w_re, b_re, tm=512):
    n = x2.shape[0]
    wr = jnp.concatenate([w_rg, w_re], axis=1).astype(F32).T
    wr = jnp.pad(wr, ((0, ROUTER_ROWS - wr.shape[0]), (0, 0)))
    wr_hi = wr.astype(BF16)
    wr = jnp.concatenate([wr_hi, (wr - wr_hi.astype(F32)).astype(BF16)], axis=0)
    br = jnp.pad(jnp.concatenate([b_rg, b_re]).astype(F32), (0, ROUTER_ROWS - N_GROUPS - N_EXPERTS))
    tri = (np.arange(tm)[:, None] < np.arange(tm)[None, :]).astype(np.float32)
    return pl.pallas_call(
        _router_kernel,
        out_shape=(jax.ShapeDtypeStruct((8, n), jnp.int32),
                   jax.ShapeDtypeStruct((8, n), F32),
                   jax.ShapeDtypeStruct((BUCKET_ROWS, 128), F32)),
        grid=(n // tm,),
        in_specs=[pl.BlockSpec((tm, D_MODEL), lambda i: (i, 0)),
                  pl.BlockSpec((1, D_MODEL), lambda i: (0, 0)),
                  pl.BlockSpec((2 * ROUTER_ROWS, D_MODEL), lambda i: (0, 0)),
                  pl.BlockSpec((ROUTER_ROWS, 1), lambda i: (0, 0)),
                  pl.BlockSpec((tm, tm), lambda i: (0, 0))],
        out_specs=(pl.BlockSpec((8, tm), lambda i: (0, i)),
                   pl.BlockSpec((8, tm), lambda i: (0, i)),
                   pl.BlockSpec((BUCKET_ROWS, 128), lambda i: (0, 0))),
        scratch_shapes=[pltpu.VMEM((BUCKET_ROWS, 1), F32)],
        compiler_params=_params("arbitrary"),
        name="moe_router",
    )(x2, gain.reshape(1, D_MODEL), wr, br.reshape(ROUTER_ROWS, 1), jnp.asarray(tri, dtype=BF16))


ROW_COPY_UNROLL = 8


def _start_row_copies(idx_ref, n_rows, copy_for_row, prepare_rows=None):
    def start_group(base):
        for j in range(ROW_COPY_UNROLL):
            copy_for_row(base + j, idx_ref[0, 0, base + j]).start(priority=j % 2)

    n_groups = n_rows // ROW_COPY_UNROLL
    if prepare_rows is None:
        def body(g, carry):
            start_group(pl.multiple_of(g * ROW_COPY_UNROLL, ROW_COPY_UNROLL))
            return carry

        lax.fori_loop(0, n_groups, body, 0)
        return

    prepare_rows(0)

    def body(g, carry):
        base = pl.multiple_of(g * ROW_COPY_UNROLL, ROW_COPY_UNROLL)
        prepare_rows(base + ROW_COPY_UNROLL)
        start_group(base)
        return carry

    lax.fori_loop(0, n_groups - 1, body, 0)
    start_group(n_rows - ROW_COPY_UNROLL)


def _row_slab(view_ref, p):
    return view_ref.at[p >> ROW_TILE_SHIFT, :, p & (ROW_TILE[0] - 1)]


def _view_columns(view_ref, n_cols):
    rows = view_ref.shape[0] * ROW_TILE[0]
    return jnp.concatenate([view_ref[:, c].reshape(rows, LANES) for c in range(n_cols)], axis=1)


def _rows_to_tiles(x):
    rows = x.shape[0]
    return x.reshape(rows * ROW_TILE[0], ROW_TILE[1]).reshape(rows, *ROW_TILE)


def _tiles_to_rows(x3):
    rows = x3.shape[0]
    return x3.reshape(rows * ROW_TILE[0], ROW_TILE[1]).reshape(rows, D_MODEL)


def _dispatch_kernel(tail_blk_ref, tail_on_ref, pos_ref, x_ref, w_ref, xs_ref, buf_ref, zero_ref, wcol_ref, sem,
                     *, tile):
    tm = x_ref.shape[0]
    tile_blks = tile // ROW_TILE[0]

    @pl.when(pl.program_id(0) == 0)
    def _():
        zero_ref[...] = jnp.zeros_like(zero_ref)

        def zero_copy(k):
            blk = pl.multiple_of(tail_blk_ref[k], tile_blks)
            return pltpu.make_async_copy(zero_ref, xs_ref.at[pl.ds(blk, tile_blks)], sem)

        for k in range(2 * N_BUCKETS):
            pl.when(tail_on_ref[k] > 0)(lambda k=k: zero_copy(k).start())
        for k in range(2 * N_BUCKETS):
            pl.when(tail_on_ref[k] > 0)(lambda k=k: zero_copy(k).wait())

    wpad = jnp.concatenate([w_ref[...], jnp.zeros((LANES - w_ref.shape[0], tm), F32)], axis=0)
    wcol_ref[...] = wpad.T

    def stage(base):
        rows = pl.ds(base, ROW_COPY_UNROLL)
        buf_ref[rows, :X_TILES, :] = _rows_to_tiles(x_ref[rows, :])
        buf_ref[rows, X_TILES, :] = wcol_ref[rows, :]

    _start_row_copies(pos_ref, tm, lambda r, p: pltpu.make_async_copy(buf_ref.at[r], _row_slab(xs_ref, p), sem),
                      prepare_rows=stage)
    done = xs_ref.at[pl.ds(0, tm // ROW_TILE[0])]
    pltpu.make_async_copy(done, done, sem).wait()


def _dispatch(x2, wts, pos3, tails, n_rows_sorted, tile, tm):
    n = x2.shape[0]
    tail_blk, tail_on = tails
    grid_spec = pltpu.PrefetchScalarGridSpec(
        num_scalar_prefetch=2,
        grid=(n // tm,),
        in_specs=[pl.BlockSpec((1, 1, tm), lambda i, *_: (i, 0, 0), memory_space=pltpu.SMEM),
                  pl.BlockSpec((tm, D_MODEL), lambda i, *_: (i, 0)),
                  pl.BlockSpec((8, tm), lambda i, *_: (0, i))],
        out_specs=pl.BlockSpec(memory_space=pl.ANY),
        scratch_shapes=[pltpu.VMEM((tm, XS_TILES, LANES), F32),
                        pltpu.VMEM((tile // ROW_TILE[0], XS_TILES, *ROW_TILE), F32),
                        pltpu.VMEM((tm, LANES), F32),
                        pltpu.SemaphoreType.DMA],
    )
    return pl.pallas_call(
        functools.partial(_dispatch_kernel, tile=tile),
        out_shape=jax.ShapeDtypeStruct((n_rows_sorted // ROW_TILE[0], XS_TILES, *ROW_TILE), F32),
        grid_spec=grid_spec,
        compiler_params=_params("arbitrary"),
        name="moe_dispatch",
    )(tail_blk, tail_on, pos3, x2, wts)


def _experts_kernel(elo_ref, ehi_ref, nvalid_ref, xs_ref, g_ref, wg_lo, wu_lo, wg_hi, wu_hi,
                    wd_lo, wd_hi, o_ref):
    del elo_ref, ehi_ref
    t = pl.program_id(0)

    @pl.when(t < nvalid_ref[0])
    def _():
        rows = xs_ref.shape[0] * ROW_TILE[0]
        xt = _view_columns(xs_ref, X_TILES)
        h = _rms(xt, g_ref[...]).astype(BF16)
        extra = xs_ref[:, X_TILES].reshape(rows, LANES)
        w_lo = extra[:, 0:1]
        w_hi = extra[:, 1:2]

        def expert(wg, wu, wd, w):
            gate = jnp.dot(h, wg[0], preferred_element_type=F32)
            up = jnp.dot(h, wu[0], preferred_element_type=F32)
            hid = (_silu(gate) * up * w).astype(BF16)
            return jnp.dot(hid, wd[0], preferred_element_type=F32)

        out = xt + expert(wg_lo, wu_lo, wd_lo, w_lo) + expert(wg_hi, wu_hi, wd_hi, w_hi)
        for c in range(X_TILES):
            o_ref[:, c] = out[:, c * LANES:(c + 1) * LANES].reshape(o_ref.shape[0], *ROW_TILE)

    @pl.when(t >= nvalid_ref[0])
    def _():
        o_ref[...] = jnp.zeros_like(o_ref)


def _experts(xs, gain, tables, wg, wu, wd, n_tiles, t):
    elo, ehi, nvalid = tables
    blks = t // ROW_TILE[0]
    row = lambda i, elo, ehi, nv: (i, 0, 0, 0)
    row_in = lambda i, elo, ehi, nv: (jnp.minimum(i, nv[0] - 1), 0, 0, 0)
    lo3 = lambda i, elo, ehi, nv: (elo[i], 0, 0)
    hi3 = lambda i, elo, ehi, nv: (ehi[i], 0, 0)
    grid_spec = pltpu.PrefetchScalarGridSpec(
        num_scalar_prefetch=3,
        grid=(n_tiles,),
        in_specs=[pl.BlockSpec((blks, XS_TILES, *ROW_TILE), row_in),
                  pl.BlockSpec((1, D_MODEL), lambda i, *_: (0, 0)),
                  pl.BlockSpec((1, D_MODEL, D_EXPERT), lo3),
                  pl.BlockSpec((1, D_MODEL, D_EXPERT), lo3),
                  pl.BlockSpec((1, D_MODEL, D_EXPERT), hi3),
                  pl.BlockSpec((1, D_MODEL, D_EXPERT), hi3),
                  pl.BlockSpec((1, D_EXPERT, D_MODEL), lo3),
                  pl.BlockSpec((1, D_EXPERT, D_MODEL), hi3)],
        out_specs=pl.BlockSpec((blks, X_TILES, *ROW_TILE), row),
    )
    return pl.pallas_call(
        _experts_kernel,
        out_shape=jax.ShapeDtypeStruct((xs.shape[0], X_TILES, *ROW_TILE), F32),
        grid_spec=grid_spec,
        compiler_params=_params("arbitrary"),
        name="moe_experts",
    )(elo, ehi, nvalid, xs, gain.reshape(1, D_MODEL), wg, wu, wg, wu, wd, wd)


def _combine_kernel(pos_ref, ys_ref, o_ref, buf_ref, sem):
    tm = o_ref.shape[0]
    _start_row_copies(pos_ref, tm, lambda r, p: pltpu.make_async_copy(_row_slab(ys_ref, p), buf_ref.at[r], sem))
    done = ys_ref.at[pl.ds(0, tm // ROW_TILE[0])]
    pltpu.make_async_copy(done, done, sem).wait()
    o_ref[...] = _tiles_to_rows(buf_ref[...])


def _combine(ys, pos3, n, tm):
    return pl.pallas_call(
        _combine_kernel,
        out_shape=jax.ShapeDtypeStruct((n, D_MODEL), F32),
        grid=(n // tm,),
        in_specs=[pl.BlockSpec((1, 1, tm), lambda i: (i, 0, 0), memory_space=pltpu.SMEM),
                  pl.BlockSpec(memory_space=pl.ANY)],
        out_specs=pl.BlockSpec((tm, D_MODEL), lambda i: (i, 0)),
        scratch_shapes=[pltpu.VMEM((tm, *ROW_TILE), F32), pltpu.SemaphoreType.DMA],
        compiler_params=_params("arbitrary"),
        name="moe_combine",
    )(pos3, ys)


def _moe_tables(idx, cnt, n_tiles, t):
    bucket, rank = idx[0], idx[1]
    counts = cnt[:N_BUCKETS, 0].astype(jnp.int32)
    tiles_b = (counts + t - 1) // t
    tile_end = jnp.cumsum(tiles_b)
    pos = (tile_end - tiles_b)[bucket] * t + rank
    total = tile_end[-1]
    tt = jnp.arange(n_tiles, dtype=jnp.int32)
    valid = tt < total
    tb = jnp.sum((tile_end[None, :] <= jnp.where(valid, tt, total - 1)[:, None]).astype(jnp.int32), axis=1)
    tb = jnp.minimum(tb, N_BUCKETS - 1)
    pair_lo = jnp.asarray([0, 0, 0, 1, 1, 2], jnp.int32)
    pair_hi = jnp.asarray([1, 2, 3, 2, 3, 3], jnp.int32)
    base = (tb // N_PAIRS) * EXPERTS_PER_GROUP
    idle = total + jnp.arange(N_BUCKETS, dtype=jnp.int32)
    idle_on = idle < n_tiles
    blks = t // ROW_TILE[0]
    tails = (jnp.concatenate([(tile_end - 1) * blks, jnp.where(idle_on, idle, 0) * blks]),
             jnp.concatenate([tiles_b > 0, idle_on]).astype(jnp.int32))
    return pos, tails, (base + pair_lo[tb % N_PAIRS], base + pair_hi[tb % N_PAIRS], total.reshape(1))


def _moe(x2, gain, w_rg, b_rg, w_re, b_re, wg, wu, wd, t=MOE_TILE, tm=1024):
    n = x2.shape[0]
    idx, wts, cnt = _router(x2, gain, w_rg, b_rg, w_re, b_re)
    n_tiles = n // t + N_BUCKETS
    pos, tails, tables = _moe_tables(idx, cnt, n_tiles, t)
    pos3 = pos.reshape(n // tm, 1, tm)
    xs = _dispatch(x2, wts, pos3, tails, n_tiles * t, t, tm)
    ys = _experts(xs, gain, tables, wg.astype(BF16), wu.astype(BF16), wd.astype(BF16), n_tiles, t)
    return _combine(ys, pos3, n, tm)


LOG2E = math.log2(math.e)
V_EXT = 2 * HEAD_DIM


def _head_norm_t(y_t, n_heads, scale):
    tm = y_t.shape[1]
    y3 = y_t.reshape(n_heads, HEAD_DIM, tm)
    ms = jnp.mean(y3 * y3, axis=1, keepdims=True)
    return y3 * (lax.rsqrt(ms + EPS) * scale)


def _qkv_kernel(x_ref, g_ref, wqt_ref, wkt_ref, wvt_ref, vone_ref, kg_ref, qt_ref, k_ref, vt_ref):
    h = _rms(x_ref[...], g_ref[...]).astype(BF16)
    tm = h.shape[0]
    qn = _head_norm_t(_nt_dot(wqt_ref[...], h), N_Q_HEADS, HEAD_DIM ** -0.5 * LOG2E)
    qt_ref[...] = qn.reshape(N_Q_HEADS * HEAD_DIM, tm).astype(BF16)
    kn = _head_norm_t(_nt_dot(wkt_ref[...], h), N_KV_HEADS, kg_ref[...].reshape(N_KV_HEADS, HEAD_DIM, 1))
    for hk in range(N_KV_HEADS):
        k_ref[hk] = kn[hk].T.astype(BF16)
    vt_ref[...] = (_nt_dot(wvt_ref[...], h) + vone_ref[...]).astype(BF16)


def _qkv(x2, gain, wqkv, q_gain, k_gain, tm=512):
    n = x2.shape[0]
    qw = N_Q_HEADS * HEAD_DIM
    wqt = wqkv[:, :qw].T.astype(BF16)
    wkt = wqkv[:, qw:qw + KV_WIDTH].T.astype(BF16)
    wvt = wqkv[:, qw + KV_WIDTH:].T.astype(BF16).reshape(N_KV_HEADS, HEAD_DIM, D_MODEL)
    wvt = jnp.pad(wvt, ((0, 0), (0, V_EXT - HEAD_DIM), (0, 0))).reshape(N_KV_HEADS * V_EXT, D_MODEL)
    vone = np.zeros((N_KV_HEADS * V_EXT, 1), np.float32)
    vone[HEAD_DIM::V_EXT, 0] = 1.0
    kg = jnp.tile((k_gain.astype(F32) * q_gain.astype(F32)), N_KV_HEADS).reshape(KV_WIDTH, 1)
    full = lambda r, c: pl.BlockSpec((r, c), lambda i: (0, 0))
    return pl.pallas_call(
        _qkv_kernel,
        out_shape=(jax.ShapeDtypeStruct((qw, n), BF16),
                   jax.ShapeDtypeStruct((N_KV_HEADS, n, HEAD_DIM), BF16),
                   jax.ShapeDtypeStruct((N_KV_HEADS * V_EXT, n), BF16)),
        grid=(n // tm,),
        in_specs=[pl.BlockSpec((tm, D_MODEL), lambda i: (i, 0)), full(1, D_MODEL),
                  full(qw, D_MODEL), full(KV_WIDTH, D_MODEL), full(N_KV_HEADS * V_EXT, D_MODEL),
                  full(N_KV_HEADS * V_EXT, 1), full(KV_WIDTH, 1)],
        out_specs=(pl.BlockSpec((qw, tm), lambda i: (0, i)),
                   pl.BlockSpec((N_KV_HEADS, tm, HEAD_DIM), lambda i: (0, i, 0)),
                   pl.BlockSpec((N_KV_HEADS * V_EXT, tm), lambda i: (0, i))),
        compiler_params=_params("parallel"),
        name="odd_qkv",
    )(x2, gain.reshape(1, D_MODEL), wqt, wkt, wvt, jnp.asarray(vone), kg)


def _attn_bias():
    blk = ATT_BLOCK
    qi = np.arange(blk)[None, :]
    ki = np.arange(2 * blk)[:, None]
    dist = qi - ki + blk
    band = (dist >= 0) & (dist < blk)
    slopes = 2.0 ** (-8.0 * np.arange(1, N_Q_HEADS + 1) / N_Q_HEADS)
    pen = -slopes[:, None, None] * dist[None].astype(np.float64) * LOG2E
    inner = np.where(band[None], pen, -np.inf)
    first = np.where((band & (ki >= blk))[None], pen, -np.inf)
    tab = np.stack([inner, first]).astype(np.float32)
    tab = tab.reshape(2, N_KV_HEADS, GQA_GROUP, 2 * blk, blk).transpose(0, 1, 3, 2, 4)
    return tab.reshape(2, N_KV_HEADS, 2 * blk, GQA_GROUP * blk)


def _attn_kernel(qt_ref, kp_ref, kc_ref, vtp_ref, vtc_ref, bias_ref, sink_ref, x_ref, wo_ref, o_ref):
    first = (pl.program_id(1) == 0).astype(jnp.int32)
    vt = jnp.concatenate([vtp_ref[...], vtc_ref[...]], axis=1)
    att_t = []
    for hk in range(N_KV_HEADS):
        keys = jnp.concatenate([kp_ref[hk], kc_ref[hk]], axis=0)
        q_t = jnp.concatenate([qt_ref[(hk * GQA_GROUP + g) * HEAD_DIM:(hk * GQA_GROUP + g + 1) * HEAD_DIM, :]
                               for g in range(GQA_GROUP)], axis=1)
        s = jnp.dot(keys, q_t, preferred_element_type=F32) + bias_ref[first, hk]
        sink = sink_ref[hk]
        m = jnp.maximum(jnp.max(s, axis=0, keepdims=True), sink)
        p = jnp.exp2(s - m).astype(BF16)
        pv = jnp.dot(vt[hk * V_EXT:(hk + 1) * V_EXT, :], p, preferred_element_type=F32)
        den = pv[HEAD_DIM:HEAD_DIM + 1, :] + jnp.exp2(sink - m)
        o_t = (pv[:HEAD_DIM, :] * (1.0 / den)).astype(BF16)
        att_t += [o_t[:, g * ATT_BLOCK:(g + 1) * ATT_BLOCK] for g in range(GQA_GROUP)]
    att_t = jnp.concatenate(att_t, axis=0)
    mix = lax.dot_general(att_t, wo_ref[...], (((0,), (0,)), ((), ())), preferred_element_type=F32)
    o_ref[...] = x_ref[...] + mix


def _attn(qt, k, vt, x2, sinks, wo, bsz, seqlen):
    blk = ATT_BLOCK
    nb = seqlen // blk
    qw = N_Q_HEADS * HEAD_DIM
    cols = GQA_GROUP * blk
    cur = lambda b, n: (b * nb + n, 0)
    cur_t = lambda b, n: (0, b * nb + n)
    prev_t = lambda b, n: (0, b * nb + jnp.maximum(n - 1, 0))
    sink_row = jnp.repeat(sinks.astype(F32) * LOG2E, blk).reshape(N_KV_HEADS, 1, cols)
    return pl.pallas_call(
        _attn_kernel,
        out_shape=jax.ShapeDtypeStruct((bsz * seqlen, D_MODEL), F32),
        grid=(bsz, nb),
        in_specs=[pl.BlockSpec((qw, blk), cur_t),
                  pl.BlockSpec((N_KV_HEADS, blk, HEAD_DIM), lambda b, n: (0, b * nb + jnp.maximum(n - 1, 0), 0)),
                  pl.BlockSpec((N_KV_HEADS, blk, HEAD_DIM), lambda b, n: (0, b * nb + n, 0)),
                  pl.BlockSpec((N_KV_HEADS * V_EXT, blk), prev_t),
                  pl.BlockSpec((N_KV_HEADS * V_EXT, blk), cur_t),
                  pl.BlockSpec((2, N_KV_HEADS, 2 * blk, cols), lambda b, n: (0, 0, 0, 0)),
                  pl.BlockSpec((N_KV_HEADS, 1, cols), lambda b, n: (0, 0, 0)),
                  pl.BlockSpec((blk, D_MODEL), cur),
                  pl.BlockSpec((qw, D_MODEL), lambda b, n: (0, 0))],
        out_specs=pl.BlockSpec((blk, D_MODEL), cur),
        compiler_params=_params("parallel", "parallel"),
        name="odd_attn",
    )(qt, k, k, vt, vt, jnp.asarray(_attn_bias()), sink_row, x2, wo.astype(BF16))


def kernel(x, even_mix_norm, even_in_proj, s5_lambda_re, s5_lambda_im, s5_log_step, s5_b_re, s5_b_im,
           s5_c_re, s5_c_im, s5_d, s5_glu_w, hgrn_lower_bounds, hgrn_o_norm, even_out_proj, odd_mix_norm,
           odd_wqkv, odd_q_norm, odd_k_norm, odd_sinks, odd_out_proj, moe_norm, moe_router_group,
           moe_router_group_bias, moe_router_expert, moe_router_expert_bias, moe_w_gate, moe_w_up,
           moe_w_down):
    bsz, seqlen, dm = x.shape
    n = bsz * seqlen
    x2 = x.reshape(n, dm)
    lower_bounds = jnp.cumsum(jax.nn.softmax(hgrn_lower_bounds.astype(F32), axis=0), axis=0)

    def moe(xx, layer):
        return _moe(xx, moe_norm[layer], moe_router_group[layer], moe_router_group_bias[layer],
                    moe_router_expert[layer], moe_router_expert_bias[layer],
                    moe_w_gate[layer], moe_w_up[layer], moe_w_down[layer])

    u, h4 = _inproj(x2, even_mix_norm[0], even_in_proj[0].astype(BF16))
    ops = _s5_operators(s5_lambda_re[0], s5_lambda_im[0], s5_log_step[0], s5_b_re[0], s5_b_im[0],
                        s5_c_re[0], s5_c_im[0])
    u_g = _s5_pack(u.reshape(bsz, seqlen, S5_WIDTH))
    y_g = _s5_scan(u_g, ops, bsz)
    ys = _s5_unpack(y_g, bsz, seqlen).reshape(n, S5_WIDTH)
    b_out = _hgrn(h4.reshape(bsz, seqlen, 4 * HG_WIDTH), lower_bounds[0], hgrn_o_norm[0], bsz, seqlen)
    x2 = _evenout(x2, ys, u, b_out.reshape(n, HG_WIDTH), s5_d[0], s5_glu_w[0], even_out_proj[0])
    x2 = moe(x2, 0)

    q, kt, v = _qkv(x2, odd_mix_norm[0], odd_wqkv[0], odd_q_norm[0], odd_k_norm[0])
    x2 = _attn(q, kt, v, x2, odd_sinks[0], odd_out_proj[0], bsz, seqlen)
    x2 = moe(x2, 1)
    return x2.reshape(bsz, seqlen, dm)
```

```python
import functools
import math

import jax
import jax.numpy as jnp
import numpy as np
from jax import lax
from jax.experimental import pallas as pl
from jax.experimental.pallas import tpu as pltpu

F32 = jnp.float32
BF16 = jnp.bfloat16
EPS = 1e-6

D_MODEL = 1024
S5_WIDTH = 512
S5_GROUP = 16
S5_GROUPS = 32
S5_STATE = 64
S5_CHUNK = 16
HG_WIDTH = 512
HG_HEAD_DIM = 128
HG_HEADS = 4
HG_CHUNK = 32
HEAD_DIM = 64
N_Q_HEADS = 16
N_KV_HEADS = 2
GQA_GROUP = 8
KV_WIDTH = N_KV_HEADS * HEAD_DIM
ATT_BLOCK = 128
N_GROUPS = 4
EXPERTS_PER_GROUP = 4
N_EXPERTS = 16
D_EXPERT = 256
ROUTER_ROWS = 32
N_PAIRS = 6
N_BUCKETS = N_GROUPS * N_PAIRS
BUCKET_ROWS = 32
MOE_TILE = 512
ROW_TILE = (8, 128)
ROW_TILE_SHIFT = ROW_TILE[0].bit_length() - 1
X_TILES = D_MODEL // ROW_TILE[1]
XS_TILES = X_TILES + 1

VMEM_LIMIT_BYTES = 56 * 1024 * 1024


def _params(*semantics):
    return pltpu.CompilerParams(dimension_semantics=semantics, vmem_limit_bytes=VMEM_LIMIT_BYTES)


def _rms(xf, gain):
    return xf * lax.rsqrt(jnp.mean(xf * xf, axis=-1, keepdims=True) + EPS) * gain


def _nt_dot(w_t, h):
    return lax.dot_general(w_t, h, (((1,), (1,)), ((), ())), preferred_element_type=F32)


def _sigmoid(x):
    return 0.5 * jnp.tanh(0.5 * x) + 0.5


def _silu(x):
    return x * _sigmoid(x)


def _inproj_kernel(x_ref, g_ref, w_ref, u_ref, h4_ref):
    h = _rms(x_ref[...], g_ref[...]).astype(BF16)
    p = jnp.dot(h, w_ref[...], preferred_element_type=F32)
    u_ref[...] = p[:, :S5_WIDTH]
    h4_ref[...] = p[:, S5_WIDTH:]


def _inproj(x2, gain, w_bf16, tm=512):
    n = x2.shape[0]
    e_in = w_bf16.shape[1]
    return pl.pallas_call(
        _inproj_kernel,
        out_shape=(jax.ShapeDtypeStruct((n, S5_WIDTH), F32),
                   jax.ShapeDtypeStruct((n, e_in - S5_WIDTH), F32)),
        grid=(n // tm,),
        in_specs=[pl.BlockSpec((tm, D_MODEL), lambda i: (i, 0)),
                  pl.BlockSpec((1, D_MODEL), lambda i: (0, 0)),
                  pl.BlockSpec((D_MODEL, e_in), lambda i: (0, 0))],
        out_specs=(pl.BlockSpec((tm, S5_WIDTH), lambda i: (i, 0)),
                   pl.BlockSpec((tm, e_in - S5_WIDTH), lambda i: (i, 0))),
        compiler_params=_params("parallel"),
        name="even_inproj",
    )(x2, gain.reshape(1, D_MODEL), w_bf16)


def _s5_lagkernel_kernel(ca_ref, bb_ref, k_ref):
    k_ref[0] = jnp.dot(ca_ref[0], bb_ref[0], preferred_element_type=F32,
                       precision=lax.Precision.HIGHEST)


def _s5_lagkernel(ca, bb):
    g, rows, k = ca.shape
    return pl.pallas_call(
        _s5_lagkernel_kernel,
        out_shape=jax.ShapeDtypeStruct((g, rows, S5_GROUP), F32),
        grid=(g,),
        in_specs=[pl.BlockSpec((1, rows, k), lambda i: (i, 0, 0)),
                  pl.BlockSpec((1, k, S5_GROUP), lambda i: (i, 0, 0))],
        out_specs=pl.BlockSpec((1, rows, S5_GROUP), lambda i: (i, 0, 0)),
        compiler_params=_params("parallel"),
        name="s5_lag_kernel",
    )(ca, bb)


def _s5_operators(lam_re, lam_im, log_step, b_re, b_im, c_re, c_im):
    t = S5_CHUNK
    lr, li = lam_re.astype(F32), lam_im.astype(F32)
    step = jnp.exp(log_step.astype(F32))[:, None]
    mag = jnp.exp(lr * step)
    ab_re = mag * jnp.cos(li * step)
    ab_im = mag * jnp.sin(li * step)
    den = lr * lr + li * li
    nr, ni = ab_re - 1.0, ab_im
    z_re = (nr * lr + ni * li) / den
    z_im = (ni * lr - nr * li) / den
    br, bi = b_re.astype(F32), b_im.astype(F32)
    bb_re = z_re[..., None] * br - z_im[..., None] * bi
    bb_im = z_re[..., None] * bi + z_im[..., None] * br
    kk = jnp.arange(t + 1, dtype=F32)[:, None, None]
    pmag = jnp.exp(kk * (lr * step)[None])
    pw_re = pmag * jnp.cos(kk * (li * step)[None])
    pw_im = pmag * jnp.sin(kk * (li * step)[None])
    cr = jnp.transpose(c_re.astype(F32), (0, 1, 2))
    ci = c_im.astype(F32)
    ca_re = cr[None] * pw_re[:, :, None, :] - ci[None] * pw_im[:, :, None, :]
    ca_im = cr[None] * pw_im[:, :, None, :] + ci[None] * pw_re[:, :, None, :]
    g = lr.shape[0]
    ca_cat = jnp.concatenate([ca_re[:t], -ca_im[:t]], axis=-1)
    ca_cat = jnp.transpose(ca_cat, (1, 0, 2, 3)).reshape(g, t * S5_GROUP, 2 * S5_STATE)
    bb_cat = jnp.concatenate([bb_re, bb_im], axis=1)
    kern = _s5_lagkernel(ca_cat, bb_cat).reshape(g, t, S5_GROUP, S5_GROUP)
    s_idx = jnp.arange(t)[:, None]
    t_idx = jnp.arange(t)[None, :]
    lag = t_idx - s_idx
    kg = kern[:, jnp.clip(lag, 0, t - 1)]
    kg = jnp.where((lag >= 0)[None, :, :, None, None], kg, 0.0)
    mt = jnp.transpose(kg, (0, 1, 4, 2, 3)).reshape(g, t * S5_GROUP, t * S5_GROUP)
    pr = pw_re[:t][::-1]
    pi = pw_im[:t][::-1]
    sb_re = pr[:, :, :, None] * bb_re[None] - pi[:, :, :, None] * bb_im[None]
    sb_im = pr[:, :, :, None] * bb_im[None] + pi[:, :, :, None] * bb_re[None]
    sb_re = jnp.transpose(sb_re, (1, 0, 3, 2)).reshape(g, t * S5_GROUP, S5_STATE)
    sb_im = jnp.transpose(sb_im, (1, 0, 3, 2)).reshape(g, t * S5_GROUP, S5_STATE)
    cp_re = jnp.transpose(ca_re[1:], (1, 3, 0, 2)).reshape(g, S5_STATE, t * S5_GROUP)
    cp_im = jnp.transpose(-ca_im[1:], (1, 3, 0, 2)).reshape(g, S5_STATE, t * S5_GROUP)

    def pair_rows(m):
        m = m.reshape(g // 2, 2, m.shape[1], m.shape[2])
        z = jnp.zeros_like(m[:, 0])
        top = jnp.concatenate([m[:, 0], z], axis=2)
        bot = jnp.concatenate([z, m[:, 1]], axis=2)
        return jnp.concatenate([top, bot], axis=1)

    at_re = pw_re[t].reshape(g // 2, 1, 2 * S5_STATE)
    at_im = pw_im[t].reshape(g // 2, 1, 2 * S5_STATE)
    return (mt.astype(BF16), pair_rows(sb_re).astype(BF16), pair_rows(sb_im).astype(BF16),
            pair_rows(cp_re).astype(BF16), pair_rows(cp_im).astype(BF16), at_re, at_im)


PACK_TOKENS = 512


LANES = 128
GROUPS_PER_TILE = LANES // S5_GROUP
TOKENS_PER_TILE = LANES // S5_GROUP
CHUNK_HALVES = S5_CHUNK // TOKENS_PER_TILE
PACK_CHUNKS = PACK_TOKENS // S5_CHUNK


def _block_swap_matrix():
    a, b, h = np.meshgrid(np.arange(TOKENS_PER_TILE), np.arange(GROUPS_PER_TILE), np.arange(S5_GROUP),
                          indexing="ij")
    src = (a * GROUPS_PER_TILE + b) * S5_GROUP + h
    dst = (b * TOKENS_PER_TILE + a) * S5_GROUP + h
    m = np.zeros((src.size, src.size), np.float32)
    m[src.ravel(), dst.ravel()] = 1.0
    return jnp.asarray(m, dtype=BF16)


def _s5_pack_kernel(u_ref, swap_ref, o_ref, *, bsz):
    for j in range(CHUNK_HALVES):
        rows = [jnp.concatenate([u_ref[b, pl.ds(j * TOKENS_PER_TILE + tt, PACK_CHUNKS, stride=S5_CHUNK), :]
                                 for tt in range(TOKENS_PER_TILE)], axis=1) for b in range(bsz)]
        lhs = jnp.concatenate(rows, axis=0).astype(BF16)
        out = jnp.dot(lhs, swap_ref[...], preferred_element_type=F32)
        for g in range(GROUPS_PER_TILE):
            for b in range(bsz):
                o_ref[g * CHUNK_HALVES + j, pl.ds(b, PACK_CHUNKS, stride=bsz), :] = (
                    out[b * PACK_CHUNKS:(b + 1) * PACK_CHUNKS, g * LANES:(g + 1) * LANES])


def _s5_pack(u3):
    bsz, seqlen, w = u3.shape
    rows = PACK_CHUNKS * bsz
    swap = _block_swap_matrix()
    return pl.pallas_call(
        functools.partial(_s5_pack_kernel, bsz=bsz),
        out_shape=jax.ShapeDtypeStruct((S5_GROUPS * CHUNK_HALVES, seqlen // S5_CHUNK * bsz, LANES), F32),
        grid=(seqlen // PACK_TOKENS, w // LANES),
        in_specs=[pl.BlockSpec((bsz, PACK_TOKENS, LANES), lambda i, k: (0, i, k)),
                  pl.BlockSpec(swap.shape, lambda i, k: (0, 0))],
        out_specs=pl.BlockSpec((GROUPS_PER_TILE * CHUNK_HALVES, rows, LANES), lambda i, k: (k, i, 0)),
        compiler_params=_params("parallel", "parallel"),
        name="s5_pack",
    )(u3, swap)


def _s5_unpack_kernel(y_ref, swap_ref, o_ref, *, bsz):
    for j in range(CHUNK_HALVES):
        rows = [jnp.concatenate([y_ref[g * CHUNK_HALVES + j, pl.ds(b, PACK_CHUNKS, stride=bsz), :]
                                 for g in range(GROUPS_PER_TILE)], axis=1) for b in range(bsz)]
        lhs = jnp.concatenate(rows, axis=0).astype(BF16)
        out = jnp.dot(lhs, swap_ref[...], preferred_element_type=F32)
        for tt in range(TOKENS_PER_TILE):
            for b in range(bsz):
                o_ref[b, pl.ds(j * TOKENS_PER_TILE + tt, PACK_CHUNKS, stride=S5_CHUNK), :] = (
                    out[b * PACK_CHUNKS:(b + 1) * PACK_CHUNKS, tt * LANES:(tt + 1) * LANES])


def _s5_unpack(y_g, bsz, seqlen):
    rows = PACK_CHUNKS * bsz
    swap = _block_swap_matrix()
    return pl.pallas_call(
        functools.partial(_s5_unpack_kernel, bsz=bsz),
        out_shape=jax.ShapeDtypeStruct((bsz, seqlen, S5_WIDTH), F32),
        grid=(seqlen // PACK_TOKENS, S5_WIDTH // LANES),
        in_specs=[pl.BlockSpec((GROUPS_PER_TILE * CHUNK_HALVES, rows, LANES), lambda i, k: (k, i, 0)),
                  pl.BlockSpec(swap.shape, lambda i, k: (0, 0))],
        out_specs=pl.BlockSpec((bsz, PACK_TOKENS, LANES), lambda i, k: (0, i, k)),
        compiler_params=_params("parallel", "parallel"),
        name="s5_unpack",
    )(y_g, swap)


def _s5_kernel(u_ref, mt_ref, wre_ref, wim_ref, cre_ref, cim_ref, atr_ref, ati_ref, y_ref,
               sre_ref, sim_ref, xre_ref, xim_ref, *, n_chunks, bsz):
    ucat = jnp.concatenate([u_ref[i] for i in range(2 * CHUNK_HALVES)], axis=1).astype(BF16)
    w = S5_CHUNK * S5_GROUP
    u0 = ucat[:, :w]
    u1 = ucat[:, w:]
    sre_ref[...] = jnp.dot(ucat, wre_ref[0], preferred_element_type=F32)
    sim_ref[...] = jnp.dot(ucat, wim_ref[0], preferred_element_type=F32)
    atr = jnp.broadcast_to(atr_ref[0], (bsz, 2 * S5_STATE))
    ati = jnp.broadcast_to(ati_ref[0], (bsz, 2 * S5_STATE))

    def body(c, carry):
        xr, xi = carry
        rows = pl.ds(pl.multiple_of(c * bsz, bsz), bsz)
        xre_ref[rows, :] = xr
        xim_ref[rows, :] = xi
        nxr = atr * xr - ati * xi + sre_ref[rows, :]
        nxi = atr * xi + ati * xr + sim_ref[rows, :]
        return nxr, nxi

    zero = jnp.zeros((bsz, 2 * S5_STATE), F32)
    lax.fori_loop(0, n_chunks, body, (zero, zero))
    ycar = (jnp.dot(xre_ref[...].astype(BF16), cre_ref[0], preferred_element_type=F32)
            + jnp.dot(xim_ref[...].astype(BF16), cim_ref[0], preferred_element_type=F32))
    y0 = jnp.dot(u0, mt_ref[0], preferred_element_type=F32) + ycar[:, :w]
    y1 = jnp.dot(u1, mt_ref[1], preferred_element_type=F32) + ycar[:, w:]
    for i in range(CHUNK_HALVES):
        y_ref[i] = y0[:, i * LANES:(i + 1) * LANES]
        y_ref[CHUNK_HALVES + i] = y1[:, i * LANES:(i + 1) * LANES]


def _s5_scan(u_g, ops, bsz):
    mt, wre, wim, cre, cim, atr, ati = ops
    tiles, r, _ = u_g.shape
    g = tiles // CHUNK_HALVES
    w = S5_CHUNK * S5_GROUP
    n_chunks = r // bsz
    p2 = 2 * S5_STATE
    kern = functools.partial(_s5_kernel, n_chunks=n_chunks, bsz=bsz)
    pair_tiles = pl.BlockSpec((2 * CHUNK_HALVES, r, LANES), lambda i: (i, 0, 0))
    return pl.pallas_call(
        kern,
        out_shape=jax.ShapeDtypeStruct((tiles, r, LANES), F32),
        grid=(g // 2,),
        in_specs=[pair_tiles,
                  pl.BlockSpec((2, w, w), lambda i: (i, 0, 0)),
                  pl.BlockSpec((1, 2 * w, p2), lambda i: (i, 0, 0)),
                  pl.BlockSpec((1, 2 * w, p2), lambda i: (i, 0, 0)),
                  pl.BlockSpec((1, p2, 2 * w), lambda i: (i, 0, 0)),
                  pl.BlockSpec((1, p2, 2 * w), lambda i: (i, 0, 0)),
                  pl.BlockSpec((1, 1, p2), lambda i: (i, 0, 0)),
                  pl.BlockSpec((1, 1, p2), lambda i: (i, 0, 0))],
        out_specs=pair_tiles,
        scratch_shapes=[pltpu.VMEM((r, p2), F32)] * 4,
        compiler_params=_params("parallel"),
        name="s5_scan",
    )(u_g, mt, wre, wim, cre, cim, atr, ati)


def _hgrn_kernel(q_ref, f_ref, i_ref, g_ref, lb_ref, og_ref, o_ref, st_ref, *, seqlen):
    c = HG_CHUNK
    nc = seqlen // c
    d = HG_HEAD_DIM
    lb = lb_ref[...]
    q = q_ref[0]
    qs = _silu(q)
    f = lb + (1.0 - lb) * _sigmoid(f_ref[0])
    lf = jnp.log(f)
    k = 1.0 - f
    v = i_ref[0]
    pos = lax.broadcasted_iota(jnp.int32, (seqlen, d), 0) % c
    b = lf
    sh = 1
    while sh < c:
        b = b + jnp.where(pos >= sh, pltpu.roll(b, sh, axis=0), 0.0)
        sh *= 2
    b3 = b.reshape(nc, c, d)
    b_last = b3[:, c - 1:c, :]
    b_ref = b3[:, c // 2 - 1:c // 2, :]
    qs3 = qs.reshape(nc, c, d)
    k3 = k.reshape(nc, c, d)
    v3 = v.reshape(nc, c, d).astype(BF16)
    qe_f = qs3 * jnp.exp(b3 - b_ref)
    ke_f = k3 * jnp.exp(b_ref - b3)
    qe = qe_f.astype(BF16)
    ke = ke_f.astype(BF16)
    kd = (ke_f * jnp.exp(b_last - b_ref)).astype(BF16)
    qb = (qe_f * jnp.exp(b_ref)).astype(BF16)
    scores = jnp.einsum('ctd,csd->cts', qe, ke, preferred_element_type=F32)
    ti = lax.broadcasted_iota(jnp.int32, (c, c), 0)
    si = lax.broadcasted_iota(jnp.int32, (c, c), 1)
    scores = jnp.where((ti >= si)[None], scores, 0.0)
    o_intra = jnp.einsum('cts,csv->ctv', scores.astype(BF16), v3, preferred_element_type=F32)
    ut = jnp.einsum('csv,csd->cvd', v3, kd, preferred_element_type=F32)
    decay = jnp.exp(b_last)
    state = jnp.zeros((d, d), F32)
    for ci in range(nc):
        st_ref[ci] = state.astype(BF16)
        state = decay[ci] * state + ut[ci]
    o_inter = jnp.einsum('ctd,cvd->ctv', qb, st_ref[...], preferred_element_type=F32)
    o = (o_intra + o_inter).reshape(seqlen, d)
    o = _rms(o, og_ref[...])
    o_ref[0] = (o * _silu(g_ref[0])).astype(BF16)


def _hgrn(h4, lower_bound, o_gain, bsz, seqlen):
    d = HG_HEAD_DIM
    kern = functools.partial(_hgrn_kernel, seqlen=seqlen)

    def col(part):
        return pl.BlockSpec((1, seqlen, d), lambda b, h: (b, 0, part * HG_HEADS + h))

    return pl.pallas_call(
        kern,
        out_shape=jax.ShapeDtypeStruct((bsz, seqlen, HG_WIDTH), BF16),
        grid=(bsz, HG_HEADS),
        in_specs=[col(0), col(1), col(2), col(3),
                  pl.BlockSpec((1, d), lambda b, h: (0, h)),
                  pl.BlockSpec((1, d), lambda b, h: (0, 0))],
        out_specs=pl.BlockSpec((1, seqlen, d), lambda b, h: (b, 0, h)),
        scratch_shapes=[pltpu.VMEM((seqlen // HG_CHUNK, d, d), BF16)],
        compiler_params=_params("parallel", "parallel"),
        name="hgrn2",
    )(h4, h4, h4, h4, lower_bound.reshape(1, HG_WIDTH), o_gain.reshape(1, d))


def _evenout_kernel(x_ref, ys_ref, u_ref, b_ref, d_ref, wglu_ref, wa_ref, wb_ref, *rest):
    route_in, (o_ref, *route_out), route_scratch = rest[:4], rest[4:8], rest[8:]
    y = ys_ref[...] + d_ref[...] * u_ref[...]
    y = jax.nn.gelu(y)
    gate = _sigmoid(jnp.dot(y.astype(BF16), wglu_ref[...], preferred_element_type=F32))
    a = (y * gate).astype(BF16)
    mix = (jnp.dot(a, wa_ref[...], preferred_element_type=F32)
           + jnp.dot(b_ref[...], wb_ref[...], preferred_element_type=F32))
    x_new = x_ref[...] + mix
    o_ref[...] = x_new
    _route(x_new, pl.program_id(0) == 0, *route_in, *route_out, *route_scratch)


def _evenout(x2, ys, u, b_out, d_skip, wglu, wout, router_params, tm=512):
    n = x2.shape[0]
    route = _RouterPlumbing(n, tm, lambda i: i, *router_params)
    row = lambda w: pl.BlockSpec((tm, w), lambda i: (i, 0))
    full = lambda r, c: pl.BlockSpec((r, c), lambda i: (0, 0))
    return pl.pallas_call(
        _evenout_kernel,
        out_shape=(jax.ShapeDtypeStruct((n, D_MODEL), F32), *route.out_shape),
        grid=(n // tm,),
        in_specs=[row(D_MODEL), row(S5_WIDTH), row(S5_WIDTH), row(HG_WIDTH),
                  full(1, S5_WIDTH), full(S5_WIDTH, S5_WIDTH),
                  full(S5_WIDTH, D_MODEL), full(HG_WIDTH, D_MODEL), *route.in_specs],
        out_specs=(row(D_MODEL), *route.out_specs),
        scratch_shapes=route.scratch_shapes,
        compiler_params=_params("arbitrary"),
        name="even_out",
    )(x2, ys, u, b_out, d_skip.reshape(1, S5_WIDTH), wglu.astype(BF16),
      wout[:S5_WIDTH].astype(BF16), wout[S5_WIDTH:].astype(BF16), *route.operands)


def _route(x, is_first_step, g_ref, wr_ref, br_ref, tri_ref, idx_ref, wts_ref, cnt_ref, run_ref):
    @pl.when(is_first_step)
    def _():
        run_ref[...] = jnp.zeros_like(run_ref)

    h = _rms(x, g_ref[...])
    h_hi = h.astype(BF16)
    h_lo = (h - h_hi.astype(F32)).astype(BF16)
    both = _nt_dot(wr_ref[...], h_hi)
    lt = (both[:ROUTER_ROWS] + both[ROUTER_ROWS:] + _nt_dot(wr_ref[:ROUTER_ROWS, :], h_lo)
          + br_ref[...])
    gl = [lt[i:i + 1] for i in range(N_GROUPS)]
    el = [lt[N_GROUPS + i:N_GROUPS + i + 1] for i in range(N_EXPERTS)]
    gmax = jnp.maximum(jnp.maximum(gl[0], gl[1]), jnp.maximum(gl[2], gl[3]))
    gexp = [jnp.exp(v - gmax) for v in gl]
    gsum = gexp[0] + gexp[1] + gexp[2] + gexp[3]
    gprob = [v / gsum for v in gexp]
    g_gate = jnp.maximum(jnp.maximum(gprob[0], gprob[1]), jnp.maximum(gprob[2], gprob[3]))
    g_idx = jnp.where(gprob[0] == g_gate, 0,
                      jnp.where(gprob[1] == g_gate, 1, jnp.where(gprob[2] == g_gate, 2, 3)))
    es = []
    for j in range(EXPERTS_PER_GROUP):
        es.append(jnp.where(g_idx == 0, el[j],
                            jnp.where(g_idx == 1, el[4 + j],
                                      jnp.where(g_idx == 2, el[8 + j], el[12 + j]))))
    emax = jnp.maximum(jnp.maximum(es[0], es[1]), jnp.maximum(es[2], es[3]))
    eexp = [jnp.exp(v - emax) for v in es]
    esum = eexp[0] + eexp[1] + eexp[2] + eexp[3]
    ep = [v / esum for v in eexp]
    p1 = jnp.maximum(jnp.maximum(ep[0], ep[1]), jnp.maximum(ep[2], ep[3]))
    i1 = jnp.where(ep[0] == p1, 0, jnp.where(ep[1] == p1, 1, jnp.where(ep[2] == p1, 2, 3)))
    neg = jnp.float32(-1.0)
    rest = [jnp.where(i1 == j, neg, ep[j]) for j in range(EXPERTS_PER_GROUP)]
    p2 = jnp.maximum(jnp.maximum(rest[0], rest[1]), jnp.maximum(rest[2], rest[3]))
    i2 = jnp.where(rest[0] == p2, 0, jnp.where(rest[1] == p2, 1, jnp.where(rest[2] == p2, 2, 3)))
    wsum = p1 + p2
    w1 = g_gate * (p1 / wsum)
    w2 = g_gate * (p2 / wsum)
    first_lo = i1 < i2
    lo = jnp.where(first_lo, i1, i2)
    hi = jnp.where(first_lo, i2, i1)
    w_lo = jnp.where(first_lo, w1, w2)
    w_hi = jnp.where(first_lo, w2, w1)
    pair = jnp.where(lo == 0, 0, jnp.where(lo == 1, 3, 5)) + hi - lo - 1
    bucket = g_idx * N_PAIRS + pair
    tm = bucket.shape[1]
    rowid = lax.broadcasted_iota(jnp.int32, (BUCKET_ROWS, tm), 0)
    onehot = (rowid == bucket).astype(F32)
    prefix = jnp.dot(onehot.astype(BF16), tri_ref[...], preferred_element_type=F32)
    run = run_ref[...]
    rank = jnp.sum(onehot * (prefix + run), axis=0, keepdims=True)
    run = run + jnp.sum(onehot, axis=1, keepdims=True)
    run_ref[...] = run
    cnt_ref[...] = jnp.broadcast_to(run, cnt_ref.shape)
    idx_ref[...] = jnp.concatenate([bucket, rank.astype(jnp.int32), jnp.zeros((6, tm), jnp.int32)], axis=0)
    wts_ref[...] = jnp.concatenate([w_lo, w_hi, jnp.zeros((6, tm), F32)], axis=0)


class _RouterPlumbing:
    def __init__(self, n, tm, tile_index, gain, w_rg, b_rg, w_re, b_re):
        wr = jnp.concatenate([w_rg, w_re], axis=1).astype(F32).T
        wr = jnp.pad(wr, ((0, ROUTER_ROWS - wr.shape[0]), (0, 0)))
        wr_hi = wr.astype(BF16)
        wr = jnp.concatenate([wr_hi, (wr - wr_hi.astype(F32)).astype(BF16)], axis=0)
        br = jnp.pad(jnp.concatenate([b_rg, b_re]).astype(F32), (0, ROUTER_ROWS - N_GROUPS - N_EXPERTS))
        tri = (np.arange(tm)[:, None] < np.arange(tm)[None, :]).astype(np.float32)
        const = lambda *_: (0, 0)
        self.operands = (gain.reshape(1, D_MODEL), wr, br.reshape(ROUTER_ROWS, 1), jnp.asarray(tri, dtype=BF16))
        self.in_specs = [pl.BlockSpec((1, D_MODEL), const),
                         pl.BlockSpec((2 * ROUTER_ROWS, D_MODEL), const),
                         pl.BlockSpec((ROUTER_ROWS, 1), const),
                         pl.BlockSpec((tm, tm), const)]
        self.out_shape = (jax.ShapeDtypeStruct((8, n), jnp.int32),
                          jax.ShapeDtypeStruct((8, n), F32),
                          jax.ShapeDtypeStruct((BUCKET_ROWS, LANES), F32))
        self.out_specs = (pl.BlockSpec((8, tm), lambda *g: (0, tile_index(*g))),
                          pl.BlockSpec((8, tm), lambda *g: (0, tile_index(*g))),
                          pl.BlockSpec((BUCKET_ROWS, LANES), const))
        self.scratch_shapes = [pltpu.VMEM((BUCKET_ROWS, 1), F32)]


def _router_kernel(x_ref, *route_refs):
    _route(x_ref[...], pl.program_id(0) == 0, *route_refs)


def _router(x2, router_params, tm=512):
    n = x2.shape[0]
    route = _RouterPlumbing(n, tm, lambda i: i, *router_params)
    return pl.pallas_call(
        _router_kernel,
        out_shape=route.out_shape,
        grid=(n // tm,),
        in_specs=[pl.BlockSpec((tm, D_MODEL), lambda i: (i, 0)), *route.in_specs],
        out_specs=route.out_specs,
        scratch_shapes=route.scratch_shapes,
        compiler_params=_params("arbitrary"),
        name="moe_router",
    )(x2, *route.operands)


ROW_COPY_UNROLL = 8


def _start_row_copies(idx_ref, n_rows, copy_for_row, prepare_rows=None):
    def start_group(base):
        for j in range(ROW_COPY_UNROLL):
            copy_for_row(base + j, idx_ref[0, 0, base + j]).start(priority=j % 2)

    n_groups = n_rows // ROW_COPY_UNROLL
    if prepare_rows is None:
        def body(g, carry):
            start_group(pl.multiple_of(g * ROW_COPY_UNROLL, ROW_COPY_UNROLL))
            return carry

        lax.fori_loop(0, n_groups, body, 0)
        return

    prepare_rows(0)

    def body(g, carry):
        base = pl.multiple_of(g * ROW_COPY_UNROLL, ROW_COPY_UNROLL)
        prepare_rows(base + ROW_COPY_UNROLL)
        start_group(base)
        return carry

    lax.fori_loop(0, n_groups - 1, body, 0)
    start_group(n_rows - ROW_COPY_UNROLL)


def _row_slab(view_ref, p):
    return view_ref.at[p >> ROW_TILE_SHIFT, :, p & (ROW_TILE[0] - 1)]


def _view_columns(view_ref, n_cols):
    rows = view_ref.shape[0] * ROW_TILE[0]
    return jnp.concatenate([view_ref[:, c].reshape(rows, LANES) for c in range(n_cols)], axis=1)


def _rows_to_tiles(x):
    rows = x.shape[0]
    return x.reshape(rows * ROW_TILE[0], ROW_TILE[1]).reshape(rows, *ROW_TILE)


def _tiles_to_rows(x3):
    rows = x3.shape[0]
    return x3.reshape(rows * ROW_TILE[0], ROW_TILE[1]).reshape(rows, D_MODEL)


def _dispatch_kernel(tail_blk_ref, tail_on_ref, pos_ref, x_ref, w_ref, xs_ref, buf_ref, zero_ref, wcol_ref, sem,
                     *, tile):
    tm = x_ref.shape[0]
    tile_blks = tile // ROW_TILE[0]

    @pl.when(pl.program_id(0) == 0)
    def _():
        zero_ref[...] = jnp.zeros_like(zero_ref)

        def zero_copy(k):
            blk = pl.multiple_of(tail_blk_ref[k], tile_blks)
            return pltpu.make_async_copy(zero_ref, xs_ref.at[pl.ds(blk, tile_blks)], sem)

        for k in range(2 * N_BUCKETS):
            pl.when(tail_on_ref[k] > 0)(lambda k=k: zero_copy(k).start())
        for k in range(2 * N_BUCKETS):
            pl.when(tail_on_ref[k] > 0)(lambda k=k: zero_copy(k).wait())

    wpad = jnp.concatenate([w_ref[...], jnp.zeros((LANES - w_ref.shape[0], tm), F32)], axis=0)
    wcol_ref[...] = wpad.T

    def stage(base):
        rows = pl.ds(base, ROW_COPY_UNROLL)
        buf_ref[rows, :X_TILES, :] = _rows_to_tiles(x_ref[rows, :])
        buf_ref[rows, X_TILES, :] = wcol_ref[rows, :]

    _start_row_copies(pos_ref, tm, lambda r, p: pltpu.make_async_copy(buf_ref.at[r], _row_slab(xs_ref, p), sem),
                      prepare_rows=stage)
    done = xs_ref.at[pl.ds(0, tm // ROW_TILE[0])]
    pltpu.make_async_copy(done, done, sem).wait()


def _dispatch(x2, wts, pos3, tails, n_rows_sorted, tile, tm):
    n = x2.shape[0]
    tail_blk, tail_on = tails
    grid_spec = pltpu.PrefetchScalarGridSpec(
        num_scalar_prefetch=2,
        grid=(n // tm,),
        in_specs=[pl.BlockSpec((1, 1, tm), lambda i, *_: (i, 0, 0), memory_space=pltpu.SMEM),
                  pl.BlockSpec((tm, D_MODEL), lambda i, *_: (i, 0)),
                  pl.BlockSpec((8, tm), lambda i, *_: (0, i))],
        out_specs=pl.BlockSpec(memory_space=pl.ANY),
        scratch_shapes=[pltpu.VMEM((tm, XS_TILES, LANES), F32),
                        pltpu.VMEM((tile // ROW_TILE[0], XS_TILES, *ROW_TILE), F32),
                        pltpu.VMEM((tm, LANES), F32),
                        pltpu.SemaphoreType.DMA],
    )
    return pl.pallas_call(
        functools.partial(_dispatch_kernel, tile=tile),
        out_shape=jax.ShapeDtypeStruct((n_rows_sorted // ROW_TILE[0], XS_TILES, *ROW_TILE), F32),
        grid_spec=grid_spec,
        compiler_params=_params("arbitrary"),
        name="moe_dispatch",
    )(tail_blk, tail_on, pos3, x2, wts)


def _experts_kernel(elo_ref, ehi_ref, nvalid_ref, xs_ref, g_ref, wg_lo, wu_lo, wg_hi, wu_hi,
                    wd_lo, wd_hi, o_ref):
    del elo_ref, ehi_ref
    t = pl.program_id(0)

    @pl.when(t < nvalid_ref[0])
    def _():
        rows = xs_ref.shape[0] * ROW_TILE[0]
        xt = _view_columns(xs_ref, X_TILES)
        h = _rms(xt, g_ref[...]).astype(BF16)
        extra = xs_ref[:, X_TILES].reshape(rows, LANES)
        w_lo = extra[:, 0:1]
        w_hi = extra[:, 1:2]

        def expert(wg, wu, wd, w):
            gate = jnp.dot(h, wg[0], preferred_element_type=F32)
            up = jnp.dot(h, wu[0], preferred_element_type=F32)
            hid = (_silu(gate) * up * w).astype(BF16)
            return jnp.dot(hid, wd[0], preferred_element_type=F32)

        out = xt + expert(wg_lo, wu_lo, wd_lo, w_lo) + expert(wg_hi, wu_hi, wd_hi, w_hi)
        for c in range(X_TILES):
            o_ref[:, c] = out[:, c * LANES:(c + 1) * LANES].reshape(o_ref.shape[0], *ROW_TILE)

    @pl.when(t >= nvalid_ref[0])
    def _():
        o_ref[...] = jnp.zeros_like(o_ref)


def _experts(xs, gain, tables, wg, wu, wd, n_tiles, t):
    elo, ehi, nvalid = tables
    blks = t // ROW_TILE[0]
    row = lambda i, elo, ehi, nv: (i, 0, 0, 0)
    row_in = lambda i, elo, ehi, nv: (jnp.minimum(i, nv[0] - 1), 0, 0, 0)
    lo3 = lambda i, elo, ehi, nv: (elo[i], 0, 0)
    hi3 = lambda i, elo, ehi, nv: (ehi[i], 0, 0)
    grid_spec = pltpu.PrefetchScalarGridSpec(
        num_scalar_prefetch=3,
        grid=(n_tiles,),
        in_specs=[pl.BlockSpec((blks, XS_TILES, *ROW_TILE), row_in),
                  pl.BlockSpec((1, D_MODEL), lambda i, *_: (0, 0)),
                  pl.BlockSpec((1, D_MODEL, D_EXPERT), lo3),
                  pl.BlockSpec((1, D_MODEL, D_EXPERT), lo3),
                  pl.BlockSpec((1, D_MODEL, D_EXPERT), hi3),
                  pl.BlockSpec((1, D_MODEL, D_EXPERT), hi3),
                  pl.BlockSpec((1, D_EXPERT, D_MODEL), lo3),
                  pl.BlockSpec((1, D_EXPERT, D_MODEL), hi3)],
        out_specs=pl.BlockSpec((blks, X_TILES, *ROW_TILE), row),
    )
    return pl.pallas_call(
        _experts_kernel,
        out_shape=jax.ShapeDtypeStruct((xs.shape[0], X_TILES, *ROW_TILE), F32),
        grid_spec=grid_spec,
        compiler_params=_params("arbitrary"),
        name="moe_experts",
    )(elo, ehi, nvalid, xs, gain.reshape(1, D_MODEL), wg, wu, wg, wu, wd, wd)


def _combine_kernel(pos_ref, ys_ref, o_ref, buf_ref, sem):
    tm = o_ref.shape[0]
    _start_row_copies(pos_ref, tm, lambda r, p: pltpu.make_async_copy(_row_slab(ys_ref, p), buf_ref.at[r], sem))
    done = ys_ref.at[pl.ds(0, tm // ROW_TILE[0])]
    pltpu.make_async_copy(done, done, sem).wait()
    o_ref[...] = _tiles_to_rows(buf_ref[...])


def _combine(ys, pos3, n, tm):
    return pl.pallas_call(
        _combine_kernel,
        out_shape=jax.ShapeDtypeStruct((n, D_MODEL), F32),
        grid=(n // tm,),
        in_specs=[pl.BlockSpec((1, 1, tm), lambda i: (i, 0, 0), memory_space=pltpu.SMEM),
                  pl.BlockSpec(memory_space=pl.ANY)],
        out_specs=pl.BlockSpec((tm, D_MODEL), lambda i: (i, 0)),
        scratch_shapes=[pltpu.VMEM((tm, *ROW_TILE), F32), pltpu.SemaphoreType.DMA],
        compiler_params=_params("arbitrary"),
        name="moe_combine",
    )(pos3, ys)


def _moe_tables(idx, cnt, n_tiles, t):
    bucket, rank = idx[0], idx[1]
    counts = cnt[:N_BUCKETS, 0].astype(jnp.int32)
    tiles_b = (counts + t - 1) // t
    tile_end = jnp.cumsum(tiles_b)
    pos = (tile_end - tiles_b)[bucket] * t + rank
    total = tile_end[-1]
    tt = jnp.arange(n_tiles, dtype=jnp.int32)
    valid = tt < total
    tb = jnp.sum((tile_end[None, :] <= jnp.where(valid, tt, total - 1)[:, None]).astype(jnp.int32), axis=1)
    tb = jnp.minimum(tb, N_BUCKETS - 1)
    pair_lo = jnp.asarray([0, 0, 0, 1, 1, 2], jnp.int32)
    pair_hi = jnp.asarray([1, 2, 3, 2, 3, 3], jnp.int32)
    base = (tb // N_PAIRS) * EXPERTS_PER_GROUP
    idle = total + jnp.arange(N_BUCKETS, dtype=jnp.int32)
    idle_on = idle < n_tiles
    blks = t // ROW_TILE[0]
    tails = (jnp.concatenate([(tile_end - 1) * blks, jnp.where(idle_on, idle, 0) * blks]),
             jnp.concatenate([tiles_b > 0, idle_on]).astype(jnp.int32))
    return pos, tails, (base + pair_lo[tb % N_PAIRS], base + pair_hi[tb % N_PAIRS], total.reshape(1))


def _moe(x2, routing, gain, wg, wu, wd, t=MOE_TILE, tm=1024):
    n = x2.shape[0]
    idx, wts, cnt = routing
    n_tiles = n // t + N_BUCKETS
    pos, tails, tables = _moe_tables(idx, cnt, n_tiles, t)
    pos3 = pos.reshape(n // tm, 1, tm)
    xs = _dispatch(x2, wts, pos3, tails, n_tiles * t, t, tm)
    ys = _experts(xs, gain, tables, wg.astype(BF16), wu.astype(BF16), wd.astype(BF16), n_tiles, t)
    return _combine(ys, pos3, n, tm)


LOG2E = math.log2(math.e)
V_EXT = 2 * HEAD_DIM


def _head_norm_t(y_t, n_heads, scale):
    tm = y_t.shape[1]
    y3 = y_t.reshape(n_heads, HEAD_DIM, tm)
    ms = jnp.mean(y3 * y3, axis=1, keepdims=True)
    return y3 * (lax.rsqrt(ms + EPS) * scale)


def _qkv_kernel(x_ref, g_ref, wqt_ref, wkt_ref, wvt_ref, vone_ref, kg_ref, qt_ref, k_ref, vt_ref):
    h = _rms(x_ref[...], g_ref[...]).astype(BF16)
    tm = h.shape[0]
    qn = _head_norm_t(_nt_dot(wqt_ref[...], h), N_Q_HEADS, HEAD_DIM ** -0.5 * LOG2E)
    qt_ref[...] = qn.reshape(N_Q_HEADS * HEAD_DIM, tm).astype(BF16)
    kn = _head_norm_t(_nt_dot(wkt_ref[...], h), N_KV_HEADS, kg_ref[...].reshape(N_KV_HEADS, HEAD_DIM, 1))
    for hk in range(N_KV_HEADS):
        k_ref[hk] = kn[hk].T.astype(BF16)
    vt_ref[...] = (_nt_dot(wvt_ref[...], h) + vone_ref[...]).astype(BF16)


def _qkv(x2, gain, wqkv, q_gain, k_gain, tm=512):
    n = x2.shape[0]
    qw = N_Q_HEADS * HEAD_DIM
    wqt = wqkv[:, :qw].T.astype(BF16)
    wkt = wqkv[:, qw:qw + KV_WIDTH].T.astype(BF16)
    wvt = wqkv[:, qw + KV_WIDTH:].T.astype(BF16).reshape(N_KV_HEADS, HEAD_DIM, D_MODEL)
    wvt = jnp.pad(wvt, ((0, 0), (0, V_EXT - HEAD_DIM), (0, 0))).reshape(N_KV_HEADS * V_EXT, D_MODEL)
    vone = np.zeros((N_KV_HEADS * V_EXT, 1), np.float32)
    vone[HEAD_DIM::V_EXT, 0] = 1.0
    kg = jnp.tile((k_gain.astype(F32) * q_gain.astype(F32)), N_KV_HEADS).reshape(KV_WIDTH, 1)
    full = lambda r, c: pl.BlockSpec((r, c), lambda i: (0, 0))
    return pl.pallas_call(
        _qkv_kernel,
        out_shape=(jax.ShapeDtypeStruct((qw, n), BF16),
                   jax.ShapeDtypeStruct((N_KV_HEADS, n, HEAD_DIM), BF16),
                   jax.ShapeDtypeStruct((N_KV_HEADS * V_EXT, n), BF16)),
        grid=(n // tm,),
        in_specs=[pl.BlockSpec((tm, D_MODEL), lambda i: (i, 0)), full(1, D_MODEL),
                  full(qw, D_MODEL), full(KV_WIDTH, D_MODEL), full(N_KV_HEADS * V_EXT, D_MODEL),
                  full(N_KV_HEADS * V_EXT, 1), full(KV_WIDTH, 1)],
        out_specs=(pl.BlockSpec((qw, tm), lambda i: (0, i)),
                   pl.BlockSpec((N_KV_HEADS, tm, HEAD_DIM), lambda i: (0, i, 0)),
                   pl.BlockSpec((N_KV_HEADS * V_EXT, tm), lambda i: (0, i))),
        compiler_params=_params("parallel"),
        name="odd_qkv",
    )(x2, gain.reshape(1, D_MODEL), wqt, wkt, wvt, jnp.asarray(vone), kg)


def _attn_bias():
    blk = ATT_BLOCK
    qi = np.arange(blk)[None, :]
    ki = np.arange(2 * blk)[:, None]
    dist = qi - ki + blk
    band = (dist >= 0) & (dist < blk)
    slopes = 2.0 ** (-8.0 * np.arange(1, N_Q_HEADS + 1) / N_Q_HEADS)
    pen = -slopes[:, None, None] * dist[None].astype(np.float64) * LOG2E
    inner = np.where(band[None], pen, -np.inf)
    first = np.where((band & (ki >= blk))[None], pen, -np.inf)
    tab = np.stack([inner, first]).astype(np.float32)
    tab = tab.reshape(2, N_KV_HEADS, GQA_GROUP, 2 * blk, blk).transpose(0, 1, 3, 2, 4)
    return tab.reshape(2, N_KV_HEADS, 2 * blk, GQA_GROUP * blk)


def _attn_kernel(qt_ref, kp_ref, kc_ref, vtp_ref, vtc_ref, bias_ref, sink_ref, x_ref, wo_ref, o_ref):
    first = (pl.program_id(1) == 0).astype(jnp.int32)
    vt = jnp.concatenate([vtp_ref[...], vtc_ref[...]], axis=1)
    att_t = []
    for hk in range(N_KV_HEADS):
        keys = jnp.concatenate([kp_ref[hk], kc_ref[hk]], axis=0)
        q_t = jnp.concatenate([qt_ref[(hk * GQA_GROUP + g) * HEAD_DIM:(hk * GQA_GROUP + g + 1) * HEAD_DIM, :]
                               for g in range(GQA_GROUP)], axis=1)
        s = jnp.dot(keys, q_t, preferred_element_type=F32) + bias_ref[first, hk]
        sink = sink_ref[hk]
        m = jnp.maximum(jnp.max(s, axis=0, keepdims=True), sink)
        p = jnp.exp2(s - m).astype(BF16)
        pv = jnp.dot(vt[hk * V_EXT:(hk + 1) * V_EXT, :], p, preferred_element_type=F32)
        den = pv[HEAD_DIM:HEAD_DIM + 1, :] + jnp.exp2(sink - m)
        o_t = (pv[:HEAD_DIM, :] * (1.0 / den)).astype(BF16)
        att_t += [o_t[:, g * ATT_BLOCK:(g + 1) * ATT_BLOCK] for g in range(GQA_GROUP)]
    att_t = jnp.concatenate(att_t, axis=0)
    mix = lax.dot_general(att_t, wo_ref[...], (((0,), (0,)), ((), ())), preferred_element_type=F32)
    o_ref[...] = x_ref[...] + mix


def _attn(qt, k, vt, x2, sinks, wo, bsz, seqlen):
    blk = ATT_BLOCK
    nb = seqlen // blk
    qw = N_Q_HEADS * HEAD_DIM
    cols = GQA_GROUP * blk
    cur = lambda b, n: (b * nb + n, 0)
    cur_t = lambda b, n: (0, b * nb + n)
    prev_t = lambda b, n: (0, b * nb + jnp.maximum(n - 1, 0))
    sink_row = jnp.repeat(sinks.astype(F32) * LOG2E, blk).reshape(N_KV_HEADS, 1, cols)
    return pl.pallas_call(
        _attn_kernel,
        out_shape=jax.ShapeDtypeStruct((bsz * seqlen, D_MODEL), F32),
        grid=(bsz, nb),
        in_specs=[pl.BlockSpec((qw, blk), cur_t),
                  pl.BlockSpec((N_KV_HEADS, blk, HEAD_DIM), lambda b, n: (0, b * nb + jnp.maximum(n - 1, 0), 0)),
                  pl.BlockSpec((N_KV_HEADS, blk, HEAD_DIM), lambda b, n: (0, b * nb + n, 0)),
                  pl.BlockSpec((N_KV_HEADS * V_EXT, blk), prev_t),
                  pl.BlockSpec((N_KV_HEADS * V_EXT, blk), cur_t),
                  pl.BlockSpec((2, N_KV_HEADS, 2 * blk, cols), lambda b, n: (0, 0, 0, 0)),
                  pl.BlockSpec((N_KV_HEADS, 1, cols), lambda b, n: (0, 0, 0)),
                  pl.BlockSpec((blk, D_MODEL), cur),
                  pl.BlockSpec((qw, D_MODEL), lambda b, n: (0, 0))],
        out_specs=pl.BlockSpec((blk, D_MODEL), cur),
        compiler_params=_params("parallel", "parallel"),
        name="odd_attn",
    )(qt, k, k, vt, vt, jnp.asarray(_attn_bias()), sink_row, x2, wo.astype(BF16))


def kernel(x, even_mix_norm, even_in_proj, s5_lambda_re, s5_lambda_im, s5_log_step, s5_b_re, s5_b_im,
           s5_c_re, s5_c_im, s5_d, s5_glu_w, hgrn_lower_bounds, hgrn_o_norm, even_out_proj, odd_mix_norm,
           odd_wqkv, odd_q_norm, odd_k_norm, odd_sinks, odd_out_proj, moe_norm, moe_router_group,
           moe_router_group_bias, moe_router_expert, moe_router_expert_bias, moe_w_gate, moe_w_up,
           moe_w_down):
    bsz, seqlen, dm = x.shape
    n = bsz * seqlen
    x2 = x.reshape(n, dm)
    lower_bounds = jnp.cumsum(jax.nn.softmax(hgrn_lower_bounds.astype(F32), axis=0), axis=0)

    def router_params(layer):
        return (moe_norm[layer], moe_router_group[layer], moe_router_group_bias[layer],
                moe_router_expert[layer], moe_router_expert_bias[layer])

    def moe(xx, routing, layer):
        return _moe(xx, routing, moe_norm[layer], moe_w_gate[layer], moe_w_up[layer], moe_w_down[layer])

    u, h4 = _inproj(x2, even_mix_norm[0], even_in_proj[0].astype(BF16))
    ops = _s5_operators(s5_lambda_re[0], s5_lambda_im[0], s5_log_step[0], s5_b_re[0], s5_b_im[0],
                        s5_c_re[0], s5_c_im[0])
    u_g = _s5_pack(u.reshape(bsz, seqlen, S5_WIDTH))
    y_g = _s5_scan(u_g, ops, bsz)
    ys = _s5_unpack(y_g, bsz, seqlen).reshape(n, S5_WIDTH)
    b_out = _hgrn(h4.reshape(bsz, seqlen, 4 * HG_WIDTH), lower_bounds[0], hgrn_o_norm[0], bsz, seqlen)
    x2, *routing = _evenout(x2, ys, u, b_out.reshape(n, HG_WIDTH), s5_d[0], s5_glu_w[0], even_out_proj[0],
                            router_params(0))
    x2 = moe(x2, routing, 0)

    q, kt, v = _qkv(x2, odd_mix_norm[0], odd_wqkv[0], odd_q_norm[0], odd_k_norm[0])
    x2 = _attn(q, kt, v, x2, odd_sinks[0], odd_out_proj[0], bsz, seqlen)
    x2 = moe(x2, _router(x2, router_params(1)), 1)
    return x2.reshape(bsz, seqlen, dm)
```

```python
import functools
import math

import jax
import jax.numpy as jnp
import numpy as np
from jax import lax
from jax.experimental import pallas as pl
from jax.experimental.pallas import tpu as pltpu

F32 = jnp.float32
BF16 = jnp.bfloat16
EPS = 1e-6

D_MODEL = 1024
S5_WIDTH = 512
S5_GROUP = 16
S5_GROUPS = 32
S5_STATE = 64
S5_CHUNK = 16
HG_WIDTH = 512
HG_HEAD_DIM = 128
HG_HEADS = 4
HG_CHUNK = 32
HEAD_DIM = 64
N_Q_HEADS = 16
N_KV_HEADS = 2
GQA_GROUP = 8
KV_WIDTH = N_KV_HEADS * HEAD_DIM
ATT_BLOCK = 128
N_GROUPS = 4
EXPERTS_PER_GROUP = 4
N_EXPERTS = 16
D_EXPERT = 256
ROUTER_ROWS = 32
N_PAIRS = 6
N_BUCKETS = N_GROUPS * N_PAIRS
BUCKET_ROWS = 32
MOE_TILE = 512
ROW_TILE = (8, 128)
ROW_TILE_SHIFT = ROW_TILE[0].bit_length() - 1
X_TILES = D_MODEL // ROW_TILE[1]
XS_TILES = X_TILES + 1

VMEM_LIMIT_BYTES = 56 * 1024 * 1024


def _params(*semantics):
    return pltpu.CompilerParams(dimension_semantics=semantics, vmem_limit_bytes=VMEM_LIMIT_BYTES)


def _rms(xf, gain):
    return xf * lax.rsqrt(jnp.mean(xf * xf, axis=-1, keepdims=True) + EPS) * gain


def _nt_dot(w_t, h):
    return lax.dot_general(w_t, h, (((1,), (1,)), ((), ())), preferred_element_type=F32)


def _sigmoid(x):
    return 0.5 * jnp.tanh(0.5 * x) + 0.5


def _silu(x):
    return x * _sigmoid(x)


def _inproj_kernel(x_ref, g_ref, w_ref, u_ref, h4_ref):
    h = _rms(x_ref[...], g_ref[...]).astype(BF16)
    p = jnp.dot(h, w_ref[...], preferred_element_type=F32)
    u_ref[...] = p[:, :S5_WIDTH]
    h4_ref[...] = p[:, S5_WIDTH:]


def _inproj(x2, gain, w_bf16, tm=512):
    n = x2.shape[0]
    e_in = w_bf16.shape[1]
    return pl.pallas_call(
        _inproj_kernel,
        out_shape=(jax.ShapeDtypeStruct((n, S5_WIDTH), F32),
                   jax.ShapeDtypeStruct((n, e_in - S5_WIDTH), F32)),
        grid=(n // tm,),
        in_specs=[pl.BlockSpec((tm, D_MODEL), lambda i: (i, 0)),
                  pl.BlockSpec((1, D_MODEL), lambda i: (0, 0)),
                  pl.BlockSpec((D_MODEL, e_in), lambda i: (0, 0))],
        out_specs=(pl.BlockSpec((tm, S5_WIDTH), lambda i: (i, 0)),
                   pl.BlockSpec((tm, e_in - S5_WIDTH), lambda i: (i, 0))),
        compiler_params=_params("parallel"),
        name="even_inproj",
    )(x2, gain.reshape(1, D_MODEL), w_bf16)


def _s5_toeplitz_kernel(ca_ref, bbt_ref, mt_ref):
    kt = lax.dot_general(bbt_ref[0], ca_ref[0], (((1,), (1,)), ((), ())), preferred_element_type=F32,
                         precision=lax.Precision.HIGHEST)
    width = kt.shape[1]
    blocks = [kt] + [jnp.concatenate([jnp.zeros((S5_GROUP, s * S5_GROUP), F32), kt[:, :width - s * S5_GROUP]],
                                     axis=1) for s in range(1, S5_CHUNK)]
    mt_ref[0] = jnp.concatenate(blocks, axis=0).astype(BF16)


def _s5_toeplitz(ca, bbt):
    g, rows, k = ca.shape
    return pl.pallas_call(
        _s5_toeplitz_kernel,
        out_shape=jax.ShapeDtypeStruct((g, rows, rows), BF16),
        grid=(g,),
        in_specs=[pl.BlockSpec((1, rows, k), lambda i: (i, 0, 0)),
                  pl.BlockSpec((1, S5_GROUP, k), lambda i: (i, 0, 0))],
        out_specs=pl.BlockSpec((1, rows, rows), lambda i: (i, 0, 0)),
        compiler_params=_params("parallel"),
        name="s5_toeplitz",
    )(ca, bbt)


def _s5_operators(lam_re, lam_im, log_step, b_re, b_im, c_re, c_im):
    t = S5_CHUNK
    lr, li = lam_re.astype(F32), lam_im.astype(F32)
    step = jnp.exp(log_step.astype(F32))[:, None]
    mag = jnp.exp(lr * step)
    ab_re = mag * jnp.cos(li * step)
    ab_im = mag * jnp.sin(li * step)
    den = lr * lr + li * li
    nr, ni = ab_re - 1.0, ab_im
    z_re = (nr * lr + ni * li) / den
    z_im = (ni * lr - nr * li) / den
    br, bi = b_re.astype(F32), b_im.astype(F32)
    bb_re = z_re[..., None] * br - z_im[..., None] * bi
    bb_im = z_re[..., None] * bi + z_im[..., None] * br
    kk = jnp.arange(t + 1, dtype=F32)[:, None, None]
    pmag = jnp.exp(kk * (lr * step)[None])
    pw_re = pmag * jnp.cos(kk * (li * step)[None])
    pw_im = pmag * jnp.sin(kk * (li * step)[None])
    cr = jnp.transpose(c_re.astype(F32), (0, 1, 2))
    ci = c_im.astype(F32)
    ca_re = cr[None] * pw_re[:, :, None, :] - ci[None] * pw_im[:, :, None, :]
    ca_im = cr[None] * pw_im[:, :, None, :] + ci[None] * pw_re[:, :, None, :]
    g = lr.shape[0]
    ca_cat = jnp.concatenate([ca_re[:t], -ca_im[:t]], axis=-1)
    ca_cat = jnp.transpose(ca_cat, (1, 0, 2, 3)).reshape(g, t * S5_GROUP, 2 * S5_STATE)
    bb_cat_t = jnp.concatenate([bb_re, bb_im], axis=1).transpose(0, 2, 1)
    mt = _s5_toeplitz(ca_cat, bb_cat_t)
    pr = pw_re[:t][::-1]
    pi = pw_im[:t][::-1]
    sb_re = pr[:, :, :, None] * bb_re[None] - pi[:, :, :, None] * bb_im[None]
    sb_im = pr[:, :, :, None] * bb_im[None] + pi[:, :, :, None] * bb_re[None]
    sb_re = jnp.transpose(sb_re, (1, 0, 3, 2)).reshape(g, t * S5_GROUP, S5_STATE)
    sb_im = jnp.transpose(sb_im, (1, 0, 3, 2)).reshape(g, t * S5_GROUP, S5_STATE)
    cp_re = jnp.transpose(ca_re[1:], (1, 3, 0, 2)).reshape(g, S5_STATE, t * S5_GROUP)
    cp_im = jnp.transpose(-ca_im[1:], (1, 3, 0, 2)).reshape(g, S5_STATE, t * S5_GROUP)

    def pair_rows(m):
        m = m.reshape(g // 2, 2, m.shape[1], m.shape[2])
        z = jnp.zeros_like(m[:, 0])
        top = jnp.concatenate([m[:, 0], z], axis=2)
        bot = jnp.concatenate([z, m[:, 1]], axis=2)
        return jnp.concatenate([top, bot], axis=1)

    at_re = pw_re[t].reshape(g // 2, 1, 2 * S5_STATE)
    at_im = pw_im[t].reshape(g // 2, 1, 2 * S5_STATE)
    return (mt, pair_rows(sb_re).astype(BF16), pair_rows(sb_im).astype(BF16),
            pair_rows(cp_re).astype(BF16), pair_rows(cp_im).astype(BF16), at_re, at_im)


PACK_TOKENS = 512


LANES = 128
GROUPS_PER_TILE = LANES // S5_GROUP
TOKENS_PER_TILE = LANES // S5_GROUP
CHUNK_HALVES = S5_CHUNK // TOKENS_PER_TILE
PACK_CHUNKS = PACK_TOKENS // S5_CHUNK


def _block_swap_matrix():
    a, b, h = np.meshgrid(np.arange(TOKENS_PER_TILE), np.arange(GROUPS_PER_TILE), np.arange(S5_GROUP),
                          indexing="ij")
    src = (a * GROUPS_PER_TILE + b) * S5_GROUP + h
    dst = (b * TOKENS_PER_TILE + a) * S5_GROUP + h
    m = np.zeros((src.size, src.size), np.float32)
    m[src.ravel(), dst.ravel()] = 1.0
    return jnp.asarray(m, dtype=BF16)


def _s5_pack_kernel(u_ref, swap_ref, o_ref, *, bsz):
    for j in range(CHUNK_HALVES):
        rows = [jnp.concatenate([u_ref[b, pl.ds(j * TOKENS_PER_TILE + tt, PACK_CHUNKS, stride=S5_CHUNK), :]
                                 for tt in range(TOKENS_PER_TILE)], axis=1) for b in range(bsz)]
        lhs = jnp.concatenate(rows, axis=0).astype(BF16)
        out = jnp.dot(lhs, swap_ref[...], preferred_element_type=F32)
        for g in range(GROUPS_PER_TILE):
            for b in range(bsz):
                o_ref[g * CHUNK_HALVES + j, pl.ds(b, PACK_CHUNKS, stride=bsz), :] = (
                    out[b * PACK_CHUNKS:(b + 1) * PACK_CHUNKS, g * LANES:(g + 1) * LANES])


def _s5_pack(u3):
    bsz, seqlen, w = u3.shape
    rows = PACK_CHUNKS * bsz
    swap = _block_swap_matrix()
    return pl.pallas_call(
        functools.partial(_s5_pack_kernel, bsz=bsz),
        out_shape=jax.ShapeDtypeStruct((S5_GROUPS * CHUNK_HALVES, seqlen // S5_CHUNK * bsz, LANES), F32),
        grid=(seqlen // PACK_TOKENS, w // LANES),
        in_specs=[pl.BlockSpec((bsz, PACK_TOKENS, LANES), lambda i, k: (0, i, k)),
                  pl.BlockSpec(swap.shape, lambda i, k: (0, 0))],
        out_specs=pl.BlockSpec((GROUPS_PER_TILE * CHUNK_HALVES, rows, LANES), lambda i, k: (k, i, 0)),
        compiler_params=_params("parallel", "parallel"),
        name="s5_pack",
    )(u3, swap)


def _s5_unpack_kernel(y_ref, swap_ref, o_ref, *, bsz):
    for j in range(CHUNK_HALVES):
        rows = [jnp.concatenate([y_ref[g * CHUNK_HALVES + j, pl.ds(b, PACK_CHUNKS, stride=bsz), :]
                                 for g in range(GROUPS_PER_TILE)], axis=1) for b in range(bsz)]
        lhs = jnp.concatenate(rows, axis=0).astype(BF16)
        out = jnp.dot(lhs, swap_ref[...], preferred_element_type=F32)
        for tt in range(TOKENS_PER_TILE):
            for b in range(bsz):
                o_ref[b, pl.ds(j * TOKENS_PER_TILE + tt, PACK_CHUNKS, stride=S5_CHUNK), :] = (
                    out[b * PACK_CHUNKS:(b + 1) * PACK_CHUNKS, tt * LANES:(tt + 1) * LANES])


def _s5_unpack(y_g, bsz, seqlen):
    rows = PACK_CHUNKS * bsz
    swap = _block_swap_matrix()
    return pl.pallas_call(
        functools.partial(_s5_unpack_kernel, bsz=bsz),
        out_shape=jax.ShapeDtypeStruct((bsz, seqlen, S5_WIDTH), F32),
        grid=(seqlen // PACK_TOKENS, S5_WIDTH // LANES),
        in_specs=[pl.BlockSpec((GROUPS_PER_TILE * CHUNK_HALVES, rows, LANES), lambda i, k: (k, i, 0)),
                  pl.BlockSpec(swap.shape, lambda i, k: (0, 0))],
        out_specs=pl.BlockSpec((bsz, PACK_TOKENS, LANES), lambda i, k: (0, i, k)),
        compiler_params=_params("parallel", "parallel"),
        name="s5_unpack",
    )(y_g, swap)


def _s5_kernel(u_ref, mt_ref, wre_ref, wim_ref, cre_ref, cim_ref, atr_ref, ati_ref, y_ref,
               sre_ref, sim_ref, xre_ref, xim_ref, *, n_chunks, bsz):
    ucat = jnp.concatenate([u_ref[i] for i in range(2 * CHUNK_HALVES)], axis=1).astype(BF16)
    w = S5_CHUNK * S5_GROUP
    u0 = ucat[:, :w]
    u1 = ucat[:, w:]
    sre_ref[...] = jnp.dot(ucat, wre_ref[0], preferred_element_type=F32)
    sim_ref[...] = jnp.dot(ucat, wim_ref[0], preferred_element_type=F32)
    atr = jnp.broadcast_to(atr_ref[0], (bsz, 2 * S5_STATE))
    ati = jnp.broadcast_to(ati_ref[0], (bsz, 2 * S5_STATE))

    def body(c, carry):
        xr, xi = carry
        rows = pl.ds(pl.multiple_of(c * bsz, bsz), bsz)
        xre_ref[rows, :] = xr
        xim_ref[rows, :] = xi
        nxr = atr * xr - ati * xi + sre_ref[rows, :]
        nxi = atr * xi + ati * xr + sim_ref[rows, :]
        return nxr, nxi

    zero = jnp.zeros((bsz, 2 * S5_STATE), F32)
    lax.fori_loop(0, n_chunks, body, (zero, zero))
    ycar = (jnp.dot(xre_ref[...].astype(BF16), cre_ref[0], preferred_element_type=F32)
            + jnp.dot(xim_ref[...].astype(BF16), cim_ref[0], preferred_element_type=F32))
    y0 = jnp.dot(u0, mt_ref[0], preferred_element_type=F32) + ycar[:, :w]
    y1 = jnp.dot(u1, mt_ref[1], preferred_element_type=F32) + ycar[:, w:]
    for i in range(CHUNK_HALVES):
        y_ref[i] = y0[:, i * LANES:(i + 1) * LANES]
        y_ref[CHUNK_HALVES + i] = y1[:, i * LANES:(i + 1) * LANES]


def _s5_scan(u_g, ops, bsz):
    mt, wre, wim, cre, cim, atr, ati = ops
    tiles, r, _ = u_g.shape
    g = tiles // CHUNK_HALVES
    w = S5_CHUNK * S5_GROUP
    n_chunks = r // bsz
    p2 = 2 * S5_STATE
    kern = functools.partial(_s5_kernel, n_chunks=n_chunks, bsz=bsz)
    pair_tiles = pl.BlockSpec((2 * CHUNK_HALVES, r, LANES), lambda i: (i, 0, 0))
    return pl.pallas_call(
        kern,
        out_shape=jax.ShapeDtypeStruct((tiles, r, LANES), F32),
        grid=(g // 2,),
        in_specs=[pair_tiles,
                  pl.BlockSpec((2, w, w), lambda i: (i, 0, 0)),
                  pl.BlockSpec((1, 2 * w, p2), lambda i: (i, 0, 0)),
                  pl.BlockSpec((1, 2 * w, p2), lambda i: (i, 0, 0)),
                  pl.BlockSpec((1, p2, 2 * w), lambda i: (i, 0, 0)),
                  pl.BlockSpec((1, p2, 2 * w), lambda i: (i, 0, 0)),
                  pl.BlockSpec((1, 1, p2), lambda i: (i, 0, 0)),
                  pl.BlockSpec((1, 1, p2), lambda i: (i, 0, 0))],
        out_specs=pair_tiles,
        scratch_shapes=[pltpu.VMEM((r, p2), F32)] * 4,
        compiler_params=_params("parallel"),
        name="s5_scan",
    )(u_g, mt, wre, wim, cre, cim, atr, ati)


def _hgrn_kernel(q_ref, f_ref, i_ref, g_ref, lb_ref, og_ref, o_ref, st_ref, *, seqlen):
    c = HG_CHUNK
    nc = seqlen // c
    d = HG_HEAD_DIM
    lb = lb_ref[...]
    q = q_ref[0]
    qs = _silu(q)
    f = lb + (1.0 - lb) * _sigmoid(f_ref[0])
    lf = jnp.log(f)
    k = 1.0 - f
    v = i_ref[0]
    sub = ROW_TILE[0]
    b8 = lf.reshape(seqlen // sub, sub, d)
    row = lax.broadcasted_iota(jnp.int32, b8.shape, 1)
    sh = 1
    while sh < sub:
        b8 = b8 + jnp.where(row >= sh, pltpu.roll(b8, sh, axis=1), 0.0)
        sh *= 2
    b4 = b8.reshape(nc, c // sub, sub, d)
    groups, run = [], None
    for gi in range(c // sub):
        grp = b4[:, gi]
        groups.append(grp if run is None else grp + run)
        total = grp[:, sub - 1:sub, :]
        run = total if run is None else run + total
    b3 = jnp.concatenate(groups, axis=1)
    b_last = b3[:, c - 1:c, :]
    b_ref = b3[:, c // 2 - 1:c // 2, :]
    qs3 = qs.reshape(nc, c, d)
    k3 = k.reshape(nc, c, d)
    v3 = v.reshape(nc, c, d).astype(BF16)
    qe_f = qs3 * jnp.exp(b3 - b_ref)
    ke_f = k3 * jnp.exp(b_ref - b3)
    qe = qe_f.astype(BF16)
    ke = ke_f.astype(BF16)
    kd = (ke_f * jnp.exp(b_last - b_ref)).astype(BF16)
    qb = (qe_f * jnp.exp(b_ref)).astype(BF16)
    scores = jnp.einsum('ctd,csd->cts', qe, ke, preferred_element_type=F32)
    ti = lax.broadcasted_iota(jnp.int32, (c, c), 0)
    si = lax.broadcasted_iota(jnp.int32, (c, c), 1)
    scores = jnp.where((ti >= si)[None], scores, 0.0)
    o_intra = jnp.einsum('cts,csv->ctv', scores.astype(BF16), v3, preferred_element_type=F32)
    ut = jnp.einsum('csv,csd->cvd', v3, kd, preferred_element_type=F32)
    decay = jnp.exp(b_last)
    state = jnp.zeros((d, d), F32)
    for ci in range(nc):
        st_ref[ci] = state.astype(BF16)
        state = decay[ci] * state + ut[ci]
    o_inter = jnp.einsum('ctd,cvd->ctv', qb, st_ref[...], preferred_element_type=F32)
    o = (o_intra + o_inter).reshape(seqlen, d)
    o = _rms(o, og_ref[...])
    o_ref[0] = (o * _silu(g_ref[0])).astype(BF16)


def _hgrn(h4, lower_bound, o_gain, bsz, seqlen):
    d = HG_HEAD_DIM
    kern = functools.partial(_hgrn_kernel, seqlen=seqlen)

    def col(part):
        return pl.BlockSpec((1, seqlen, d), lambda b, h: (b, 0, part * HG_HEADS + h))

    return pl.pallas_call(
        kern,
        out_shape=jax.ShapeDtypeStruct((bsz, seqlen, HG_WIDTH), BF16),
        grid=(bsz, HG_HEADS),
        in_specs=[col(0), col(1), col(2), col(3),
                  pl.BlockSpec((1, d), lambda b, h: (0, h)),
                  pl.BlockSpec((1, d), lambda b, h: (0, 0))],
        out_specs=pl.BlockSpec((1, seqlen, d), lambda b, h: (b, 0, h)),
        scratch_shapes=[pltpu.VMEM((seqlen // HG_CHUNK, d, d), BF16)],
        compiler_params=_params("parallel", "parallel"),
        name="hgrn2",
    )(h4, h4, h4, h4, lower_bound.reshape(1, HG_WIDTH), o_gain.reshape(1, d))


def _evenout_kernel(x_ref, ys_ref, u_ref, b_ref, d_ref, wglu_ref, wa_ref, wb_ref, *rest):
    route_in, (o_ref, *route_out), route_scratch = rest[:4], rest[4:8], rest[8:]
    y = ys_ref[...] + d_ref[...] * u_ref[...]
    y = jax.nn.gelu(y)
    gate = _sigmoid(jnp.dot(y.astype(BF16), wglu_ref[...], preferred_element_type=F32))
    a = (y * gate).astype(BF16)
    mix = (jnp.dot(a, wa_ref[...], preferred_element_type=F32)
           + jnp.dot(b_ref[...], wb_ref[...], preferred_element_type=F32))
    x_new = x_ref[...] + mix
    o_ref[...] = x_new
    _route(x_new, pl.program_id(0) == 0, *route_in, *route_out, *route_scratch)


def _evenout(x2, ys, u, b_out, d_skip, wglu, wout, router_params, tm=512):
    n = x2.shape[0]
    route = _RouterPlumbing(n, tm, lambda i: i, *router_params)
    row = lambda w: pl.BlockSpec((tm, w), lambda i: (i, 0))
    full = lambda r, c: pl.BlockSpec((r, c), lambda i: (0, 0))
    return pl.pallas_call(
        _evenout_kernel,
        out_shape=(jax.ShapeDtypeStruct((n, D_MODEL), F32), *route.out_shape),
        grid=(n // tm,),
        in_specs=[row(D_MODEL), row(S5_WIDTH), row(S5_WIDTH), row(HG_WIDTH),
                  full(1, S5_WIDTH), full(S5_WIDTH, S5_WIDTH),
                  full(S5_WIDTH, D_MODEL), full(HG_WIDTH, D_MODEL), *route.in_specs],
        out_specs=(row(D_MODEL), *route.out_specs),
        scratch_shapes=route.scratch_shapes,
        compiler_params=_params("arbitrary"),
        name="even_out",
    )(x2, ys, u, b_out, d_skip.reshape(1, S5_WIDTH), wglu.astype(BF16),
      wout[:S5_WIDTH].astype(BF16), wout[S5_WIDTH:].astype(BF16), *route.operands)


def _route(x, is_first_step, g_ref, wr_ref, br_ref, tri_ref, idx_ref, wts_ref, cnt_ref, run_ref):
    @pl.when(is_first_step)
    def _():
        run_ref[...] = jnp.zeros_like(run_ref)

    h = _rms(x, g_ref[...])
    h_hi = h.astype(BF16)
    h_lo = (h - h_hi.astype(F32)).astype(BF16)
    both = _nt_dot(wr_ref[...], h_hi)
    lt = (both[:ROUTER_ROWS] + both[ROUTER_ROWS:] + _nt_dot(wr_ref[:ROUTER_ROWS, :], h_lo)
          + br_ref[...])
    gl = [lt[i:i + 1] for i in range(N_GROUPS)]
    el = [lt[N_GROUPS + i:N_GROUPS + i + 1] for i in range(N_EXPERTS)]
    gmax = jnp.maximum(jnp.maximum(gl[0], gl[1]), jnp.maximum(gl[2], gl[3]))
    g_idx = jnp.where(gl[0] == gmax, 0, jnp.where(gl[1] == gmax, 1, jnp.where(gl[2] == gmax, 2, 3)))
    g_gate = 1.0 / (jnp.exp(gl[0] - gmax) + jnp.exp(gl[1] - gmax) + jnp.exp(gl[2] - gmax) + jnp.exp(gl[3] - gmax))
    es = []
    for j in range(EXPERTS_PER_GROUP):
        es.append(jnp.where(g_idx == 0, el[j],
                            jnp.where(g_idx == 1, el[4 + j],
                                      jnp.where(g_idx == 2, el[8 + j], el[12 + j]))))
    e1 = jnp.maximum(jnp.maximum(es[0], es[1]), jnp.maximum(es[2], es[3]))
    i1 = jnp.where(es[0] == e1, 0, jnp.where(es[1] == e1, 1, jnp.where(es[2] == e1, 2, 3)))
    rest = [jnp.where(i1 == j, -jnp.inf, es[j]) for j in range(EXPERTS_PER_GROUP)]
    e2 = jnp.maximum(jnp.maximum(rest[0], rest[1]), jnp.maximum(rest[2], rest[3]))
    i2 = jnp.where(rest[0] == e2, 0, jnp.where(rest[1] == e2, 1, jnp.where(rest[2] == e2, 2, 3)))
    r = jnp.exp(e2 - e1)
    w1 = g_gate / (1.0 + r)
    w2 = w1 * r
    first_lo = i1 < i2
    lo = jnp.where(first_lo, i1, i2)
    hi = jnp.where(first_lo, i2, i1)
    w_lo = jnp.where(first_lo, w1, w2)
    w_hi = jnp.where(first_lo, w2, w1)
    pair = jnp.where(lo == 0, 0, jnp.where(lo == 1, 3, 5)) + hi - lo - 1
    bucket = g_idx * N_PAIRS + pair
    tm = bucket.shape[1]
    rowid = lax.broadcasted_iota(jnp.int32, (BUCKET_ROWS, tm), 0)
    onehot = (rowid == bucket).astype(F32)
    prefix = jnp.dot(onehot.astype(BF16), tri_ref[...], preferred_element_type=F32)
    run = run_ref[...]
    rank = jnp.sum(onehot * (prefix + run), axis=0, keepdims=True)
    run = run + jnp.sum(onehot, axis=1, keepdims=True)
    run_ref[...] = run
    cnt_ref[...] = jnp.broadcast_to(run, cnt_ref.shape)
    idx_ref[...] = jnp.concatenate([bucket, rank.astype(jnp.int32), jnp.zeros((6, tm), jnp.int32)], axis=0)
    wts_ref[...] = jnp.concatenate([w_lo, w_hi, jnp.zeros((6, tm), F32)], axis=0)


class _RouterPlumbing:
    def __init__(self, n, tm, tile_index, gain, w_rg, b_rg, w_re, b_re):
        wr = jnp.concatenate([w_rg, w_re], axis=1).astype(F32).T
        wr = jnp.pad(wr, ((0, ROUTER_ROWS - wr.shape[0]), (0, 0)))
        wr_hi = wr.astype(BF16)
        wr = jnp.concatenate([wr_hi, (wr - wr_hi.astype(F32)).astype(BF16)], axis=0)
        br = jnp.pad(jnp.concatenate([b_rg, b_re]).astype(F32), (0, ROUTER_ROWS - N_GROUPS - N_EXPERTS))
        tri = (np.arange(tm)[:, None] < np.arange(tm)[None, :]).astype(np.float32)
        const = lambda *_: (0, 0)
        self.operands = (gain.reshape(1, D_MODEL), wr, br.reshape(ROUTER_ROWS, 1), jnp.asarray(tri, dtype=BF16))
        self.in_specs = [pl.BlockSpec((1, D_MODEL), const),
                         pl.BlockSpec((2 * ROUTER_ROWS, D_MODEL), const),
                         pl.BlockSpec((ROUTER_ROWS, 1), const),
                         pl.BlockSpec((tm, tm), const)]
        self.out_shape = (jax.ShapeDtypeStruct((8, n), jnp.int32),
                          jax.ShapeDtypeStruct((8, n), F32),
                          jax.ShapeDtypeStruct((BUCKET_ROWS, LANES), F32))
        self.out_specs = (pl.BlockSpec((8, tm), lambda *g: (0, tile_index(*g))),
                          pl.BlockSpec((8, tm), lambda *g: (0, tile_index(*g))),
                          pl.BlockSpec((BUCKET_ROWS, LANES), const))
        self.scratch_shapes = [pltpu.VMEM((BUCKET_ROWS, 1), F32)]


def _router_kernel(x_ref, *route_refs):
    _route(x_ref[...], pl.program_id(0) == 0, *route_refs)


def _router(x2, router_params, tm=512):
    n = x2.shape[0]
    route = _RouterPlumbing(n, tm, lambda i: i, *router_params)
    return pl.pallas_call(
        _router_kernel,
        out_shape=route.out_shape,
        grid=(n // tm,),
        in_specs=[pl.BlockSpec((tm, D_MODEL), lambda i: (i, 0)), *route.in_specs],
        out_specs=route.out_specs,
        scratch_shapes=route.scratch_shapes,
        compiler_params=_params("arbitrary"),
        name="moe_router",
    )(x2, *route.operands)


ROW_COPY_UNROLL = 8


def _start_row_copies(idx_ref, n_rows, copy_for_row, prepare_rows=None):
    def start_group(base):
        for j in range(ROW_COPY_UNROLL):
            copy_for_row(base + j, idx_ref[0, 0, base + j]).start(priority=j % 2)

    n_groups = n_rows // ROW_COPY_UNROLL
    if prepare_rows is None:
        def body(g, carry):
            start_group(pl.multiple_of(g * ROW_COPY_UNROLL, ROW_COPY_UNROLL))
            return carry

        lax.fori_loop(0, n_groups, body, 0)
        return

    prepare_rows(0)

    def body(g, carry):
        base = pl.multiple_of(g * ROW_COPY_UNROLL, ROW_COPY_UNROLL)
        prepare_rows(base + ROW_COPY_UNROLL)
        start_group(base)
        return carry

    lax.fori_loop(0, n_groups - 1, body, 0)
    start_group(n_rows - ROW_COPY_UNROLL)


def _row_slab(view_ref, p):
    return view_ref.at[p >> ROW_TILE_SHIFT, :, p & (ROW_TILE[0] - 1)]


def _view_columns(view_ref, n_cols):
    rows = view_ref.shape[0] * ROW_TILE[0]
    return jnp.concatenate([view_ref[:, c].reshape(rows, LANES) for c in range(n_cols)], axis=1)


def _rows_to_tiles(x):
    rows = x.shape[0]
    return x.reshape(rows * ROW_TILE[0], ROW_TILE[1]).reshape(rows, *ROW_TILE)


def _tiles_to_rows(x3):
    rows = x3.shape[0]
    return x3.reshape(rows * ROW_TILE[0], ROW_TILE[1]).reshape(rows, D_MODEL)


def _dispatch_kernel(tail_blk_ref, tail_on_ref, pos_ref, x_ref, w_ref, xs_ref, buf_ref, zero_ref, wcol_ref, sem,
                     *, tile):
    tm = x_ref.shape[0]
    tile_blks = tile // ROW_TILE[0]

    @pl.when(pl.program_id(0) == 0)
    def _():
        zero_ref[...] = jnp.zeros_like(zero_ref)

        def zero_copy(k):
            blk = pl.multiple_of(tail_blk_ref[k], tile_blks)
            return pltpu.make_async_copy(zero_ref, xs_ref.at[pl.ds(blk, tile_blks)], sem)

        for k in range(2 * N_BUCKETS):
            pl.when(tail_on_ref[k] > 0)(lambda k=k: zero_copy(k).start())
        for k in range(2 * N_BUCKETS):
            pl.when(tail_on_ref[k] > 0)(lambda k=k: zero_copy(k).wait())

    wpad = jnp.concatenate([w_ref[...], jnp.zeros((LANES - w_ref.shape[0], tm), F32)], axis=0)
    wcol_ref[...] = wpad.T

    def stage(base):
        rows = pl.ds(base, ROW_COPY_UNROLL)
        buf_ref[rows, :X_TILES, :] = _rows_to_tiles(x_ref[rows, :])
        buf_ref[rows, X_TILES, :] = wcol_ref[rows, :]

    _start_row_copies(pos_ref, tm, lambda r, p: pltpu.make_async_copy(buf_ref.at[r], _row_slab(xs_ref, p), sem),
                      prepare_rows=stage)
    done = xs_ref.at[pl.ds(0, tm // ROW_TILE[0])]
    pltpu.make_async_copy(done, done, sem).wait()


def _dispatch(x2, wts, pos3, tails, n_rows_sorted, tile, tm):
    n = x2.shape[0]
    tail_blk, tail_on = tails
    grid_spec = pltpu.PrefetchScalarGridSpec(
        num_scalar_prefetch=2,
        grid=(n // tm,),
        in_specs=[pl.BlockSpec((1, 1, tm), lambda i, *_: (i, 0, 0), memory_space=pltpu.SMEM),
                  pl.BlockSpec((tm, D_MODEL), lambda i, *_: (i, 0)),
                  pl.BlockSpec((8, tm), lambda i, *_: (0, i))],
        out_specs=pl.BlockSpec(memory_space=pl.ANY),
        scratch_shapes=[pltpu.VMEM((tm, XS_TILES, LANES), F32),
                        pltpu.VMEM((tile // ROW_TILE[0], XS_TILES, *ROW_TILE), F32),
                        pltpu.VMEM((tm, LANES), F32),
                        pltpu.SemaphoreType.DMA],
    )
    return pl.pallas_call(
        functools.partial(_dispatch_kernel, tile=tile),
        out_shape=jax.ShapeDtypeStruct((n_rows_sorted // ROW_TILE[0], XS_TILES, *ROW_TILE), F32),
        grid_spec=grid_spec,
        compiler_params=_params("arbitrary"),
        name="moe_dispatch",
    )(tail_blk, tail_on, pos3, x2, wts)


def _experts_kernel(elo_ref, ehi_ref, nvalid_ref, xs_ref, g_ref, wg_lo, wu_lo, wg_hi, wu_hi,
                    wd_lo, wd_hi, o_ref):
    del elo_ref, ehi_ref
    t = pl.program_id(0)

    @pl.when(t < nvalid_ref[0])
    def _():
        rows = xs_ref.shape[0] * ROW_TILE[0]
        xt = _view_columns(xs_ref, X_TILES)
        h = _rms(xt, g_ref[...]).astype(BF16)
        extra = xs_ref[:, X_TILES].reshape(rows, LANES)
        w_lo = extra[:, 0:1]
        w_hi = extra[:, 1:2]

        def expert(wg, wu, wd, w):
            gate = jnp.dot(h, wg[0], preferred_element_type=F32)
            up = jnp.dot(h, wu[0], preferred_element_type=F32)
            hid = (_silu(gate) * up * w).astype(BF16)
            return jnp.dot(hid, wd[0], preferred_element_type=F32)

        out = xt + expert(wg_lo, wu_lo, wd_lo, w_lo) + expert(wg_hi, wu_hi, wd_hi, w_hi)
        for c in range(X_TILES):
            o_ref[:, c] = out[:, c * LANES:(c + 1) * LANES].reshape(o_ref.shape[0], *ROW_TILE)

    @pl.when(t >= nvalid_ref[0])
    def _():
        o_ref[...] = jnp.zeros_like(o_ref)


def _experts(xs, gain, tables, wg, wu, wd, n_tiles, t):
    elo, ehi, nvalid = tables
    blks = t // ROW_TILE[0]
    row = lambda i, elo, ehi, nv: (i, 0, 0, 0)
    row_in = lambda i, elo, ehi, nv: (jnp.minimum(i, nv[0] - 1), 0, 0, 0)
    lo3 = lambda i, elo, ehi, nv: (elo[i], 0, 0)
    hi3 = lambda i, elo, ehi, nv: (ehi[i], 0, 0)
    grid_spec = pltpu.PrefetchScalarGridSpec(
        num_scalar_prefetch=3,
        grid=(n_tiles,),
        in_specs=[pl.BlockSpec((blks, XS_TILES, *ROW_TILE), row_in),
                  pl.BlockSpec((1, D_MODEL), lambda i, *_: (0, 0)),
                  pl.BlockSpec((1, D_MODEL, D_EXPERT), lo3),
                  pl.BlockSpec((1, D_MODEL, D_EXPERT), lo3),
                  pl.BlockSpec((1, D_MODEL, D_EXPERT), hi3),
                  pl.BlockSpec((1, D_MODEL, D_EXPERT), hi3),
                  pl.BlockSpec((1, D_EXPERT, D_MODEL), lo3),
                  pl.BlockSpec((1, D_EXPERT, D_MODEL), hi3)],
        out_specs=pl.BlockSpec((blks, X_TILES, *ROW_TILE), row),
    )
    return pl.pallas_call(
        _experts_kernel,
        out_shape=jax.ShapeDtypeStruct((xs.shape[0], X_TILES, *ROW_TILE), F32),
        grid_spec=grid_spec,
        compiler_params=_params("arbitrary"),
        name="moe_experts",
    )(elo, ehi, nvalid, xs, gain.reshape(1, D_MODEL), wg, wu, wg, wu, wd, wd)


def _combine_kernel(pos_ref, ys_ref, o_ref, buf_ref, sem):
    tm = o_ref.shape[0]
    _start_row_copies(pos_ref, tm, lambda r, p: pltpu.make_async_copy(_row_slab(ys_ref, p), buf_ref.at[r], sem))
    done = ys_ref.at[pl.ds(0, tm // ROW_TILE[0])]
    pltpu.make_async_copy(done, done, sem).wait()
    o_ref[...] = _tiles_to_rows(buf_ref[...])


def _combine(ys, pos3, n, tm):
    return pl.pallas_call(
        _combine_kernel,
        out_shape=jax.ShapeDtypeStruct((n, D_MODEL), F32),
        grid=(n // tm,),
        in_specs=[pl.BlockSpec((1, 1, tm), lambda i: (i, 0, 0), memory_space=pltpu.SMEM),
                  pl.BlockSpec(memory_space=pl.ANY)],
        out_specs=pl.BlockSpec((tm, D_MODEL), lambda i: (i, 0)),
        scratch_shapes=[pltpu.VMEM((tm, *ROW_TILE), F32), pltpu.SemaphoreType.DMA],
        compiler_params=_params("arbitrary"),
        name="moe_combine",
    )(pos3, ys)


def _moe_tables(idx, cnt, n_tiles, t):
    bucket, rank = idx[0], idx[1]
    counts = cnt[:N_BUCKETS, 0].astype(jnp.int32)
    tiles_b = (counts + t - 1) // t
    tile_end = jnp.cumsum(tiles_b)
    pos = (tile_end - tiles_b)[bucket] * t + rank
    total = tile_end[-1]
    tt = jnp.arange(n_tiles, dtype=jnp.int32)
    valid = tt < total
    tb = jnp.sum((tile_end[None, :] <= jnp.where(valid, tt, total - 1)[:, None]).astype(jnp.int32), axis=1)
    tb = jnp.minimum(tb, N_BUCKETS - 1)
    pair_lo = jnp.asarray([0, 0, 0, 1, 1, 2], jnp.int32)
    pair_hi = jnp.asarray([1, 2, 3, 2, 3, 3], jnp.int32)
    base = (tb // N_PAIRS) * EXPERTS_PER_GROUP
    idle = total + jnp.arange(N_BUCKETS, dtype=jnp.int32)
    idle_on = idle < n_tiles
    blks = t // ROW_TILE[0]
    tails = (jnp.concatenate([(tile_end - 1) * blks, jnp.where(idle_on, idle, 0) * blks]),
             jnp.concatenate([tiles_b > 0, idle_on]).astype(jnp.int32))
    return pos, tails, (base + pair_lo[tb % N_PAIRS], base + pair_hi[tb % N_PAIRS], total.reshape(1))


def _moe(x2, routing, gain, wg, wu, wd, t=MOE_TILE, tm=1024):
    n = x2.shape[0]
    idx, wts, cnt = routing
    n_tiles = n // t + N_BUCKETS
    pos, tails, tables = _moe_tables(idx, cnt, n_tiles, t)
    pos3 = pos.reshape(n // tm, 1, tm)
    xs = _dispatch(x2, wts, pos3, tails, n_tiles * t, t, tm)
    ys = _experts(xs, gain, tables, wg.astype(BF16), wu.astype(BF16), wd.astype(BF16), n_tiles, t)
    return _combine(ys, pos3, n, tm)


LOG2E = math.log2(math.e)
V_EXT = 2 * HEAD_DIM


def _head_norm_t(y_t, n_heads, scale):
    tm = y_t.shape[1]
    y3 = y_t.reshape(n_heads, HEAD_DIM, tm)
    ms = jnp.mean(y3 * y3, axis=1, keepdims=True)
    return y3 * (lax.rsqrt(ms + EPS) * scale)


def _qkv_kernel(x_ref, g_ref, wqt_ref, wkt_ref, wvt_ref, vone_ref, kg_ref, qt_ref, k_ref, vt_ref):
    h = _rms(x_ref[...], g_ref[...]).astype(BF16)
    tm = h.shape[0]
    qn = _head_norm_t(_nt_dot(wqt_ref[...], h), N_Q_HEADS, HEAD_DIM ** -0.5 * LOG2E)
    qt_ref[...] = qn.reshape(N_Q_HEADS * HEAD_DIM, tm).astype(BF16)
    kn = _head_norm_t(_nt_dot(wkt_ref[...], h), N_KV_HEADS, kg_ref[...].reshape(N_KV_HEADS, HEAD_DIM, 1))
    for hk in range(N_KV_HEADS):
        k_ref[hk] = kn[hk].T.astype(BF16)
    vt_ref[...] = (_nt_dot(wvt_ref[...], h) + vone_ref[...]).astype(BF16)


def _qkv(x2, gain, wqkv, q_gain, k_gain, tm=512):
    n = x2.shape[0]
    qw = N_Q_HEADS * HEAD_DIM
    wqt = wqkv[:, :qw].T.astype(BF16)
    wkt = wqkv[:, qw:qw + KV_WIDTH].T.astype(BF16)
    wvt = wqkv[:, qw + KV_WIDTH:].T.astype(BF16).reshape(N_KV_HEADS, HEAD_DIM, D_MODEL)
    wvt = jnp.pad(wvt, ((0, 0), (0, V_EXT - HEAD_DIM), (0, 0))).reshape(N_KV_HEADS * V_EXT, D_MODEL)
    vone = np.zeros((N_KV_HEADS * V_EXT, 1), np.float32)
    vone[HEAD_DIM::V_EXT, 0] = 1.0
    kg = jnp.tile((k_gain.astype(F32) * q_gain.astype(F32)), N_KV_HEADS).reshape(KV_WIDTH, 1)
    full = lambda r, c: pl.BlockSpec((r, c), lambda i: (0, 0))
    return pl.pallas_call(
        _qkv_kernel,
        out_shape=(jax.ShapeDtypeStruct((qw, n), BF16),
                   jax.ShapeDtypeStruct((N_KV_HEADS, n, HEAD_DIM), BF16),
                   jax.ShapeDtypeStruct((N_KV_HEADS * V_EXT, n), BF16)),
        grid=(n // tm,),
        in_specs=[pl.BlockSpec((tm, D_MODEL), lambda i: (i, 0)), full(1, D_MODEL),
                  full(qw, D_MODEL), full(KV_WIDTH, D_MODEL), full(N_KV_HEADS * V_EXT, D_MODEL),
                  full(N_KV_HEADS * V_EXT, 1), full(KV_WIDTH, 1)],
        out_specs=(pl.BlockSpec((qw, tm), lambda i: (0, i)),
                   pl.BlockSpec((N_KV_HEADS, tm, HEAD_DIM), lambda i: (0, i, 0)),
                   pl.BlockSpec((N_KV_HEADS * V_EXT, tm), lambda i: (0, i))),
        compiler_params=_params("parallel"),
        name="odd_qkv",
    )(x2, gain.reshape(1, D_MODEL), wqt, wkt, wvt, jnp.asarray(vone), kg)


def _attn_bias():
    blk = ATT_BLOCK
    qi = np.arange(blk)[None, :]
    ki = np.arange(2 * blk)[:, None]
    dist = qi - ki + blk
    band = (dist >= 0) & (dist < blk)
    slopes = 2.0 ** (-8.0 * np.arange(1, N_Q_HEADS + 1) / N_Q_HEADS)
    pen = -slopes[:, None, None] * dist[None].astype(np.float64) * LOG2E
    inner = np.where(band[None], pen, -np.inf)
    first = np.where((band & (ki >= blk))[None], pen, -np.inf)
    tab = np.stack([inner, first]).astype(np.float32)
    tab = tab.reshape(2, N_KV_HEADS, GQA_GROUP, 2 * blk, blk).transpose(0, 1, 3, 2, 4)
    return tab.reshape(2, N_KV_HEADS, 2 * blk, GQA_GROUP * blk)


def _attn_kernel(qt_ref, kp_ref, kc_ref, vtp_ref, vtc_ref, bias_ref, sink_ref, x_ref, wo_ref, o_ref):
    first = (pl.program_id(1) == 0).astype(jnp.int32)
    vt = jnp.concatenate([vtp_ref[...], vtc_ref[...]], axis=1)
    att_t = []
    for hk in range(N_KV_HEADS):
        keys = jnp.concatenate([kp_ref[hk], kc_ref[hk]], axis=0)
        q_t = jnp.concatenate([qt_ref[(hk * GQA_GROUP + g) * HEAD_DIM:(hk * GQA_GROUP + g + 1) * HEAD_DIM, :]
                               for g in range(GQA_GROUP)], axis=1)
        s = jnp.dot(keys, q_t, preferred_element_type=F32) + bias_ref[first, hk]
        sink = sink_ref[hk]
        m = jnp.maximum(jnp.max(s, axis=0, keepdims=True), sink)
        p = jnp.exp2(s - m).astype(BF16)
        pv = jnp.dot(vt[hk * V_EXT:(hk + 1) * V_EXT, :], p, preferred_element_type=F32)
        den = pv[HEAD_DIM:HEAD_DIM + 1, :] + jnp.exp2(sink - m)
        o_t = (pv[:HEAD_DIM, :] * (1.0 / den)).astype(BF16)
        att_t += [o_t[:, g * ATT_BLOCK:(g + 1) * ATT_BLOCK] for g in range(GQA_GROUP)]
    att_t = jnp.concatenate(att_t, axis=0)
    mix = lax.dot_general(att_t, wo_ref[...], (((0,), (0,)), ((), ())), preferred_element_type=F32)
    o_ref[...] = x_ref[...] + mix


def _attn(qt, k, vt, x2, sinks, wo, bsz, seqlen):
    blk = ATT_BLOCK
    nb = seqlen // blk
    qw = N_Q_HEADS * HEAD_DIM
    cols = GQA_GROUP * blk
    cur = lambda b, n: (b * nb + n, 0)
    cur_t = lambda b, n: (0, b * nb + n)
    prev_t = lambda b, n: (0, b * nb + jnp.maximum(n - 1, 0))
    sink_row = jnp.repeat(sinks.astype(F32) * LOG2E, blk).reshape(N_KV_HEADS, 1, cols)
    return pl.pallas_call(
        _attn_kernel,
        out_shape=jax.ShapeDtypeStruct((bsz * seqlen, D_MODEL), F32),
        grid=(bsz, nb),
        in_specs=[pl.BlockSpec((qw, blk), cur_t),
                  pl.BlockSpec((N_KV_HEADS, blk, HEAD_DIM), lambda b, n: (0, b * nb + jnp.maximum(n - 1, 0), 0)),
                  pl.BlockSpec((N_KV_HEADS, blk, HEAD_DIM), lambda b, n: (0, b * nb + n, 0)),
                  pl.BlockSpec((N_KV_HEADS * V_EXT, blk), prev_t),
                  pl.BlockSpec((N_KV_HEADS * V_EXT, blk), cur_t),
                  pl.BlockSpec((2, N_KV_HEADS, 2 * blk, cols), lambda b, n: (0, 0, 0, 0)),
                  pl.BlockSpec((N_KV_HEADS, 1, cols), lambda b, n: (0, 0, 0)),
                  pl.BlockSpec((blk, D_MODEL), cur),
                  pl.BlockSpec((qw, D_MODEL), lambda b, n: (0, 0))],
        out_specs=pl.BlockSpec((blk, D_MODEL), cur),
        compiler_params=_params("parallel", "parallel"),
        name="odd_attn",
    )(qt, k, k, vt, vt, jnp.asarray(_attn_bias()), sink_row, x2, wo.astype(BF16))


def kernel(x, even_mix_norm, even_in_proj, s5_lambda_re, s5_lambda_im, s5_log_step, s5_b_re, s5_b_im,
           s5_c_re, s5_c_im, s5_d, s5_glu_w, hgrn_lower_bounds, hgrn_o_norm, even_out_proj, odd_mix_norm,
           odd_wqkv, odd_q_norm, odd_k_norm, odd_sinks, odd_out_proj, moe_norm, moe_router_group,
           moe_router_group_bias, moe_router_expert, moe_router_expert_bias, moe_w_gate, moe_w_up,
           moe_w_down):
    bsz, seqlen, dm = x.shape
    n = bsz * seqlen
    x2 = x.reshape(n, dm)
    lower_bounds = jnp.cumsum(jax.nn.softmax(hgrn_lower_bounds.astype(F32), axis=0), axis=0)

    def router_params(layer):
        return (moe_norm[layer], moe_router_group[layer], moe_router_group_bias[layer],
                moe_router_expert[layer], moe_router_expert_bias[layer])

    def moe(xx, routing, layer):
        return _moe(xx, routing, moe_norm[layer], moe_w_gate[layer], moe_w_up[layer], moe_w_down[layer])

    u, h4 = _inproj(x2, even_mix_norm[0], even_in_proj[0].astype(BF16))
    ops = _s5_operators(s5_lambda_re[0], s5_lambda_im[0], s5_log_step[0], s5_b_re[0], s5_b_im[0],
                        s5_c_re[0], s5_c_im[0])
    u_g = _s5_pack(u.reshape(bsz, seqlen, S5_WIDTH))
    y_g = _s5_scan(u_g, ops, bsz)
    ys = _s5_unpack(y_g, bsz, seqlen).reshape(n, S5_WIDTH)
    b_out = _hgrn(h4.reshape(bsz, seqlen, 4 * HG_WIDTH), lower_bounds[0], hgrn_o_norm[0], bsz, seqlen)
    x2, *routing = _evenout(x2, ys, u, b_out.reshape(n, HG_WIDTH), s5_d[0], s5_glu_w[0], even_out_proj[0],
                            router_params(0))
    x2 = moe(x2, routing, 0)

    q, kt, v = _qkv(x2, odd_mix_norm[0], odd_wqkv[0], odd_q_norm[0], odd_k_norm[0])
    x2 = _attn(q, kt, v, x2, odd_sinks[0], odd_out_proj[0], bsz, seqlen)
    x2 = moe(x2, _router(x2, router_params(1)), 1)
    return x2.reshape(bsz, seqlen, dm)
```

```python
import functools
import math

import jax
import jax.numpy as jnp
import numpy as np
from jax import lax
from jax.experimental import pallas as pl
from jax.experimental.pallas import tpu as pltpu

F32 = jnp.float32
BF16 = jnp.bfloat16
EPS = 1e-6

D_MODEL = 1024
S5_WIDTH = 512
S5_GROUP = 16
S5_GROUPS = 32
S5_STATE = 64
S5_CHUNK = 16
HG_WIDTH = 512
HG_HEAD_DIM = 128
HG_HEADS = 4
HG_CHUNK = 32
HEAD_DIM = 64
N_Q_HEADS = 16
N_KV_HEADS = 2
GQA_GROUP = 8
KV_WIDTH = N_KV_HEADS * HEAD_DIM
ATT_BLOCK = 128
ATT_STEP_BLOCKS = 2
N_GROUPS = 4
EXPERTS_PER_GROUP = 4
N_EXPERTS = 16
D_EXPERT = 256
ROUTER_ROWS = 32
N_PAIRS = 6
N_BUCKETS = N_GROUPS * N_PAIRS
BUCKET_ROWS = 32
MOE_TILE = 512
ROW_TILE = (8, 128)
ROW_TILE_SHIFT = ROW_TILE[0].bit_length() - 1
X_TILES = D_MODEL // ROW_TILE[1]
XS_TILES = X_TILES + 1

VMEM_LIMIT_BYTES = 56 * 1024 * 1024


def _params(*semantics):
    return pltpu.CompilerParams(dimension_semantics=semantics, vmem_limit_bytes=VMEM_LIMIT_BYTES)


def _rms(xf, gain):
    return xf * lax.rsqrt(jnp.mean(xf * xf, axis=-1, keepdims=True) + EPS) * gain


def _nt_dot(w_t, h):
    return lax.dot_general(w_t, h, (((1,), (1,)), ((), ())), preferred_element_type=F32)


def _sigmoid(x):
    return 0.5 * jnp.tanh(0.5 * x) + 0.5


def _silu(x):
    return x * _sigmoid(x)


def _inproj_kernel(x_ref, g_ref, w_ref, u_ref, h4_ref):
    h = _rms(x_ref[...], g_ref[...]).astype(BF16)
    p = jnp.dot(h, w_ref[...], preferred_element_type=F32)
    u_ref[...] = p[:, :S5_WIDTH]
    h4_ref[...] = p[:, S5_WIDTH:]


def _inproj(x2, gain, w_bf16, tm=512):
    n = x2.shape[0]
    e_in = w_bf16.shape[1]
    return pl.pallas_call(
        _inproj_kernel,
        out_shape=(jax.ShapeDtypeStruct((n, S5_WIDTH), F32),
                   jax.ShapeDtypeStruct((n, e_in - S5_WIDTH), F32)),
        grid=(n // tm,),
        in_specs=[pl.BlockSpec((tm, D_MODEL), lambda i: (i, 0)),
                  pl.BlockSpec((1, D_MODEL), lambda i: (0, 0)),
                  pl.BlockSpec((D_MODEL, e_in), lambda i: (0, 0))],
        out_specs=(pl.BlockSpec((tm, S5_WIDTH), lambda i: (i, 0)),
                   pl.BlockSpec((tm, e_in - S5_WIDTH), lambda i: (i, 0))),
        compiler_params=_params("parallel"),
        name="even_inproj",
    )(x2, gain.reshape(1, D_MODEL), w_bf16)


def _s5_toeplitz_kernel(ca_ref, bbt_ref, mt_ref):
    for gi in range(ca_ref.shape[0]):
        kt = lax.dot_general(bbt_ref[gi], ca_ref[gi], (((1,), (1,)), ((), ())), preferred_element_type=F32,
                             precision=lax.Precision.HIGHEST)
        width = kt.shape[1]
        blocks = [kt] + [jnp.concatenate([jnp.zeros((S5_GROUP, s * S5_GROUP), F32),
                                          kt[:, :width - s * S5_GROUP]], axis=1) for s in range(1, S5_CHUNK)]
        mt_ref[gi] = jnp.concatenate(blocks, axis=0).astype(BF16)


def _s5_toeplitz(ca, bbt, groups_per_step=8):
    g, rows, k = ca.shape
    gs = groups_per_step
    return pl.pallas_call(
        _s5_toeplitz_kernel,
        out_shape=jax.ShapeDtypeStruct((g, rows, rows), BF16),
        grid=(g // gs,),
        in_specs=[pl.BlockSpec((gs, rows, k), lambda i: (i, 0, 0)),
                  pl.BlockSpec((gs, S5_GROUP, k), lambda i: (i, 0, 0))],
        out_specs=pl.BlockSpec((gs, rows, rows), lambda i: (i, 0, 0)),
        compiler_params=_params("parallel"),
        name="s5_toeplitz",
    )(ca, bbt)


def _s5_operators(lam_re, lam_im, log_step, b_re, b_im, c_re, c_im):
    t = S5_CHUNK
    lr, li = lam_re.astype(F32), lam_im.astype(F32)
    step = jnp.exp(log_step.astype(F32))[:, None]
    mag = jnp.exp(lr * step)
    ab_re = mag * jnp.cos(li * step)
    ab_im = mag * jnp.sin(li * step)
    den = lr * lr + li * li
    nr, ni = ab_re - 1.0, ab_im
    z_re = (nr * lr + ni * li) / den
    z_im = (ni * lr - nr * li) / den
    br, bi = b_re.astype(F32), b_im.astype(F32)
    bb_re = z_re[..., None] * br - z_im[..., None] * bi
    bb_im = z_re[..., None] * bi + z_im[..., None] * br
    kk = jnp.arange(t + 1, dtype=F32)[:, None, None]
    pmag = jnp.exp(kk * (lr * step)[None])
    pw_re = pmag * jnp.cos(kk * (li * step)[None])
    pw_im = pmag * jnp.sin(kk * (li * step)[None])
    cr = jnp.transpose(c_re.astype(F32), (0, 1, 2))
    ci = c_im.astype(F32)
    ca_re = cr[None] * pw_re[:, :, None, :] - ci[None] * pw_im[:, :, None, :]
    ca_im = cr[None] * pw_im[:, :, None, :] + ci[None] * pw_re[:, :, None, :]
    g = lr.shape[0]
    ca_cat = jnp.concatenate([ca_re[:t], -ca_im[:t]], axis=-1)
    ca_cat = jnp.transpose(ca_cat, (1, 0, 2, 3)).reshape(g, t * S5_GROUP, 2 * S5_STATE)
    bb_cat_t = jnp.concatenate([bb_re, bb_im], axis=1).transpose(0, 2, 1)
    mt = _s5_toeplitz(ca_cat, bb_cat_t)
    pr = pw_re[:t][::-1]
    pi = pw_im[:t][::-1]
    sb_re = pr[:, :, :, None] * bb_re[None] - pi[:, :, :, None] * bb_im[None]
    sb_im = pr[:, :, :, None] * bb_im[None] + pi[:, :, :, None] * bb_re[None]
    sb_re = jnp.transpose(sb_re, (1, 0, 3, 2)).reshape(g, t * S5_GROUP, S5_STATE)
    sb_im = jnp.transpose(sb_im, (1, 0, 3, 2)).reshape(g, t * S5_GROUP, S5_STATE)
    cp_re = jnp.transpose(ca_re[1:], (1, 3, 0, 2)).reshape(g, S5_STATE, t * S5_GROUP)
    cp_im = jnp.transpose(-ca_im[1:], (1, 3, 0, 2)).reshape(g, S5_STATE, t * S5_GROUP)

    def pair_rows(m):
        m = m.reshape(g // 2, 2, m.shape[1], m.shape[2])
        z = jnp.zeros_like(m[:, 0])
        top = jnp.concatenate([m[:, 0], z], axis=2)
        bot = jnp.concatenate([z, m[:, 1]], axis=2)
        return jnp.concatenate([top, bot], axis=1)

    at_re = pw_re[t].reshape(g // 2, 1, 2 * S5_STATE)
    at_im = pw_im[t].reshape(g // 2, 1, 2 * S5_STATE)
    return (mt, pair_rows(sb_re).astype(BF16), pair_rows(sb_im).astype(BF16),
            pair_rows(cp_re).astype(BF16), pair_rows(cp_im).astype(BF16), at_re, at_im)


PACK_TOKENS = 512


LANES = 128
GROUPS_PER_TILE = LANES // S5_GROUP
TOKENS_PER_TILE = LANES // S5_GROUP
CHUNK_HALVES = S5_CHUNK // TOKENS_PER_TILE
PACK_CHUNKS = PACK_TOKENS // S5_CHUNK


def _block_swap_matrix():
    a, b, h = np.meshgrid(np.arange(TOKENS_PER_TILE), np.arange(GROUPS_PER_TILE), np.arange(S5_GROUP),
                          indexing="ij")
    src = (a * GROUPS_PER_TILE + b) * S5_GROUP + h
    dst = (b * TOKENS_PER_TILE + a) * S5_GROUP + h
    m = np.zeros((src.size, src.size), np.float32)
    m[src.ravel(), dst.ravel()] = 1.0
    return jnp.asarray(m, dtype=BF16)


def _s5_pack_kernel(u_ref, swap_ref, o_ref, *, bsz):
    for j in range(CHUNK_HALVES):
        rows = [jnp.concatenate([u_ref[b, pl.ds(j * TOKENS_PER_TILE + tt, PACK_CHUNKS, stride=S5_CHUNK), :]
                                 for tt in range(TOKENS_PER_TILE)], axis=1) for b in range(bsz)]
        lhs = jnp.concatenate(rows, axis=0).astype(BF16)
        out = jnp.dot(lhs, swap_ref[...], preferred_element_type=F32)
        for g in range(GROUPS_PER_TILE):
            for b in range(bsz):
                o_ref[g * CHUNK_HALVES + j, pl.ds(b, PACK_CHUNKS, stride=bsz), :] = (
                    out[b * PACK_CHUNKS:(b + 1) * PACK_CHUNKS, g * LANES:(g + 1) * LANES])


def _s5_pack(u3):
    bsz, seqlen, w = u3.shape
    rows = PACK_CHUNKS * bsz
    swap = _block_swap_matrix()
    return pl.pallas_call(
        functools.partial(_s5_pack_kernel, bsz=bsz),
        out_shape=jax.ShapeDtypeStruct((S5_GROUPS * CHUNK_HALVES, seqlen // S5_CHUNK * bsz, LANES), F32),
        grid=(seqlen // PACK_TOKENS, w // LANES),
        in_specs=[pl.BlockSpec((bsz, PACK_TOKENS, LANES), lambda i, k: (0, i, k)),
                  pl.BlockSpec(swap.shape, lambda i, k: (0, 0))],
        out_specs=pl.BlockSpec((GROUPS_PER_TILE * CHUNK_HALVES, rows, LANES), lambda i, k: (k, i, 0)),
        compiler_params=_params("parallel", "parallel"),
        name="s5_pack",
    )(u3, swap)


def _s5_unpack_kernel(y_ref, swap_ref, o_ref, *, bsz):
    for j in range(CHUNK_HALVES):
        rows = [jnp.concatenate([y_ref[g * CHUNK_HALVES + j, pl.ds(b, PACK_CHUNKS, stride=bsz), :]
                                 for g in range(GROUPS_PER_TILE)], axis=1) for b in range(bsz)]
        lhs = jnp.concatenate(rows, axis=0).astype(BF16)
        out = jnp.dot(lhs, swap_ref[...], preferred_element_type=F32)
        for tt in range(TOKENS_PER_TILE):
            for b in range(bsz):
                o_ref[b, pl.ds(j * TOKENS_PER_TILE + tt, PACK_CHUNKS, stride=S5_CHUNK), :] = (
                    out[b * PACK_CHUNKS:(b + 1) * PACK_CHUNKS, tt * LANES:(tt + 1) * LANES])


def _s5_unpack(y_g, bsz, seqlen):
    rows = PACK_CHUNKS * bsz
    swap = _block_swap_matrix()
    return pl.pallas_call(
        functools.partial(_s5_unpack_kernel, bsz=bsz),
        out_shape=jax.ShapeDtypeStruct((bsz, seqlen, S5_WIDTH), F32),
        grid=(seqlen // PACK_TOKENS, S5_WIDTH // LANES),
        in_specs=[pl.BlockSpec((GROUPS_PER_TILE * CHUNK_HALVES, rows, LANES), lambda i, k: (k, i, 0)),
                  pl.BlockSpec(swap.shape, lambda i, k: (0, 0))],
        out_specs=pl.BlockSpec((bsz, PACK_TOKENS, LANES), lambda i, k: (0, i, k)),
        compiler_params=_params("parallel", "parallel"),
        name="s5_unpack",
    )(y_g, swap)


def _s5_kernel(u_ref, mt_ref, wre_ref, wim_ref, cre_ref, cim_ref, atr_ref, ati_ref, y_ref,
               sre_ref, sim_ref, xre_ref, xim_ref, *, n_chunks, bsz):
    ucat = jnp.concatenate([u_ref[i] for i in range(2 * CHUNK_HALVES)], axis=1).astype(BF16)
    w = S5_CHUNK * S5_GROUP
    u0 = ucat[:, :w]
    u1 = ucat[:, w:]
    sre_ref[...] = jnp.dot(ucat, wre_ref[0], preferred_element_type=F32)
    sim_ref[...] = jnp.dot(ucat, wim_ref[0], preferred_element_type=F32)
    atr = jnp.broadcast_to(atr_ref[0], (bsz, 2 * S5_STATE))
    ati = jnp.broadcast_to(ati_ref[0], (bsz, 2 * S5_STATE))

    def body(c, carry):
        xr, xi = carry
        rows = pl.ds(pl.multiple_of(c * bsz, bsz), bsz)
        xre_ref[rows, :] = xr
        xim_ref[rows, :] = xi
        nxr = atr * xr - ati * xi + sre_ref[rows, :]
        nxi = atr * xi + ati * xr + sim_ref[rows, :]
        return nxr, nxi

    zero = jnp.zeros((bsz, 2 * S5_STATE), F32)
    lax.fori_loop(0, n_chunks, body, (zero, zero))
    ycar = (jnp.dot(xre_ref[...].astype(BF16), cre_ref[0], preferred_element_type=F32)
            + jnp.dot(xim_ref[...].astype(BF16), cim_ref[0], preferred_element_type=F32))
    y0 = jnp.dot(u0, mt_ref[0], preferred_element_type=F32) + ycar[:, :w]
    y1 = jnp.dot(u1, mt_ref[1], preferred_element_type=F32) + ycar[:, w:]
    for i in range(CHUNK_HALVES):
        y_ref[i] = y0[:, i * LANES:(i + 1) * LANES]
        y_ref[CHUNK_HALVES + i] = y1[:, i * LANES:(i + 1) * LANES]


def _s5_scan(u_g, ops, bsz):
    mt, wre, wim, cre, cim, atr, ati = ops
    tiles, r, _ = u_g.shape
    g = tiles // CHUNK_HALVES
    w = S5_CHUNK * S5_GROUP
    n_chunks = r // bsz
    p2 = 2 * S5_STATE
    kern = functools.partial(_s5_kernel, n_chunks=n_chunks, bsz=bsz)
    pair_tiles = pl.BlockSpec((2 * CHUNK_HALVES, r, LANES), lambda i: (i, 0, 0))
    return pl.pallas_call(
        kern,
        out_shape=jax.ShapeDtypeStruct((tiles, r, LANES), F32),
        grid=(g // 2,),
        in_specs=[pair_tiles,
                  pl.BlockSpec((2, w, w), lambda i: (i, 0, 0)),
                  pl.BlockSpec((1, 2 * w, p2), lambda i: (i, 0, 0)),
                  pl.BlockSpec((1, 2 * w, p2), lambda i: (i, 0, 0)),
                  pl.BlockSpec((1, p2, 2 * w), lambda i: (i, 0, 0)),
                  pl.BlockSpec((1, p2, 2 * w), lambda i: (i, 0, 0)),
                  pl.BlockSpec((1, 1, p2), lambda i: (i, 0, 0)),
                  pl.BlockSpec((1, 1, p2), lambda i: (i, 0, 0))],
        out_specs=pair_tiles,
        scratch_shapes=[pltpu.VMEM((r, p2), F32)] * 4,
        compiler_params=_params("parallel"),
        name="s5_scan",
    )(u_g, mt, wre, wim, cre, cim, atr, ati)


def _hgrn_kernel(q_ref, f_ref, i_ref, g_ref, lb_ref, og_ref, o_ref, st_ref, *, seqlen):
    c = HG_CHUNK
    nc = seqlen // c
    d = HG_HEAD_DIM
    lb = lb_ref[...]
    q = q_ref[0]
    qs = _silu(q)
    f = lb + (1.0 - lb) * _sigmoid(f_ref[0])
    lf = jnp.log(f)
    k = 1.0 - f
    v = i_ref[0]
    sub = ROW_TILE[0]
    b8 = lf.reshape(seqlen // sub, sub, d)
    row = lax.broadcasted_iota(jnp.int32, b8.shape, 1)
    sh = 1
    while sh < sub:
        b8 = b8 + jnp.where(row >= sh, pltpu.roll(b8, sh, axis=1), 0.0)
        sh *= 2
    b4 = b8.reshape(nc, c // sub, sub, d)
    groups, run = [], None
    for gi in range(c // sub):
        grp = b4[:, gi]
        groups.append(grp if run is None else grp + run)
        total = grp[:, sub - 1:sub, :]
        run = total if run is None else run + total
    b3 = jnp.concatenate(groups, axis=1)
    b_last = b3[:, c - 1:c, :]
    b_ref = b3[:, c // 2 - 1:c // 2, :]
    qs3 = qs.reshape(nc, c, d)
    k3 = k.reshape(nc, c, d)
    v3 = v.reshape(nc, c, d).astype(BF16)
    qe_f = qs3 * jnp.exp(b3 - b_ref)
    ke_f = k3 * jnp.exp(b_ref - b3)
    qe = qe_f.astype(BF16)
    ke = ke_f.astype(BF16)
    kd = (ke_f * jnp.exp(b_last - b_ref)).astype(BF16)
    qb = (qe_f * jnp.exp(b_ref)).astype(BF16)
    scores = jnp.einsum('ctd,csd->cts', qe, ke, preferred_element_type=F32)
    ti = lax.broadcasted_iota(jnp.int32, (c, c), 0)
    si = lax.broadcasted_iota(jnp.int32, (c, c), 1)
    scores = jnp.where((ti >= si)[None], scores, 0.0)
    o_intra = jnp.einsum('cts,csv->ctv', scores.astype(BF16), v3, preferred_element_type=F32)
    ut = jnp.einsum('csv,csd->cvd', v3, kd, preferred_element_type=F32)
    decay = jnp.exp(b_last)
    state = jnp.zeros((d, d), F32)
    for ci in range(nc):
        st_ref[ci] = state.astype(BF16)
        state = decay[ci] * state + ut[ci]
    o_inter = jnp.einsum('ctd,cvd->ctv', qb, st_ref[...], preferred_element_type=F32)
    o = (o_intra + o_inter).reshape(seqlen, d)
    o = _rms(o, og_ref[...])
    o_ref[0] = (o * _silu(g_ref[0])).astype(BF16)


def _hgrn(h4, lower_bound, o_gain, bsz, seqlen):
    d = HG_HEAD_DIM
    kern = functools.partial(_hgrn_kernel, seqlen=seqlen)

    def col(part):
        return pl.BlockSpec((1, seqlen, d), lambda b, h: (b, 0, part * HG_HEADS + h))

    return pl.pallas_call(
        kern,
        out_shape=jax.ShapeDtypeStruct((bsz, seqlen, HG_WIDTH), BF16),
        grid=(bsz, HG_HEADS),
        in_specs=[col(0), col(1), col(2), col(3),
                  pl.BlockSpec((1, d), lambda b, h: (0, h)),
                  pl.BlockSpec((1, d), lambda b, h: (0, 0))],
        out_specs=pl.BlockSpec((1, seqlen, d), lambda b, h: (b, 0, h)),
        scratch_shapes=[pltpu.VMEM((seqlen // HG_CHUNK, d, d), BF16)],
        compiler_params=_params("parallel", "parallel"),
        name="hgrn2",
    )(h4, h4, h4, h4, lower_bound.reshape(1, HG_WIDTH), o_gain.reshape(1, d))


def _evenout_kernel(x_ref, ys_ref, u_ref, b_ref, d_ref, wglu_ref, wa_ref, wb_ref, *rest):
    route_in, (o_ref, *route_out), route_scratch = rest[:4], rest[4:8], rest[8:]
    y = ys_ref[...] + d_ref[...] * u_ref[...]
    y = jax.nn.gelu(y)
    gate = _sigmoid(jnp.dot(y.astype(BF16), wglu_ref[...], preferred_element_type=F32))
    a = (y * gate).astype(BF16)
    mix = (jnp.dot(a, wa_ref[...], preferred_element_type=F32)
           + jnp.dot(b_ref[...], wb_ref[...], preferred_element_type=F32))
    x_new = x_ref[...] + mix
    o_ref[...] = x_new
    _route(x_new, pl.program_id(0) == 0, *route_in, *route_out, *route_scratch)


def _evenout(x2, ys, u, b_out, d_skip, wglu, wout, router_params, tm=512):
    n = x2.shape[0]
    route = _RouterPlumbing(n, tm, lambda i: i, *router_params)
    row = lambda w: pl.BlockSpec((tm, w), lambda i: (i, 0))
    full = lambda r, c: pl.BlockSpec((r, c), lambda i: (0, 0))
    return pl.pallas_call(
        _evenout_kernel,
        out_shape=(jax.ShapeDtypeStruct((n, D_MODEL), F32), *route.out_shape),
        grid=(n // tm,),
        in_specs=[row(D_MODEL), row(S5_WIDTH), row(S5_WIDTH), row(HG_WIDTH),
                  full(1, S5_WIDTH), full(S5_WIDTH, S5_WIDTH),
                  full(S5_WIDTH, D_MODEL), full(HG_WIDTH, D_MODEL), *route.in_specs],
        out_specs=(row(D_MODEL), *route.out_specs),
        scratch_shapes=route.scratch_shapes,
        compiler_params=_params("arbitrary"),
        name="even_out",
    )(x2, ys, u, b_out, d_skip.reshape(1, S5_WIDTH), wglu.astype(BF16),
      wout[:S5_WIDTH].astype(BF16), wout[S5_WIDTH:].astype(BF16), *route.operands)


def _route(x, is_first_step, g_ref, wr_ref, br_ref, tri_ref, idx_ref, wts_ref, cnt_ref, run_ref):
    @pl.when(is_first_step)
    def _():
        run_ref[...] = jnp.zeros_like(run_ref)

    h = _rms(x, g_ref[...])
    h_hi = h.astype(BF16)
    h_lo = (h - h_hi.astype(F32)).astype(BF16)
    both = _nt_dot(wr_ref[...], h_hi)
    lt = (both[:ROUTER_ROWS] + both[ROUTER_ROWS:] + _nt_dot(wr_ref[:ROUTER_ROWS, :], h_lo)
          + br_ref[...])
    gl = [lt[i:i + 1] for i in range(N_GROUPS)]
    el = [lt[N_GROUPS + i:N_GROUPS + i + 1] for i in range(N_EXPERTS)]
    gmax = jnp.maximum(jnp.maximum(gl[0], gl[1]), jnp.maximum(gl[2], gl[3]))
    g_idx = jnp.where(gl[0] == gmax, 0, jnp.where(gl[1] == gmax, 1, jnp.where(gl[2] == gmax, 2, 3)))
    g_gate = 1.0 / (jnp.exp(gl[0] - gmax) + jnp.exp(gl[1] - gmax) + jnp.exp(gl[2] - gmax) + jnp.exp(gl[3] - gmax))
    es = []
    for j in range(EXPERTS_PER_GROUP):
        es.append(jnp.where(g_idx == 0, el[j],
                            jnp.where(g_idx == 1, el[4 + j],
                                      jnp.where(g_idx == 2, el[8 + j], el[12 + j]))))
    e1 = jnp.maximum(jnp.maximum(es[0], es[1]), jnp.maximum(es[2], es[3]))
    i1 = jnp.where(es[0] == e1, 0, jnp.where(es[1] == e1, 1, jnp.where(es[2] == e1, 2, 3)))
    rest = [jnp.where(i1 == j, -jnp.inf, es[j]) for j in range(EXPERTS_PER_GROUP)]
    e2 = jnp.maximum(jnp.maximum(rest[0], rest[1]), jnp.maximum(rest[2], rest[3]))
    i2 = jnp.where(rest[0] == e2, 0, jnp.where(rest[1] == e2, 1, jnp.where(rest[2] == e2, 2, 3)))
    r = jnp.exp(e2 - e1)
    w1 = g_gate / (1.0 + r)
    w2 = w1 * r
    first_lo = i1 < i2
    lo = jnp.where(first_lo, i1, i2)
    hi = jnp.where(first_lo, i2, i1)
    w_lo = jnp.where(first_lo, w1, w2)
    w_hi = jnp.where(first_lo, w2, w1)
    pair = jnp.where(lo == 0, 0, jnp.where(lo == 1, 3, 5)) + hi - lo - 1
    bucket = g_idx * N_PAIRS + pair
    tm = bucket.shape[1]
    rowid = lax.broadcasted_iota(jnp.int32, (BUCKET_ROWS, tm), 0)
    onehot = (rowid == bucket).astype(F32)
    prefix = jnp.dot(onehot.astype(BF16), tri_ref[...], preferred_element_type=F32)
    run = run_ref[...]
    rank = jnp.sum(onehot * (prefix + run), axis=0, keepdims=True)
    run = run + jnp.sum(onehot, axis=1, keepdims=True)
    run_ref[...] = run
    cnt_ref[...] = jnp.broadcast_to(run, cnt_ref.shape)
    idx_ref[...] = jnp.concatenate([bucket, rank.astype(jnp.int32), jnp.zeros((6, tm), jnp.int32)], axis=0)
    wts_ref[...] = jnp.concatenate([w_lo, w_hi, jnp.zeros((6, tm), F32)], axis=0)


class _RouterPlumbing:
    def __init__(self, n, tm, tile_index, gain, w_rg, b_rg, w_re, b_re):
        wr = jnp.concatenate([w_rg, w_re], axis=1).astype(F32).T
        wr = jnp.pad(wr, ((0, ROUTER_ROWS - wr.shape[0]), (0, 0)))
        wr_hi = wr.astype(BF16)
        wr = jnp.concatenate([wr_hi, (wr - wr_hi.astype(F32)).astype(BF16)], axis=0)
        br = jnp.pad(jnp.concatenate([b_rg, b_re]).astype(F32), (0, ROUTER_ROWS - N_GROUPS - N_EXPERTS))
        tri = (np.arange(tm)[:, None] < np.arange(tm)[None, :]).astype(np.float32)
        const = lambda *_: (0, 0)
        self.operands = (gain.reshape(1, D_MODEL), wr, br.reshape(ROUTER_ROWS, 1), jnp.asarray(tri, dtype=BF16))
        self.in_specs = [pl.BlockSpec((1, D_MODEL), const),
                         pl.BlockSpec((2 * ROUTER_ROWS, D_MODEL), const),
                         pl.BlockSpec((ROUTER_ROWS, 1), const),
                         pl.BlockSpec((tm, tm), const)]
        self.out_shape = (jax.ShapeDtypeStruct((8, n), jnp.int32),
                          jax.ShapeDtypeStruct((8, n), F32),
                          jax.ShapeDtypeStruct((BUCKET_ROWS, LANES), F32))
        self.out_specs = (pl.BlockSpec((8, tm), lambda *g: (0, tile_index(*g))),
                          pl.BlockSpec((8, tm), lambda *g: (0, tile_index(*g))),
                          pl.BlockSpec((BUCKET_ROWS, LANES), const))
        self.scratch_shapes = [pltpu.VMEM((BUCKET_ROWS, 1), F32)]


def _router_kernel(x_ref, *route_refs):
    _route(x_ref[...], pl.program_id(0) == 0, *route_refs)


def _router(x2, router_params, tm=512):
    n = x2.shape[0]
    route = _RouterPlumbing(n, tm, lambda i: i, *router_params)
    return pl.pallas_call(
        _router_kernel,
        out_shape=route.out_shape,
        grid=(n // tm,),
        in_specs=[pl.BlockSpec((tm, D_MODEL), lambda i: (i, 0)), *route.in_specs],
        out_specs=route.out_specs,
        scratch_shapes=route.scratch_shapes,
        compiler_params=_params("arbitrary"),
        name="moe_router",
    )(x2, *route.operands)


ROW_COPY_UNROLL = 8


def _start_row_copies(idx_ref, n_rows, copy_for_row, prepare_rows=None):
    def start_group(base):
        for j in range(ROW_COPY_UNROLL):
            copy_for_row(base + j, idx_ref[0, 0, base + j]).start(priority=j % 2)

    n_groups = n_rows // ROW_COPY_UNROLL
    if prepare_rows is None:
        def body(g, carry):
            start_group(pl.multiple_of(g * ROW_COPY_UNROLL, ROW_COPY_UNROLL))
            return carry

        lax.fori_loop(0, n_groups, body, 0)
        return

    prepare_rows(0)

    def body(g, carry):
        base = pl.multiple_of(g * ROW_COPY_UNROLL, ROW_COPY_UNROLL)
        prepare_rows(base + ROW_COPY_UNROLL)
        start_group(base)
        return carry

    lax.fori_loop(0, n_groups - 1, body, 0)
    start_group(n_rows - ROW_COPY_UNROLL)


def _row_slab(view_ref, p):
    return view_ref.at[p >> ROW_TILE_SHIFT, :, p & (ROW_TILE[0] - 1)]


def _view_columns(view_ref, n_cols):
    rows = view_ref.shape[0] * ROW_TILE[0]
    return jnp.concatenate([view_ref[:, c].reshape(rows, LANES) for c in range(n_cols)], axis=1)


def _rows_to_tiles(x):
    rows = x.shape[0]
    return x.reshape(rows * ROW_TILE[0], ROW_TILE[1]).reshape(rows, *ROW_TILE)


def _tiles_to_rows(x3):
    rows = x3.shape[0]
    return x3.reshape(rows * ROW_TILE[0], ROW_TILE[1]).reshape(rows, D_MODEL)


def _dispatch_kernel(tail_blk_ref, tail_on_ref, pos_ref, x_ref, w_ref, xs_ref, buf_ref, zero_ref, wcol_ref, sem,
                     *, tile):
    tm = x_ref.shape[0]
    tile_blks = tile // ROW_TILE[0]

    @pl.when(pl.program_id(0) == 0)
    def _():
        zero_ref[...] = jnp.zeros_like(zero_ref)

        def zero_copy(k):
            blk = pl.multiple_of(tail_blk_ref[k], tile_blks)
            return pltpu.make_async_copy(zero_ref, xs_ref.at[pl.ds(blk, tile_blks)], sem)

        for k in range(2 * N_BUCKETS):
            pl.when(tail_on_ref[k] > 0)(lambda k=k: zero_copy(k).start())
        for k in range(2 * N_BUCKETS):
            pl.when(tail_on_ref[k] > 0)(lambda k=k: zero_copy(k).wait())

    wpad = jnp.concatenate([w_ref[...], jnp.zeros((LANES - w_ref.shape[0], tm), F32)], axis=0)
    wcol_ref[...] = wpad.T

    def stage(base):
        rows = pl.ds(base, ROW_COPY_UNROLL)
        buf_ref[rows, :X_TILES, :] = _rows_to_tiles(x_ref[rows, :])
        buf_ref[rows, X_TILES, :] = wcol_ref[rows, :]

    _start_row_copies(pos_ref, tm, lambda r, p: pltpu.make_async_copy(buf_ref.at[r], _row_slab(xs_ref, p), sem),
                      prepare_rows=stage)
    done = xs_ref.at[pl.ds(0, tm // ROW_TILE[0])]
    pltpu.make_async_copy(done, done, sem).wait()


def _dispatch(x2, wts, pos3, tails, n_rows_sorted, tile, tm):
    n = x2.shape[0]
    tail_blk, tail_on = tails
    grid_spec = pltpu.PrefetchScalarGridSpec(
        num_scalar_prefetch=2,
        grid=(n // tm,),
        in_specs=[pl.BlockSpec((1, 1, tm), lambda i, *_: (i, 0, 0), memory_space=pltpu.SMEM),
                  pl.BlockSpec((tm, D_MODEL), lambda i, *_: (i, 0)),
                  pl.BlockSpec((8, tm), lambda i, *_: (0, i))],
        out_specs=pl.BlockSpec(memory_space=pl.ANY),
        scratch_shapes=[pltpu.VMEM((tm, XS_TILES, LANES), F32),
                        pltpu.VMEM((tile // ROW_TILE[0], XS_TILES, *ROW_TILE), F32),
                        pltpu.VMEM((tm, LANES), F32),
                        pltpu.SemaphoreType.DMA],
    )
    return pl.pallas_call(
        functools.partial(_dispatch_kernel, tile=tile),
        out_shape=jax.ShapeDtypeStruct((n_rows_sorted // ROW_TILE[0], XS_TILES, *ROW_TILE), F32),
        grid_spec=grid_spec,
        compiler_params=_params("arbitrary"),
        name="moe_dispatch",
    )(tail_blk, tail_on, pos3, x2, wts)


def _experts_kernel(elo_ref, ehi_ref, nvalid_ref, xs_ref, g_ref, wg_lo, wu_lo, wg_hi, wu_hi,
                    wd_lo, wd_hi, o_ref):
    del elo_ref, ehi_ref
    t = pl.program_id(0)

    @pl.when(t < nvalid_ref[0])
    def _():
        rows = xs_ref.shape[0] * ROW_TILE[0]
        xt = _view_columns(xs_ref, X_TILES)
        h = _rms(xt, g_ref[...]).astype(BF16)
        extra = xs_ref[:, X_TILES].reshape(rows, LANES)
        w_lo = extra[:, 0:1]
        w_hi = extra[:, 1:2]

        def expert(wg, wu, wd, w):
            gate = jnp.dot(h, wg[0], preferred_element_type=F32)
            up = jnp.dot(h, wu[0], preferred_element_type=F32)
            hid = (_silu(gate) * up * w).astype(BF16)
            return jnp.dot(hid, wd[0], preferred_element_type=F32)

        out = xt + expert(wg_lo, wu_lo, wd_lo, w_lo) + expert(wg_hi, wu_hi, wd_hi, w_hi)
        for c in range(X_TILES):
            o_ref[:, c] = out[:, c * LANES:(c + 1) * LANES].reshape(o_ref.shape[0], *ROW_TILE)

    @pl.when(t >= nvalid_ref[0])
    def _():
        o_ref[...] = jnp.zeros_like(o_ref)


def _experts(xs, gain, tables, wg, wu, wd, n_tiles, t):
    elo, ehi, nvalid = tables
    blks = t // ROW_TILE[0]
    row = lambda i, elo, ehi, nv: (i, 0, 0, 0)
    row_in = lambda i, elo, ehi, nv: (jnp.minimum(i, nv[0] - 1), 0, 0, 0)
    lo3 = lambda i, elo, ehi, nv: (elo[i], 0, 0)
    hi3 = lambda i, elo, ehi, nv: (ehi[i], 0, 0)
    grid_spec = pltpu.PrefetchScalarGridSpec(
        num_scalar_prefetch=3,
        grid=(n_tiles,),
        in_specs=[pl.BlockSpec((blks, XS_TILES, *ROW_TILE), row_in),
                  pl.BlockSpec((1, D_MODEL), lambda i, *_: (0, 0)),
                  pl.BlockSpec((1, D_MODEL, D_EXPERT), lo3),
                  pl.BlockSpec((1, D_MODEL, D_EXPERT), lo3),
                  pl.BlockSpec((1, D_MODEL, D_EXPERT), hi3),
                  pl.BlockSpec((1, D_MODEL, D_EXPERT), hi3),
                  pl.BlockSpec((1, D_EXPERT, D_MODEL), lo3),
                  pl.BlockSpec((1, D_EXPERT, D_MODEL), hi3)],
        out_specs=pl.BlockSpec((blks, X_TILES, *ROW_TILE), row),
    )
    return pl.pallas_call(
        _experts_kernel,
        out_shape=jax.ShapeDtypeStruct((xs.shape[0], X_TILES, *ROW_TILE), F32),
        grid_spec=grid_spec,
        compiler_params=_params("arbitrary"),
        name="moe_experts",
    )(elo, ehi, nvalid, xs, gain.reshape(1, D_MODEL), wg, wu, wg, wu, wd, wd)


def _combine_kernel(pos_ref, ys_ref, o_ref, buf_ref, sem):
    tm = o_ref.shape[0]
    _start_row_copies(pos_ref, tm, lambda r, p: pltpu.make_async_copy(_row_slab(ys_ref, p), buf_ref.at[r], sem))
    done = ys_ref.at[pl.ds(0, tm // ROW_TILE[0])]
    pltpu.make_async_copy(done, done, sem).wait()
    o_ref[...] = _tiles_to_rows(buf_ref[...])


def _combine(ys, pos3, n, tm):
    return pl.pallas_call(
        _combine_kernel,
        out_shape=jax.ShapeDtypeStruct((n, D_MODEL), F32),
        grid=(n // tm,),
        in_specs=[pl.BlockSpec((1, 1, tm), lambda i: (i, 0, 0), memory_space=pltpu.SMEM),
                  pl.BlockSpec(memory_space=pl.ANY)],
        out_specs=pl.BlockSpec((tm, D_MODEL), lambda i: (i, 0)),
        scratch_shapes=[pltpu.VMEM((tm, *ROW_TILE), F32), pltpu.SemaphoreType.DMA],
        compiler_params=_params("arbitrary"),
        name="moe_combine",
    )(pos3, ys)


def _moe_tables(idx, cnt, n_tiles, t):
    bucket, rank = idx[0], idx[1]
    counts = cnt[:N_BUCKETS, 0].astype(jnp.int32)
    tiles_b = (counts + t - 1) // t
    tile_end = jnp.cumsum(tiles_b)
    pos = (tile_end - tiles_b)[bucket] * t + rank
    total = tile_end[-1]
    tt = jnp.arange(n_tiles, dtype=jnp.int32)
    valid = tt < total
    tb = jnp.sum((tile_end[None, :] <= jnp.where(valid, tt, total - 1)[:, None]).astype(jnp.int32), axis=1)
    tb = jnp.minimum(tb, N_BUCKETS - 1)
    pair_lo = jnp.asarray([0, 0, 0, 1, 1, 2], jnp.int32)
    pair_hi = jnp.asarray([1, 2, 3, 2, 3, 3], jnp.int32)
    base = (tb // N_PAIRS) * EXPERTS_PER_GROUP
    idle = total + jnp.arange(N_BUCKETS, dtype=jnp.int32)
    idle_on = idle < n_tiles
    blks = t // ROW_TILE[0]
    tails = (jnp.concatenate([(tile_end - 1) * blks, jnp.where(idle_on, idle, 0) * blks]),
             jnp.concatenate([tiles_b > 0, idle_on]).astype(jnp.int32))
    return pos, tails, (base + pair_lo[tb % N_PAIRS], base + pair_hi[tb % N_PAIRS], total.reshape(1))


def _moe(x2, routing, gain, wg, wu, wd, t=MOE_TILE, tm=1024):
    n = x2.shape[0]
    idx, wts, cnt = routing
    n_tiles = n // t + N_BUCKETS
    pos, tails, tables = _moe_tables(idx, cnt, n_tiles, t)
    pos3 = pos.reshape(n // tm, 1, tm)
    xs = _dispatch(x2, wts, pos3, tails, n_tiles * t, t, tm)
    ys = _experts(xs, gain, tables, wg.astype(BF16), wu.astype(BF16), wd.astype(BF16), n_tiles, t)
    return _combine(ys, pos3, n, tm)


LOG2E = math.log2(math.e)
V_EXT = 2 * HEAD_DIM


def _head_norm_t(y_t, n_heads, scale):
    tm = y_t.shape[1]
    y3 = y_t.reshape(n_heads, HEAD_DIM, tm)
    ms = jnp.mean(y3 * y3, axis=1, keepdims=True)
    return y3 * (lax.rsqrt(ms + EPS) * scale)


def _qkv_kernel(x_ref, g_ref, wqt_ref, wkt_ref, wvt_ref, vone_ref, kg_ref, qt_ref, k_ref, vt_ref):
    h = _rms(x_ref[...], g_ref[...]).astype(BF16)
    tm = h.shape[0]
    qn = _head_norm_t(_nt_dot(wqt_ref[...], h), N_Q_HEADS, HEAD_DIM ** -0.5 * LOG2E)
    qt_ref[...] = qn.reshape(N_Q_HEADS * HEAD_DIM, tm).astype(BF16)
    kn = _head_norm_t(_nt_dot(wkt_ref[...], h), N_KV_HEADS, kg_ref[...].reshape(N_KV_HEADS, HEAD_DIM, 1))
    for hk in range(N_KV_HEADS):
        k_ref[hk] = kn[hk].T.astype(BF16)
    vt_ref[...] = (_nt_dot(wvt_ref[...], h) + vone_ref[...]).astype(BF16)


def _qkv(x2, gain, wqkv, q_gain, k_gain, tm=512):
    n = x2.shape[0]
    qw = N_Q_HEADS * HEAD_DIM
    wqt = wqkv[:, :qw].T.astype(BF16)
    wkt = wqkv[:, qw:qw + KV_WIDTH].T.astype(BF16)
    wvt = wqkv[:, qw + KV_WIDTH:].T.astype(BF16).reshape(N_KV_HEADS, HEAD_DIM, D_MODEL)
    wvt = jnp.pad(wvt, ((0, 0), (0, V_EXT - HEAD_DIM), (0, 0))).reshape(N_KV_HEADS * V_EXT, D_MODEL)
    vone = np.zeros((N_KV_HEADS * V_EXT, 1), np.float32)
    vone[HEAD_DIM::V_EXT, 0] = 1.0
    kg = jnp.tile((k_gain.astype(F32) * q_gain.astype(F32)), N_KV_HEADS).reshape(KV_WIDTH, 1)
    full = lambda r, c: pl.BlockSpec((r, c), lambda i: (0, 0))
    return pl.pallas_call(
        _qkv_kernel,
        out_shape=(jax.ShapeDtypeStruct((qw, n), BF16),
                   jax.ShapeDtypeStruct((N_KV_HEADS, n, HEAD_DIM), BF16),
                   jax.ShapeDtypeStruct((N_KV_HEADS * V_EXT, n), BF16)),
        grid=(n // tm,),
        in_specs=[pl.BlockSpec((tm, D_MODEL), lambda i: (i, 0)), full(1, D_MODEL),
                  full(qw, D_MODEL), full(KV_WIDTH, D_MODEL), full(N_KV_HEADS * V_EXT, D_MODEL),
                  full(N_KV_HEADS * V_EXT, 1), full(KV_WIDTH, 1)],
        out_specs=(pl.BlockSpec((qw, tm), lambda i: (0, i)),
                   pl.BlockSpec((N_KV_HEADS, tm, HEAD_DIM), lambda i: (0, i, 0)),
                   pl.BlockSpec((N_KV_HEADS * V_EXT, tm), lambda i: (0, i))),
        compiler_params=_params("parallel"),
        name="odd_qkv",
    )(x2, gain.reshape(1, D_MODEL), wqt, wkt, wvt, jnp.asarray(vone), kg)


def _attn_bias():
    blk = ATT_BLOCK
    qi = np.arange(blk)[None, :]
    ki = np.arange(2 * blk)[:, None]
    dist = qi - ki + blk
    band = (dist >= 0) & (dist < blk)
    slopes = 2.0 ** (-8.0 * np.arange(1, N_Q_HEADS + 1) / N_Q_HEADS)
    pen = -slopes[:, None, None] * dist[None].astype(np.float64) * LOG2E
    inner = np.where(band[None], pen, -np.inf)
    first = np.where((band & (ki >= blk))[None], pen, -np.inf)
    tab = np.stack([inner, first]).astype(np.float32)
    tab = tab.reshape(2, N_KV_HEADS, GQA_GROUP, 2 * blk, blk).transpose(0, 1, 3, 2, 4)
    return tab.reshape(2, N_KV_HEADS, 2 * blk, GQA_GROUP * blk)


def _attn_kernel(qt_ref, kp_ref, kc_ref, vtp_ref, vtc_ref, bias_ref, sink_ref, x_ref, wo_ref, o_ref):
    blk = ATT_BLOCK
    for sb in range(ATT_STEP_BLOCKS):
        tok = slice(sb * blk, (sb + 1) * blk)
        if sb == 0:
            first = (pl.program_id(1) == 0).astype(jnp.int32)
            k_prev = [kp_ref[hk] for hk in range(N_KV_HEADS)]
            vt_prev = vtp_ref[...]
        else:
            first = 0
            k_prev = [kc_ref[hk, (sb - 1) * blk:sb * blk, :] for hk in range(N_KV_HEADS)]
            vt_prev = vtc_ref[:, (sb - 1) * blk:sb * blk]
        vt = jnp.concatenate([vt_prev, vtc_ref[:, tok]], axis=1)
        att_t = []
        for hk in range(N_KV_HEADS):
            keys = jnp.concatenate([k_prev[hk], kc_ref[hk, tok, :]], axis=0)
            q_t = jnp.concatenate(
                [qt_ref[(hk * GQA_GROUP + g) * HEAD_DIM:(hk * GQA_GROUP + g + 1) * HEAD_DIM, tok]
                 for g in range(GQA_GROUP)], axis=1)
            s = jnp.dot(keys, q_t, preferred_element_type=F32) + bias_ref[first, hk]
            sink = sink_ref[hk]
            m = jnp.maximum(jnp.max(s, axis=0, keepdims=True), sink)
            p = jnp.exp2(s - m).astype(BF16)
            pv = jnp.dot(vt[hk * V_EXT:(hk + 1) * V_EXT, :], p, preferred_element_type=F32)
            den = pv[HEAD_DIM:HEAD_DIM + 1, :] + jnp.exp2(sink - m)
            o_t = (pv[:HEAD_DIM, :] * (1.0 / den)).astype(BF16)
            att_t += [o_t[:, g * blk:(g + 1) * blk] for g in range(GQA_GROUP)]
        att_t = jnp.concatenate(att_t, axis=0)
        mix = lax.dot_general(att_t, wo_ref[...], (((0,), (0,)), ((), ())), preferred_element_type=F32)
        o_ref[tok, :] = x_ref[tok, :] + mix


def _attn(qt, k, vt, x2, sinks, wo, bsz, seqlen):
    blk = ATT_BLOCK
    step = ATT_STEP_BLOCKS * blk
    nb = seqlen // blk
    ns = seqlen // step
    qw = N_Q_HEADS * HEAD_DIM
    cols = GQA_GROUP * blk
    cur = lambda b, n: (b * ns + n, 0)
    cur_t = lambda b, n: (0, b * ns + n)
    prev_blk = lambda b, n: b * nb + jnp.maximum(ATT_STEP_BLOCKS * n - 1, 0)
    sink_row = jnp.repeat(sinks.astype(F32) * LOG2E, blk).reshape(N_KV_HEADS, 1, cols)
    return pl.pallas_call(
        _attn_kernel,
        out_shape=jax.ShapeDtypeStruct((bsz * seqlen, D_MODEL), F32),
        grid=(bsz, ns),
        in_specs=[pl.BlockSpec((qw, step), cur_t),
                  pl.BlockSpec((N_KV_HEADS, blk, HEAD_DIM), lambda b, n: (0, prev_blk(b, n), 0)),
                  pl.BlockSpec((N_KV_HEADS, step, HEAD_DIM), lambda b, n: (0, b * ns + n, 0)),
                  pl.BlockSpec((N_KV_HEADS * V_EXT, blk), lambda b, n: (0, prev_blk(b, n))),
                  pl.BlockSpec((N_KV_HEADS * V_EXT, step), cur_t),
                  pl.BlockSpec((2, N_KV_HEADS, 2 * blk, cols), lambda b, n: (0, 0, 0, 0)),
                  pl.BlockSpec((N_KV_HEADS, 1, cols), lambda b, n: (0, 0, 0)),
                  pl.BlockSpec((step, D_MODEL), cur),
                  pl.BlockSpec((qw, D_MODEL), lambda b, n: (0, 0))],
        out_specs=pl.BlockSpec((step, D_MODEL), cur),
        compiler_params=_params("parallel", "parallel"),
        name="odd_attn",
    )(qt, k, k, vt, vt, jnp.asarray(_attn_bias()), sink_row, x2, wo.astype(BF16))


def kernel(x, even_mix_norm, even_in_proj, s5_lambda_re, s5_lambda_im, s5_log_step, s5_b_re, s5_b_im,
           s5_c_re, s5_c_im, s5_d, s5_glu_w, hgrn_lower_bounds, hgrn_o_norm, even_out_proj, odd_mix_norm,
           odd_wqkv, odd_q_norm, odd_k_norm, odd_sinks, odd_out_proj, moe_norm, moe_router_group,
           moe_router_group_bias, moe_router_expert, moe_router_expert_bias, moe_w_gate, moe_w_up,
           moe_w_down):
    bsz, seqlen, dm = x.shape
    n = bsz * seqlen
    x2 = x.reshape(n, dm)
    lower_bounds = jnp.cumsum(jax.nn.softmax(hgrn_lower_bounds.astype(F32), axis=0), axis=0)

    def router_params(layer):
        return (moe_norm[layer], moe_router_group[layer], moe_router_group_bias[layer],
                moe_router_expert[layer], moe_router_expert_bias[layer])

    def moe(xx, routing, layer):
        return _moe(xx, routing, moe_norm[layer], moe_w_gate[layer], moe_w_up[layer], moe_w_down[layer])

    u, h4 = _inproj(x2, even_mix_norm[0], even_in_proj[0].astype(BF16))
    ops = _s5_operators(s5_lambda_re[0], s5_lambda_im[0], s5_log_step[0], s5_b_re[0], s5_b_im[0],
                        s5_c_re[0], s5_c_im[0])
    u_g = _s5_pack(u.reshape(bsz, seqlen, S5_WIDTH))
    y_g = _s5_scan(u_g, ops, bsz)
    ys = _s5_unpack(y_g, bsz, seqlen).reshape(n, S5_WIDTH)
    b_out = _hgrn(h4.reshape(bsz, seqlen, 4 * HG_WIDTH), lower_bounds[0], hgrn_o_norm[0], bsz, seqlen)
    x2, *routing = _evenout(x2, ys, u, b_out.reshape(n, HG_WIDTH), s5_d[0], s5_glu_w[0], even_out_proj[0],
                            router_params(0))
    x2 = moe(x2, routing, 0)

    q, kt, v = _qkv(x2, odd_mix_norm[0], odd_wqkv[0], odd_q_norm[0], odd_k_norm[0])
    x2 = _attn(q, kt, v, x2, odd_sinks[0], odd_out_proj[0], bsz, seqlen)
    x2 = moe(x2, _router(x2, router_params(1)), 1)
    return x2.reshape(bsz, seqlen, dm)
```

```python
import functools
import math

import jax
import jax.numpy as jnp
import numpy as np
from jax import lax
from jax.experimental import pallas as pl
from jax.experimental.pallas import tpu as pltpu

F32 = jnp.float32
BF16 = jnp.bfloat16
EPS = 1e-6

D_MODEL = 1024
S5_WIDTH = 512
S5_GROUP = 16
S5_GROUPS = 32
S5_STATE = 64
S5_CHUNK = 16
HG_WIDTH = 512
HG_HEAD_DIM = 128
HG_HEADS = 4
HG_CHUNK = 32
HEAD_DIM = 64
N_Q_HEADS = 16
N_KV_HEADS = 2
GQA_GROUP = 8
KV_WIDTH = N_KV_HEADS * HEAD_DIM
ATT_BLOCK = 128
ATT_STEP_BLOCKS = 4
N_GROUPS = 4
EXPERTS_PER_GROUP = 4
N_EXPERTS = 16
D_EXPERT = 256
ROUTER_ROWS = 32
N_PAIRS = 6
N_BUCKETS = N_GROUPS * N_PAIRS
BUCKET_ROWS = 32
MOE_TILE = 512
ROW_TILE = (8, 128)
ROW_TILE_SHIFT = ROW_TILE[0].bit_length() - 1
X_TILES = D_MODEL // ROW_TILE[1]
XS_TILES = X_TILES + 1

VMEM_LIMIT_BYTES = 56 * 1024 * 1024


def _params(*semantics):
    return pltpu.CompilerParams(dimension_semantics=semantics, vmem_limit_bytes=VMEM_LIMIT_BYTES)


def _rms(xf, gain):
    return xf * lax.rsqrt(jnp.mean(xf * xf, axis=-1, keepdims=True) + EPS) * gain


def _nt_dot(w_t, h):
    return lax.dot_general(w_t, h, (((1,), (1,)), ((), ())), preferred_element_type=F32)


def _sigmoid(x):
    return 0.5 * jnp.tanh(0.5 * x) + 0.5


def _silu(x):
    return x * _sigmoid(x)


def _inproj_kernel(x_ref, g_ref, w_ref, u_ref, h4_ref):
    h = _rms(x_ref[...], g_ref[...]).astype(BF16)
    p = jnp.dot(h, w_ref[...], preferred_element_type=F32)
    u_ref[...] = p[:, :S5_WIDTH]
    h4_ref[...] = p[:, S5_WIDTH:]


def _inproj(x2, gain, w_bf16, tm=512):
    n = x2.shape[0]
    e_in = w_bf16.shape[1]
    return pl.pallas_call(
        _inproj_kernel,
        out_shape=(jax.ShapeDtypeStruct((n, S5_WIDTH), F32),
                   jax.ShapeDtypeStruct((n, e_in - S5_WIDTH), F32)),
        grid=(n // tm,),
        in_specs=[pl.BlockSpec((tm, D_MODEL), lambda i: (i, 0)),
                  pl.BlockSpec((1, D_MODEL), lambda i: (0, 0)),
                  pl.BlockSpec((D_MODEL, e_in), lambda i: (0, 0))],
        out_specs=(pl.BlockSpec((tm, S5_WIDTH), lambda i: (i, 0)),
                   pl.BlockSpec((tm, e_in - S5_WIDTH), lambda i: (i, 0))),
        compiler_params=_params("parallel"),
        name="even_inproj",
    )(x2, gain.reshape(1, D_MODEL), w_bf16)


def _s5_toeplitz_kernel(ca_ref, bbt_ref, mt_ref):
    for gi in range(ca_ref.shape[0]):
        kt = lax.dot_general(bbt_ref[gi], ca_ref[gi], (((1,), (1,)), ((), ())), preferred_element_type=F32,
                             precision=lax.Precision.HIGHEST)
        width = kt.shape[1]
        blocks = [kt] + [jnp.concatenate([jnp.zeros((S5_GROUP, s * S5_GROUP), F32),
                                          kt[:, :width - s * S5_GROUP]], axis=1) for s in range(1, S5_CHUNK)]
        mt_ref[gi] = jnp.concatenate(blocks, axis=0).astype(BF16)


def _s5_toeplitz(ca, bbt, groups_per_step=8):
    g, rows, k = ca.shape
    gs = groups_per_step
    return pl.pallas_call(
        _s5_toeplitz_kernel,
        out_shape=jax.ShapeDtypeStruct((g, rows, rows), BF16),
        grid=(g // gs,),
        in_specs=[pl.BlockSpec((gs, rows, k), lambda i: (i, 0, 0)),
                  pl.BlockSpec((gs, S5_GROUP, k), lambda i: (i, 0, 0))],
        out_specs=pl.BlockSpec((gs, rows, rows), lambda i: (i, 0, 0)),
        compiler_params=_params("parallel"),
        name="s5_toeplitz",
    )(ca, bbt)


def _s5_operators(lam_re, lam_im, log_step, b_re, b_im, c_re, c_im):
    t = S5_CHUNK
    lr, li = lam_re.astype(F32), lam_im.astype(F32)
    step = jnp.exp(log_step.astype(F32))[:, None]
    mag = jnp.exp(lr * step)
    ab_re = mag * jnp.cos(li * step)
    ab_im = mag * jnp.sin(li * step)
    den = lr * lr + li * li
    nr, ni = ab_re - 1.0, ab_im
    z_re = (nr * lr + ni * li) / den
    z_im = (ni * lr - nr * li) / den
    br, bi = b_re.astype(F32), b_im.astype(F32)
    bb_re = z_re[..., None] * br - z_im[..., None] * bi
    bb_im = z_re[..., None] * bi + z_im[..., None] * br
    kk = jnp.arange(t + 1, dtype=F32)[:, None, None]
    pmag = jnp.exp(kk * (lr * step)[None])
    pw_re = pmag * jnp.cos(kk * (li * step)[None])
    pw_im = pmag * jnp.sin(kk * (li * step)[None])
    cr = jnp.transpose(c_re.astype(F32), (0, 1, 2))
    ci = c_im.astype(F32)
    ca_re = cr[None] * pw_re[:, :, None, :] - ci[None] * pw_im[:, :, None, :]
    ca_im = cr[None] * pw_im[:, :, None, :] + ci[None] * pw_re[:, :, None, :]
    g = lr.shape[0]
    ca_cat = jnp.concatenate([ca_re[:t], -ca_im[:t]], axis=-1)
    ca_cat = jnp.transpose(ca_cat, (1, 0, 2, 3)).reshape(g, t * S5_GROUP, 2 * S5_STATE)
    bb_cat_t = jnp.concatenate([bb_re, bb_im], axis=1).transpose(0, 2, 1)
    mt = _s5_toeplitz(ca_cat, bb_cat_t)
    pr = pw_re[:t][::-1]
    pi = pw_im[:t][::-1]
    sb_re = pr[:, :, :, None] * bb_re[None] - pi[:, :, :, None] * bb_im[None]
    sb_im = pr[:, :, :, None] * bb_im[None] + pi[:, :, :, None] * bb_re[None]
    sb_re = jnp.transpose(sb_re, (1, 0, 3, 2)).reshape(g, t * S5_GROUP, S5_STATE)
    sb_im = jnp.transpose(sb_im, (1, 0, 3, 2)).reshape(g, t * S5_GROUP, S5_STATE)
    cp_re = jnp.transpose(ca_re[1:], (1, 3, 0, 2)).reshape(g, S5_STATE, t * S5_GROUP)
    cp_im = jnp.transpose(-ca_im[1:], (1, 3, 0, 2)).reshape(g, S5_STATE, t * S5_GROUP)

    def pair_rows(m):
        m = m.reshape(g // 2, 2, m.shape[1], m.shape[2])
        z = jnp.zeros_like(m[:, 0])
        top = jnp.concatenate([m[:, 0], z], axis=2)
        bot = jnp.concatenate([z, m[:, 1]], axis=2)
        return jnp.concatenate([top, bot], axis=1)

    at_re = pw_re[t].reshape(g // 2, 1, 2 * S5_STATE)
    at_im = pw_im[t].reshape(g // 2, 1, 2 * S5_STATE)
    return (mt, pair_rows(sb_re).astype(BF16), pair_rows(sb_im).astype(BF16),
            pair_rows(cp_re).astype(BF16), pair_rows(cp_im).astype(BF16), at_re, at_im)


PACK_TOKENS = 512


LANES = 128
GROUPS_PER_TILE = LANES // S5_GROUP
TOKENS_PER_TILE = LANES // S5_GROUP
CHUNK_HALVES = S5_CHUNK // TOKENS_PER_TILE
PACK_CHUNKS = PACK_TOKENS // S5_CHUNK


def _block_swap_matrix():
    a, b, h = np.meshgrid(np.arange(TOKENS_PER_TILE), np.arange(GROUPS_PER_TILE), np.arange(S5_GROUP),
                          indexing="ij")
    src = (a * GROUPS_PER_TILE + b) * S5_GROUP + h
    dst = (b * TOKENS_PER_TILE + a) * S5_GROUP + h
    m = np.zeros((src.size, src.size), np.float32)
    m[src.ravel(), dst.ravel()] = 1.0
    return jnp.asarray(m, dtype=BF16)


def _s5_pack_kernel(u_ref, swap_ref, o_ref, *, bsz):
    for j in range(CHUNK_HALVES):
        rows = [jnp.concatenate([u_ref[b, pl.ds(j * TOKENS_PER_TILE + tt, PACK_CHUNKS, stride=S5_CHUNK), :]
                                 for tt in range(TOKENS_PER_TILE)], axis=1) for b in range(bsz)]
        lhs = jnp.concatenate(rows, axis=0).astype(BF16)
        out = jnp.dot(lhs, swap_ref[...], preferred_element_type=F32)
        for g in range(GROUPS_PER_TILE):
            for b in range(bsz):
                o_ref[g * CHUNK_HALVES + j, pl.ds(b, PACK_CHUNKS, stride=bsz), :] = (
                    out[b * PACK_CHUNKS:(b + 1) * PACK_CHUNKS, g * LANES:(g + 1) * LANES])


def _s5_pack(u3):
    bsz, seqlen, w = u3.shape
    rows = PACK_CHUNKS * bsz
    swap = _block_swap_matrix()
    return pl.pallas_call(
        functools.partial(_s5_pack_kernel, bsz=bsz),
        out_shape=jax.ShapeDtypeStruct((S5_GROUPS * CHUNK_HALVES, seqlen // S5_CHUNK * bsz, LANES), F32),
        grid=(seqlen // PACK_TOKENS, w // LANES),
        in_specs=[pl.BlockSpec((bsz, PACK_TOKENS, LANES), lambda i, k: (0, i, k)),
                  pl.BlockSpec(swap.shape, lambda i, k: (0, 0))],
        out_specs=pl.BlockSpec((GROUPS_PER_TILE * CHUNK_HALVES, rows, LANES), lambda i, k: (k, i, 0)),
        compiler_params=_params("parallel", "parallel"),
        name="s5_pack",
    )(u3, swap)


def _s5_unpack_kernel(y_ref, swap_ref, o_ref, *, bsz):
    for j in range(CHUNK_HALVES):
        rows = [jnp.concatenate([y_ref[g * CHUNK_HALVES + j, pl.ds(b, PACK_CHUNKS, stride=bsz), :]
                                 for g in range(GROUPS_PER_TILE)], axis=1) for b in range(bsz)]
        lhs = jnp.concatenate(rows, axis=0).astype(BF16)
        out = jnp.dot(lhs, swap_ref[...], preferred_element_type=F32)
        for tt in range(TOKENS_PER_TILE):
            for b in range(bsz):
                o_ref[b, pl.ds(j * TOKENS_PER_TILE + tt, PACK_CHUNKS, stride=S5_CHUNK), :] = (
                    out[b * PACK_CHUNKS:(b + 1) * PACK_CHUNKS, tt * LANES:(tt + 1) * LANES])


def _s5_unpack(y_g, bsz, seqlen):
    rows = PACK_CHUNKS * bsz
    swap = _block_swap_matrix()
    return pl.pallas_call(
        functools.partial(_s5_unpack_kernel, bsz=bsz),
        out_shape=jax.ShapeDtypeStruct((bsz, seqlen, S5_WIDTH), F32),
        grid=(seqlen // PACK_TOKENS, S5_WIDTH // LANES),
        in_specs=[pl.BlockSpec((GROUPS_PER_TILE * CHUNK_HALVES, rows, LANES), lambda i, k: (k, i, 0)),
                  pl.BlockSpec(swap.shape, lambda i, k: (0, 0))],
        out_specs=pl.BlockSpec((bsz, PACK_TOKENS, LANES), lambda i, k: (0, i, k)),
        compiler_params=_params("parallel", "parallel"),
        name="s5_unpack",
    )(y_g, swap)


def _s5_kernel(u_ref, mt_ref, wre_ref, wim_ref, cre_ref, cim_ref, atr_ref, ati_ref, y_ref,
               sre_ref, sim_ref, xre_ref, xim_ref, *, n_chunks, bsz):
    ucat = jnp.concatenate([u_ref[i] for i in range(2 * CHUNK_HALVES)], axis=1).astype(BF16)
    w = S5_CHUNK * S5_GROUP
    u0 = ucat[:, :w]
    u1 = ucat[:, w:]
    sre_ref[...] = jnp.dot(ucat, wre_ref[0], preferred_element_type=F32)
    sim_ref[...] = jnp.dot(ucat, wim_ref[0], preferred_element_type=F32)
    atr = jnp.broadcast_to(atr_ref[0], (bsz, 2 * S5_STATE))
    ati = jnp.broadcast_to(ati_ref[0], (bsz, 2 * S5_STATE))

    def body(c, carry):
        xr, xi = carry
        rows = pl.ds(pl.multiple_of(c * bsz, bsz), bsz)
        xre_ref[rows, :] = xr
        xim_ref[rows, :] = xi
        nxr = atr * xr - ati * xi + sre_ref[rows, :]
        nxi = atr * xi + ati * xr + sim_ref[rows, :]
        return nxr, nxi

    zero = jnp.zeros((bsz, 2 * S5_STATE), F32)
    lax.fori_loop(0, n_chunks, body, (zero, zero))
    ycar = (jnp.dot(xre_ref[...].astype(BF16), cre_ref[0], preferred_element_type=F32)
            + jnp.dot(xim_ref[...].astype(BF16), cim_ref[0], preferred_element_type=F32))
    y0 = jnp.dot(u0, mt_ref[0], preferred_element_type=F32) + ycar[:, :w]
    y1 = jnp.dot(u1, mt_ref[1], preferred_element_type=F32) + ycar[:, w:]
    for i in range(CHUNK_HALVES):
        y_ref[i] = y0[:, i * LANES:(i + 1) * LANES]
        y_ref[CHUNK_HALVES + i] = y1[:, i * LANES:(i + 1) * LANES]


def _s5_scan(u_g, ops, bsz):
    mt, wre, wim, cre, cim, atr, ati = ops
    tiles, r, _ = u_g.shape
    g = tiles // CHUNK_HALVES
    w = S5_CHUNK * S5_GROUP
    n_chunks = r // bsz
    p2 = 2 * S5_STATE
    kern = functools.partial(_s5_kernel, n_chunks=n_chunks, bsz=bsz)
    pair_tiles = pl.BlockSpec((2 * CHUNK_HALVES, r, LANES), lambda i: (i, 0, 0))
    return pl.pallas_call(
        kern,
        out_shape=jax.ShapeDtypeStruct((tiles, r, LANES), F32),
        grid=(g // 2,),
        in_specs=[pair_tiles,
                  pl.BlockSpec((2, w, w), lambda i: (i, 0, 0)),
                  pl.BlockSpec((1, 2 * w, p2), lambda i: (i, 0, 0)),
                  pl.BlockSpec((1, 2 * w, p2), lambda i: (i, 0, 0)),
                  pl.BlockSpec((1, p2, 2 * w), lambda i: (i, 0, 0)),
                  pl.BlockSpec((1, p2, 2 * w), lambda i: (i, 0, 0)),
                  pl.BlockSpec((1, 1, p2), lambda i: (i, 0, 0)),
                  pl.BlockSpec((1, 1, p2), lambda i: (i, 0, 0))],
        out_specs=pair_tiles,
        scratch_shapes=[pltpu.VMEM((r, p2), F32)] * 4,
        compiler_params=_params("parallel"),
        name="s5_scan",
    )(u_g, mt, wre, wim, cre, cim, atr, ati)


def _hgrn_kernel(q_ref, f_ref, i_ref, g_ref, lb_ref, og_ref, o_ref, *st_refs, seqlen):
    d = HG_HEAD_DIM
    for hh, st_ref in enumerate(st_refs):
        lanes = slice(hh * d, (hh + 1) * d)
        o_ref[0, :, lanes] = _hgrn_head(q_ref[0, :, lanes], f_ref[0, :, lanes], i_ref[0, :, lanes],
                                        g_ref[0, :, lanes], lb_ref[:, lanes], og_ref[...], st_ref, seqlen)


def _hgrn_head(q, f_logit, v, g, lb, o_gain, st_ref, seqlen):
    c = HG_CHUNK
    nc = seqlen // c
    d = HG_HEAD_DIM
    qs = _silu(q)
    f = lb + (1.0 - lb) * _sigmoid(f_logit)
    lf = jnp.log(f)
    k = 1.0 - f
    sub = ROW_TILE[0]
    b8 = lf.reshape(seqlen // sub, sub, d)
    row = lax.broadcasted_iota(jnp.int32, b8.shape, 1)
    sh = 1
    while sh < sub:
        b8 = b8 + jnp.where(row >= sh, pltpu.roll(b8, sh, axis=1), 0.0)
        sh *= 2
    b4 = b8.reshape(nc, c // sub, sub, d)
    groups, run = [], None
    for gi in range(c // sub):
        grp = b4[:, gi]
        groups.append(grp if run is None else grp + run)
        total = grp[:, sub - 1:sub, :]
        run = total if run is None else run + total
    b3 = jnp.concatenate(groups, axis=1)
    b_last = b3[:, c - 1:c, :]
    b_ref = b3[:, c // 2 - 1:c // 2, :]
    qs3 = qs.reshape(nc, c, d)
    k3 = k.reshape(nc, c, d)
    v3 = v.reshape(nc, c, d).astype(BF16)
    qe_f = qs3 * jnp.exp(b3 - b_ref)
    ke_f = k3 * jnp.exp(b_ref - b3)
    qe = qe_f.astype(BF16)
    ke = ke_f.astype(BF16)
    kd = (ke_f * jnp.exp(b_last - b_ref)).astype(BF16)
    qb = (qe_f * jnp.exp(b_ref)).astype(BF16)
    scores = jnp.einsum('ctd,csd->cts', qe, ke, preferred_element_type=F32)
    ti = lax.broadcasted_iota(jnp.int32, (c, c), 0)
    si = lax.broadcasted_iota(jnp.int32, (c, c), 1)
    scores = jnp.where((ti >= si)[None], scores, 0.0)
    o_intra = jnp.einsum('cts,csv->ctv', scores.astype(BF16), v3, preferred_element_type=F32)
    ut = jnp.einsum('csv,csd->cvd', v3, kd, preferred_element_type=F32)
    decay = jnp.exp(b_last)
    state = jnp.zeros((d, d), F32)
    for ci in range(nc):
        st_ref[ci] = state.astype(BF16)
        state = decay[ci] * state + ut[ci]
    o_inter = jnp.einsum('ctd,cvd->ctv', qb, st_ref[...], preferred_element_type=F32)
    o = (o_intra + o_inter).reshape(seqlen, d)
    o = _rms(o, o_gain)
    return (o * _silu(g)).astype(BF16)


def _hgrn(h4, lower_bound, o_gain, bsz, seqlen, heads_per_step=2):
    d = HG_HEAD_DIM
    w = heads_per_step * d
    steps = HG_HEADS // heads_per_step
    kern = functools.partial(_hgrn_kernel, seqlen=seqlen)

    def col(part):
        return pl.BlockSpec((1, seqlen, w), lambda b, h: (b, 0, part * steps + h))

    return pl.pallas_call(
        kern,
        out_shape=jax.ShapeDtypeStruct((bsz, seqlen, HG_WIDTH), BF16),
        grid=(bsz, steps),
        in_specs=[col(0), col(1), col(2), col(3),
                  pl.BlockSpec((1, w), lambda b, h: (0, h)),
                  pl.BlockSpec((1, d), lambda b, h: (0, 0))],
        out_specs=pl.BlockSpec((1, seqlen, w), lambda b, h: (b, 0, h)),
        scratch_shapes=[pltpu.VMEM((seqlen // HG_CHUNK, d, d), BF16)] * heads_per_step,
        compiler_params=_params("parallel", "parallel"),
        name="hgrn2",
    )(h4, h4, h4, h4, lower_bound.reshape(1, HG_WIDTH), o_gain.reshape(1, d))


def _evenout_kernel(x_ref, ys_ref, u_ref, b_ref, d_ref, wglu_ref, wa_ref, wb_ref, *rest):
    route_in, (o_ref, *route_out), route_scratch = rest[:4], rest[4:8], rest[8:]
    y = ys_ref[...] + d_ref[...] * u_ref[...]
    y = jax.nn.gelu(y)
    gate = _sigmoid(jnp.dot(y.astype(BF16), wglu_ref[...], preferred_element_type=F32))
    a = (y * gate).astype(BF16)
    mix = (jnp.dot(a, wa_ref[...], preferred_element_type=F32)
           + jnp.dot(b_ref[...], wb_ref[...], preferred_element_type=F32))
    x_new = x_ref[...] + mix
    o_ref[...] = x_new
    _route(x_new, pl.program_id(0) == 0, *route_in, *route_out, *route_scratch)


def _evenout(x2, ys, u, b_out, d_skip, wglu, wout, router_params, tm=512):
    n = x2.shape[0]
    route = _RouterPlumbing(n, tm, lambda i: i, *router_params)
    row = lambda w: pl.BlockSpec((tm, w), lambda i: (i, 0))
    full = lambda r, c: pl.BlockSpec((r, c), lambda i: (0, 0))
    return pl.pallas_call(
        _evenout_kernel,
        out_shape=(jax.ShapeDtypeStruct((n, D_MODEL), F32), *route.out_shape),
        grid=(n // tm,),
        in_specs=[row(D_MODEL), row(S5_WIDTH), row(S5_WIDTH), row(HG_WIDTH),
                  full(1, S5_WIDTH), full(S5_WIDTH, S5_WIDTH),
                  full(S5_WIDTH, D_MODEL), full(HG_WIDTH, D_MODEL), *route.in_specs],
        out_specs=(row(D_MODEL), *route.out_specs),
        scratch_shapes=route.scratch_shapes,
        compiler_params=_params("arbitrary"),
        name="even_out",
    )(x2, ys, u, b_out, d_skip.reshape(1, S5_WIDTH), wglu.astype(BF16),
      wout[:S5_WIDTH].astype(BF16), wout[S5_WIDTH:].astype(BF16), *route.operands)


def _route(x, is_first_step, g_ref, wr_ref, br_ref, tri_ref, idx_ref, wts_ref, cnt_ref, run_ref):
    @pl.when(is_first_step)
    def _():
        run_ref[...] = jnp.zeros_like(run_ref)

    h = _rms(x, g_ref[...])
    h_hi = h.astype(BF16)
    h_lo = (h - h_hi.astype(F32)).astype(BF16)
    both = _nt_dot(wr_ref[...], h_hi)
    lt = (both[:ROUTER_ROWS] + both[ROUTER_ROWS:] + _nt_dot(wr_ref[:ROUTER_ROWS, :], h_lo)
          + br_ref[...])
    gl = [lt[i:i + 1] for i in range(N_GROUPS)]
    el = [lt[N_GROUPS + i:N_GROUPS + i + 1] for i in range(N_EXPERTS)]
    gmax = jnp.maximum(jnp.maximum(gl[0], gl[1]), jnp.maximum(gl[2], gl[3]))
    g_idx = jnp.where(gl[0] == gmax, 0, jnp.where(gl[1] == gmax, 1, jnp.where(gl[2] == gmax, 2, 3)))
    g_gate = 1.0 / (jnp.exp(gl[0] - gmax) + jnp.exp(gl[1] - gmax) + jnp.exp(gl[2] - gmax) + jnp.exp(gl[3] - gmax))
    es = []
    for j in range(EXPERTS_PER_GROUP):
        es.append(jnp.where(g_idx == 0, el[j],
                            jnp.where(g_idx == 1, el[4 + j],
                                      jnp.where(g_idx == 2, el[8 + j], el[12 + j]))))
    e1 = jnp.maximum(jnp.maximum(es[0], es[1]), jnp.maximum(es[2], es[3]))
    i1 = jnp.where(es[0] == e1, 0, jnp.where(es[1] == e1, 1, jnp.where(es[2] == e1, 2, 3)))
    rest = [jnp.where(i1 == j, -jnp.inf, es[j]) for j in range(EXPERTS_PER_GROUP)]
    e2 = jnp.maximum(jnp.maximum(rest[0], rest[1]), jnp.maximum(rest[2], rest[3]))
    i2 = jnp.where(rest[0] == e2, 0, jnp.where(rest[1] == e2, 1, jnp.where(rest[2] == e2, 2, 3)))
    r = jnp.exp(e2 - e1)
    w1 = g_gate / (1.0 + r)
    w2 = w1 * r
    first_lo = i1 < i2
    lo = jnp.where(first_lo, i1, i2)
    hi = jnp.where(first_lo, i2, i1)
    w_lo = jnp.where(first_lo, w1, w2)
    w_hi = jnp.where(first_lo, w2, w1)
    pair = jnp.where(lo == 0, 0, jnp.where(lo == 1, 3, 5)) + hi - lo - 1
    bucket = g_idx * N_PAIRS + pair
    tm = bucket.shape[1]
    rowid = lax.broadcasted_iota(jnp.int32, (BUCKET_ROWS, tm), 0)
    onehot = (rowid == bucket).astype(F32)
    prefix = jnp.dot(onehot.astype(BF16), tri_ref[...], preferred_element_type=F32)
    run = run_ref[...]
    rank = jnp.sum(onehot * (prefix + run), axis=0, keepdims=True)
    run = run + jnp.sum(onehot, axis=1, keepdims=True)
    run_ref[...] = run
    cnt_ref[...] = jnp.broadcast_to(run, cnt_ref.shape)
    idx_ref[...] = jnp.concatenate([bucket, rank.astype(jnp.int32), jnp.zeros((6, tm), jnp.int32)], axis=0)
    wts_ref[...] = jnp.concatenate([w_lo, w_hi, jnp.zeros((6, tm), F32)], axis=0)


class _RouterPlumbing:
    def __init__(self, n, tm, tile_index, gain, w_rg, b_rg, w_re, b_re):
        wr = jnp.concatenate([w_rg, w_re], axis=1).astype(F32).T
        wr = jnp.pad(wr, ((0, ROUTER_ROWS - wr.shape[0]), (0, 0)))
        wr_hi = wr.astype(BF16)
        wr = jnp.concatenate([wr_hi, (wr - wr_hi.astype(F32)).astype(BF16)], axis=0)
        br = jnp.pad(jnp.concatenate([b_rg, b_re]).astype(F32), (0, ROUTER_ROWS - N_GROUPS - N_EXPERTS))
        tri = (np.arange(tm)[:, None] < np.arange(tm)[None, :]).astype(np.float32)
        const = lambda *_: (0, 0)
        self.operands = (gain.reshape(1, D_MODEL), wr, br.reshape(ROUTER_ROWS, 1), jnp.asarray(tri, dtype=BF16))
        self.in_specs = [pl.BlockSpec((1, D_MODEL), const),
                         pl.BlockSpec((2 * ROUTER_ROWS, D_MODEL), const),
                         pl.BlockSpec((ROUTER_ROWS, 1), const),
                         pl.BlockSpec((tm, tm), const)]
        self.out_shape = (jax.ShapeDtypeStruct((8, n), jnp.int32),
                          jax.ShapeDtypeStruct((8, n), F32),
                          jax.ShapeDtypeStruct((BUCKET_ROWS, LANES), F32))
        self.out_specs = (pl.BlockSpec((8, tm), lambda *g: (0, tile_index(*g))),
                          pl.BlockSpec((8, tm), lambda *g: (0, tile_index(*g))),
                          pl.BlockSpec((BUCKET_ROWS, LANES), const))
        self.scratch_shapes = [pltpu.VMEM((BUCKET_ROWS, 1), F32)]


def _router_kernel(x_ref, *route_refs):
    _route(x_ref[...], pl.program_id(0) == 0, *route_refs)


def _router(x2, router_params, tm=512):
    n = x2.shape[0]
    route = _RouterPlumbing(n, tm, lambda i: i, *router_params)
    return pl.pallas_call(
        _router_kernel,
        out_shape=route.out_shape,
        grid=(n // tm,),
        in_specs=[pl.BlockSpec((tm, D_MODEL), lambda i: (i, 0)), *route.in_specs],
        out_specs=route.out_specs,
        scratch_shapes=route.scratch_shapes,
        compiler_params=_params("arbitrary"),
        name="moe_router",
    )(x2, *route.operands)


ROW_COPY_UNROLL = 8


def _start_row_copies(idx_ref, n_rows, copy_for_row, prepare_rows=None):
    def start_group(base):
        for j in range(ROW_COPY_UNROLL):
            copy_for_row(base + j, idx_ref[0, 0, base + j]).start(priority=j % 2)

    n_groups = n_rows // ROW_COPY_UNROLL
    if prepare_rows is None:
        def body(g, carry):
            start_group(pl.multiple_of(g * ROW_COPY_UNROLL, ROW_COPY_UNROLL))
            return carry

        lax.fori_loop(0, n_groups, body, 0)
        return

    prepare_rows(0)

    def body(g, carry):
        base = pl.multiple_of(g * ROW_COPY_UNROLL, ROW_COPY_UNROLL)
        prepare_rows(base + ROW_COPY_UNROLL)
        start_group(base)
        return carry

    lax.fori_loop(0, n_groups - 1, body, 0)
    start_group(n_rows - ROW_COPY_UNROLL)


def _row_slab(view_ref, p):
    return view_ref.at[p >> ROW_TILE_SHIFT, :, p & (ROW_TILE[0] - 1)]


def _view_columns(view_ref, n_cols):
    rows = view_ref.shape[0] * ROW_TILE[0]
    return jnp.concatenate([view_ref[:, c].reshape(rows, LANES) for c in range(n_cols)], axis=1)


def _rows_to_tiles(x):
    rows = x.shape[0]
    return x.reshape(rows * ROW_TILE[0], ROW_TILE[1]).reshape(rows, *ROW_TILE)


def _tiles_to_rows(x3):
    rows = x3.shape[0]
    return x3.reshape(rows * ROW_TILE[0], ROW_TILE[1]).reshape(rows, D_MODEL)


def _dispatch_kernel(tail_blk_ref, tail_on_ref, pos_ref, x_ref, w_ref, xs_ref, buf_ref, zero_ref, wcol_ref, sem,
                     *, tile):
    tm = x_ref.shape[0]
    tile_blks = tile // ROW_TILE[0]

    @pl.when(pl.program_id(0) == 0)
    def _():
        zero_ref[...] = jnp.zeros_like(zero_ref)

        def zero_copy(k):
            blk = pl.multiple_of(tail_blk_ref[k], tile_blks)
            return pltpu.make_async_copy(zero_ref, xs_ref.at[pl.ds(blk, tile_blks)], sem)

        for k in range(2 * N_BUCKETS):
            pl.when(tail_on_ref[k] > 0)(lambda k=k: zero_copy(k).start())
        for k in range(2 * N_BUCKETS):
            pl.when(tail_on_ref[k] > 0)(lambda k=k: zero_copy(k).wait())

    wpad = jnp.concatenate([w_ref[...], jnp.zeros((LANES - w_ref.shape[0], tm), F32)], axis=0)
    wcol_ref[...] = wpad.T

    def stage(base):
        rows = pl.ds(base, ROW_COPY_UNROLL)
        buf_ref[rows, :X_TILES, :] = _rows_to_tiles(x_ref[rows, :])
        buf_ref[rows, X_TILES, :] = wcol_ref[rows, :]

    _start_row_copies(pos_ref, tm, lambda r, p: pltpu.make_async_copy(buf_ref.at[r], _row_slab(xs_ref, p), sem),
                      prepare_rows=stage)
    done = xs_ref.at[pl.ds(0, tm // ROW_TILE[0])]
    pltpu.make_async_copy(done, done, sem).wait()


def _dispatch(x2, wts, pos3, tails, n_rows_sorted, tile, tm):
    n = x2.shape[0]
    tail_blk, tail_on = tails
    grid_spec = pltpu.PrefetchScalarGridSpec(
        num_scalar_prefetch=2,
        grid=(n // tm,),
        in_specs=[pl.BlockSpec((1, 1, tm), lambda i, *_: (i, 0, 0), memory_space=pltpu.SMEM),
                  pl.BlockSpec((tm, D_MODEL), lambda i, *_: (i, 0)),
                  pl.BlockSpec((8, tm), lambda i, *_: (0, i))],
        out_specs=pl.BlockSpec(memory_space=pl.ANY),
        scratch_shapes=[pltpu.VMEM((tm, XS_TILES, LANES), F32),
                        pltpu.VMEM((tile // ROW_TILE[0], XS_TILES, *ROW_TILE), F32),
                        pltpu.VMEM((tm, LANES), F32),
                        pltpu.SemaphoreType.DMA],
    )
    return pl.pallas_call(
        functools.partial(_dispatch_kernel, tile=tile),
        out_shape=jax.ShapeDtypeStruct((n_rows_sorted // ROW_TILE[0], XS_TILES, *ROW_TILE), F32),
        grid_spec=grid_spec,
        compiler_params=_params("arbitrary"),
        name="moe_dispatch",
    )(tail_blk, tail_on, pos3, x2, wts)


def _experts_kernel(elo_ref, ehi_ref, nvalid_ref, xs_ref, g_ref, wg_lo, wu_lo, wg_hi, wu_hi,
                    wd_lo, wd_hi, o_ref):
    del elo_ref, ehi_ref
    t = pl.program_id(0)

    @pl.when(t < nvalid_ref[0])
    def _():
        rows = xs_ref.shape[0] * ROW_TILE[0]
        xt = _view_columns(xs_ref, X_TILES)
        h = _rms(xt, g_ref[...]).astype(BF16)
        extra = xs_ref[:, X_TILES].reshape(rows, LANES)
        w_lo = extra[:, 0:1]
        w_hi = extra[:, 1:2]

        def expert(wg, wu, wd, w):
            gate = jnp.dot(h, wg[0], preferred_element_type=F32)
            up = jnp.dot(h, wu[0], preferred_element_type=F32)
            hid = (_silu(gate) * up * w).astype(BF16)
            return jnp.dot(hid, wd[0], preferred_element_type=F32)

        out = xt + expert(wg_lo, wu_lo, wd_lo, w_lo) + expert(wg_hi, wu_hi, wd_hi, w_hi)
        for c in range(X_TILES):
            o_ref[:, c] = out[:, c * LANES:(c + 1) * LANES].reshape(o_ref.shape[0], *ROW_TILE)

    @pl.when(t >= nvalid_ref[0])
    def _():
        o_ref[...] = jnp.zeros_like(o_ref)


def _experts(xs, gain, tables, wg, wu, wd, n_tiles, t):
    elo, ehi, nvalid = tables
    blks = t // ROW_TILE[0]
    row = lambda i, elo, ehi, nv: (i, 0, 0, 0)
    row_in = lambda i, elo, ehi, nv: (jnp.minimum(i, nv[0] - 1), 0, 0, 0)
    lo3 = lambda i, elo, ehi, nv: (elo[i], 0, 0)
    hi3 = lambda i, elo, ehi, nv: (ehi[i], 0, 0)
    grid_spec = pltpu.PrefetchScalarGridSpec(
        num_scalar_prefetch=3,
        grid=(n_tiles,),
        in_specs=[pl.BlockSpec((blks, XS_TILES, *ROW_TILE), row_in),
                  pl.BlockSpec((1, D_MODEL), lambda i, *_: (0, 0)),
                  pl.BlockSpec((1, D_MODEL, D_EXPERT), lo3),
                  pl.BlockSpec((1, D_MODEL, D_EXPERT), lo3),
                  pl.BlockSpec((1, D_MODEL, D_EXPERT), hi3),
                  pl.BlockSpec((1, D_MODEL, D_EXPERT), hi3),
                  pl.BlockSpec((1, D_EXPERT, D_MODEL), lo3),
                  pl.BlockSpec((1, D_EXPERT, D_MODEL), hi3)],
        out_specs=pl.BlockSpec((blks, X_TILES, *ROW_TILE), row),
    )
    return pl.pallas_call(
        _experts_kernel,
        out_shape=jax.ShapeDtypeStruct((xs.shape[0], X_TILES, *ROW_TILE), F32),
        grid_spec=grid_spec,
        compiler_params=_params("arbitrary"),
        name="moe_experts",
    )(elo, ehi, nvalid, xs, gain.reshape(1, D_MODEL), wg, wu, wg, wu, wd, wd)


def _combine_kernel(pos_ref, ys_ref, o_ref, buf_ref, sem):
    tm = o_ref.shape[0]
    _start_row_copies(pos_ref, tm, lambda r, p: pltpu.make_async_copy(_row_slab(ys_ref, p), buf_ref.at[r], sem))
    done = ys_ref.at[pl.ds(0, tm // ROW_TILE[0])]
    pltpu.make_async_copy(done, done, sem).wait()
    o_ref[...] = _tiles_to_rows(buf_ref[...])


def _combine(ys, pos3, n, tm):
    return pl.pallas_call(
        _combine_kernel,
        out_shape=jax.ShapeDtypeStruct((n, D_MODEL), F32),
        grid=(n // tm,),
        in_specs=[pl.BlockSpec((1, 1, tm), lambda i: (i, 0, 0), memory_space=pltpu.SMEM),
                  pl.BlockSpec(memory_space=pl.ANY)],
        out_specs=pl.BlockSpec((tm, D_MODEL), lambda i: (i, 0)),
        scratch_shapes=[pltpu.VMEM((tm, *ROW_TILE), F32), pltpu.SemaphoreType.DMA],
        compiler_params=_params("arbitrary"),
        name="moe_combine",
    )(pos3, ys)


def _moe_tables(idx, cnt, n_tiles, t):
    bucket, rank = idx[0], idx[1]
    counts = cnt[:N_BUCKETS, 0].astype(jnp.int32)
    tiles_b = (counts + t - 1) // t
    tile_end = jnp.cumsum(tiles_b)
    pos = (tile_end - tiles_b)[bucket] * t + rank
    total = tile_end[-1]
    tt = jnp.arange(n_tiles, dtype=jnp.int32)
    valid = tt < total
    tb = jnp.sum((tile_end[None, :] <= jnp.where(valid, tt, total - 1)[:, None]).astype(jnp.int32), axis=1)
    tb = jnp.minimum(tb, N_BUCKETS - 1)
    pair_lo = jnp.asarray([0, 0, 0, 1, 1, 2], jnp.int32)
    pair_hi = jnp.asarray([1, 2, 3, 2, 3, 3], jnp.int32)
    base = (tb // N_PAIRS) * EXPERTS_PER_GROUP
    idle = total + jnp.arange(N_BUCKETS, dtype=jnp.int32)
    idle_on = idle < n_tiles
    blks = t // ROW_TILE[0]
    tails = (jnp.concatenate([(tile_end - 1) * blks, jnp.where(idle_on, idle, 0) * blks]),
             jnp.concatenate([tiles_b > 0, idle_on]).astype(jnp.int32))
    return pos, tails, (base + pair_lo[tb % N_PAIRS], base + pair_hi[tb % N_PAIRS], total.reshape(1))


def _moe(x2, routing, gain, wg, wu, wd, t=MOE_TILE, tm=1024):
    n = x2.shape[0]
    idx, wts, cnt = routing
    n_tiles = n // t + N_BUCKETS
    pos, tails, tables = _moe_tables(idx, cnt, n_tiles, t)
    pos3 = pos.reshape(n // tm, 1, tm)
    xs = _dispatch(x2, wts, pos3, tails, n_tiles * t, t, tm)
    ys = _experts(xs, gain, tables, wg.astype(BF16), wu.astype(BF16), wd.astype(BF16), n_tiles, t)
    return _combine(ys, pos3, n, tm)


LOG2E = math.log2(math.e)
V_EXT = 2 * HEAD_DIM


def _head_norm_t(y_t, n_heads, scale):
    tm = y_t.shape[1]
    y3 = y_t.reshape(n_heads, HEAD_DIM, tm)
    ms = jnp.mean(y3 * y3, axis=1, keepdims=True)
    return y3 * (lax.rsqrt(ms + EPS) * scale)


def _qkv_kernel(x_ref, g_ref, wqt_ref, wkt_ref, wvt_ref, vone_ref, kg_ref, qt_ref, k_ref, vt_ref):
    h = _rms(x_ref[...], g_ref[...]).astype(BF16)
    tm = h.shape[0]
    qn = _head_norm_t(_nt_dot(wqt_ref[...], h), N_Q_HEADS, HEAD_DIM ** -0.5 * LOG2E)
    qt_ref[...] = qn.reshape(N_Q_HEADS * HEAD_DIM, tm).astype(BF16)
    kn = _head_norm_t(_nt_dot(wkt_ref[...], h), N_KV_HEADS, kg_ref[...].reshape(N_KV_HEADS, HEAD_DIM, 1))
    for hk in range(N_KV_HEADS):
        k_ref[hk] = kn[hk].T.astype(BF16)
    vt_ref[...] = (_nt_dot(wvt_ref[...], h) + vone_ref[...]).astype(BF16)


def _qkv(x2, gain, wqkv, q_gain, k_gain, tm=512):
    n = x2.shape[0]
    qw = N_Q_HEADS * HEAD_DIM
    wqt = wqkv[:, :qw].T.astype(BF16)
    wkt = wqkv[:, qw:qw + KV_WIDTH].T.astype(BF16)
    wvt = wqkv[:, qw + KV_WIDTH:].T.astype(BF16).reshape(N_KV_HEADS, HEAD_DIM, D_MODEL)
    wvt = jnp.pad(wvt, ((0, 0), (0, V_EXT - HEAD_DIM), (0, 0))).reshape(N_KV_HEADS * V_EXT, D_MODEL)
    vone = np.zeros((N_KV_HEADS * V_EXT, 1), np.float32)
    vone[HEAD_DIM::V_EXT, 0] = 1.0
    kg = jnp.tile((k_gain.astype(F32) * q_gain.astype(F32)), N_KV_HEADS).reshape(KV_WIDTH, 1)
    full = lambda r, c: pl.BlockSpec((r, c), lambda i: (0, 0))
    return pl.pallas_call(
        _qkv_kernel,
        out_shape=(jax.ShapeDtypeStruct((qw, n), BF16),
                   jax.ShapeDtypeStruct((N_KV_HEADS, n, HEAD_DIM), BF16),
                   jax.ShapeDtypeStruct((N_KV_HEADS * V_EXT, n), BF16)),
        grid=(n // tm,),
        in_specs=[pl.BlockSpec((tm, D_MODEL), lambda i: (i, 0)), full(1, D_MODEL),
                  full(qw, D_MODEL), full(KV_WIDTH, D_MODEL), full(N_KV_HEADS * V_EXT, D_MODEL),
                  full(N_KV_HEADS * V_EXT, 1), full(KV_WIDTH, 1)],
        out_specs=(pl.BlockSpec((qw, tm), lambda i: (0, i)),
                   pl.BlockSpec((N_KV_HEADS, tm, HEAD_DIM), lambda i: (0, i, 0)),
                   pl.BlockSpec((N_KV_HEADS * V_EXT, tm), lambda i: (0, i))),
        compiler_params=_params("parallel"),
        name="odd_qkv",
    )(x2, gain.reshape(1, D_MODEL), wqt, wkt, wvt, jnp.asarray(vone), kg)


def _attn_bias():
    blk = ATT_BLOCK
    qi = np.arange(blk)[None, :]
    ki = np.arange(2 * blk)[:, None]
    dist = qi - ki + blk
    band = (dist >= 0) & (dist < blk)
    slopes = 2.0 ** (-8.0 * np.arange(1, N_Q_HEADS + 1) / N_Q_HEADS)
    pen = -slopes[:, None, None] * dist[None].astype(np.float64) * LOG2E
    inner = np.where(band[None], pen, -np.inf)
    first = np.where((band & (ki >= blk))[None], pen, -np.inf)
    tab = np.stack([inner, first]).astype(np.float32)
    tab = tab.reshape(2, N_KV_HEADS, GQA_GROUP, 2 * blk, blk).transpose(0, 1, 3, 2, 4)
    return tab.reshape(2, N_KV_HEADS, 2 * blk, GQA_GROUP * blk)


def _attn_kernel(qt_ref, kp_ref, kc_ref, vtp_ref, vtc_ref, bias_ref, sink_ref, x_ref, wo_ref, o_ref):
    blk = ATT_BLOCK
    for sb in range(ATT_STEP_BLOCKS):
        tok = slice(sb * blk, (sb + 1) * blk)
        if sb == 0:
            first = (pl.program_id(1) == 0).astype(jnp.int32)
            k_prev = [kp_ref[hk] for hk in range(N_KV_HEADS)]
            vt_prev = vtp_ref[...]
        else:
            first = 0
            k_prev = [kc_ref[hk, (sb - 1) * blk:sb * blk, :] for hk in range(N_KV_HEADS)]
            vt_prev = vtc_ref[:, (sb - 1) * blk:sb * blk]
        vt = jnp.concatenate([vt_prev, vtc_ref[:, tok]], axis=1)
        att_t = []
        for hk in range(N_KV_HEADS):
            keys = jnp.concatenate([k_prev[hk], kc_ref[hk, tok, :]], axis=0)
            q_t = jnp.concatenate(
                [qt_ref[(hk * GQA_GROUP + g) * HEAD_DIM:(hk * GQA_GROUP + g + 1) * HEAD_DIM, tok]
                 for g in range(GQA_GROUP)], axis=1)
            s = jnp.dot(keys, q_t, preferred_element_type=F32) + bias_ref[first, hk]
            sink = sink_ref[hk]
            m = jnp.maximum(jnp.max(s, axis=0, keepdims=True), sink)
            p = jnp.exp2(s - m).astype(BF16)
            pv = jnp.dot(vt[hk * V_EXT:(hk + 1) * V_EXT, :], p, preferred_element_type=F32)
            den = pv[HEAD_DIM:HEAD_DIM + 1, :] + jnp.exp2(sink - m)
            o_t = (pv[:HEAD_DIM, :] * (1.0 / den)).astype(BF16)
            att_t += [o_t[:, g * blk:(g + 1) * blk] for g in range(GQA_GROUP)]
        att_t = jnp.concatenate(att_t, axis=0)
        mix = lax.dot_general(att_t, wo_ref[...], (((0,), (0,)), ((), ())), preferred_element_type=F32)
        o_ref[tok, :] = x_ref[tok, :] + mix


def _attn(qt, k, vt, x2, sinks, wo, bsz, seqlen):
    blk = ATT_BLOCK
    step = ATT_STEP_BLOCKS * blk
    nb = seqlen // blk
    ns = seqlen // step
    qw = N_Q_HEADS * HEAD_DIM
    cols = GQA_GROUP * blk
    cur = lambda b, n: (b * ns + n, 0)
    cur_t = lambda b, n: (0, b * ns + n)
    prev_blk = lambda b, n: b * nb + jnp.maximum(ATT_STEP_BLOCKS * n - 1, 0)
    sink_row = jnp.repeat(sinks.astype(F32) * LOG2E, blk).reshape(N_KV_HEADS, 1, cols)
    return pl.pallas_call(
        _attn_kernel,
        out_shape=jax.ShapeDtypeStruct((bsz * seqlen, D_MODEL), F32),
        grid=(bsz, ns),
        in_specs=[pl.BlockSpec((qw, step), cur_t),
                  pl.BlockSpec((N_KV_HEADS, blk, HEAD_DIM), lambda b, n: (0, prev_blk(b, n), 0)),
                  pl.BlockSpec((N_KV_HEADS, step, HEAD_DIM), lambda b, n: (0, b * ns + n, 0)),
                  pl.BlockSpec((N_KV_HEADS * V_EXT, blk), lambda b, n: (0, prev_blk(b, n))),
                  pl.BlockSpec((N_KV_HEADS * V_EXT, step), cur_t),
                  pl.BlockSpec((2, N_KV_HEADS, 2 * blk, cols), lambda b, n: (0, 0, 0, 0)),
                  pl.BlockSpec((N_KV_HEADS, 1, cols), lambda b, n: (0, 0, 0)),
                  pl.BlockSpec((step, D_MODEL), cur),
                  pl.BlockSpec((qw, D_MODEL), lambda b, n: (0, 0))],
        out_specs=pl.BlockSpec((step, D_MODEL), cur),
        compiler_params=_params("parallel", "parallel"),
        name="odd_attn",
    )(qt, k, k, vt, vt, jnp.asarray(_attn_bias()), sink_row, x2, wo.astype(BF16))


def kernel(x, even_mix_norm, even_in_proj, s5_lambda_re, s5_lambda_im, s5_log_step, s5_b_re, s5_b_im,
           s5_c_re, s5_c_im, s5_d, s5_glu_w, hgrn_lower_bounds, hgrn_o_norm, even_out_proj, odd_mix_norm,
           odd_wqkv, odd_q_norm, odd_k_norm, odd_sinks, odd_out_proj, moe_norm, moe_router_group,
           moe_router_group_bias, moe_router_expert, moe_router_expert_bias, moe_w_gate, moe_w_up,
           moe_w_down):
    bsz, seqlen, dm = x.shape
    n = bsz * seqlen
    x2 = x.reshape(n, dm)
    lower_bounds = jnp.cumsum(jax.nn.softmax(hgrn_lower_bounds.astype(F32), axis=0), axis=0)

    def router_params(layer):
        return (moe_norm[layer], moe_router_group[layer], moe_router_group_bias[layer],
                moe_router_expert[layer], moe_router_expert_bias[layer])

    def moe(xx, routing, layer):
        return _moe(xx, routing, moe_norm[layer], moe_w_gate[layer], moe_w_up[layer], moe_w_down[layer])

    u, h4 = _inproj(x2, even_mix_norm[0], even_in_proj[0].astype(BF16))
    ops = _s5_operators(s5_lambda_re[0], s5_lambda_im[0], s5_log_step[0], s5_b_re[0], s5_b_im[0],
                        s5_c_re[0], s5_c_im[0])
    u_g = _s5_pack(u.reshape(bsz, seqlen, S5_WIDTH))
    y_g = _s5_scan(u_g, ops, bsz)
    ys = _s5_unpack(y_g, bsz, seqlen).reshape(n, S5_WIDTH)
    b_out = _hgrn(h4.reshape(bsz, seqlen, 4 * HG_WIDTH), lower_bounds[0], hgrn_o_norm[0], bsz, seqlen)
    x2, *routing = _evenout(x2, ys, u, b_out.reshape(n, HG_WIDTH), s5_d[0], s5_glu_w[0], even_out_proj[0],
                            router_params(0))
    x2 = moe(x2, routing, 0)

    q, kt, v = _qkv(x2, odd_mix_norm[0], odd_wqkv[0], odd_q_norm[0], odd_k_norm[0])
    x2 = _attn(q, kt, v, x2, odd_sinks[0], odd_out_proj[0], bsz, seqlen)
    x2 = moe(x2, _router(x2, router_params(1)), 1)
    return x2.reshape(bsz, seqlen, dm)
```

```python
import functools
import math

import jax
import jax.numpy as jnp
import numpy as np
from jax import lax
from jax.experimental import pallas as pl
from jax.experimental.pallas import tpu as pltpu

F32 = jnp.float32
BF16 = jnp.bfloat16
EPS = 1e-6

D_MODEL = 1024
S5_WIDTH = 512
S5_GROUP = 16
S5_GROUPS = 32
S5_STATE = 64
S5_CHUNK = 16
HG_WIDTH = 512
HG_HEAD_DIM = 128
HG_HEADS = 4
HG_CHUNK = 32
HEAD_DIM = 64
N_Q_HEADS = 16
N_KV_HEADS = 2
GQA_GROUP = 8
KV_WIDTH = N_KV_HEADS * HEAD_DIM
ATT_BLOCK = 128
ATT_STEP_BLOCKS = 4
N_GROUPS = 4
EXPERTS_PER_GROUP = 4
N_EXPERTS = 16
D_EXPERT = 256
ROUTER_ROWS = 32
N_PAIRS = 6
N_BUCKETS = N_GROUPS * N_PAIRS
BUCKET_ROWS = 32
MOE_TILE = 512
ROW_TILE = (8, 128)
ROW_TILE_SHIFT = ROW_TILE[0].bit_length() - 1
X_TILES = D_MODEL // ROW_TILE[1]
XS_TILES = X_TILES + 1

VMEM_LIMIT_BYTES = 56 * 1024 * 1024


def _params(*semantics):
    return pltpu.CompilerParams(dimension_semantics=semantics, vmem_limit_bytes=VMEM_LIMIT_BYTES)


def _rms(xf, gain):
    return xf * lax.rsqrt(jnp.mean(xf * xf, axis=-1, keepdims=True) + EPS) * gain


def _nt_dot(w_t, h):
    return lax.dot_general(w_t, h, (((1,), (1,)), ((), ())), preferred_element_type=F32)


def _sigmoid(x):
    return 0.5 * jnp.tanh(0.5 * x) + 0.5


def _silu(x):
    return x * _sigmoid(x)


def _inproj_kernel(x_ref, g_ref, w_ref, u_ref, h4_ref):
    h = _rms(x_ref[...], g_ref[...]).astype(BF16)
    p = jnp.dot(h, w_ref[...], preferred_element_type=F32)
    u_ref[...] = p[:, :S5_WIDTH]
    h4_ref[...] = p[:, S5_WIDTH:]


def _inproj(x2, gain, w_bf16, tm=1024):
    n = x2.shape[0]
    e_in = w_bf16.shape[1]
    return pl.pallas_call(
        _inproj_kernel,
        out_shape=(jax.ShapeDtypeStruct((n, S5_WIDTH), F32),
                   jax.ShapeDtypeStruct((n, e_in - S5_WIDTH), F32)),
        grid=(n // tm,),
        in_specs=[pl.BlockSpec((tm, D_MODEL), lambda i: (i, 0)),
                  pl.BlockSpec((1, D_MODEL), lambda i: (0, 0)),
                  pl.BlockSpec((D_MODEL, e_in), lambda i: (0, 0))],
        out_specs=(pl.BlockSpec((tm, S5_WIDTH), lambda i: (i, 0)),
                   pl.BlockSpec((tm, e_in - S5_WIDTH), lambda i: (i, 0))),
        compiler_params=_params("parallel"),
        name="even_inproj",
    )(x2, gain.reshape(1, D_MODEL), w_bf16)


def _s5_toeplitz_kernel(ca_ref, bbt_ref, mt_ref):
    for gi in range(ca_ref.shape[0]):
        kt = lax.dot_general(bbt_ref[gi], ca_ref[gi], (((1,), (1,)), ((), ())), preferred_element_type=F32,
                             precision=lax.Precision.HIGHEST)
        width = kt.shape[1]
        blocks = [kt] + [jnp.concatenate([jnp.zeros((S5_GROUP, s * S5_GROUP), F32),
                                          kt[:, :width - s * S5_GROUP]], axis=1) for s in range(1, S5_CHUNK)]
        mt_ref[gi] = jnp.concatenate(blocks, axis=0).astype(BF16)


def _s5_toeplitz(ca, bbt, groups_per_step=8):
    g, rows, k = ca.shape
    gs = groups_per_step
    return pl.pallas_call(
        _s5_toeplitz_kernel,
        out_shape=jax.ShapeDtypeStruct((g, rows, rows), BF16),
        grid=(g // gs,),
        in_specs=[pl.BlockSpec((gs, rows, k), lambda i: (i, 0, 0)),
                  pl.BlockSpec((gs, S5_GROUP, k), lambda i: (i, 0, 0))],
        out_specs=pl.BlockSpec((gs, rows, rows), lambda i: (i, 0, 0)),
        compiler_params=_params("parallel"),
        name="s5_toeplitz",
    )(ca, bbt)


def _s5_operators(lam_re, lam_im, log_step, b_re, b_im, c_re, c_im):
    t = S5_CHUNK
    lr, li = lam_re.astype(F32), lam_im.astype(F32)
    step = jnp.exp(log_step.astype(F32))[:, None]
    mag = jnp.exp(lr * step)
    ab_re = mag * jnp.cos(li * step)
    ab_im = mag * jnp.sin(li * step)
    den = lr * lr + li * li
    nr, ni = ab_re - 1.0, ab_im
    z_re = (nr * lr + ni * li) / den
    z_im = (ni * lr - nr * li) / den
    br, bi = b_re.astype(F32), b_im.astype(F32)
    bb_re = z_re[..., None] * br - z_im[..., None] * bi
    bb_im = z_re[..., None] * bi + z_im[..., None] * br
    kk = jnp.arange(t + 1, dtype=F32)[:, None, None]
    pmag = jnp.exp(kk * (lr * step)[None])
    pw_re = pmag * jnp.cos(kk * (li * step)[None])
    pw_im = pmag * jnp.sin(kk * (li * step)[None])
    cr = jnp.transpose(c_re.astype(F32), (0, 1, 2))
    ci = c_im.astype(F32)
    ca_re = cr[None] * pw_re[:, :, None, :] - ci[None] * pw_im[:, :, None, :]
    ca_im = cr[None] * pw_im[:, :, None, :] + ci[None] * pw_re[:, :, None, :]
    g = lr.shape[0]
    ca_cat = jnp.concatenate([ca_re[:t], -ca_im[:t]], axis=-1)
    ca_cat = jnp.transpose(ca_cat, (1, 0, 2, 3)).reshape(g, t * S5_GROUP, 2 * S5_STATE)
    bb_cat_t = jnp.concatenate([bb_re, bb_im], axis=1).transpose(0, 2, 1)
    mt = _s5_toeplitz(ca_cat, bb_cat_t)
    pr = pw_re[:t][::-1]
    pi = pw_im[:t][::-1]
    sb_re = pr[:, :, :, None] * bb_re[None] - pi[:, :, :, None] * bb_im[None]
    sb_im = pr[:, :, :, None] * bb_im[None] + pi[:, :, :, None] * bb_re[None]
    sb_re = jnp.transpose(sb_re, (1, 0, 3, 2)).reshape(g, t * S5_GROUP, S5_STATE)
    sb_im = jnp.transpose(sb_im, (1, 0, 3, 2)).reshape(g, t * S5_GROUP, S5_STATE)
    cp_re = jnp.transpose(ca_re[1:], (1, 3, 0, 2)).reshape(g, S5_STATE, t * S5_GROUP)
    cp_im = jnp.transpose(-ca_im[1:], (1, 3, 0, 2)).reshape(g, S5_STATE, t * S5_GROUP)

    def pair_rows(m):
        m = m.reshape(g // 2, 2, m.shape[1], m.shape[2])
        z = jnp.zeros_like(m[:, 0])
        top = jnp.concatenate([m[:, 0], z], axis=2)
        bot = jnp.concatenate([z, m[:, 1]], axis=2)
        return jnp.concatenate([top, bot], axis=1)

    at_re = pw_re[t].reshape(g // 2, 1, 2 * S5_STATE)
    at_im = pw_im[t].reshape(g // 2, 1, 2 * S5_STATE)
    return (mt, pair_rows(sb_re).astype(BF16), pair_rows(sb_im).astype(BF16),
            pair_rows(cp_re).astype(BF16), pair_rows(cp_im).astype(BF16), at_re, at_im)


PACK_TOKENS = 512


LANES = 128
GROUPS_PER_TILE = LANES // S5_GROUP
TOKENS_PER_TILE = LANES // S5_GROUP
CHUNK_HALVES = S5_CHUNK // TOKENS_PER_TILE
PACK_CHUNKS = PACK_TOKENS // S5_CHUNK


def _block_swap_matrix():
    a, b, h = np.meshgrid(np.arange(TOKENS_PER_TILE), np.arange(GROUPS_PER_TILE), np.arange(S5_GROUP),
                          indexing="ij")
    src = (a * GROUPS_PER_TILE + b) * S5_GROUP + h
    dst = (b * TOKENS_PER_TILE + a) * S5_GROUP + h
    m = np.zeros((src.size, src.size), np.float32)
    m[src.ravel(), dst.ravel()] = 1.0
    return jnp.asarray(m, dtype=BF16)


def _s5_pack_kernel(u_ref, swap_ref, o_ref, *, bsz):
    for j in range(CHUNK_HALVES):
        rows = [jnp.concatenate([u_ref[b, pl.ds(j * TOKENS_PER_TILE + tt, PACK_CHUNKS, stride=S5_CHUNK), :]
                                 for tt in range(TOKENS_PER_TILE)], axis=1) for b in range(bsz)]
        lhs = jnp.concatenate(rows, axis=0).astype(BF16)
        out = jnp.dot(lhs, swap_ref[...], preferred_element_type=F32)
        for g in range(GROUPS_PER_TILE):
            for b in range(bsz):
                o_ref[g * CHUNK_HALVES + j, pl.ds(b, PACK_CHUNKS, stride=bsz), :] = (
                    out[b * PACK_CHUNKS:(b + 1) * PACK_CHUNKS, g * LANES:(g + 1) * LANES])


def _s5_pack(u3):
    bsz, seqlen, w = u3.shape
    rows = PACK_CHUNKS * bsz
    swap = _block_swap_matrix()
    return pl.pallas_call(
        functools.partial(_s5_pack_kernel, bsz=bsz),
        out_shape=jax.ShapeDtypeStruct((S5_GROUPS * CHUNK_HALVES, seqlen // S5_CHUNK * bsz, LANES), F32),
        grid=(seqlen // PACK_TOKENS, w // LANES),
        in_specs=[pl.BlockSpec((bsz, PACK_TOKENS, LANES), lambda i, k: (0, i, k)),
                  pl.BlockSpec(swap.shape, lambda i, k: (0, 0))],
        out_specs=pl.BlockSpec((GROUPS_PER_TILE * CHUNK_HALVES, rows, LANES), lambda i, k: (k, i, 0)),
        compiler_params=_params("parallel", "parallel"),
        name="s5_pack",
    )(u3, swap)


def _s5_unpack_kernel(y_ref, swap_ref, o_ref, *, bsz):
    for j in range(CHUNK_HALVES):
        rows = [jnp.concatenate([y_ref[g * CHUNK_HALVES + j, pl.ds(b, PACK_CHUNKS, stride=bsz), :]
                                 for g in range(GROUPS_PER_TILE)], axis=1) for b in range(bsz)]
        lhs = jnp.concatenate(rows, axis=0).astype(BF16)
        out = jnp.dot(lhs, swap_ref[...], preferred_element_type=F32)
        for tt in range(TOKENS_PER_TILE):
            for b in range(bsz):
                o_ref[b, pl.ds(j * TOKENS_PER_TILE + tt, PACK_CHUNKS, stride=S5_CHUNK), :] = (
                    out[b * PACK_CHUNKS:(b + 1) * PACK_CHUNKS, tt * LANES:(tt + 1) * LANES])


def _s5_unpack(y_g, bsz, seqlen):
    rows = PACK_CHUNKS * bsz
    swap = _block_swap_matrix()
    return pl.pallas_call(
        functools.partial(_s5_unpack_kernel, bsz=bsz),
        out_shape=jax.ShapeDtypeStruct((bsz, seqlen, S5_WIDTH), F32),
        grid=(seqlen // PACK_TOKENS, S5_WIDTH // LANES),
        in_specs=[pl.BlockSpec((GROUPS_PER_TILE * CHUNK_HALVES, rows, LANES), lambda i, k: (k, i, 0)),
                  pl.BlockSpec(swap.shape, lambda i, k: (0, 0))],
        out_specs=pl.BlockSpec((bsz, PACK_TOKENS, LANES), lambda i, k: (0, i, k)),
        compiler_params=_params("parallel", "parallel"),
        name="s5_unpack",
    )(y_g, swap)


def _s5_kernel(u_ref, mt_ref, wre_ref, wim_ref, cre_ref, cim_ref, atr_ref, ati_ref, y_ref,
               sre_ref, sim_ref, xre_ref, xim_ref, *, n_chunks, bsz):
    ucat = jnp.concatenate([u_ref[i] for i in range(2 * CHUNK_HALVES)], axis=1).astype(BF16)
    w = S5_CHUNK * S5_GROUP
    u0 = ucat[:, :w]
    u1 = ucat[:, w:]
    sre_ref[...] = jnp.dot(ucat, wre_ref[0], preferred_element_type=F32)
    sim_ref[...] = jnp.dot(ucat, wim_ref[0], preferred_element_type=F32)
    atr = jnp.broadcast_to(atr_ref[0], (bsz, 2 * S5_STATE))
    ati = jnp.broadcast_to(ati_ref[0], (bsz, 2 * S5_STATE))

    def body(c, carry):
        xr, xi = carry
        rows = pl.ds(pl.multiple_of(c * bsz, bsz), bsz)
        xre_ref[rows, :] = xr
        xim_ref[rows, :] = xi
        nxr = atr * xr - ati * xi + sre_ref[rows, :]
        nxi = atr * xi + ati * xr + sim_ref[rows, :]
        return nxr, nxi

    zero = jnp.zeros((bsz, 2 * S5_STATE), F32)
    lax.fori_loop(0, n_chunks, body, (zero, zero))
    ycar = (jnp.dot(xre_ref[...].astype(BF16), cre_ref[0], preferred_element_type=F32)
            + jnp.dot(xim_ref[...].astype(BF16), cim_ref[0], preferred_element_type=F32))
    y0 = jnp.dot(u0, mt_ref[0], preferred_element_type=F32) + ycar[:, :w]
    y1 = jnp.dot(u1, mt_ref[1], preferred_element_type=F32) + ycar[:, w:]
    for i in range(CHUNK_HALVES):
        y_ref[i] = y0[:, i * LANES:(i + 1) * LANES]
        y_ref[CHUNK_HALVES + i] = y1[:, i * LANES:(i + 1) * LANES]


def _s5_scan(u_g, ops, bsz):
    mt, wre, wim, cre, cim, atr, ati = ops
    tiles, r, _ = u_g.shape
    g = tiles // CHUNK_HALVES
    w = S5_CHUNK * S5_GROUP
    n_chunks = r // bsz
    p2 = 2 * S5_STATE
    kern = functools.partial(_s5_kernel, n_chunks=n_chunks, bsz=bsz)
    pair_tiles = pl.BlockSpec((2 * CHUNK_HALVES, r, LANES), lambda i: (i, 0, 0))
    return pl.pallas_call(
        kern,
        out_shape=jax.ShapeDtypeStruct((tiles, r, LANES), F32),
        grid=(g // 2,),
        in_specs=[pair_tiles,
                  pl.BlockSpec((2, w, w), lambda i: (i, 0, 0)),
                  pl.BlockSpec((1, 2 * w, p2), lambda i: (i, 0, 0)),
                  pl.BlockSpec((1, 2 * w, p2), lambda i: (i, 0, 0)),
                  pl.BlockSpec((1, p2, 2 * w), lambda i: (i, 0, 0)),
                  pl.BlockSpec((1, p2, 2 * w), lambda i: (i, 0, 0)),
                  pl.BlockSpec((1, 1, p2), lambda i: (i, 0, 0)),
                  pl.BlockSpec((1, 1, p2), lambda i: (i, 0, 0))],
        out_specs=pair_tiles,
        scratch_shapes=[pltpu.VMEM((r, p2), F32)] * 4,
        compiler_params=_params("parallel"),
        name="s5_scan",
    )(u_g, mt, wre, wim, cre, cim, atr, ati)


def _hgrn_kernel(q_ref, f_ref, i_ref, g_ref, lb_ref, og_ref, o_ref, *st_refs, seqlen):
    d = HG_HEAD_DIM
    for hh, st_ref in enumerate(st_refs):
        lanes = slice(hh * d, (hh + 1) * d)
        o_ref[0, :, lanes] = _hgrn_head(q_ref[0, :, lanes], f_ref[0, :, lanes], i_ref[0, :, lanes],
                                        g_ref[0, :, lanes], lb_ref[:, lanes], og_ref[...], st_ref, seqlen)


def _hgrn_head(q, f_logit, v, g, lb, o_gain, st_ref, seqlen):
    c = HG_CHUNK
    nc = seqlen // c
    d = HG_HEAD_DIM
    qs = _silu(q)
    f = lb + (1.0 - lb) * _sigmoid(f_logit)
    lf = jnp.log(f)
    k = 1.0 - f
    sub = ROW_TILE[0]
    b8 = lf.reshape(seqlen // sub, sub, d)
    row = lax.broadcasted_iota(jnp.int32, b8.shape, 1)
    sh = 1
    while sh < sub:
        b8 = b8 + jnp.where(row >= sh, pltpu.roll(b8, sh, axis=1), 0.0)
        sh *= 2
    b4 = b8.reshape(nc, c // sub, sub, d)
    groups, run = [], None
    for gi in range(c // sub):
        grp = b4[:, gi]
        groups.append(grp if run is None else grp + run)
        total = grp[:, sub - 1:sub, :]
        run = total if run is None else run + total
    b3 = jnp.concatenate(groups, axis=1)
    b_last = b3[:, c - 1:c, :]
    b_ref = b3[:, c // 2 - 1:c // 2, :]
    qs3 = qs.reshape(nc, c, d)
    k3 = k.reshape(nc, c, d)
    v3 = v.reshape(nc, c, d).astype(BF16)
    qe_f = qs3 * jnp.exp(b3 - b_ref)
    ke_f = k3 * jnp.exp(b_ref - b3)
    qe = qe_f.astype(BF16)
    ke = ke_f.astype(BF16)
    kd = (ke_f * jnp.exp(b_last - b_ref)).astype(BF16)
    qb = (qe_f * jnp.exp(b_ref)).astype(BF16)
    scores = jnp.einsum('ctd,csd->cts', qe, ke, preferred_element_type=F32)
    ti = lax.broadcasted_iota(jnp.int32, (c, c), 0)
    si = lax.broadcasted_iota(jnp.int32, (c, c), 1)
    scores = jnp.where((ti >= si)[None], scores, 0.0)
    o_intra = jnp.einsum('cts,csv->ctv', scores.astype(BF16), v3, preferred_element_type=F32)
    ut = jnp.einsum('csv,csd->cvd', v3, kd, preferred_element_type=F32)
    decay = jnp.exp(b_last)
    state = jnp.zeros((d, d), F32)
    for ci in range(nc):
        st_ref[ci] = state.astype(BF16)
        state = decay[ci] * state + ut[ci]
    o_inter = jnp.einsum('ctd,cvd->ctv', qb, st_ref[...], preferred_element_type=F32)
    o = (o_intra + o_inter).reshape(seqlen, d)
    o = _rms(o, o_gain)
    return (o * _silu(g)).astype(BF16)


def _hgrn(h4, lower_bound, o_gain, bsz, seqlen, heads_per_step=2):
    d = HG_HEAD_DIM
    w = heads_per_step * d
    steps = HG_HEADS // heads_per_step
    kern = functools.partial(_hgrn_kernel, seqlen=seqlen)

    def col(part):
        return pl.BlockSpec((1, seqlen, w), lambda b, h: (b, 0, part * steps + h))

    return pl.pallas_call(
        kern,
        out_shape=jax.ShapeDtypeStruct((bsz, seqlen, HG_WIDTH), BF16),
        grid=(bsz, steps),
        in_specs=[col(0), col(1), col(2), col(3),
                  pl.BlockSpec((1, w), lambda b, h: (0, h)),
                  pl.BlockSpec((1, d), lambda b, h: (0, 0))],
        out_specs=pl.BlockSpec((1, seqlen, w), lambda b, h: (b, 0, h)),
        scratch_shapes=[pltpu.VMEM((seqlen // HG_CHUNK, d, d), BF16)] * heads_per_step,
        compiler_params=_params("parallel", "parallel"),
        name="hgrn2",
    )(h4, h4, h4, h4, lower_bound.reshape(1, HG_WIDTH), o_gain.reshape(1, d))


def _evenout_kernel(x_ref, ys_ref, u_ref, b_ref, d_ref, wglu_ref, wa_ref, wb_ref, *rest):
    route_in, (o_ref, *route_out), route_scratch = rest[:4], rest[4:8], rest[8:]
    y = ys_ref[...] + d_ref[...] * u_ref[...]
    y = jax.nn.gelu(y)
    gate = _sigmoid(jnp.dot(y.astype(BF16), wglu_ref[...], preferred_element_type=F32))
    a = (y * gate).astype(BF16)
    mix = (jnp.dot(a, wa_ref[...], preferred_element_type=F32)
           + jnp.dot(b_ref[...], wb_ref[...], preferred_element_type=F32))
    x_new = x_ref[...] + mix
    o_ref[...] = x_new
    _route(x_new, pl.program_id(0) == 0, *route_in, *route_out, *route_scratch)


def _evenout(x2, ys, u, b_out, d_skip, wglu, wout, router_params, tm=512):
    n = x2.shape[0]
    route = _RouterPlumbing(n, tm, lambda i: i, *router_params)
    row = lambda w: pl.BlockSpec((tm, w), lambda i: (i, 0))
    full = lambda r, c: pl.BlockSpec((r, c), lambda i: (0, 0))
    return pl.pallas_call(
        _evenout_kernel,
        out_shape=(jax.ShapeDtypeStruct((n, D_MODEL), F32), *route.out_shape),
        grid=(n // tm,),
        in_specs=[row(D_MODEL), row(S5_WIDTH), row(S5_WIDTH), row(HG_WIDTH),
                  full(1, S5_WIDTH), full(S5_WIDTH, S5_WIDTH),
                  full(S5_WIDTH, D_MODEL), full(HG_WIDTH, D_MODEL), *route.in_specs],
        out_specs=(row(D_MODEL), *route.out_specs),
        scratch_shapes=route.scratch_shapes,
        compiler_params=_params("arbitrary"),
        name="even_out",
    )(x2, ys, u, b_out, d_skip.reshape(1, S5_WIDTH), wglu.astype(BF16),
      wout[:S5_WIDTH].astype(BF16), wout[S5_WIDTH:].astype(BF16), *route.operands)


def _route(x, is_first_step, g_ref, wr_ref, br_ref, tri_ref, idx_ref, wts_ref, cnt_ref, run_ref):
    @pl.when(is_first_step)
    def _():
        run_ref[...] = jnp.zeros_like(run_ref)

    h = _rms(x, g_ref[...])
    h_hi = h.astype(BF16)
    h_lo = (h - h_hi.astype(F32)).astype(BF16)
    both = _nt_dot(wr_ref[...], h_hi)
    lt = (both[:ROUTER_ROWS] + both[ROUTER_ROWS:] + _nt_dot(wr_ref[:ROUTER_ROWS, :], h_lo)
          + br_ref[...])
    gl = [lt[i:i + 1] for i in range(N_GROUPS)]
    el = [lt[N_GROUPS + i:N_GROUPS + i + 1] for i in range(N_EXPERTS)]
    gmax = jnp.maximum(jnp.maximum(gl[0], gl[1]), jnp.maximum(gl[2], gl[3]))
    g_idx = jnp.where(gl[0] == gmax, 0, jnp.where(gl[1] == gmax, 1, jnp.where(gl[2] == gmax, 2, 3)))
    g_gate = 1.0 / (jnp.exp(gl[0] - gmax) + jnp.exp(gl[1] - gmax) + jnp.exp(gl[2] - gmax) + jnp.exp(gl[3] - gmax))
    es = []
    for j in range(EXPERTS_PER_GROUP):
        es.append(jnp.where(g_idx == 0, el[j],
                            jnp.where(g_idx == 1, el[4 + j],
                                      jnp.where(g_idx == 2, el[8 + j], el[12 + j]))))
    e1 = jnp.maximum(jnp.maximum(es[0], es[1]), jnp.maximum(es[2], es[3]))
    i1 = jnp.where(es[0] == e1, 0, jnp.where(es[1] == e1, 1, jnp.where(es[2] == e1, 2, 3)))
    rest = [jnp.where(i1 == j, -jnp.inf, es[j]) for j in range(EXPERTS_PER_GROUP)]
    e2 = jnp.maximum(jnp.maximum(rest[0], rest[1]), jnp.maximum(rest[2], rest[3]))
    i2 = jnp.where(rest[0] == e2, 0, jnp.where(rest[1] == e2, 1, jnp.where(rest[2] == e2, 2, 3)))
    r = jnp.exp(e2 - e1)
    w1 = g_gate / (1.0 + r)
    w2 = w1 * r
    first_lo = i1 < i2
    lo = jnp.where(first_lo, i1, i2)
    hi = jnp.where(first_lo, i2, i1)
    w_lo = jnp.where(first_lo, w1, w2)
    w_hi = jnp.where(first_lo, w2, w1)
    pair = jnp.where(lo == 0, 0, jnp.where(lo == 1, 3, 5)) + hi - lo - 1
    bucket = g_idx * N_PAIRS + pair
    tm = bucket.shape[1]
    rowid = lax.broadcasted_iota(jnp.int32, (BUCKET_ROWS, tm), 0)
    onehot = (rowid == bucket).astype(F32)
    prefix = jnp.dot(onehot.astype(BF16), tri_ref[...], preferred_element_type=F32)
    run = run_ref[...]
    rank = jnp.sum(onehot * (prefix + run), axis=0, keepdims=True)
    run = run + jnp.sum(onehot, axis=1, keepdims=True)
    run_ref[...] = run
    cnt_ref[...] = jnp.broadcast_to(run, cnt_ref.shape)
    idx_ref[...] = jnp.concatenate([bucket, rank.astype(jnp.int32), jnp.zeros((6, tm), jnp.int32)], axis=0)
    wts_ref[...] = jnp.concatenate([w_lo, w_hi, jnp.zeros((6, tm), F32)], axis=0)


class _RouterPlumbing:
    def __init__(self, n, tm, tile_index, gain, w_rg, b_rg, w_re, b_re):
        wr = jnp.concatenate([w_rg, w_re], axis=1).astype(F32).T
        wr = jnp.pad(wr, ((0, ROUTER_ROWS - wr.shape[0]), (0, 0)))
        wr_hi = wr.astype(BF16)
        wr = jnp.concatenate([wr_hi, (wr - wr_hi.astype(F32)).astype(BF16)], axis=0)
        br = jnp.pad(jnp.concatenate([b_rg, b_re]).astype(F32), (0, ROUTER_ROWS - N_GROUPS - N_EXPERTS))
        tri = (np.arange(tm)[:, None] < np.arange(tm)[None, :]).astype(np.float32)
        const = lambda *_: (0, 0)
        self.operands = (gain.reshape(1, D_MODEL), wr, br.reshape(ROUTER_ROWS, 1), jnp.asarray(tri, dtype=BF16))
        self.in_specs = [pl.BlockSpec((1, D_MODEL), const),
                         pl.BlockSpec((2 * ROUTER_ROWS, D_MODEL), const),
                         pl.BlockSpec((ROUTER_ROWS, 1), const),
                         pl.BlockSpec((tm, tm), const)]
        self.out_shape = (jax.ShapeDtypeStruct((8, n), jnp.int32),
                          jax.ShapeDtypeStruct((8, n), F32),
                          jax.ShapeDtypeStruct((BUCKET_ROWS, LANES), F32))
        self.out_specs = (pl.BlockSpec((8, tm), lambda *g: (0, tile_index(*g))),
                          pl.BlockSpec((8, tm), lambda *g: (0, tile_index(*g))),
                          pl.BlockSpec((BUCKET_ROWS, LANES), const))
        self.scratch_shapes = [pltpu.VMEM((BUCKET_ROWS, 1), F32)]


def _router_kernel(x_ref, *route_refs):
    _route(x_ref[...], pl.program_id(0) == 0, *route_refs)


def _router(x2, router_params, tm=512):
    n = x2.shape[0]
    route = _RouterPlumbing(n, tm, lambda i: i, *router_params)
    return pl.pallas_call(
        _router_kernel,
        out_shape=route.out_shape,
        grid=(n // tm,),
        in_specs=[pl.BlockSpec((tm, D_MODEL), lambda i: (i, 0)), *route.in_specs],
        out_specs=route.out_specs,
        scratch_shapes=route.scratch_shapes,
        compiler_params=_params("arbitrary"),
        name="moe_router",
    )(x2, *route.operands)


ROW_COPY_UNROLL = 8


def _start_row_copies(idx_ref, n_rows, copy_for_row, prepare_rows=None):
    def start_group(base):
        for j in range(ROW_COPY_UNROLL):
            copy_for_row(base + j, idx_ref[0, 0, base + j]).start(priority=j % 2)

    n_groups = n_rows // ROW_COPY_UNROLL
    if prepare_rows is None:
        def body(g, carry):
            start_group(pl.multiple_of(g * ROW_COPY_UNROLL, ROW_COPY_UNROLL))
            return carry

        lax.fori_loop(0, n_groups, body, 0)
        return

    prepare_rows(0)

    def body(g, carry):
        base = pl.multiple_of(g * ROW_COPY_UNROLL, ROW_COPY_UNROLL)
        prepare_rows(base + ROW_COPY_UNROLL)
        start_group(base)
        return carry

    lax.fori_loop(0, n_groups - 1, body, 0)
    start_group(n_rows - ROW_COPY_UNROLL)


def _row_slab(view_ref, p):
    return view_ref.at[p >> ROW_TILE_SHIFT, :, p & (ROW_TILE[0] - 1)]


def _view_columns(view_ref, n_cols):
    rows = view_ref.shape[0] * ROW_TILE[0]
    return jnp.concatenate([view_ref[:, c].reshape(rows, LANES) for c in range(n_cols)], axis=1)


def _rows_to_tiles(x):
    rows = x.shape[0]
    return x.reshape(rows * ROW_TILE[0], ROW_TILE[1]).reshape(rows, *ROW_TILE)


def _tiles_to_rows(x3):
    rows = x3.shape[0]
    return x3.reshape(rows * ROW_TILE[0], ROW_TILE[1]).reshape(rows, D_MODEL)


def _dispatch_kernel(tail_blk_ref, tail_on_ref, pos_ref, x_ref, w_ref, xs_ref, buf_ref, zero_ref, wcol_ref, sem,
                     *, tile):
    tm = x_ref.shape[0]
    tile_blks = tile // ROW_TILE[0]

    @pl.when(pl.program_id(0) == 0)
    def _():
        zero_ref[...] = jnp.zeros_like(zero_ref)

        def zero_copy(k):
            blk = pl.multiple_of(tail_blk_ref[k], tile_blks)
            return pltpu.make_async_copy(zero_ref, xs_ref.at[pl.ds(blk, tile_blks)], sem)

        for k in range(2 * N_BUCKETS):
            pl.when(tail_on_ref[k] > 0)(lambda k=k: zero_copy(k).start())
        for k in range(2 * N_BUCKETS):
            pl.when(tail_on_ref[k] > 0)(lambda k=k: zero_copy(k).wait())

    wpad = jnp.concatenate([w_ref[...], jnp.zeros((LANES - w_ref.shape[0], tm), F32)], axis=0)
    wcol_ref[...] = wpad.T

    def stage(base):
        rows = pl.ds(base, ROW_COPY_UNROLL)
        buf_ref[rows, :X_TILES, :] = _rows_to_tiles(x_ref[rows, :])
        buf_ref[rows, X_TILES, :] = wcol_ref[rows, :]

    _start_row_copies(pos_ref, tm, lambda r, p: pltpu.make_async_copy(buf_ref.at[r], _row_slab(xs_ref, p), sem),
                      prepare_rows=stage)
    done = xs_ref.at[pl.ds(0, tm // ROW_TILE[0])]
    pltpu.make_async_copy(done, done, sem).wait()


def _dispatch(x2, wts, pos3, tails, n_rows_sorted, tile, tm):
    n = x2.shape[0]
    tail_blk, tail_on = tails
    grid_spec = pltpu.PrefetchScalarGridSpec(
        num_scalar_prefetch=2,
        grid=(n // tm,),
        in_specs=[pl.BlockSpec((1, 1, tm), lambda i, *_: (i, 0, 0), memory_space=pltpu.SMEM),
                  pl.BlockSpec((tm, D_MODEL), lambda i, *_: (i, 0)),
                  pl.BlockSpec((8, tm), lambda i, *_: (0, i))],
        out_specs=pl.BlockSpec(memory_space=pl.ANY),
        scratch_shapes=[pltpu.VMEM((tm, XS_TILES, LANES), F32),
                        pltpu.VMEM((tile // ROW_TILE[0], XS_TILES, *ROW_TILE), F32),
                        pltpu.VMEM((tm, LANES), F32),
                        pltpu.SemaphoreType.DMA],
    )
    return pl.pallas_call(
        functools.partial(_dispatch_kernel, tile=tile),
        out_shape=jax.ShapeDtypeStruct((n_rows_sorted // ROW_TILE[0], XS_TILES, *ROW_TILE), F32),
        grid_spec=grid_spec,
        compiler_params=_params("arbitrary"),
        name="moe_dispatch",
    )(tail_blk, tail_on, pos3, x2, wts)


def _experts_kernel(elo_ref, ehi_ref, nvalid_ref, xs_ref, g_ref, wg_lo, wu_lo, wg_hi, wu_hi,
                    wd_lo, wd_hi, o_ref):
    del elo_ref, ehi_ref
    t = pl.program_id(0)

    @pl.when(t < nvalid_ref[0])
    def _():
        rows = xs_ref.shape[0] * ROW_TILE[0]
        xt = _view_columns(xs_ref, X_TILES)
        h = _rms(xt, g_ref[...]).astype(BF16)
        extra = xs_ref[:, X_TILES].reshape(rows, LANES)
        w_lo = extra[:, 0:1]
        w_hi = extra[:, 1:2]

        def expert(wg, wu, wd, w):
            gate = jnp.dot(h, wg[0], preferred_element_type=F32)
            up = jnp.dot(h, wu[0], preferred_element_type=F32)
            hid = (_silu(gate) * up * w).astype(BF16)
            return jnp.dot(hid, wd[0], preferred_element_type=F32)

        out = xt + expert(wg_lo, wu_lo, wd_lo, w_lo) + expert(wg_hi, wu_hi, wd_hi, w_hi)
        for c in range(X_TILES):
            o_ref[:, c] = out[:, c * LANES:(c + 1) * LANES].reshape(o_ref.shape[0], *ROW_TILE)

    @pl.when(t >= nvalid_ref[0])
    def _():
        o_ref[...] = jnp.zeros_like(o_ref)


def _experts(xs, gain, tables, wg, wu, wd, n_tiles, t):
    elo, ehi, nvalid = tables
    blks = t // ROW_TILE[0]
    row = lambda i, elo, ehi, nv: (i, 0, 0, 0)
    row_in = lambda i, elo, ehi, nv: (jnp.minimum(i, nv[0] - 1), 0, 0, 0)
    lo3 = lambda i, elo, ehi, nv: (elo[i], 0, 0)
    hi3 = lambda i, elo, ehi, nv: (ehi[i], 0, 0)
    grid_spec = pltpu.PrefetchScalarGridSpec(
        num_scalar_prefetch=3,
        grid=(n_tiles,),
        in_specs=[pl.BlockSpec((blks, XS_TILES, *ROW_TILE), row_in),
                  pl.BlockSpec((1, D_MODEL), lambda i, *_: (0, 0)),
                  pl.BlockSpec((1, D_MODEL, D_EXPERT), lo3),
                  pl.BlockSpec((1, D_MODEL, D_EXPERT), lo3),
                  pl.BlockSpec((1, D_MODEL, D_EXPERT), hi3),
                  pl.BlockSpec((1, D_MODEL, D_EXPERT), hi3),
                  pl.BlockSpec((1, D_EXPERT, D_MODEL), lo3),
                  pl.BlockSpec((1, D_EXPERT, D_MODEL), hi3)],
        out_specs=pl.BlockSpec((blks, X_TILES, *ROW_TILE), row),
    )
    return pl.pallas_call(
        _experts_kernel,
        out_shape=jax.ShapeDtypeStruct((xs.shape[0], X_TILES, *ROW_TILE), F32),
        grid_spec=grid_spec,
        compiler_params=_params("arbitrary"),
        name="moe_experts",
    )(elo, ehi, nvalid, xs, gain.reshape(1, D_MODEL), wg, wu, wg, wu, wd, wd)


def _combine_kernel(pos_ref, ys_ref, o_ref, buf_ref, sem):
    tm = o_ref.shape[0]
    _start_row_copies(pos_ref, tm, lambda r, p: pltpu.make_async_copy(_row_slab(ys_ref, p), buf_ref.at[r], sem))
    done = ys_ref.at[pl.ds(0, tm // ROW_TILE[0])]
    pltpu.make_async_copy(done, done, sem).wait()
    o_ref[...] = _tiles_to_rows(buf_ref[...])


def _combine(ys, pos3, n, tm):
    return pl.pallas_call(
        _combine_kernel,
        out_shape=jax.ShapeDtypeStruct((n, D_MODEL), F32),
        grid=(n // tm,),
        in_specs=[pl.BlockSpec((1, 1, tm), lambda i: (i, 0, 0), memory_space=pltpu.SMEM),
                  pl.BlockSpec(memory_space=pl.ANY)],
        out_specs=pl.BlockSpec((tm, D_MODEL), lambda i: (i, 0)),
        scratch_shapes=[pltpu.VMEM((tm, *ROW_TILE), F32), pltpu.SemaphoreType.DMA],
        compiler_params=_params("arbitrary"),
        name="moe_combine",
    )(pos3, ys)


def _moe_tables(idx, cnt, n_tiles, t):
    bucket, rank = idx[0], idx[1]
    counts = cnt[:N_BUCKETS, 0].astype(jnp.int32)
    tiles_b = (counts + t - 1) // t
    tile_end = jnp.cumsum(tiles_b)
    pos = (tile_end - tiles_b)[bucket] * t + rank
    total = tile_end[-1]
    tt = jnp.arange(n_tiles, dtype=jnp.int32)
    valid = tt < total
    tb = jnp.sum((tile_end[None, :] <= jnp.where(valid, tt, total - 1)[:, None]).astype(jnp.int32), axis=1)
    tb = jnp.minimum(tb, N_BUCKETS - 1)
    pair_lo = jnp.asarray([0, 0, 0, 1, 1, 2], jnp.int32)
    pair_hi = jnp.asarray([1, 2, 3, 2, 3, 3], jnp.int32)
    base = (tb // N_PAIRS) * EXPERTS_PER_GROUP
    idle = total + jnp.arange(N_BUCKETS, dtype=jnp.int32)
    idle_on = idle < n_tiles
    blks = t // ROW_TILE[0]
    tails = (jnp.concatenate([(tile_end - 1) * blks, jnp.where(idle_on, idle, 0) * blks]),
             jnp.concatenate([tiles_b > 0, idle_on]).astype(jnp.int32))
    return pos, tails, (base + pair_lo[tb % N_PAIRS], base + pair_hi[tb % N_PAIRS], total.reshape(1))


def _moe(x2, routing, gain, wg, wu, wd, t=MOE_TILE, tm_dispatch=1024, tm_combine=2048):
    n = x2.shape[0]
    idx, wts, cnt = routing
    n_tiles = n // t + N_BUCKETS
    pos, tails, tables = _moe_tables(idx, cnt, n_tiles, t)
    xs = _dispatch(x2, wts, pos.reshape(n // tm_dispatch, 1, tm_dispatch), tails, n_tiles * t, t, tm_dispatch)
    ys = _experts(xs, gain, tables, wg.astype(BF16), wu.astype(BF16), wd.astype(BF16), n_tiles, t)
    return _combine(ys, pos.reshape(n // tm_combine, 1, tm_combine), n, tm_combine)


LOG2E = math.log2(math.e)
V_EXT = 2 * HEAD_DIM


def _head_norm_t(y_t, n_heads, scale):
    tm = y_t.shape[1]
    y3 = y_t.reshape(n_heads, HEAD_DIM, tm)
    ms = jnp.mean(y3 * y3, axis=1, keepdims=True)
    return y3 * (lax.rsqrt(ms + EPS) * scale)


def _qkv_kernel(x_ref, g_ref, wqt_ref, wkt_ref, wvt_ref, vone_ref, kg_ref, qt_ref, k_ref, vt_ref):
    h = _rms(x_ref[...], g_ref[...]).astype(BF16)
    tm = h.shape[0]
    qn = _head_norm_t(_nt_dot(wqt_ref[...], h), N_Q_HEADS, HEAD_DIM ** -0.5 * LOG2E)
    qt_ref[...] = qn.reshape(N_Q_HEADS * HEAD_DIM, tm).astype(BF16)
    kn = _head_norm_t(_nt_dot(wkt_ref[...], h), N_KV_HEADS, kg_ref[...].reshape(N_KV_HEADS, HEAD_DIM, 1))
    for hk in range(N_KV_HEADS):
        k_ref[hk] = kn[hk].T.astype(BF16)
    vt_ref[...] = (_nt_dot(wvt_ref[...], h) + vone_ref[...]).astype(BF16)


def _qkv(x2, gain, wqkv, q_gain, k_gain, tm=1024):
    n = x2.shape[0]
    qw = N_Q_HEADS * HEAD_DIM
    wqt = wqkv[:, :qw].T.astype(BF16)
    wkt = wqkv[:, qw:qw + KV_WIDTH].T.astype(BF16)
    wvt = wqkv[:, qw + KV_WIDTH:].T.astype(BF16).reshape(N_KV_HEADS, HEAD_DIM, D_MODEL)
    wvt = jnp.pad(wvt, ((0, 0), (0, V_EXT - HEAD_DIM), (0, 0))).reshape(N_KV_HEADS * V_EXT, D_MODEL)
    vone = np.zeros((N_KV_HEADS * V_EXT, 1), np.float32)
    vone[HEAD_DIM::V_EXT, 0] = 1.0
    kg = jnp.tile((k_gain.astype(F32) * q_gain.astype(F32)), N_KV_HEADS).reshape(KV_WIDTH, 1)
    full = lambda r, c: pl.BlockSpec((r, c), lambda i: (0, 0))
    return pl.pallas_call(
        _qkv_kernel,
        out_shape=(jax.ShapeDtypeStruct((qw, n), BF16),
                   jax.ShapeDtypeStruct((N_KV_HEADS, n, HEAD_DIM), BF16),
                   jax.ShapeDtypeStruct((N_KV_HEADS * V_EXT, n), BF16)),
        grid=(n // tm,),
        in_specs=[pl.BlockSpec((tm, D_MODEL), lambda i: (i, 0)), full(1, D_MODEL),
                  full(qw, D_MODEL), full(KV_WIDTH, D_MODEL), full(N_KV_HEADS * V_EXT, D_MODEL),
                  full(N_KV_HEADS * V_EXT, 1), full(KV_WIDTH, 1)],
        out_specs=(pl.BlockSpec((qw, tm), lambda i: (0, i)),
                   pl.BlockSpec((N_KV_HEADS, tm, HEAD_DIM), lambda i: (0, i, 0)),
                   pl.BlockSpec((N_KV_HEADS * V_EXT, tm), lambda i: (0, i))),
        compiler_params=_params("parallel"),
        name="odd_qkv",
    )(x2, gain.reshape(1, D_MODEL), wqt, wkt, wvt, jnp.asarray(vone), kg)


def _attn_bias():
    blk = ATT_BLOCK
    qi = np.arange(blk)[None, :]
    ki = np.arange(2 * blk)[:, None]
    dist = qi - ki + blk
    band = (dist >= 0) & (dist < blk)
    slopes = 2.0 ** (-8.0 * np.arange(1, N_Q_HEADS + 1) / N_Q_HEADS)
    pen = -slopes[:, None, None] * dist[None].astype(np.float64) * LOG2E
    inner = np.where(band[None], pen, -np.inf)
    first = np.where((band & (ki >= blk))[None], pen, -np.inf)
    tab = np.stack([inner, first]).astype(np.float32)
    tab = tab.reshape(2, N_KV_HEADS, GQA_GROUP, 2 * blk, blk).transpose(0, 1, 3, 2, 4)
    return tab.reshape(2, N_KV_HEADS, 2 * blk, GQA_GROUP * blk)


def _attn_kernel(qt_ref, kp_ref, kc_ref, vtp_ref, vtc_ref, bias_ref, sink_ref, x_ref, wo_ref, o_ref):
    blk = ATT_BLOCK
    for sb in range(ATT_STEP_BLOCKS):
        tok = slice(sb * blk, (sb + 1) * blk)
        if sb == 0:
            first = (pl.program_id(1) == 0).astype(jnp.int32)
            k_prev = [kp_ref[hk] for hk in range(N_KV_HEADS)]
            vt_prev = vtp_ref[...]
        else:
            first = 0
            k_prev = [kc_ref[hk, (sb - 1) * blk:sb * blk, :] for hk in range(N_KV_HEADS)]
            vt_prev = vtc_ref[:, (sb - 1) * blk:sb * blk]
        vt = jnp.concatenate([vt_prev, vtc_ref[:, tok]], axis=1)
        att_t = []
        for hk in range(N_KV_HEADS):
            keys = jnp.concatenate([k_prev[hk], kc_ref[hk, tok, :]], axis=0)
            q_t = jnp.concatenate(
                [qt_ref[(hk * GQA_GROUP + g) * HEAD_DIM:(hk * GQA_GROUP + g + 1) * HEAD_DIM, tok]
                 for g in range(GQA_GROUP)], axis=1)
            s = jnp.dot(keys, q_t, preferred_element_type=F32) + bias_ref[first, hk]
            sink = sink_ref[hk]
            m = jnp.maximum(jnp.max(s, axis=0, keepdims=True), sink)
            p = jnp.exp2(s - m).astype(BF16)
            pv = jnp.dot(vt[hk * V_EXT:(hk + 1) * V_EXT, :], p, preferred_element_type=F32)
            den = pv[HEAD_DIM:HEAD_DIM + 1, :] + jnp.exp2(sink - m)
            o_t = (pv[:HEAD_DIM, :] * (1.0 / den)).astype(BF16)
            att_t += [o_t[:, g * blk:(g + 1) * blk] for g in range(GQA_GROUP)]
        att_t = jnp.concatenate(att_t, axis=0)
        mix = lax.dot_general(att_t, wo_ref[...], (((0,), (0,)), ((), ())), preferred_element_type=F32)
        o_ref[tok, :] = x_ref[tok, :] + mix


def _attn(qt, k, vt, x2, sinks, wo, bsz, seqlen):
    blk = ATT_BLOCK
    step = ATT_STEP_BLOCKS * blk
    nb = seqlen // blk
    ns = seqlen // step
    qw = N_Q_HEADS * HEAD_DIM
    cols = GQA_GROUP * blk
    cur = lambda b, n: (b * ns + n, 0)
    cur_t = lambda b, n: (0, b * ns + n)
    prev_blk = lambda b, n: b * nb + jnp.maximum(ATT_STEP_BLOCKS * n - 1, 0)
    sink_row = jnp.repeat(sinks.astype(F32) * LOG2E, blk).reshape(N_KV_HEADS, 1, cols)
    return pl.pallas_call(
        _attn_kernel,
        out_shape=jax.ShapeDtypeStruct((bsz * seqlen, D_MODEL), F32),
        grid=(bsz, ns),
        in_specs=[pl.BlockSpec((qw, step), cur_t),
                  pl.BlockSpec((N_KV_HEADS, blk, HEAD_DIM), lambda b, n: (0, prev_blk(b, n), 0)),
                  pl.BlockSpec((N_KV_HEADS, step, HEAD_DIM), lambda b, n: (0, b * ns + n, 0)),
                  pl.BlockSpec((N_KV_HEADS * V_EXT, blk), lambda b, n: (0, prev_blk(b, n))),
                  pl.BlockSpec((N_KV_HEADS * V_EXT, step), cur_t),
                  pl.BlockSpec((2, N_KV_HEADS, 2 * blk, cols), lambda b, n: (0, 0, 0, 0)),
                  pl.BlockSpec((N_KV_HEADS, 1, cols), lambda b, n: (0, 0, 0)),
                  pl.BlockSpec((step, D_MODEL), cur),
                  pl.BlockSpec((qw, D_MODEL), lambda b, n: (0, 0))],
        out_specs=pl.BlockSpec((step, D_MODEL), cur),
        compiler_params=_params("parallel", "parallel"),
        name="odd_attn",
    )(qt, k, k, vt, vt, jnp.asarray(_attn_bias()), sink_row, x2, wo.astype(BF16))


def kernel(x, even_mix_norm, even_in_proj, s5_lambda_re, s5_lambda_im, s5_log_step, s5_b_re, s5_b_im,
           s5_c_re, s5_c_im, s5_d, s5_glu_w, hgrn_lower_bounds, hgrn_o_norm, even_out_proj, odd_mix_norm,
           odd_wqkv, odd_q_norm, odd_k_norm, odd_sinks, odd_out_proj, moe_norm, moe_router_group,
           moe_router_group_bias, moe_router_expert, moe_router_expert_bias, moe_w_gate, moe_w_up,
           moe_w_down):
    bsz, seqlen, dm = x.shape
    n = bsz * seqlen
    x2 = x.reshape(n, dm)
    lower_bounds = jnp.cumsum(jax.nn.softmax(hgrn_lower_bounds.astype(F32), axis=0), axis=0)

    def router_params(layer):
        return (moe_norm[layer], moe_router_group[layer], moe_router_group_bias[layer],
                moe_router_expert[layer], moe_router_expert_bias[layer])

    def moe(xx, routing, layer):
        return _moe(xx, routing, moe_norm[layer], moe_w_gate[layer], moe_w_up[layer], moe_w_down[layer])

    u, h4 = _inproj(x2, even_mix_norm[0], even_in_proj[0].astype(BF16))
    ops = _s5_operators(s5_lambda_re[0], s5_lambda_im[0], s5_log_step[0], s5_b_re[0], s5_b_im[0],
                        s5_c_re[0], s5_c_im[0])
    u_g = _s5_pack(u.reshape(bsz, seqlen, S5_WIDTH))
    y_g = _s5_scan(u_g, ops, bsz)
    ys = _s5_unpack(y_g, bsz, seqlen).reshape(n, S5_WIDTH)
    b_out = _hgrn(h4.reshape(bsz, seqlen, 4 * HG_WIDTH), lower_bounds[0], hgrn_o_norm[0], bsz, seqlen)
    x2, *routing = _evenout(x2, ys, u, b_out.reshape(n, HG_WIDTH), s5_d[0], s5_glu_w[0], even_out_proj[0],
                            router_params(0))
    x2 = moe(x2, routing, 0)

    q, kt, v = _qkv(x2, odd_mix_norm[0], odd_wqkv[0], odd_q_norm[0], odd_k_norm[0])
    x2 = _attn(q, kt, v, x2, odd_sinks[0], odd_out_proj[0], bsz, seqlen)
    x2 = moe(x2, _router(x2, router_params(1)), 1)
    return x2.reshape(bsz, seqlen, dm)
```

```python
import functools
import math

import jax
import jax.numpy as jnp
import numpy as np
from jax import lax
from jax.experimental import pallas as pl
from jax.experimental.pallas import tpu as pltpu

F32 = jnp.float32
BF16 = jnp.bfloat16
EPS = 1e-6

D_MODEL = 1024
S5_WIDTH = 512
S5_GROUP = 16
S5_GROUPS = 32
S5_STATE = 64
S5_CHUNK = 16
HG_WIDTH = 512
HG_HEAD_DIM = 128
HG_HEADS = 4
HG_CHUNK = 32
HEAD_DIM = 64
N_Q_HEADS = 16
N_KV_HEADS = 2
GQA_GROUP = 8
KV_WIDTH = N_KV_HEADS * HEAD_DIM
ATT_BLOCK = 128
ATT_STEP_BLOCKS = 8
N_GROUPS = 4
EXPERTS_PER_GROUP = 4
N_EXPERTS = 16
D_EXPERT = 256
ROUTER_ROWS = 32
N_PAIRS = 6
N_BUCKETS = N_GROUPS * N_PAIRS
BUCKET_ROWS = 32
MOE_TILE = 512
ROW_TILE = (8, 128)
ROW_TILE_SHIFT = ROW_TILE[0].bit_length() - 1
X_TILES = D_MODEL // ROW_TILE[1]
XS_TILES = X_TILES + 1

VMEM_LIMIT_BYTES = 56 * 1024 * 1024


def _params(*semantics):
    return pltpu.CompilerParams(dimension_semantics=semantics, vmem_limit_bytes=VMEM_LIMIT_BYTES)


def _rms(xf, gain):
    return xf * lax.rsqrt(jnp.mean(xf * xf, axis=-1, keepdims=True) + EPS) * gain


def _nt_dot(w_t, h):
    return lax.dot_general(w_t, h, (((1,), (1,)), ((), ())), preferred_element_type=F32)


def _sigmoid(x):
    return 0.5 * jnp.tanh(0.5 * x) + 0.5


def _silu(x):
    return x * _sigmoid(x)


def _inproj_kernel(x_ref, g_ref, w_ref, u_ref, h4_ref):
    h = _rms(x_ref[...], g_ref[...]).astype(BF16)
    p = jnp.dot(h, w_ref[...], preferred_element_type=F32)
    u_ref[...] = p[:, :S5_WIDTH]
    h4_ref[...] = p[:, S5_WIDTH:]


def _inproj(x2, gain, w_bf16, tm=1024):
    n = x2.shape[0]
    e_in = w_bf16.shape[1]
    return pl.pallas_call(
        _inproj_kernel,
        out_shape=(jax.ShapeDtypeStruct((n, S5_WIDTH), F32),
                   jax.ShapeDtypeStruct((n, e_in - S5_WIDTH), F32)),
        grid=(n // tm,),
        in_specs=[pl.BlockSpec((tm, D_MODEL), lambda i: (i, 0)),
                  pl.BlockSpec((1, D_MODEL), lambda i: (0, 0)),
                  pl.BlockSpec((D_MODEL, e_in), lambda i: (0, 0))],
        out_specs=(pl.BlockSpec((tm, S5_WIDTH), lambda i: (i, 0)),
                   pl.BlockSpec((tm, e_in - S5_WIDTH), lambda i: (i, 0))),
        compiler_params=_params("parallel"),
        name="even_inproj",
    )(x2, gain.reshape(1, D_MODEL), w_bf16)


def _s5_toeplitz_kernel(ca_ref, bbt_ref, mt_ref):
    for gi in range(ca_ref.shape[0]):
        kt = lax.dot_general(bbt_ref[gi], ca_ref[gi], (((1,), (1,)), ((), ())), preferred_element_type=F32,
                             precision=lax.Precision.HIGHEST)
        width = kt.shape[1]
        blocks = [kt] + [jnp.concatenate([jnp.zeros((S5_GROUP, s * S5_GROUP), F32),
                                          kt[:, :width - s * S5_GROUP]], axis=1) for s in range(1, S5_CHUNK)]
        mt_ref[gi] = jnp.concatenate(blocks, axis=0).astype(BF16)


def _s5_toeplitz(ca, bbt, groups_per_step=8):
    g, rows, k = ca.shape
    gs = groups_per_step
    return pl.pallas_call(
        _s5_toeplitz_kernel,
        out_shape=jax.ShapeDtypeStruct((g, rows, rows), BF16),
        grid=(g // gs,),
        in_specs=[pl.BlockSpec((gs, rows, k), lambda i: (i, 0, 0)),
                  pl.BlockSpec((gs, S5_GROUP, k), lambda i: (i, 0, 0))],
        out_specs=pl.BlockSpec((gs, rows, rows), lambda i: (i, 0, 0)),
        compiler_params=_params("parallel"),
        name="s5_toeplitz",
    )(ca, bbt)


def _s5_operators(lam_re, lam_im, log_step, b_re, b_im, c_re, c_im):
    t = S5_CHUNK
    lr, li = lam_re.astype(F32), lam_im.astype(F32)
    step = jnp.exp(log_step.astype(F32))[:, None]
    mag = jnp.exp(lr * step)
    ab_re = mag * jnp.cos(li * step)
    ab_im = mag * jnp.sin(li * step)
    den = lr * lr + li * li
    nr, ni = ab_re - 1.0, ab_im
    z_re = (nr * lr + ni * li) / den
    z_im = (ni * lr - nr * li) / den
    br, bi = b_re.astype(F32), b_im.astype(F32)
    bb_re = z_re[..., None] * br - z_im[..., None] * bi
    bb_im = z_re[..., None] * bi + z_im[..., None] * br
    kk = jnp.arange(t + 1, dtype=F32)[:, None, None]
    pmag = jnp.exp(kk * (lr * step)[None])
    pw_re = pmag * jnp.cos(kk * (li * step)[None])
    pw_im = pmag * jnp.sin(kk * (li * step)[None])
    cr = jnp.transpose(c_re.astype(F32), (0, 1, 2))
    ci = c_im.astype(F32)
    ca_re = cr[None] * pw_re[:, :, None, :] - ci[None] * pw_im[:, :, None, :]
    ca_im = cr[None] * pw_im[:, :, None, :] + ci[None] * pw_re[:, :, None, :]
    g = lr.shape[0]
    ca_cat = jnp.concatenate([ca_re[:t], -ca_im[:t]], axis=-1)
    ca_cat = jnp.transpose(ca_cat, (1, 0, 2, 3)).reshape(g, t * S5_GROUP, 2 * S5_STATE)
    bb_cat_t = jnp.concatenate([bb_re, bb_im], axis=1).transpose(0, 2, 1)
    mt = _s5_toeplitz(ca_cat, bb_cat_t)
    pr = pw_re[:t][::-1]
    pi = pw_im[:t][::-1]
    sb_re = pr[:, :, :, None] * bb_re[None] - pi[:, :, :, None] * bb_im[None]
    sb_im = pr[:, :, :, None] * bb_im[None] + pi[:, :, :, None] * bb_re[None]
    sb_re = jnp.transpose(sb_re, (1, 0, 3, 2)).reshape(g, t * S5_GROUP, S5_STATE)
    sb_im = jnp.transpose(sb_im, (1, 0, 3, 2)).reshape(g, t * S5_GROUP, S5_STATE)
    cp_re = jnp.transpose(ca_re[1:], (1, 3, 0, 2)).reshape(g, S5_STATE, t * S5_GROUP)
    cp_im = jnp.transpose(-ca_im[1:], (1, 3, 0, 2)).reshape(g, S5_STATE, t * S5_GROUP)

    def pair_rows(m):
        m = m.reshape(g // 2, 2, m.shape[1], m.shape[2])
        z = jnp.zeros_like(m[:, 0])
        top = jnp.concatenate([m[:, 0], z], axis=2)
        bot = jnp.concatenate([z, m[:, 1]], axis=2)
        return jnp.concatenate([top, bot], axis=1)

    at_re = pw_re[t].reshape(g // 2, 1, 2 * S5_STATE)
    at_im = pw_im[t].reshape(g // 2, 1, 2 * S5_STATE)
    return (mt, pair_rows(sb_re).astype(BF16), pair_rows(sb_im).astype(BF16),
            pair_rows(cp_re).astype(BF16), pair_rows(cp_im).astype(BF16), at_re, at_im)


PACK_TOKENS = 512


LANES = 128
GROUPS_PER_TILE = LANES // S5_GROUP
TOKENS_PER_TILE = LANES // S5_GROUP
CHUNK_HALVES = S5_CHUNK // TOKENS_PER_TILE
PACK_CHUNKS = PACK_TOKENS // S5_CHUNK


def _block_swap_matrix():
    a, b, h = np.meshgrid(np.arange(TOKENS_PER_TILE), np.arange(GROUPS_PER_TILE), np.arange(S5_GROUP),
                          indexing="ij")
    src = (a * GROUPS_PER_TILE + b) * S5_GROUP + h
    dst = (b * TOKENS_PER_TILE + a) * S5_GROUP + h
    m = np.zeros((src.size, src.size), np.float32)
    m[src.ravel(), dst.ravel()] = 1.0
    return jnp.asarray(m, dtype=BF16)


def _s5_pack_kernel(u_ref, swap_ref, o_ref, *, bsz):
    for j in range(CHUNK_HALVES):
        rows = [jnp.concatenate([u_ref[b, pl.ds(j * TOKENS_PER_TILE + tt, PACK_CHUNKS, stride=S5_CHUNK), :]
                                 for tt in range(TOKENS_PER_TILE)], axis=1) for b in range(bsz)]
        lhs = jnp.concatenate(rows, axis=0).astype(BF16)
        out = jnp.dot(lhs, swap_ref[...], preferred_element_type=F32)
        for g in range(GROUPS_PER_TILE):
            for b in range(bsz):
                o_ref[g * CHUNK_HALVES + j, pl.ds(b, PACK_CHUNKS, stride=bsz), :] = (
                    out[b * PACK_CHUNKS:(b + 1) * PACK_CHUNKS, g * LANES:(g + 1) * LANES])


def _s5_pack(u3):
    bsz, seqlen, w = u3.shape
    rows = PACK_CHUNKS * bsz
    swap = _block_swap_matrix()
    return pl.pallas_call(
        functools.partial(_s5_pack_kernel, bsz=bsz),
        out_shape=jax.ShapeDtypeStruct((S5_GROUPS * CHUNK_HALVES, seqlen // S5_CHUNK * bsz, LANES), F32),
        grid=(seqlen // PACK_TOKENS, w // LANES),
        in_specs=[pl.BlockSpec((bsz, PACK_TOKENS, LANES), lambda i, k: (0, i, k)),
                  pl.BlockSpec(swap.shape, lambda i, k: (0, 0))],
        out_specs=pl.BlockSpec((GROUPS_PER_TILE * CHUNK_HALVES, rows, LANES), lambda i, k: (k, i, 0)),
        compiler_params=_params("parallel", "parallel"),
        name="s5_pack",
    )(u3, swap)


def _s5_unpack_kernel(y_ref, swap_ref, o_ref, *, bsz):
    for j in range(CHUNK_HALVES):
        rows = [jnp.concatenate([y_ref[g * CHUNK_HALVES + j, pl.ds(b, PACK_CHUNKS, stride=bsz), :]
                                 for g in range(GROUPS_PER_TILE)], axis=1) for b in range(bsz)]
        lhs = jnp.concatenate(rows, axis=0).astype(BF16)
        out = jnp.dot(lhs, swap_ref[...], preferred_element_type=F32)
        for tt in range(TOKENS_PER_TILE):
            for b in range(bsz):
                o_ref[b, pl.ds(j * TOKENS_PER_TILE + tt, PACK_CHUNKS, stride=S5_CHUNK), :] = (
                    out[b * PACK_CHUNKS:(b + 1) * PACK_CHUNKS, tt * LANES:(tt + 1) * LANES])


def _s5_unpack(y_g, bsz, seqlen):
    rows = PACK_CHUNKS * bsz
    swap = _block_swap_matrix()
    return pl.pallas_call(
        functools.partial(_s5_unpack_kernel, bsz=bsz),
        out_shape=jax.ShapeDtypeStruct((bsz, seqlen, S5_WIDTH), F32),
        grid=(seqlen // PACK_TOKENS, S5_WIDTH // LANES),
        in_specs=[pl.BlockSpec((GROUPS_PER_TILE * CHUNK_HALVES, rows, LANES), lambda i, k: (k, i, 0)),
                  pl.BlockSpec(swap.shape, lambda i, k: (0, 0))],
        out_specs=pl.BlockSpec((bsz, PACK_TOKENS, LANES), lambda i, k: (0, i, k)),
        compiler_params=_params("parallel", "parallel"),
        name="s5_unpack",
    )(y_g, swap)


def _s5_kernel(u_ref, mt_ref, wre_ref, wim_ref, cre_ref, cim_ref, atr_ref, ati_ref, y_ref,
               sre_ref, sim_ref, xre_ref, xim_ref, *, n_chunks, bsz):
    ucat = jnp.concatenate([u_ref[i] for i in range(2 * CHUNK_HALVES)], axis=1).astype(BF16)
    w = S5_CHUNK * S5_GROUP
    u0 = ucat[:, :w]
    u1 = ucat[:, w:]
    sre_ref[...] = jnp.dot(ucat, wre_ref[0], preferred_element_type=F32)
    sim_ref[...] = jnp.dot(ucat, wim_ref[0], preferred_element_type=F32)
    atr = jnp.broadcast_to(atr_ref[0], (bsz, 2 * S5_STATE))
    ati = jnp.broadcast_to(ati_ref[0], (bsz, 2 * S5_STATE))

    def body(c, carry):
        xr, xi = carry
        rows = pl.ds(pl.multiple_of(c * bsz, bsz), bsz)
        xre_ref[rows, :] = xr
        xim_ref[rows, :] = xi
        nxr = atr * xr - ati * xi + sre_ref[rows, :]
        nxi = atr * xi + ati * xr + sim_ref[rows, :]
        return nxr, nxi

    zero = jnp.zeros((bsz, 2 * S5_STATE), F32)
    lax.fori_loop(0, n_chunks, body, (zero, zero))
    ycar = (jnp.dot(xre_ref[...].astype(BF16), cre_ref[0], preferred_element_type=F32)
            + jnp.dot(xim_ref[...].astype(BF16), cim_ref[0], preferred_element_type=F32))
    y0 = jnp.dot(u0, mt_ref[0], preferred_element_type=F32) + ycar[:, :w]
    y1 = jnp.dot(u1, mt_ref[1], preferred_element_type=F32) + ycar[:, w:]
    for i in range(CHUNK_HALVES):
        y_ref[i] = y0[:, i * LANES:(i + 1) * LANES]
        y_ref[CHUNK_HALVES + i] = y1[:, i * LANES:(i + 1) * LANES]


def _s5_scan(u_g, ops, bsz):
    mt, wre, wim, cre, cim, atr, ati = ops
    tiles, r, _ = u_g.shape
    g = tiles // CHUNK_HALVES
    w = S5_CHUNK * S5_GROUP
    n_chunks = r // bsz
    p2 = 2 * S5_STATE
    kern = functools.partial(_s5_kernel, n_chunks=n_chunks, bsz=bsz)
    pair_tiles = pl.BlockSpec((2 * CHUNK_HALVES, r, LANES), lambda i: (i, 0, 0))
    return pl.pallas_call(
        kern,
        out_shape=jax.ShapeDtypeStruct((tiles, r, LANES), F32),
        grid=(g // 2,),
        in_specs=[pair_tiles,
                  pl.BlockSpec((2, w, w), lambda i: (i, 0, 0)),
                  pl.BlockSpec((1, 2 * w, p2), lambda i: (i, 0, 0)),
                  pl.BlockSpec((1, 2 * w, p2), lambda i: (i, 0, 0)),
                  pl.BlockSpec((1, p2, 2 * w), lambda i: (i, 0, 0)),
                  pl.BlockSpec((1, p2, 2 * w), lambda i: (i, 0, 0)),
                  pl.BlockSpec((1, 1, p2), lambda i: (i, 0, 0)),
                  pl.BlockSpec((1, 1, p2), lambda i: (i, 0, 0))],
        out_specs=pair_tiles,
        scratch_shapes=[pltpu.VMEM((r, p2), F32)] * 4,
        compiler_params=_params("parallel"),
        name="s5_scan",
    )(u_g, mt, wre, wim, cre, cim, atr, ati)


def _hgrn_kernel(q_ref, f_ref, i_ref, g_ref, lb_ref, og_ref, o_ref, *st_refs, seqlen):
    d = HG_HEAD_DIM
    for hh, st_ref in enumerate(st_refs):
        lanes = slice(hh * d, (hh + 1) * d)
        o_ref[0, :, lanes] = _hgrn_head(q_ref[0, :, lanes], f_ref[0, :, lanes], i_ref[0, :, lanes],
                                        g_ref[0, :, lanes], lb_ref[:, lanes], og_ref[...], st_ref, seqlen)


def _hgrn_head(q, f_logit, v, g, lb, o_gain, st_ref, seqlen):
    c = HG_CHUNK
    nc = seqlen // c
    d = HG_HEAD_DIM
    qs = _silu(q)
    f = lb + (1.0 - lb) * _sigmoid(f_logit)
    lf = jnp.log(f)
    k = 1.0 - f
    sub = ROW_TILE[0]
    b8 = lf.reshape(seqlen // sub, sub, d)
    row = lax.broadcasted_iota(jnp.int32, b8.shape, 1)
    sh = 1
    while sh < sub:
        b8 = b8 + jnp.where(row >= sh, pltpu.roll(b8, sh, axis=1), 0.0)
        sh *= 2
    b4 = b8.reshape(nc, c // sub, sub, d)
    groups, run = [], None
    for gi in range(c // sub):
        grp = b4[:, gi]
        groups.append(grp if run is None else grp + run)
        total = grp[:, sub - 1:sub, :]
        run = total if run is None else run + total
    b3 = jnp.concatenate(groups, axis=1)
    b_last = b3[:, c - 1:c, :]
    b_ref = b3[:, c // 2 - 1:c // 2, :]
    qs3 = qs.reshape(nc, c, d)
    k3 = k.reshape(nc, c, d)
    v3 = v.reshape(nc, c, d).astype(BF16)
    qe_f = qs3 * jnp.exp(b3 - b_ref)
    ke_f = k3 * jnp.exp(b_ref - b3)
    qe = qe_f.astype(BF16)
    ke = ke_f.astype(BF16)
    kd = (ke_f * jnp.exp(b_last - b_ref)).astype(BF16)
    qb = (qe_f * jnp.exp(b_ref)).astype(BF16)
    scores = jnp.einsum('ctd,csd->cts', qe, ke, preferred_element_type=F32)
    ti = lax.broadcasted_iota(jnp.int32, (c, c), 0)
    si = lax.broadcasted_iota(jnp.int32, (c, c), 1)
    scores = jnp.where((ti >= si)[None], scores, 0.0)
    o_intra = jnp.einsum('cts,csv->ctv', scores.astype(BF16), v3, preferred_element_type=F32)
    ut = jnp.einsum('csv,csd->cvd', v3, kd, preferred_element_type=F32)
    decay = jnp.exp(b_last)
    state = jnp.zeros((d, d), F32)
    for ci in range(nc):
        st_ref[ci] = state.astype(BF16)
        state = decay[ci] * state + ut[ci]
    o_inter = jnp.einsum('ctd,cvd->ctv', qb, st_ref[...], preferred_element_type=F32)
    o = (o_intra + o_inter).reshape(seqlen, d)
    o = _rms(o, o_gain)
    return (o * _silu(g)).astype(BF16)


def _hgrn(h4, lower_bound, o_gain, bsz, seqlen, heads_per_step=4):
    d = HG_HEAD_DIM
    w = heads_per_step * d
    steps = HG_HEADS // heads_per_step
    kern = functools.partial(_hgrn_kernel, seqlen=seqlen)

    def col(part):
        return pl.BlockSpec((1, seqlen, w), lambda b, h: (b, 0, part * steps + h))

    return pl.pallas_call(
        kern,
        out_shape=jax.ShapeDtypeStruct((bsz, seqlen, HG_WIDTH), BF16),
        grid=(bsz, steps),
        in_specs=[col(0), col(1), col(2), col(3),
                  pl.BlockSpec((1, w), lambda b, h: (0, h)),
                  pl.BlockSpec((1, d), lambda b, h: (0, 0))],
        out_specs=pl.BlockSpec((1, seqlen, w), lambda b, h: (b, 0, h)),
        scratch_shapes=[pltpu.VMEM((seqlen // HG_CHUNK, d, d), BF16)] * heads_per_step,
        compiler_params=_params("parallel", "parallel"),
        name="hgrn2",
    )(h4, h4, h4, h4, lower_bound.reshape(1, HG_WIDTH), o_gain.reshape(1, d))


def _evenout_kernel(x_ref, ys_ref, u_ref, b_ref, d_ref, wglu_ref, wa_ref, wb_ref, *rest):
    route_in, (o_ref, *route_out), route_scratch = rest[:4], rest[4:8], rest[8:]
    y = ys_ref[...] + d_ref[...] * u_ref[...]
    y = jax.nn.gelu(y)
    gate = _sigmoid(jnp.dot(y.astype(BF16), wglu_ref[...], preferred_element_type=F32))
    a = (y * gate).astype(BF16)
    mix = (jnp.dot(a, wa_ref[...], preferred_element_type=F32)
           + jnp.dot(b_ref[...], wb_ref[...], preferred_element_type=F32))
    x_new = x_ref[...] + mix
    o_ref[...] = x_new
    _route(x_new, pl.program_id(0) == 0, *route_in, *route_out, *route_scratch)


def _evenout(x2, ys, u, b_out, d_skip, wglu, wout, router_params, tm=1024):
    n = x2.shape[0]
    route = _RouterPlumbing(n, tm, lambda i: i, *router_params)
    row = lambda w: pl.BlockSpec((tm, w), lambda i: (i, 0))
    full = lambda r, c: pl.BlockSpec((r, c), lambda i: (0, 0))
    return pl.pallas_call(
        _evenout_kernel,
        out_shape=(jax.ShapeDtypeStruct((n, D_MODEL), F32), *route.out_shape),
        grid=(n // tm,),
        in_specs=[row(D_MODEL), row(S5_WIDTH), row(S5_WIDTH), row(HG_WIDTH),
                  full(1, S5_WIDTH), full(S5_WIDTH, S5_WIDTH),
                  full(S5_WIDTH, D_MODEL), full(HG_WIDTH, D_MODEL), *route.in_specs],
        out_specs=(row(D_MODEL), *route.out_specs),
        scratch_shapes=route.scratch_shapes,
        compiler_params=_params("arbitrary"),
        name="even_out",
    )(x2, ys, u, b_out, d_skip.reshape(1, S5_WIDTH), wglu.astype(BF16),
      wout[:S5_WIDTH].astype(BF16), wout[S5_WIDTH:].astype(BF16), *route.operands)


def _route(x, is_first_step, g_ref, wr_ref, br_ref, tri_ref, idx_ref, wts_ref, cnt_ref, run_ref):
    @pl.when(is_first_step)
    def _():
        run_ref[...] = jnp.zeros_like(run_ref)

    h = _rms(x, g_ref[...])
    h_hi = h.astype(BF16)
    h_lo = (h - h_hi.astype(F32)).astype(BF16)
    both = _nt_dot(wr_ref[...], h_hi)
    lt = (both[:ROUTER_ROWS] + both[ROUTER_ROWS:] + _nt_dot(wr_ref[:ROUTER_ROWS, :], h_lo)
          + br_ref[...])
    gl = [lt[i:i + 1] for i in range(N_GROUPS)]
    el = [lt[N_GROUPS + i:N_GROUPS + i + 1] for i in range(N_EXPERTS)]
    gmax = jnp.maximum(jnp.maximum(gl[0], gl[1]), jnp.maximum(gl[2], gl[3]))
    g_idx = jnp.where(gl[0] == gmax, 0, jnp.where(gl[1] == gmax, 1, jnp.where(gl[2] == gmax, 2, 3)))
    g_gate = 1.0 / (jnp.exp(gl[0] - gmax) + jnp.exp(gl[1] - gmax) + jnp.exp(gl[2] - gmax) + jnp.exp(gl[3] - gmax))
    es = []
    for j in range(EXPERTS_PER_GROUP):
        es.append(jnp.where(g_idx == 0, el[j],
                            jnp.where(g_idx == 1, el[4 + j],
                                      jnp.where(g_idx == 2, el[8 + j], el[12 + j]))))
    e1 = jnp.maximum(jnp.maximum(es[0], es[1]), jnp.maximum(es[2], es[3]))
    i1 = jnp.where(es[0] == e1, 0, jnp.where(es[1] == e1, 1, jnp.where(es[2] == e1, 2, 3)))
    rest = [jnp.where(i1 == j, -jnp.inf, es[j]) for j in range(EXPERTS_PER_GROUP)]
    e2 = jnp.maximum(jnp.maximum(rest[0], rest[1]), jnp.maximum(rest[2], rest[3]))
    i2 = jnp.where(rest[0] == e2, 0, jnp.where(rest[1] == e2, 1, jnp.where(rest[2] == e2, 2, 3)))
    r = jnp.exp(e2 - e1)
    w1 = g_gate / (1.0 + r)
    w2 = w1 * r
    first_lo = i1 < i2
    lo = jnp.where(first_lo, i1, i2)
    hi = jnp.where(first_lo, i2, i1)
    w_lo = jnp.where(first_lo, w1, w2)
    w_hi = jnp.where(first_lo, w2, w1)
    pair = jnp.where(lo == 0, 0, jnp.where(lo == 1, 3, 5)) + hi - lo - 1
    bucket = g_idx * N_PAIRS + pair
    tm = bucket.shape[1]
    rowid = lax.broadcasted_iota(jnp.int32, (BUCKET_ROWS, tm), 0)
    onehot = (rowid == bucket).astype(F32)
    prefix = jnp.dot(onehot.astype(BF16), tri_ref[...], preferred_element_type=F32)
    run = run_ref[...]
    rank = jnp.sum(onehot * (prefix + run), axis=0, keepdims=True)
    run = run + jnp.sum(onehot, axis=1, keepdims=True)
    run_ref[...] = run
    cnt_ref[...] = jnp.broadcast_to(run, cnt_ref.shape)
    idx_ref[...] = jnp.concatenate([bucket, rank.astype(jnp.int32), jnp.zeros((6, tm), jnp.int32)], axis=0)
    wts_ref[...] = jnp.concatenate([w_lo, w_hi, jnp.zeros((6, tm), F32)], axis=0)


class _RouterPlumbing:
    def __init__(self, n, tm, tile_index, gain, w_rg, b_rg, w_re, b_re):
        wr = jnp.concatenate([w_rg, w_re], axis=1).astype(F32).T
        wr = jnp.pad(wr, ((0, ROUTER_ROWS - wr.shape[0]), (0, 0)))
        wr_hi = wr.astype(BF16)
        wr = jnp.concatenate([wr_hi, (wr - wr_hi.astype(F32)).astype(BF16)], axis=0)
        br = jnp.pad(jnp.concatenate([b_rg, b_re]).astype(F32), (0, ROUTER_ROWS - N_GROUPS - N_EXPERTS))
        tri = (np.arange(tm)[:, None] < np.arange(tm)[None, :]).astype(np.float32)
        const = lambda *_: (0, 0)
        self.operands = (gain.reshape(1, D_MODEL), wr, br.reshape(ROUTER_ROWS, 1), jnp.asarray(tri, dtype=BF16))
        self.in_specs = [pl.BlockSpec((1, D_MODEL), const),
                         pl.BlockSpec((2 * ROUTER_ROWS, D_MODEL), const),
                         pl.BlockSpec((ROUTER_ROWS, 1), const),
                         pl.BlockSpec((tm, tm), const)]
        self.out_shape = (jax.ShapeDtypeStruct((8, n), jnp.int32),
                          jax.ShapeDtypeStruct((8, n), F32),
                          jax.ShapeDtypeStruct((BUCKET_ROWS, LANES), F32))
        self.out_specs = (pl.BlockSpec((8, tm), lambda *g: (0, tile_index(*g))),
                          pl.BlockSpec((8, tm), lambda *g: (0, tile_index(*g))),
                          pl.BlockSpec((BUCKET_ROWS, LANES), const))
        self.scratch_shapes = [pltpu.VMEM((BUCKET_ROWS, 1), F32)]


def _router_kernel(x_ref, *route_refs):
    _route(x_ref[...], pl.program_id(0) == 0, *route_refs)


def _router(x2, router_params, tm=1024):
    n = x2.shape[0]
    route = _RouterPlumbing(n, tm, lambda i: i, *router_params)
    return pl.pallas_call(
        _router_kernel,
        out_shape=route.out_shape,
        grid=(n // tm,),
        in_specs=[pl.BlockSpec((tm, D_MODEL), lambda i: (i, 0)), *route.in_specs],
        out_specs=route.out_specs,
        scratch_shapes=route.scratch_shapes,
        compiler_params=_params("arbitrary"),
        name="moe_router",
    )(x2, *route.operands)


ROW_COPY_UNROLL = 8


def _start_row_copies(idx_ref, n_rows, copy_for_row, prepare_rows=None):
    def start_group(base):
        for j in range(ROW_COPY_UNROLL):
            copy_for_row(base + j, idx_ref[0, 0, base + j]).start(priority=j % 2)

    n_groups = n_rows // ROW_COPY_UNROLL
    if prepare_rows is None:
        def body(g, carry):
            start_group(pl.multiple_of(g * ROW_COPY_UNROLL, ROW_COPY_UNROLL))
            return carry

        lax.fori_loop(0, n_groups, body, 0)
        return

    prepare_rows(0)

    def body(g, carry):
        base = pl.multiple_of(g * ROW_COPY_UNROLL, ROW_COPY_UNROLL)
        prepare_rows(base + ROW_COPY_UNROLL)
        start_group(base)
        return carry

    lax.fori_loop(0, n_groups - 1, body, 0)
    start_group(n_rows - ROW_COPY_UNROLL)


def _row_slab(view_ref, p):
    return view_ref.at[p >> ROW_TILE_SHIFT, :, p & (ROW_TILE[0] - 1)]


def _view_columns(view_ref, n_cols):
    rows = view_ref.shape[0] * ROW_TILE[0]
    return jnp.concatenate([view_ref[:, c].reshape(rows, LANES) for c in range(n_cols)], axis=1)


def _rows_to_tiles(x):
    rows = x.shape[0]
    return x.reshape(rows * ROW_TILE[0], ROW_TILE[1]).reshape(rows, *ROW_TILE)


def _tiles_to_rows(x3):
    rows = x3.shape[0]
    return x3.reshape(rows * ROW_TILE[0], ROW_TILE[1]).reshape(rows, D_MODEL)


def _dispatch_kernel(tail_blk_ref, tail_on_ref, pos_ref, x_ref, w_ref, xs_ref, buf_ref, zero_ref, wcol_ref, sem,
                     *, tile):
    tm = x_ref.shape[0]
    tile_blks = tile // ROW_TILE[0]

    @pl.when(pl.program_id(0) == 0)
    def _():
        zero_ref[...] = jnp.zeros_like(zero_ref)

        def zero_copy(k):
            blk = pl.multiple_of(tail_blk_ref[k], tile_blks)
            return pltpu.make_async_copy(zero_ref, xs_ref.at[pl.ds(blk, tile_blks)], sem)

        for k in range(2 * N_BUCKETS):
            pl.when(tail_on_ref[k] > 0)(lambda k=k: zero_copy(k).start())
        for k in range(2 * N_BUCKETS):
            pl.when(tail_on_ref[k] > 0)(lambda k=k: zero_copy(k).wait())

    wpad = jnp.concatenate([w_ref[...], jnp.zeros((LANES - w_ref.shape[0], tm), F32)], axis=0)
    wcol_ref[...] = wpad.T

    def stage(base):
        rows = pl.ds(base, ROW_COPY_UNROLL)
        buf_ref[rows, :X_TILES, :] = _rows_to_tiles(x_ref[rows, :])
        buf_ref[rows, X_TILES, :] = wcol_ref[rows, :]

    _start_row_copies(pos_ref, tm, lambda r, p: pltpu.make_async_copy(buf_ref.at[r], _row_slab(xs_ref, p), sem),
                      prepare_rows=stage)
    done = xs_ref.at[pl.ds(0, tm // ROW_TILE[0])]
    pltpu.make_async_copy(done, done, sem).wait()


def _dispatch(x2, wts, pos3, tails, n_rows_sorted, tile, tm):
    n = x2.shape[0]
    tail_blk, tail_on = tails
    grid_spec = pltpu.PrefetchScalarGridSpec(
        num_scalar_prefetch=2,
        grid=(n // tm,),
        in_specs=[pl.BlockSpec((1, 1, tm), lambda i, *_: (i, 0, 0), memory_space=pltpu.SMEM),
                  pl.BlockSpec((tm, D_MODEL), lambda i, *_: (i, 0)),
                  pl.BlockSpec((8, tm), lambda i, *_: (0, i))],
        out_specs=pl.BlockSpec(memory_space=pl.ANY),
        scratch_shapes=[pltpu.VMEM((tm, XS_TILES, LANES), F32),
                        pltpu.VMEM((tile // ROW_TILE[0], XS_TILES, *ROW_TILE), F32),
                        pltpu.VMEM((tm, LANES), F32),
                        pltpu.SemaphoreType.DMA],
    )
    return pl.pallas_call(
        functools.partial(_dispatch_kernel, tile=tile),
        out_shape=jax.ShapeDtypeStruct((n_rows_sorted // ROW_TILE[0], XS_TILES, *ROW_TILE), F32),
        grid_spec=grid_spec,
        compiler_params=_params("arbitrary"),
        name="moe_dispatch",
    )(tail_blk, tail_on, pos3, x2, wts)


def _experts_kernel(elo_ref, ehi_ref, nvalid_ref, xs_ref, g_ref, wg_lo, wu_lo, wg_hi, wu_hi,
                    wd_lo, wd_hi, o_ref):
    del elo_ref, ehi_ref
    t = pl.program_id(0)

    @pl.when(t < nvalid_ref[0])
    def _():
        rows = xs_ref.shape[0] * ROW_TILE[0]
        xt = _view_columns(xs_ref, X_TILES)
        h = _rms(xt, g_ref[...]).astype(BF16)
        extra = xs_ref[:, X_TILES].reshape(rows, LANES)
        w_lo = extra[:, 0:1]
        w_hi = extra[:, 1:2]

        def expert(wg, wu, wd, w):
            gate = jnp.dot(h, wg[0], preferred_element_type=F32)
            up = jnp.dot(h, wu[0], preferred_element_type=F32)
            hid = (_silu(gate) * up * w).astype(BF16)
            return jnp.dot(hid, wd[0], preferred_element_type=F32)

        out = xt + expert(wg_lo, wu_lo, wd_lo, w_lo) + expert(wg_hi, wu_hi, wd_hi, w_hi)
        for c in range(X_TILES):
            o_ref[:, c] = out[:, c * LANES:(c + 1) * LANES].reshape(o_ref.shape[0], *ROW_TILE)

    @pl.when(t >= nvalid_ref[0])
    def _():
        o_ref[...] = jnp.zeros_like(o_ref)


def _experts(xs, gain, tables, wg, wu, wd, n_tiles, t):
    elo, ehi, nvalid = tables
    blks = t // ROW_TILE[0]
    row = lambda i, elo, ehi, nv: (i, 0, 0, 0)
    row_in = lambda i, elo, ehi, nv: (jnp.minimum(i, nv[0] - 1), 0, 0, 0)
    lo3 = lambda i, elo, ehi, nv: (elo[i], 0, 0)
    hi3 = lambda i, elo, ehi, nv: (ehi[i], 0, 0)
    grid_spec = pltpu.PrefetchScalarGridSpec(
        num_scalar_prefetch=3,
        grid=(n_tiles,),
        in_specs=[pl.BlockSpec((blks, XS_TILES, *ROW_TILE), row_in),
                  pl.BlockSpec((1, D_MODEL), lambda i, *_: (0, 0)),
                  pl.BlockSpec((1, D_MODEL, D_EXPERT), lo3),
                  pl.BlockSpec((1, D_MODEL, D_EXPERT), lo3),
                  pl.BlockSpec((1, D_MODEL, D_EXPERT), hi3),
                  pl.BlockSpec((1, D_MODEL, D_EXPERT), hi3),
                  pl.BlockSpec((1, D_EXPERT, D_MODEL), lo3),
                  pl.BlockSpec((1, D_EXPERT, D_MODEL), hi3)],
        out_specs=pl.BlockSpec((blks, X_TILES, *ROW_TILE), row),
    )
    return pl.pallas_call(
        _experts_kernel,
        out_shape=jax.ShapeDtypeStruct((xs.shape[0], X_TILES, *ROW_TILE), F32),
        grid_spec=grid_spec,
        compiler_params=_params("arbitrary"),
        name="moe_experts",
    )(elo, ehi, nvalid, xs, gain.reshape(1, D_MODEL), wg, wu, wg, wu, wd, wd)


def _combine_kernel(pos_ref, ys_ref, o_ref, buf_ref, sem):
    tm = o_ref.shape[0]
    _start_row_copies(pos_ref, tm, lambda r, p: pltpu.make_async_copy(_row_slab(ys_ref, p), buf_ref.at[r], sem))
    done = ys_ref.at[pl.ds(0, tm // ROW_TILE[0])]
    pltpu.make_async_copy(done, done, sem).wait()
    o_ref[...] = _tiles_to_rows(buf_ref[...])


def _combine(ys, pos3, n, tm):
    return pl.pallas_call(
        _combine_kernel,
        out_shape=jax.ShapeDtypeStruct((n, D_MODEL), F32),
        grid=(n // tm,),
        in_specs=[pl.BlockSpec((1, 1, tm), lambda i: (i, 0, 0), memory_space=pltpu.SMEM),
                  pl.BlockSpec(memory_space=pl.ANY)],
        out_specs=pl.BlockSpec((tm, D_MODEL), lambda i: (i, 0)),
        scratch_shapes=[pltpu.VMEM((tm, *ROW_TILE), F32), pltpu.SemaphoreType.DMA],
        compiler_params=_params("arbitrary"),
        name="moe_combine",
    )(pos3, ys)


def _moe_tables(idx, cnt, n_tiles, t):
    bucket, rank = idx[0], idx[1]
    counts = cnt[:N_BUCKETS, 0].astype(jnp.int32)
    tiles_b = (counts + t - 1) // t
    tile_end = jnp.cumsum(tiles_b)
    pos = (tile_end - tiles_b)[bucket] * t + rank
    total = tile_end[-1]
    tt = jnp.arange(n_tiles, dtype=jnp.int32)
    valid = tt < total
    tb = jnp.sum((tile_end[None, :] <= jnp.where(valid, tt, total - 1)[:, None]).astype(jnp.int32), axis=1)
    tb = jnp.minimum(tb, N_BUCKETS - 1)
    pair_lo = jnp.asarray([0, 0, 0, 1, 1, 2], jnp.int32)
    pair_hi = jnp.asarray([1, 2, 3, 2, 3, 3], jnp.int32)
    base = (tb // N_PAIRS) * EXPERTS_PER_GROUP
    idle = total + jnp.arange(N_BUCKETS, dtype=jnp.int32)
    idle_on = idle < n_tiles
    blks = t // ROW_TILE[0]
    tails = (jnp.concatenate([(tile_end - 1) * blks, jnp.where(idle_on, idle, 0) * blks]),
             jnp.concatenate([tiles_b > 0, idle_on]).astype(jnp.int32))
    return pos, tails, (base + pair_lo[tb % N_PAIRS], base + pair_hi[tb % N_PAIRS], total.reshape(1))


def _moe(x2, routing, gain, wg, wu, wd, t=MOE_TILE, tm_dispatch=2048, tm_combine=2048):
    n = x2.shape[0]
    idx, wts, cnt = routing
    n_tiles = n // t + N_BUCKETS
    pos, tails, tables = _moe_tables(idx, cnt, n_tiles, t)
    xs = _dispatch(x2, wts, pos.reshape(n // tm_dispatch, 1, tm_dispatch), tails, n_tiles * t, t, tm_dispatch)
    ys = _experts(xs, gain, tables, wg.astype(BF16), wu.astype(BF16), wd.astype(BF16), n_tiles, t)
    return _combine(ys, pos.reshape(n // tm_combine, 1, tm_combine), n, tm_combine)


LOG2E = math.log2(math.e)
V_EXT = 2 * HEAD_DIM


def _head_norm_t(y_t, n_heads, scale):
    tm = y_t.shape[1]
    y3 = y_t.reshape(n_heads, HEAD_DIM, tm)
    ms = jnp.mean(y3 * y3, axis=1, keepdims=True)
    return y3 * (lax.rsqrt(ms + EPS) * scale)


def _qkv_kernel(x_ref, g_ref, wqt_ref, wkt_ref, wvt_ref, vone_ref, kg_ref, qt_ref, k_ref, vt_ref):
    h = _rms(x_ref[...], g_ref[...]).astype(BF16)
    tm = h.shape[0]
    qn = _head_norm_t(_nt_dot(wqt_ref[...], h), N_Q_HEADS, HEAD_DIM ** -0.5 * LOG2E)
    qt_ref[...] = qn.reshape(N_Q_HEADS * HEAD_DIM, tm).astype(BF16)
    kn = _head_norm_t(_nt_dot(wkt_ref[...], h), N_KV_HEADS, kg_ref[...].reshape(N_KV_HEADS, HEAD_DIM, 1))
    for hk in range(N_KV_HEADS):
        k_ref[hk] = kn[hk].T.astype(BF16)
    vt_ref[...] = (_nt_dot(wvt_ref[...], h) + vone_ref[...]).astype(BF16)


def _qkv(x2, gain, wqkv, q_gain, k_gain, tm=1024):
    n = x2.shape[0]
    qw = N_Q_HEADS * HEAD_DIM
    wqt = wqkv[:, :qw].T.astype(BF16)
    wkt = wqkv[:, qw:qw + KV_WIDTH].T.astype(BF16)
    wvt = wqkv[:, qw + KV_WIDTH:].T.astype(BF16).reshape(N_KV_HEADS, HEAD_DIM, D_MODEL)
    wvt = jnp.pad(wvt, ((0, 0), (0, V_EXT - HEAD_DIM), (0, 0))).reshape(N_KV_HEADS * V_EXT, D_MODEL)
    vone = np.zeros((N_KV_HEADS * V_EXT, 1), np.float32)
    vone[HEAD_DIM::V_EXT, 0] = 1.0
    kg = jnp.tile((k_gain.astype(F32) * q_gain.astype(F32)), N_KV_HEADS).reshape(KV_WIDTH, 1)
    full = lambda r, c: pl.BlockSpec((r, c), lambda i: (0, 0))
    return pl.pallas_call(
        _qkv_kernel,
        out_shape=(jax.ShapeDtypeStruct((qw, n), BF16),
                   jax.ShapeDtypeStruct((N_KV_HEADS, n, HEAD_DIM), BF16),
                   jax.ShapeDtypeStruct((N_KV_HEADS * V_EXT, n), BF16)),
        grid=(n // tm,),
        in_specs=[pl.BlockSpec((tm, D_MODEL), lambda i: (i, 0)), full(1, D_MODEL),
                  full(qw, D_MODEL), full(KV_WIDTH, D_MODEL), full(N_KV_HEADS * V_EXT, D_MODEL),
                  full(N_KV_HEADS * V_EXT, 1), full(KV_WIDTH, 1)],
        out_specs=(pl.BlockSpec((qw, tm), lambda i: (0, i)),
                   pl.BlockSpec((N_KV_HEADS, tm, HEAD_DIM), lambda i: (0, i, 0)),
                   pl.BlockSpec((N_KV_HEADS * V_EXT, tm), lambda i: (0, i))),
        compiler_params=_params("parallel"),
        name="odd_qkv",
    )(x2, gain.reshape(1, D_MODEL), wqt, wkt, wvt, jnp.asarray(vone), kg)


def _attn_bias():
    blk = ATT_BLOCK
    qi = np.arange(blk)[None, :]
    ki = np.arange(2 * blk)[:, None]
    dist = qi - ki + blk
    band = (dist >= 0) & (dist < blk)
    slopes = 2.0 ** (-8.0 * np.arange(1, N_Q_HEADS + 1) / N_Q_HEADS)
    pen = -slopes[:, None, None] * dist[None].astype(np.float64) * LOG2E
    inner = np.where(band[None], pen, -np.inf)
    first = np.where((band & (ki >= blk))[None], pen, -np.inf)
    tab = np.stack([inner, first]).astype(np.float32)
    tab = tab.reshape(2, N_KV_HEADS, GQA_GROUP, 2 * blk, blk).transpose(0, 1, 3, 2, 4)
    return tab.reshape(2, N_KV_HEADS, 2 * blk, GQA_GROUP * blk)


def _attn_kernel(qt_ref, kp_ref, kc_ref, vtp_ref, vtc_ref, bias_ref, sink_ref, x_ref, wo_ref, o_ref):
    blk = ATT_BLOCK
    for sb in range(ATT_STEP_BLOCKS):
        tok = slice(sb * blk, (sb + 1) * blk)
        if sb == 0:
            first = (pl.program_id(1) == 0).astype(jnp.int32)
            k_prev = [kp_ref[hk] for hk in range(N_KV_HEADS)]
            vt_prev = vtp_ref[...]
        else:
            first = 0
            k_prev = [kc_ref[hk, (sb - 1) * blk:sb * blk, :] for hk in range(N_KV_HEADS)]
            vt_prev = vtc_ref[:, (sb - 1) * blk:sb * blk]
        vt = jnp.concatenate([vt_prev, vtc_ref[:, tok]], axis=1)
        att_t = []
        for hk in range(N_KV_HEADS):
            keys = jnp.concatenate([k_prev[hk], kc_ref[hk, tok, :]], axis=0)
            q_t = jnp.concatenate(
                [qt_ref[(hk * GQA_GROUP + g) * HEAD_DIM:(hk * GQA_GROUP + g + 1) * HEAD_DIM, tok]
                 for g in range(GQA_GROUP)], axis=1)
            s = jnp.dot(keys, q_t, preferred_element_type=F32) + bias_ref[first, hk]
            sink = sink_ref[hk]
            m = jnp.maximum(jnp.max(s, axis=0, keepdims=True), sink)
            p = jnp.exp2(s - m).astype(BF16)
            pv = jnp.dot(vt[hk * V_EXT:(hk + 1) * V_EXT, :], p, preferred_element_type=F32)
            den = pv[HEAD_DIM:HEAD_DIM + 1, :] + jnp.exp2(sink - m)
            o_t = (pv[:HEAD_DIM, :] * (1.0 / den)).astype(BF16)
            att_t += [o_t[:, g * blk:(g + 1) * blk] for g in range(GQA_GROUP)]
        att_t = jnp.concatenate(att_t, axis=0)
        mix = lax.dot_general(att_t, wo_ref[...], (((0,), (0,)), ((), ())), preferred_element_type=F32)
        o_ref[tok, :] = x_ref[tok, :] + mix


def _attn(qt, k, vt, x2, sinks, wo, bsz, seqlen):
    blk = ATT_BLOCK
    step = ATT_STEP_BLOCKS * blk
    nb = seqlen // blk
    ns = seqlen // step
    qw = N_Q_HEADS * HEAD_DIM
    cols = GQA_GROUP * blk
    cur = lambda b, n: (b * ns + n, 0)
    cur_t = lambda b, n: (0, b * ns + n)
    prev_blk = lambda b, n: b * nb + jnp.maximum(ATT_STEP_BLOCKS * n - 1, 0)
    sink_row = jnp.repeat(sinks.astype(F32) * LOG2E, blk).reshape(N_KV_HEADS, 1, cols)
    return pl.pallas_call(
        _attn_kernel,
        out_shape=jax.ShapeDtypeStruct((bsz * seqlen, D_MODEL), F32),
        grid=(bsz, ns),
        in_specs=[pl.BlockSpec((qw, step), cur_t),
                  pl.BlockSpec((N_KV_HEADS, blk, HEAD_DIM), lambda b, n: (0, prev_blk(b, n), 0)),
                  pl.BlockSpec((N_KV_HEADS, step, HEAD_DIM), lambda b, n: (0, b * ns + n, 0)),
                  pl.BlockSpec((N_KV_HEADS * V_EXT, blk), lambda b, n: (0, prev_blk(b, n))),
                  pl.BlockSpec((N_KV_HEADS * V_EXT, step), cur_t),
                  pl.BlockSpec((2, N_KV_HEADS, 2 * blk, cols), lambda b, n: (0, 0, 0, 0)),
                  pl.BlockSpec((N_KV_HEADS, 1, cols), lambda b, n: (0, 0, 0)),
                  pl.BlockSpec((step, D_MODEL), cur),
                  pl.BlockSpec((qw, D_MODEL), lambda b, n: (0, 0))],
        out_specs=pl.BlockSpec((step, D_MODEL), cur),
        compiler_params=_params("parallel", "parallel"),
        name="odd_attn",
    )(qt, k, k, vt, vt, jnp.asarray(_attn_bias()), sink_row, x2, wo.astype(BF16))


def kernel(x, even_mix_norm, even_in_proj, s5_lambda_re, s5_lambda_im, s5_log_step, s5_b_re, s5_b_im,
           s5_c_re, s5_c_im, s5_d, s5_glu_w, hgrn_lower_bounds, hgrn_o_norm, even_out_proj, odd_mix_norm,
           odd_wqkv, odd_q_norm, odd_k_norm, odd_sinks, odd_out_proj, moe_norm, moe_router_group,
           moe_router_group_bias, moe_router_expert, moe_router_expert_bias, moe_w_gate, moe_w_up,
           moe_w_down):
    bsz, seqlen, dm = x.shape
    n = bsz * seqlen
    x2 = x.reshape(n, dm)
    lower_bounds = jnp.cumsum(jax.nn.softmax(hgrn_lower_bounds.astype(F32), axis=0), axis=0)

    def router_params(layer):
        return (moe_norm[layer], moe_router_group[layer], moe_router_group_bias[layer],
                moe_router_expert[layer], moe_router_expert_bias[layer])

    def moe(xx, routing, layer):
        return _moe(xx, routing, moe_norm[layer], moe_w_gate[layer], moe_w_up[layer], moe_w_down[layer])

    u, h4 = _inproj(x2, even_mix_norm[0], even_in_proj[0].astype(BF16))
    ops = _s5_operators(s5_lambda_re[0], s5_lambda_im[0], s5_log_step[0], s5_b_re[0], s5_b_im[0],
                        s5_c_re[0], s5_c_im[0])
    u_g = _s5_pack(u.reshape(bsz, seqlen, S5_WIDTH))
    y_g = _s5_scan(u_g, ops, bsz)
    ys = _s5_unpack(y_g, bsz, seqlen).reshape(n, S5_WIDTH)
    b_out = _hgrn(h4.reshape(bsz, seqlen, 4 * HG_WIDTH), lower_bounds[0], hgrn_o_norm[0], bsz, seqlen)
    x2, *routing = _evenout(x2, ys, u, b_out.reshape(n, HG_WIDTH), s5_d[0], s5_glu_w[0], even_out_proj[0],
                            router_params(0))
    x2 = moe(x2, routing, 0)

    q, kt, v = _qkv(x2, odd_mix_norm[0], odd_wqkv[0], odd_q_norm[0], odd_k_norm[0])
    x2 = _attn(q, kt, v, x2, odd_sinks[0], odd_out_proj[0], bsz, seqlen)
    x2 = moe(x2, _router(x2, router_params(1)), 1)
    return x2.reshape(bsz, seqlen, dm)
```
